```python
import jax, jax.numpy as jnp
from jax import lax
import numpy as np

D_MODEL = 2048
BATCH = 1
SEQ = 8192
DEPTH = 4
DEC_BATCH = 8
DEC_SEQ = 32
PAST_LEN = 2048

CHUNK = 64
QBLK = 128
ROPE_THETA = 500000.0
EPS = 1e-6
N_EVEN = (DEPTH + 1) // 2
N_ODD = DEPTH // 2
H_A = 8
KV_A = 2
DH = 128
ROT_A = DH // 4
H_I = 16
D_IDX = 64
ROT_I = D_IDX // 4
TOPK_MAX = 256
H_B = 8
AB_SIZES = (H_A * DH, KV_A * DH, KV_A * DH, H_I * D_IDX, D_IDX, H_I, H_B * DH, H_B * DH, H_B * DH)
AB_COLS = H_A * DH + 2 * KV_A * DH + H_I * D_IDX + D_IDX + H_I + 3 * H_B * DH
MIX_AB = H_A * DH + H_B * DH
H_C = 16
Q_RANK = 512
KV_RANK = 512
NOPE = 128
ROPE_C = 64
VD = 128
DQKV_COLS = Q_RANK + KV_RANK + ROPE_C
D_FF = 5632
CONV_W = 3

kernel_name = 'hybrid_dsa_stickbreak_mla_convffn_step'

F32 = jnp.float32


def _split(x, sizes):
    idx = np.cumsum(np.array(sizes))[:-1].tolist()
    return jnp.split(x, idx, axis=-1)


def _rmsnorm(x, g):
    xf = x.astype(F32)
    y = xf * lax.rsqrt(jnp.mean(xf * xf, axis=-1, keepdims=True) + EPS)
    return (y * g.astype(F32)).astype(x.dtype)


def _rope(x, pos, rot):
    half = rot // 2
    freqs = ROPE_THETA ** (-jnp.arange(half, dtype=F32) / half)
    ang = pos.astype(F32)[:, None] * freqs[None, :]
    ang = ang.reshape((1, ang.shape[0]) + (1,) * (x.ndim - 3) + (half,))
    cos, sin = jnp.cos(ang), jnp.sin(ang)
    xf = x.astype(F32)
    x1, x2, rest = xf[..., :half], xf[..., half:rot], xf[..., rot:]
    out = jnp.concatenate([x1 * cos - x2 * sin, x1 * sin + x2 * cos, rest], axis=-1)
    return out.astype(x.dtype)


def _query_blocks(fn, q_pos, *qs):
    T = q_pos.shape[0]
    qb = min(QBLK, T)
    nb = T // qb

    def split(a):
        return jnp.moveaxis(a.reshape((a.shape[0], nb, qb) + a.shape[2:]), 1, 0)

    out = lax.map(lambda args: fn(*args), (q_pos.reshape(nb, qb),) + tuple(split(a) for a in qs))
    out = jnp.moveaxis(out, 0, 1)
    return out.reshape((out.shape[0], T) + out.shape[3:])


def _dsa_attention(q, k, v, qi, ki, wi, q_pos, k_pos):
    B, S = k.shape[0], k.shape[1]
    top = min(TOPK_MAX, S // 4)
    k_chunk = k_pos // CHUNK
    scale_i = (H_I * D_IDX) ** -0.5
    scale = DH ** -0.5

    def block(pos, qblk, qiblk, wiblk):
        nq = qblk.shape[1]
        q_chunk = pos // CHUNK
        rel = jax.nn.relu(jnp.einsum('bthd,bsd->btsh', qiblk.astype(F32), ki.astype(F32)))
        score = jnp.einsum('btsh,bth->bts', rel, wiblk.astype(F32)) * scale_i
        adm = k_chunk[None, :] <= q_chunk[:, None]
        score = jnp.where(adm[None], score, -jnp.inf)
        _, sel = lax.top_k(score, top)
        ok = k_chunk[sel] <= q_chunk[None, :, None]
        kg = jax.vmap(lambda kk, ss: kk[ss])(k, sel)
        vg = jax.vmap(lambda vv, ss: vv[ss])(v, sel)
        qg = qblk.reshape(B, nq, KV_A, H_A // KV_A, DH)
        s = jnp.einsum('btgrd,btkgd->btgrk', qg.astype(F32), kg.astype(F32)) * scale
        s = jnp.where(ok[:, :, None, None, :], s, -jnp.inf)
        p = jax.nn.softmax(s, axis=-1)
        o = jnp.einsum('btgrk,btkgd->btgrd', p, vg.astype(F32))
        return o.reshape(B, nq, H_A * DH).astype(q.dtype)

    return _query_blocks(block, q_pos, q, qi, wi)


def _stick_breaking(q, k, v, q_pos, k_pos):
    scale = DH ** -0.5

    def block(pos, qblk):
        z = jnp.einsum('bthd,bshd->bhts', qblk.astype(F32), k.astype(F32)) * scale
        causal = (k_pos[None, :] < pos[:, None])[None, None]
        log_beta = jax.nn.log_sigmoid(z)
        log_keep = jnp.where(causal, jax.nn.log_sigmoid(-z), 0.0)
        between = lax.cumsum(log_keep, axis=3, reverse=True) - log_keep
        w = jnp.where(causal, jnp.exp(log_beta + between), 0.0)
        o = jnp.einsum('bhts,bshd->bthd', w, v.astype(F32))
        return o.reshape(o.shape[0], o.shape[1], H_B * DH).astype(q.dtype)

    return _query_blocks(block, q_pos, q)


def _mla_attention(q_nope, q_rope, k_nope, k_rope, v, q_pos, k_pos):
    scale = (NOPE + ROPE_C) ** -0.5
    k_chunk = k_pos // CHUNK

    def block(pos, qn, qr):
        s = (jnp.einsum('bthd,bshd->bhts', qn.astype(F32), k_nope.astype(F32))
             + jnp.einsum('bthd,bsd->bhts', qr.astype(F32), k_rope.astype(F32))) * scale
        mask = (k_chunk[None, :] <= (pos // CHUNK)[:, None])[None, None]
        p = jax.nn.softmax(jnp.where(mask, s, -jnp.inf), axis=-1)
        o = jnp.einsum('bhts,bshd->bthd', p, v.astype(F32))
        return o.reshape(o.shape[0], o.shape[1], H_C * VD).astype(qn.dtype)

    return _query_blocks(block, q_pos, q_nope, q_rope)


def _ab_mixer(h, q_pos, past, w_in, w_out):
    B, T, _ = h.shape
    qa, ka, va, qi, ki, wi, qb, kb, vb = _split(h @ w_in, AB_SIZES)
    qa = _rope(qa.reshape(B, T, H_A, DH), q_pos, ROT_A)
    ka = _rope(ka.reshape(B, T, KV_A, DH), q_pos, ROT_A)
    va = va.reshape(B, T, KV_A, DH)
    qi = _rope(qi.reshape(B, T, H_I, D_IDX), q_pos, ROT_I)
    ki = _rope(ki, q_pos, ROT_I)
    qb = qb.reshape(B, T, H_B, DH)
    kb = kb.reshape(B, T, H_B, DH)
    vb = vb.reshape(B, T, H_B, DH)
    new_rows = (ka, va, ki, kb, vb)
    if past is None:
        full = new_rows
    else:
        full = tuple(jnp.concatenate([p, n], axis=1) for p, n in zip(past, new_rows))
    k_pos = jnp.arange(full[0].shape[1])
    oa = _dsa_attention(qa, full[0], full[1], qi, full[2], wi, q_pos, k_pos)
    ob = _stick_breaking(qb, full[3], full[4], q_pos, k_pos)
    out = jnp.concatenate([oa, ob], axis=-1) @ w_out
    return out, new_rows


def _mla_mixer(h, q_pos, past, w_dqkv, g_q, g_kv, w_uq, w_ukv, w_o):
    B, T, _ = h.shape
    cq, ckv, kr = _split(h @ w_dqkv, (Q_RANK, KV_RANK, ROPE_C))
    cq = _rmsnorm(cq, g_q)
    ckv = _rmsnorm(ckv, g_kv)
    kr = _rope(kr, q_pos, ROPE_C)
    q = (cq @ w_uq).reshape(B, T, H_C, NOPE + ROPE_C)
    q_nope = q[..., :NOPE]
    q_rope = _rope(q[..., NOPE:], q_pos, ROPE_C)
    new_rows = (ckv, kr)
    if past is None:
        c_all, kr_all = ckv, kr
    else:
        c_all = jnp.concatenate([past[0], ckv], axis=1)
        kr_all = jnp.concatenate([past[1], kr], axis=1)
    S = c_all.shape[1]
    kv = (c_all @ w_ukv).reshape(B, S, H_C, NOPE + VD)
    k_nope, v = kv[..., :NOPE], kv[..., NOPE:]
    k_pos = jnp.arange(S)
    o = _mla_attention(q_nope, q_rope, k_nope, kr_all, v, q_pos, k_pos)
    return o @ w_o, new_rows


def _conv_ffn(h, prev, w_up, conv_w, conv_b, w_down):
    T = h.shape[1]
    u = h @ w_up
    hp = jnp.concatenate([prev, u], axis=1)
    y = sum(conv_w[i] * hp[:, i:i + T] for i in range(CONV_W)) + conv_b
    g, val = y[..., :D_FF], y[..., D_FF:]
    out = (jax.nn.silu(g) * val) @ w_down
    return out, hp[:, -(CONV_W - 1):]


def _trunk(x, past, conv_state, norm_mix, norm_ffn, norm_final, w_in_ab, w_out_ab, w_dqkv, g_q, g_kv,
           w_uq, w_ukv, w_o_mla, w_up, conv_w, conv_b, w_down):
    B, T, _ = x.shape
    start = 0 if past is None else past[0].shape[2]
    q_pos = start + jnp.arange(T)
    ab_rows, mla_rows, conv_rows = [], [], []
    for l in range(DEPTH):
        h = _rmsnorm(x, norm_mix[l])
        if l % 2 == 0:
            e = l // 2
            lp = None if past is None else tuple(c[e] for c in past[:5])
            out, rows = _ab_mixer(h, q_pos, lp, w_in_ab[e], w_out_ab[e])
            ab_rows.append(rows)
        else:
            o = l // 2
            lp = None if past is None else (past[5][o], past[6][o])
            out, rows = _mla_mixer(h, q_pos, lp, w_dqkv[o], g_q[o], g_kv[o], w_uq[o], w_ukv[o], w_o_mla[o])
            mla_rows.append(rows)
        x = x + out
        h = _rmsnorm(x, norm_ffn[l])
        prev = jnp.zeros((B, CONV_W - 1, 2 * D_FF), x.dtype) if conv_state is None else conv_state[l]
        out, st = _conv_ffn(h, prev, w_up[l], conv_w[l], conv_b[l], w_down[l])
        conv_rows.append(st)
        x = x + out
    y = _rmsnorm(x, norm_final)
    new_ab = [jnp.stack([r[i] for r in ab_rows]) for i in range(5)]
    new_mla = [jnp.stack([r[i] for r in mla_rows]) for i in range(2)]
    return y, new_ab + new_mla + [jnp.stack(conv_rows)]


def setup_inputs(seed: int = 0) -> dict:
    key = jax.random.key(seed)
    ks = jax.random.split(key, 32)

    def nrm(k, shape, scale):
        return jax.random.normal(k, shape, F32) * scale

    return {
        'x_prompt': nrm(ks[0], (BATCH, SEQ, D_MODEL), 1.0),
        'x_sample': nrm(ks[1], (DEC_BATCH, DEC_SEQ, D_MODEL), 1.0),
        'cache_k_a': nrm(ks[2], (N_EVEN, DEC_BATCH, PAST_LEN, KV_A, DH), 1.0),
        'cache_v_a': nrm(ks[3], (N_EVEN, DEC_BATCH, PAST_LEN, KV_A, DH), 1.0),
        'cache_idx_k': nrm(ks[4], (N_EVEN, DEC_BATCH, PAST_LEN, D_IDX), 1.0),
        'cache_k_b': nrm(ks[5], (N_EVEN, DEC_BATCH, PAST_LEN, H_B, DH), 1.0),
        'cache_v_b': nrm(ks[6], (N_EVEN, DEC_BATCH, PAST_LEN, H_B, DH), 1.0),
        'cache_ckv': nrm(ks[7], (N_ODD, DEC_BATCH, PAST_LEN, KV_RANK), 1.0),
        'cache_krope': nrm(ks[8], (N_ODD, DEC_BATCH, PAST_LEN, ROPE_C), 1.0),
        'state_conv': nrm(ks[9], (DEPTH, DEC_BATCH, CONV_W - 1, 2 * D_FF), 1.0),
        'norm_mix': 1.0 + nrm(ks[10], (DEPTH, D_MODEL), 0.01),
        'norm_ffn': 1.0 + nrm(ks[11], (DEPTH, D_MODEL), 0.01),
        'norm_final': 1.0 + nrm(ks[12], (D_MODEL,), 0.01),
        'w_in_ab': nrm(ks[13], (N_EVEN, D_MODEL, AB_COLS), D_MODEL ** -0.5),
        'w_out_ab': nrm(ks[14], (N_EVEN, MIX_AB, D_MODEL), MIX_AB ** -0.5),
        'w_dqkv': nrm(ks[15], (N_ODD, D_MODEL, DQKV_COLS), D_MODEL ** -0.5),
        'g_q': 1.0 + nrm(ks[16], (N_ODD, Q_RANK), 0.01),
        'g_kv': 1.0 + nrm(ks[17], (N_ODD, KV_RANK), 0.01),
        'w_uq': nrm(ks[18], (N_ODD, Q_RANK, H_C * (NOPE + ROPE_C)), Q_RANK ** -0.5),
        'w_ukv': nrm(ks[19], (N_ODD, KV_RANK, H_C * (NOPE + VD)), KV_RANK ** -0.5),
        'w_o_mla': nrm(ks[20], (N_ODD, H_C * VD, D_MODEL), (H_C * VD) ** -0.5),
        'w_up': nrm(ks[21], (DEPTH, D_MODEL, 2 * D_FF), D_MODEL ** -0.5),
        'conv_w': nrm(ks[22], (DEPTH, CONV_W, 2 * D_FF), CONV_W ** -0.5),
        'conv_b': nrm(ks[23], (DEPTH, 2 * D_FF), 0.01),
        'w_down': nrm(ks[24], (DEPTH, D_FF, D_MODEL), D_FF ** -0.5),
    }


def reference(x_prompt, x_sample, cache_k_a, cache_v_a, cache_idx_k, cache_k_b, cache_v_b, cache_ckv,
              cache_krope, state_conv, norm_mix, norm_ffn, norm_final, w_in_ab, w_out_ab, w_dqkv, g_q, g_kv,
              w_uq, w_ukv, w_o_mla, w_up, conv_w, conv_b, w_down):
    y_prompt, st_p = _trunk(x_prompt, None, None, norm_mix, norm_ffn, norm_final, w_in_ab, w_out_ab, w_dqkv,
                            g_q, g_kv, w_uq, w_ukv, w_o_mla, w_up, conv_w, conv_b, w_down)
    past = (cache_k_a, cache_v_a, cache_idx_k, cache_k_b, cache_v_b, cache_ckv, cache_krope)
    y_sample, st_s = _trunk(x_sample, past, state_conv, norm_mix, norm_ffn, norm_final, w_in_ab, w_out_ab,
                            w_dqkv, g_q, g_kv, w_uq, w_ukv, w_o_mla, w_up, conv_w, conv_b, w_down)
    p_k_a, p_v_a, p_idx_k, p_k_b, p_v_b, p_ckv, p_krope, p_conv = st_p
    s_k_a, s_v_a, s_idx_k, s_k_b, s_v_b, s_ckv, s_krope, s_conv = st_s
    return (y_prompt, y_sample, p_k_a, p_v_a, p_idx_k, p_k_b, p_v_b, p_ckv, p_krope, p_conv,
            s_k_a, s_v_a, s_idx_k, s_k_b, s_v_b, s_ckv, s_krope, s_conv)
```

```python
import functools

import numpy as np
import jax
import jax.numpy as jnp
from jax import lax
from jax.experimental import pallas as pl
from jax.experimental.pallas import tpu as pltpu

F32 = jnp.float32
BF16 = jnp.bfloat16

D_MODEL = 2048
DEPTH = 4
CHUNK = 64
ROPE_THETA = 500000.0
EPS = 1e-6
H_A, KV_A, DH = 8, 2, 128
ROT_A = DH // 4
H_I, D_IDX = 16, 64
ROT_I = D_IDX // 4
TOPK_MAX = 256
H_B = 8
H_C, Q_RANK, KV_RANK, NOPE, ROPE_C, VD = 16, 512, 512, 128, 64, 128
D_FF = 5632
CONV_W = 3

LANES = 128
SUBLANES = 8
VMEM_LIMIT = 56 * 1024 * 1024
NEG = -1e30
INT_MIN = np.int32(-2 ** 31)
INT_MAX = np.int32(2 ** 31 - 1)


def _cparams(sem):
    return pltpu.CompilerParams(dimension_semantics=sem, vmem_limit_bytes=VMEM_LIMIT)


def _dot_nt(a, b):
    return lax.dot_general(a, b, (((1,), (1,)), ((), ())), preferred_element_type=F32)


def _rms_body(x_ref, g_ref, o_ref):
    x = x_ref[...]
    y = x * lax.rsqrt(jnp.mean(x * x, axis=-1, keepdims=True) + EPS)
    o_ref[...] = (y * g_ref[...]).astype(o_ref.dtype)


def _rmsnorm(x, g, out_dtype):
    m, d = x.shape
    tm = min(m, 512)
    return pl.pallas_call(
        _rms_body,
        grid=(m // tm,),
        in_specs=[pl.BlockSpec((tm, d), lambda i: (i, 0)),
                  pl.BlockSpec((1, d), lambda i: (0, 0))],
        out_specs=pl.BlockSpec((tm, d), lambda i: (i, 0)),
        out_shape=jax.ShapeDtypeStruct((m, d), out_dtype),
        compiler_params=_cparams(("parallel",)),
        name="rmsnorm",
    )(x, g.reshape(1, d))


def _mm_body(*refs, tn, rope_half, post_norm, has_resid, n_out, scale):
    it = iter(refs)
    x_ref, w_ref = next(it), next(it)
    if rope_half:
        c_ref, s1_ref, s2_ref = next(it), next(it), next(it)
    if post_norm:
        g_ref = next(it)
    if has_resid:
        r_ref = next(it)
    outs = [next(it) for _ in range(n_out)]

    acc = jnp.dot(x_ref[...], w_ref[...], preferred_element_type=F32)
    if post_norm:
        acc = acc * lax.rsqrt(jnp.mean(acc * acc, axis=-1, keepdims=True) + EPS) * g_ref[...]
    if rope_half:
        c, s1, s2 = c_ref[...], s1_ref[...], s2_ref[...]
        parts = []
        for gi in range(tn // LANES):
            xg = acc[:, gi * LANES:(gi + 1) * LANES]
            parts.append(xg * c + pltpu.roll(xg, rope_half, 1) * s1
                         + pltpu.roll(xg, LANES - rope_half, 1) * s2)
        acc = parts[0] if len(parts) == 1 else jnp.concatenate(parts, axis=1)
    if scale is not None:
        acc = acc * scale
    if has_resid:
        acc = acc + r_ref[...]
    for o in outs:
        o[...] = acc.astype(o.dtype)


def _mm(x, w, *, out_dtypes, rope=None, post_norm=None, resid=None, scale=None, tn=None, name="mm"):
    m, k = x.shape
    n = w.shape[1]
    tm = min(m, 512)
    if tn is None:
        tn = n if n <= 512 else 512
    assert m % tm == 0 and n % tn == 0
    in_specs = [pl.BlockSpec((tm, k), lambda i, j: (i, 0)),
                pl.BlockSpec((k, tn), lambda i, j: (0, j))]
    args = [x, w]
    rope_half = 0
    if rope is not None:
        c, s1, s2, rope_half = rope
        for t in (c, s1, s2):
            in_specs.append(pl.BlockSpec((tm, LANES), lambda i, j: (i, 0)))
            args.append(t)
    if post_norm is not None:
        in_specs.append(pl.BlockSpec((1, tn), lambda i, j: (0, j)))
        args.append(post_norm.reshape(1, n))
    if resid is not None:
        in_specs.append(pl.BlockSpec((tm, tn), lambda i, j: (i, j)))
        args.append(resid)
    body = functools.partial(_mm_body, tn=tn, rope_half=rope_half, post_norm=post_norm is not None,
                             has_resid=resid is not None, n_out=len(out_dtypes), scale=scale)
    outs = pl.pallas_call(
        body,
        grid=(m // tm, n // tn),
        in_specs=in_specs,
        out_specs=[pl.BlockSpec((tm, tn), lambda i, j: (i, j)) for _ in out_dtypes],
        out_shape=[jax.ShapeDtypeStruct((m, n), dt) for dt in out_dtypes],
        compiler_params=_cparams(("parallel", "parallel")),
        name=name,
    )(*args)
    return outs


def _dsa_body(qa_ref, qi_ref, wi_ref, ka_ref, va_ref, ki_ref, tri_ref, o_ref,
              key_scr, qs_scr, m_scr, l_scr, acc_scr, *, tq, tk, s_valid, q0, topk):
    i = pl.program_id(1)
    q_first = q0 + i * tq
    q_last = q_first + tq - 1
    adm_end = jnp.minimum((q_last // CHUNK + 1) * CHUNK, s_valid)
    nkb = (adm_end + tk - 1) // tk
    q_pos = q_first + lax.broadcasted_iota(jnp.int32, (tq, tk), 0)
    q_lim = jnp.minimum(((q_pos >> 6) + 1) * CHUNK, s_valid)
    col = lax.broadcasted_iota(jnp.int32, (tq, tk), 1)
    rep = H_A // KV_A

    w = wi_ref[0] * ((H_I * D_IDX) ** -0.5)
    hg = 4
    q_groups = [qi_ref[0, g * hg:(g + 1) * hg].reshape(hg * tq, D_IDX) for g in range(H_I // hg)]

    def p1(j, c):
        k0 = pl.multiple_of(j * tk, tk)
        kblk = ki_ref[0, pl.ds(k0, tk), :]
        sc = jnp.zeros((tq, tk), F32)
        for g in range(H_I // hg):
            rel = _dot_nt(q_groups[g], kblk)
            for hh in range(hg):
                h = g * hg + hh
                sc = sc + jnp.maximum(rel[hh * tq:(hh + 1) * tq], 0.0) * w[:, h:h + 1]
        bits = pltpu.bitcast(sc, jnp.int32)
        key = bits ^ ((bits >> 31) & INT_MAX)
        key = jnp.where(k0 + col < q_lim, key, INT_MIN)
        key_scr[:, pl.ds(k0, tk)] = key
        return c

    lax.fori_loop(0, nkb, p1, 0)

    def count_ge(v):
        def cb(j, c):
            k0 = pl.multiple_of(j * tk, tk)
            ge = jnp.where(key_scr[:, pl.ds(k0, tk)] >= v, 1.0, 0.0)
            part = ge[:, 0:LANES]
            for cc in range(1, tk // LANES):
                part = part + ge[:, cc * LANES:(cc + 1) * LANES]
            return c + part
        c = lax.fori_loop(0, nkb, cb, jnp.zeros((tq, LANES), F32))
        return jnp.sum(c, axis=1, keepdims=True)

    def bis(_, lohi):
        lo, hi = lohi
        mid = (lo >> 1) + (hi >> 1) + (lo & hi & 1)
        ok = count_ge(mid) >= float(topk)
        return jnp.where(ok, mid, lo), jnp.where(ok, hi, mid)

    thr, _ = lax.fori_loop(0, 32, bis, (jnp.full((tq, 1), INT_MIN, jnp.int32),
                                        jnp.full((tq, 1), INT_MAX, jnp.int32)))
    need = float(topk) - count_ge(thr + 1)

    qa = qa_ref[0]
    for g in range(KV_A):
        qs_scr[g] = jnp.concatenate(
            [qa[:, (g * rep + r) * DH:(g * rep + r + 1) * DH] for r in range(rep)], axis=0)
    m_scr[...] = jnp.full(m_scr.shape, NEG, F32)
    l_scr[...] = jnp.zeros(l_scr.shape, F32)
    acc_scr[...] = jnp.zeros(acc_scr.shape, F32)

    def p3(j, eqc):
        k0 = pl.multiple_of(j * tk, tk)
        key = key_scr[:, pl.ds(k0, tk)]
        eq = key == thr
        eqf = jnp.where(eq, 1.0, 0.0)
        prefix = jnp.dot(eqf.astype(BF16), tri_ref[...], preferred_element_type=F32) + eqc
        bias = jnp.where(key > thr, 0.0, jnp.where(eq, jnp.where(prefix < need, 0.0, NEG), NEG))
        bias = jnp.where(key == INT_MIN, NEG, bias)
        eqc = eqc + jnp.sum(eqf, axis=1, keepdims=True)
        for g in range(KV_A):
            kg = ka_ref[0, pl.ds(k0, tk), g * DH:(g + 1) * DH]
            vg = va_ref[0, pl.ds(k0, tk), g * DH:(g + 1) * DH]
            s = _dot_nt(qs_scr[g], kg)
            for r in range(rep):
                h = g * rep + r
                sr = s[r * tq:(r + 1) * tq] + bias
                m_prev = m_scr[h]
                m_new = jnp.maximum(m_prev, jnp.max(sr, axis=1, keepdims=True))
                alpha = jnp.exp(m_prev - m_new)
                p = jnp.exp(sr - m_new[:, 0:1])
                l_scr[h] = alpha * l_scr[h] + jnp.sum(p, axis=1, keepdims=True)
                acc_scr[h] = acc_scr[h] * alpha + jnp.dot(p.astype(BF16), vg,
                                                          preferred_element_type=F32)
                m_scr[h] = m_new
        return eqc

    lax.fori_loop(0, nkb, p3, jnp.zeros((tq, 1), F32))
    for h in range(H_A):
        o_ref[0, :, h * DH:(h + 1) * DH] = (acc_scr[h] / l_scr[h]).astype(o_ref.dtype)


def _dsa(qa, qi_hm, wi, ka, va, ki, *, tq, tk, s_valid, q0):
    b, t, _ = qa.shape
    s_pad = ka.shape[1]
    assert s_pad % tk == 0 and t % tq == 0 and tk >= TOPK_MAX
    topk = min(TOPK_MAX, s_valid // 4)
    tri = jnp.asarray(np.triu(np.ones((tk, tk), np.float32), 1), BF16)
    body = functools.partial(_dsa_body, tq=tq, tk=tk, s_valid=s_valid, q0=q0, topk=topk)
    rep = H_A // KV_A
    return pl.pallas_call(
        body,
        grid=(b, t // tq),
        in_specs=[pl.BlockSpec((1, tq, H_A * DH), lambda bi, i: (bi, i, 0)),
                  pl.BlockSpec((1, H_I, tq, D_IDX), lambda bi, i: (bi, 0, i, 0)),
                  pl.BlockSpec((1, tq, H_I), lambda bi, i: (bi, i, 0)),
                  pl.BlockSpec((1, s_pad, KV_A * DH), lambda bi, i: (bi, 0, 0)),
                  pl.BlockSpec((1, s_pad, KV_A * DH), lambda bi, i: (bi, 0, 0)),
                  pl.BlockSpec((1, s_pad, D_IDX), lambda bi, i: (bi, 0, 0)),
                  pl.BlockSpec((tk, tk), lambda bi, i: (0, 0))],
        out_specs=pl.BlockSpec((1, tq, H_A * DH), lambda bi, i: (bi, i, 0)),
        out_shape=jax.ShapeDtypeStruct((b, t, H_A * DH), BF16),
        scratch_shapes=[pltpu.VMEM((tq, s_pad), jnp.int32),
                        pltpu.VMEM((KV_A, rep * tq, DH), BF16),
                        pltpu.VMEM((H_A, tq, LANES), F32),
                        pltpu.VMEM((H_A, tq, LANES), F32),
                        pltpu.VMEM((H_A, tq, DH), F32)],
        compiler_params=_cparams(("parallel", "arbitrary")),
        name="dsa",
    )(qa, qi_hm, wi, ka, va, ki, tri)


def _sb_body(q_ref, k_ref, v_ref, low_ref, o_ref, *, tq, tk, s_pad, q0, hpg, nchunk):
    i = pl.program_id(2)
    tqc = tq // nchunk
    q_first = q0 + i * tq
    q_last = q_first + tq - 1
    nkb = jnp.minimum((q_last + tk - 1) // tk, s_pad // tk)
    n_full = jnp.minimum(q_first // tk, nkb)
    row = lax.broadcasted_iota(jnp.int32, (tqc, tk), 0)
    col = lax.broadcasted_iota(jnp.int32, (tqc, tk), 1)
    chains = [(g, c) for g in range(hpg) for c in range(nchunk)]

    def step(j, carry, masked):
        k0 = pl.multiple_of(j * tk, tk)
        zs = [_dot_nt(q_ref[0, c * tqc:(c + 1) * tqc, g * DH:(g + 1) * DH],
                      k_ref[0, pl.ds(k0, tk), g * DH:(g + 1) * DH]) for g, c in chains]
        lbs, lks, cats, causals = [], [], [], []
        for (g, c), z in zip(chains, zs):
            lb = jnp.minimum(z, 0.0) - jnp.log(1.0 + jnp.exp(-jnp.abs(z)))
            lk = lb - z
            causal = None
            if masked:
                causal = k0 + col < q_first + c * tqc + row
                lk = jnp.where(causal, lk, 0.0)
            hi = lk.astype(BF16)
            lo = (lk - hi.astype(F32)).astype(BF16)
            lbs.append(lb)
            lks.append(lk)
            causals.append(causal)
            cats.append(jnp.concatenate([hi, lo], axis=1))
        sufs = [jnp.dot(cat, low_ref[...], preferred_element_type=F32) for cat in cats]
        out = []
        for n, (g, c) in enumerate(chains):
            run, acc = carry[2 * n], carry[2 * n + 1]
            wgt = jnp.exp(lbs[n] + sufs[n] + run)
            if masked:
                wgt = jnp.where(causals[n], wgt, 0.0)
            acc = acc + jnp.dot(wgt.astype(BF16), v_ref[0, pl.ds(k0, tk), g * DH:(g + 1) * DH],
                                preferred_element_type=F32)
            run = run + jnp.sum(lks[n], axis=1, keepdims=True)
            out += [run, acc]
        return tuple(out)

    carry = []
    for _ in chains:
        carry += [jnp.zeros((tqc, 1), F32), jnp.zeros((tqc, DH), F32)]
    carry = tuple(carry)
    carry = lax.fori_loop(0, nkb - n_full, lambda s, c: step(nkb - 1 - s, c, True), carry)
    carry = lax.fori_loop(0, n_full, lambda s, c: step(n_full - 1 - s, c, False), carry)
    for n, (g, c) in enumerate(chains):
        o_ref[0, c * tqc:(c + 1) * tqc, g * DH:(g + 1) * DH] = carry[2 * n + 1].astype(o_ref.dtype)


def _stick_breaking(qb, kb, vb, *, tq, tk, q0, hpg, nchunk):
    b, t, _ = qb.shape
    s_pad = kb.shape[1]
    assert s_pad % tk == 0 and t % tq == 0 and H_B % hpg == 0 and tq % nchunk == 0
    low = np.tril(np.ones((tk, tk), np.float32), -1)
    low = jnp.asarray(np.concatenate([low, low], axis=0), BF16)
    body = functools.partial(_sb_body, tq=tq, tk=tk, s_pad=s_pad, q0=q0, hpg=hpg, nchunk=nchunk)
    wd = hpg * DH
    return pl.pallas_call(
        body,
        grid=(b, H_B // hpg, t // tq),
        in_specs=[pl.BlockSpec((1, tq, wd), lambda bi, h, i: (bi, i, h)),
                  pl.BlockSpec((1, s_pad, wd), lambda bi, h, i: (bi, 0, h)),
                  pl.BlockSpec((1, s_pad, wd), lambda bi, h, i: (bi, 0, h)),
                  pl.BlockSpec((2 * tk, tk), lambda bi, h, i: (0, 0))],
        out_specs=pl.BlockSpec((1, tq, wd), lambda bi, h, i: (bi, i, h)),
        out_shape=jax.ShapeDtypeStruct((b, t, H_B * DH), BF16),
        compiler_params=_cparams(("parallel", "parallel", "arbitrary")),
        name="stick_breaking",
    )(qb, kb, vb, low)


def _mla_body(qn_ref, qr_ref, kv_ref, kr_ref, o_ref, *, tq, tk, s_valid, q0, hpg, nchunk):
    i = pl.program_id(2)
    tqc = tq // nchunk
    q_first = q0 + i * tq
    q_last = q_first + tq - 1
    lim_first = jnp.minimum((q_first // CHUNK + 1) * CHUNK, s_valid)
    lim_last = jnp.minimum((q_last // CHUNK + 1) * CHUNK, s_valid)
    nkb = (lim_last + tk - 1) // tk
    n_full = lim_first // tk
    row = lax.broadcasted_iota(jnp.int32, (tqc, tk), 0)
    col = lax.broadcasted_iota(jnp.int32, (tqc, tk), 1)
    chains = [(g, c) for g in range(hpg) for c in range(nchunk)]
    hw = NOPE + VD

    def step(j, carry, masked):
        k0 = pl.multiple_of(j * tk, tk)
        kr = kr_ref[0, pl.ds(k0, tk), :]
        ss = []
        for g, c in chains:
            rows = slice(c * tqc, (c + 1) * tqc)
            qc = jnp.concatenate([qn_ref[0, rows, g * NOPE:(g + 1) * NOPE],
                                  qr_ref[0, rows, g * LANES:(g + 1) * LANES]], axis=1)
            kc = jnp.concatenate([kv_ref[0, pl.ds(k0, tk), g * hw:g * hw + NOPE], kr], axis=1)
            ss.append(_dot_nt(qc, kc))
        ps, stats = [], []
        for n, (g, c) in enumerate(chains):
            m_prev, l_prev = carry[3 * n], carry[3 * n + 1]
            s = ss[n]
            if masked:
                lim = jnp.minimum((((q_first + c * tqc + row) >> 6) + 1) * CHUNK, s_valid)
                s = jnp.where(k0 + col < lim, s, NEG)
            m_new = jnp.maximum(m_prev, jnp.max(s, axis=1, keepdims=True))
            alpha = jnp.exp(m_prev - m_new)
            p = jnp.exp(s - m_new)
            stats.append((m_new, alpha, alpha * l_prev + jnp.sum(p, axis=1, keepdims=True)))
            ps.append(p.astype(BF16))
        out = []
        for n, (g, c) in enumerate(chains):
            m_new, alpha, l_new = stats[n]
            vb = kv_ref[0, pl.ds(k0, tk), g * hw + NOPE:(g + 1) * hw]
            acc = carry[3 * n + 2] * alpha + jnp.dot(ps[n], vb, preferred_element_type=F32)
            out += [m_new, l_new, acc]
        return tuple(out)

    carry = []
    for _ in chains:
        carry += [jnp.full((tqc, 1), NEG, F32), jnp.zeros((tqc, 1), F32), jnp.zeros((tqc, VD), F32)]
    carry = tuple(carry)
    carry = lax.fori_loop(0, n_full, lambda j, c: step(j, c, False), carry)
    carry = lax.fori_loop(n_full, nkb, lambda j, c: step(j, c, True), carry)
    for n, (g, c) in enumerate(chains):
        o_ref[0, c * tqc:(c + 1) * tqc, g * VD:(g + 1) * VD] = (
            carry[3 * n + 2] / carry[3 * n + 1]).astype(o_ref.dtype)


def _mla(qn, qr, kv, kr, *, tq, tk, s_valid, q0, hpg, nchunk):
    b, t, _ = qn.shape
    s_pad = kv.shape[1]
    assert s_pad % tk == 0 and t % tq == 0 and H_C % hpg == 0 and tq % nchunk == 0
    body = functools.partial(_mla_body, tq=tq, tk=tk, s_valid=s_valid, q0=q0, hpg=hpg, nchunk=nchunk)
    return pl.pallas_call(
        body,
        grid=(b, H_C // hpg, t // tq),
        in_specs=[pl.BlockSpec((1, tq, hpg * NOPE), lambda bi, h, i: (bi, i, h)),
                  pl.BlockSpec((1, tq, hpg * LANES), lambda bi, h, i: (bi, i, h)),
                  pl.BlockSpec((1, s_pad, hpg * (NOPE + VD)), lambda bi, h, i: (bi, 0, h)),
                  pl.BlockSpec((1, s_pad, LANES), lambda bi, h, i: (bi, 0, 0))],
        out_specs=pl.BlockSpec((1, tq, hpg * VD), lambda bi, h, i: (bi, i, h)),
        out_shape=jax.ShapeDtypeStruct((b, t, H_C * VD), BF16),
        compiler_params=_cparams(("parallel", "parallel", "arbitrary")),
        name="mla",
    )(qn, qr, kv, kr)


def _ffn_body(x_ref, g_ref, wg_ref, wv_ref, cwg_ref, cwv_ref, cbg_ref, cbv_ref, wd_ref, pg_ref, pv_ref,
              o_ref, sg_ref, sv_ref, h_scr, acc_scr, ug_buf, uv_buf, cg_scr, cv_scr, *, seg, carried):
    i = pl.program_id(0)
    f = pl.program_id(1)
    tm = x_ref.shape[0]

    @pl.when(f == 0)
    def _():
        x = x_ref[...]
        y = x * lax.rsqrt(jnp.mean(x * x, axis=-1, keepdims=True) + EPS)
        h_scr[...] = (y * g_ref[...]).astype(BF16)
        acc_scr[...] = x

    h = h_scr[...]
    halves = ((wg_ref, cwg_ref, cbg_ref, pg_ref, sg_ref, ug_buf, cg_scr),
              (wv_ref, cwv_ref, cbv_ref, pv_ref, sv_ref, uv_buf, cv_scr))
    ys = []
    for w_ref, cw_ref, cb_ref, p_ref, s_ref, buf, c_scr in halves:
        u = jnp.dot(h, w_ref[...], preferred_element_type=F32)
        cw = cw_ref[...]
        parts = []
        for sgi in range(tm // seg):
            us = u[sgi * seg:(sgi + 1) * seg]
            if carried:
                head = jnp.where(i == 0, p_ref[0], c_scr[f])
            else:
                head = p_ref[sgi]
            buf[0:SUBLANES] = head
            buf[SUBLANES:SUBLANES + seg] = us
            y = (cw[2:3] * us + cw[1:2] * buf[SUBLANES - 1:SUBLANES - 1 + seg]
                 + cw[0:1] * buf[SUBLANES - 2:SUBLANES - 2 + seg] + cb_ref[...])
            parts.append(y)
            last = us[seg - SUBLANES:seg]
            s_ref[sgi] = last
            if carried:
                c_scr[f] = last
        ys.append(parts[0] if len(parts) == 1 else jnp.concatenate(parts, axis=0))
    yg, yv = ys
    act = (yg / (1.0 + jnp.exp(-yg))) * yv
    acc_scr[...] += jnp.dot(act.astype(BF16), wd_ref[...], preferred_element_type=F32)

    @pl.when(f == pl.num_programs(1) - 1)
    def _():
        o_ref[...] = acc_scr[...]


def _ffn(x, g, w_up, conv_w, conv_b, w_down, prev8, *, seg, carried, tm, tf):
    m, d = x.shape
    nf = D_FF // tf
    nseg = tm // seg
    assert m % tm == 0 and D_FF % tf == 0 and tm % seg == 0
    sidx = (lambda i: 0) if carried else (lambda i: i)
    body = functools.partial(_ffn_body, seg=seg, carried=carried)
    cb = conv_b.reshape(1, 2 * D_FF)
    return pl.pallas_call(
        body,
        grid=(m // tm, nf),
        in_specs=[pl.BlockSpec((tm, d), lambda i, f: (i, 0)),
                  pl.BlockSpec((1, d), lambda i, f: (0, 0)),
                  pl.BlockSpec((d, tf), lambda i, f: (0, f)),
                  pl.BlockSpec((d, tf), lambda i, f: (0, nf + f)),
                  pl.BlockSpec((CONV_W, tf), lambda i, f: (0, f)),
                  pl.BlockSpec((CONV_W, tf), lambda i, f: (0, nf + f)),
                  pl.BlockSpec((1, tf), lambda i, f: (0, f)),
                  pl.BlockSpec((1, tf), lambda i, f: (0, nf + f)),
                  pl.BlockSpec((tf, d), lambda i, f: (f, 0)),
                  pl.BlockSpec((nseg, SUBLANES, tf), lambda i, f: (sidx(i), 0, f)),
                  pl.BlockSpec((nseg, SUBLANES, tf), lambda i, f: (sidx(i), 0, nf + f))],
        out_specs=[pl.BlockSpec((tm, d), lambda i, f: (i, 0)),
                   pl.BlockSpec((nseg, SUBLANES, tf), lambda i, f: (i, 0, f)),
                   pl.BlockSpec((nseg, SUBLANES, tf), lambda i, f: (i, 0, f))],
        out_shape=[jax.ShapeDtypeStruct((m, d), F32),
                   jax.ShapeDtypeStruct((m // seg, SUBLANES, D_FF), F32),
                   jax.ShapeDtypeStruct((m // seg, SUBLANES, D_FF), F32)],
        scratch_shapes=[pltpu.VMEM((tm, d), BF16),
                        pltpu.VMEM((tm, d), F32),
                        pltpu.VMEM((SUBLANES + seg, tf), F32),
                        pltpu.VMEM((SUBLANES + seg, tf), F32),
                        pltpu.VMEM((nf, SUBLANES, tf), F32),
                        pltpu.VMEM((nf, SUBLANES, tf), F32)],
        compiler_params=_cparams(("arbitrary", "arbitrary")),
        name="conv_ffn",
    )(x, g.reshape(1, d), w_up, w_up, conv_w, conv_w, cb, cb, w_down, prev8, prev8)


def _rope_tables(pos, head_dim, rot, lanes_valid=LANES):
    half = rot // 2
    freqs = ROPE_THETA ** (-jnp.arange(half, dtype=F32) / half)
    ang = pos.astype(F32)[:, None] * freqs[None, :]
    cos, sin = jnp.cos(ang), jnp.sin(ang)
    lane = np.arange(LANES)
    d = lane % head_dim
    first = (d < half) & (lane < lanes_valid)
    second = (d >= half) & (d < rot) & (lane < lanes_valid)
    fidx = np.clip(np.where(d < half, d, d - half), 0, half - 1)
    cosl, sinl = cos[:, fidx], sin[:, fidx]
    c = jnp.where(first | second, cosl, 1.0)
    s1 = jnp.where(second, sinl, 0.0)
    s2 = jnp.where(first, -sinl, 0.0)
    return c, s1, s2, half


def _prep_weights(w_in_ab, w_out_ab, w_dqkv, w_uq, w_ukv, w_o_mla, w_up, w_down):
    n_even, n_odd = w_in_ab.shape[0], w_dqkv.shape[0]
    sizes = (H_A * DH, KV_A * DH, KV_A * DH, H_I * D_IDX, D_IDX, H_I, H_B * DH, H_B * DH, H_B * DH)
    offs = np.concatenate([[0], np.cumsum(sizes)])
    names = ("qa", "ka", "va", "qi", "ki", "wi", "qb", "kb", "vb")
    w = {}
    for nm, o0, o1 in zip(names, offs[:-1], offs[1:]):
        w[nm] = w_in_ab[:, :, o0:o1].astype(BF16)
    pad = jnp.zeros((n_even, D_MODEL, LANES - D_IDX - H_I), BF16)
    w["kiwi"] = jnp.concatenate([w["ki"], w["wi"], pad], axis=-1)
    w["out_ab"] = w_out_ab.astype(BF16)
    w["dq"] = w_dqkv[:, :, :Q_RANK + KV_RANK].astype(BF16)
    w["kr"] = jnp.concatenate([w_dqkv[:, :, Q_RANK + KV_RANK:].astype(BF16),
                               jnp.zeros((n_odd, D_MODEL, LANES - ROPE_C), BF16)], axis=-1)
    uq = w_uq.reshape(n_odd, Q_RANK, H_C, NOPE + ROPE_C)
    w["uqn"] = uq[..., :NOPE].reshape(n_odd, Q_RANK, H_C * NOPE).astype(BF16)
    uqr = uq[..., NOPE:].astype(BF16)
    w["uqr"] = jnp.concatenate([uqr, jnp.zeros_like(uqr)], axis=-1).reshape(n_odd, Q_RANK, H_C * LANES)
    w["ukv"] = w_ukv.astype(BF16)
    w["o_mla"] = w_o_mla.astype(BF16)
    w["up"] = w_up.astype(BF16)
    w["down"] = w_down.astype(BF16)
    return w


def _with_past(past, new, s_pad):
    b, _, fdim = new.shape
    parts = [new.astype(BF16)] if past is None else [past.astype(BF16), new.astype(BF16)]
    n = sum(p.shape[1] for p in parts)
    if s_pad > n:
        parts.append(jnp.zeros((b, s_pad - n, fdim), BF16))
    return parts[0] if len(parts) == 1 else jnp.concatenate(parts, axis=1)


def _trunk(x, past, conv_state, w, p, cfg):
    b, t, d = x.shape
    m = b * t
    q0 = 0 if past is None else past[0].shape[2]
    s_valid = q0 + t
    tk = cfg["tk"]
    s_pad = -(-s_valid // tk) * tk
    pos = q0 + jnp.tile(jnp.arange(t), b)
    tab_a = _rope_tables(pos, DH, ROT_A)
    tab_i = _rope_tables(pos, D_IDX, ROT_I)
    tab_ki = _rope_tables(pos, D_IDX, ROT_I, lanes_valid=D_IDX)
    tab_kr = _rope_tables(pos, ROPE_C, ROPE_C, lanes_valid=ROPE_C)

    xf = x.reshape(m, d)
    ab_rows, mla_rows, conv_rows = [], [], []
    for l in range(DEPTH):
        h = _rmsnorm(xf, p["norm_mix"][l], BF16)
        if l % 2 == 0:
            e = l // 2
            lp = None if past is None else tuple(c[e] for c in past[:5])
            (qa,) = _mm(h, w["qa"][e], out_dtypes=(BF16,), rope=tab_a, scale=DH ** -0.5)
            ka32, ka16 = _mm(h, w["ka"][e], out_dtypes=(F32, BF16), rope=tab_a)
            va32, va16 = _mm(h, w["va"][e], out_dtypes=(F32, BF16))
            (qi,) = _mm(h, w["qi"][e], out_dtypes=(BF16,), rope=tab_i)
            (kiwi,) = _mm(h, w["kiwi"][e], out_dtypes=(F32,), rope=tab_ki)
            (qb,) = _mm(h, w["qb"][e], out_dtypes=(BF16,), scale=DH ** -0.5)
            kb32, kb16 = _mm(h, w["kb"][e], out_dtypes=(F32, BF16))
            vb32, vb16 = _mm(h, w["vb"][e], out_dtypes=(F32, BF16))
            ki32 = kiwi[:, :D_IDX]
            wi = kiwi[:, D_IDX:D_IDX + H_I]
            ab_rows.append((ka32.reshape(b, t, KV_A, DH), va32.reshape(b, t, KV_A, DH),
                            ki32.reshape(b, t, D_IDX), kb32.reshape(b, t, H_B, DH),
                            vb32.reshape(b, t, H_B, DH)))

            def full(idx, new16):
                pst = None if lp is None else lp[idx].reshape(b, q0, -1)
                return _with_past(pst, new16.reshape(b, t, -1), s_pad)

            qi_hm = qi.reshape(b, t, H_I, D_IDX).transpose(0, 2, 1, 3)
            oa = _dsa(qa.reshape(b, t, -1), qi_hm, wi.reshape(b, t, H_I), full(0, ka16), full(1, va16),
                      full(2, ki32), tq=cfg["tq_a"], tk=tk, s_valid=s_valid, q0=q0)
            ob = _stick_breaking(qb.reshape(b, t, -1), full(3, kb16), full(4, vb16),
                                 tq=cfg["tq_b"], tk=cfg["tk_b"], q0=q0, hpg=cfg["hpg_b"], nchunk=cfg["nc_b"])
            o = jnp.concatenate([oa, ob], axis=-1).reshape(m, -1)
            (xf,) = _mm(o, w["out_ab"][e], out_dtypes=(F32,), resid=xf)
        else:
            od = l // 2
            gains = jnp.concatenate([p["g_q"][od], p["g_kv"][od]])
            cc32, cc16 = _mm(h, w["dq"][od], out_dtypes=(F32, BF16), post_norm=gains, tn=Q_RANK)
            (kr128,) = _mm(h, w["kr"][od], out_dtypes=(F32,), rope=tab_kr)
            cq16 = cc16[:, :Q_RANK]
            ckv32, ckv16 = cc32[:, Q_RANK:], cc16[:, Q_RANK:]
            kr32 = kr128[:, :ROPE_C]
            mla_rows.append((ckv32.reshape(b, t, KV_RANK), kr32.reshape(b, t, ROPE_C)))
            sc = (NOPE + ROPE_C) ** -0.5
            (qn,) = _mm(cq16, w["uqn"][od], out_dtypes=(BF16,), scale=sc)
            (qr,) = _mm(cq16, w["uqr"][od], out_dtypes=(BF16,), rope=tab_kr, scale=sc)
            pc = None if past is None else past[5][od]
            pk = None if past is None else jnp.pad(past[6][od], ((0, 0), (0, 0), (0, LANES - ROPE_C)))
            c_all = _with_past(pc, ckv16.reshape(b, t, KV_RANK), s_pad)
            kr_all = _with_past(pk, kr128.reshape(b, t, LANES), s_pad)
            (kv,) = _mm(c_all.reshape(b * s_pad, KV_RANK), w["ukv"][od], out_dtypes=(BF16,))
            o = _mla(qn.reshape(b, t, -1), qr.reshape(b, t, -1), kv.reshape(b, s_pad, -1), kr_all,
                     tq=cfg["tq_c"], tk=cfg["tk_c"], s_valid=s_valid, q0=q0, hpg=cfg["hpg_c"],
                     nchunk=cfg["nc_c"])
            (xf,) = _mm(o.reshape(m, -1), w["o_mla"][od], out_dtypes=(F32,), resid=xf)

        if conv_state is None:
            prev8 = jnp.zeros((b, SUBLANES, 2 * D_FF), F32)
        else:
            prev8 = jnp.concatenate([jnp.zeros((b, SUBLANES - (CONV_W - 1), 2 * D_FF), F32),
                                     conv_state[l]], axis=1)
        xf, sg, sv = _ffn(xf, p["norm_ffn"][l], w["up"][l], p["conv_w"][l], p["conv_b"][l], w["down"][l],
                          prev8, seg=cfg["seg"], carried=cfg["carried"], tm=cfg["tm_ffn"], tf=cfg["tf"])
        last = [s8.reshape(b, -1, SUBLANES, D_FF)[:, -1, SUBLANES - (CONV_W - 1):] for s8 in (sg, sv)]
        conv_rows.append(jnp.concatenate(last, axis=-1))

    y = _rmsnorm(xf, p["norm_final"], F32).reshape(b, t, d)
    new_ab = [jnp.stack([r[i] for r in ab_rows]) for i in range(5)]
    new_mla = [jnp.stack([r[i] for r in mla_rows]) for i in range(2)]
    return y, new_ab + new_mla + [jnp.stack(conv_rows)]


def kernel(x_prompt, x_sample, cache_k_a, cache_v_a, cache_idx_k, cache_k_b, cache_v_b, cache_ckv, cache_krope, state_conv, norm_mix, norm_ffn, norm_final, w_in_ab, w_out_ab, w_dqkv, g_q, g_kv, w_uq, w_ukv, w_o_mla, w_up, conv_w, conv_b, w_down):
    w = _prep_weights(w_in_ab, w_out_ab, w_dqkv, w_uq, w_ukv, w_o_mla, w_up, w_down)
    p = dict(norm_mix=norm_mix, norm_ffn=norm_ffn, norm_final=norm_final, g_q=g_q, g_kv=g_kv,
             conv_w=conv_w, conv_b=conv_b)
    cfg_p = dict(tk=512, tq_a=128, tq_b=256, tk_b=256, hpg_b=4, nc_b=1, tq_c=256, tk_c=512, hpg_c=4, nc_c=1,
                 seg=512, carried=True, tm_ffn=512, tf=512)
    t_s = x_sample.shape[1]
    cfg_s = dict(tk=256, tq_a=t_s, tq_b=t_s, tk_b=256, hpg_b=H_B, nc_b=1, tq_c=t_s, tk_c=256, hpg_c=8, nc_c=1,
                 seg=t_s, carried=False, tm_ffn=x_sample.shape[0] * t_s, tf=512)
    y_p, st_p = _trunk(x_prompt, None, None, w, p, cfg_p)
    past = (cache_k_a, cache_v_a, cache_idx_k, cache_k_b, cache_v_b, cache_ckv, cache_krope)
    y_s, st_s = _trunk(x_sample, past, state_conv, w, p, cfg_s)
    return (y_p, y_s, *st_p, *st_s)
```

```python
import functools

import numpy as np
import jax
import jax.numpy as jnp
from jax import lax
from jax.experimental import pallas as pl
from jax.experimental.pallas import tpu as pltpu

F32 = jnp.float32
BF16 = jnp.bfloat16

D_MODEL = 2048
DEPTH = 4
CHUNK = 64
ROPE_THETA = 500000.0
EPS = 1e-6
H_A, KV_A, DH = 8, 2, 128
ROT_A = DH // 4
H_I, D_IDX = 16, 64
ROT_I = D_IDX // 4
TOPK_MAX = 256
H_B = 8
H_C, Q_RANK, KV_RANK, NOPE, ROPE_C, VD = 16, 512, 512, 128, 64, 128
D_FF = 5632
CONV_W = 3

LANES = 128
SUBLANES = 8
VMEM_LIMIT = 56 * 1024 * 1024
NEG = -1e30
INT_MIN = np.int32(-2 ** 31)
INT_MAX = np.int32(2 ** 31 - 1)
LOG2E = float(np.log2(np.e))


def _cparams(sem):
    return pltpu.CompilerParams(dimension_semantics=sem, vmem_limit_bytes=VMEM_LIMIT)


def _dot_nt(a, b):
    return lax.dot_general(a, b, (((1,), (1,)), ((), ())), preferred_element_type=F32)


def _rms_body(x_ref, g_ref, o_ref):
    x = x_ref[...]
    y = x * lax.rsqrt(jnp.mean(x * x, axis=-1, keepdims=True) + EPS)
    o_ref[...] = (y * g_ref[...]).astype(o_ref.dtype)


def _rmsnorm(x, g, out_dtype):
    m, d = x.shape
    tm = min(m, 512)
    return pl.pallas_call(
        _rms_body,
        grid=(m // tm,),
        in_specs=[pl.BlockSpec((tm, d), lambda i: (i, 0)),
                  pl.BlockSpec((1, d), lambda i: (0, 0))],
        out_specs=pl.BlockSpec((tm, d), lambda i: (i, 0)),
        out_shape=jax.ShapeDtypeStruct((m, d), out_dtype),
        compiler_params=_cparams(("parallel",)),
        name="rmsnorm",
    )(x, g.reshape(1, d))


PROJ_CHUNK = 512


def _proj_body(*refs, nx, groups, tab_halves, has_gain, has_resid):
    it = iter(refs)
    x_refs = [next(it) for _ in range(nx)]
    w_ref = next(it)
    tabs = [(next(it), next(it), next(it)) for _ in tab_halves]
    g_ref = next(it) if has_gain else None
    r_ref = next(it) if has_resid else None
    outs = list(it)

    xs = [r[...] for r in x_refs]
    x = xs[0] if nx == 1 else jnp.concatenate(xs, axis=1)
    accs = []
    c0 = 0
    for grp in groups:
        width = grp["width"]
        chunk = min(width, PROJ_CHUNK)
        for cc in range(0, width, chunk):
            accs.append(jnp.dot(x, w_ref[:, c0 + cc:c0 + cc + chunk], preferred_element_type=F32))
        c0 += width
    accs = iter(accs)
    oi = 0
    for grp in groups:
        width = grp["width"]
        chunk = min(width, PROJ_CHUNK)
        for cc in range(0, width, chunk):
            acc = next(accs)
            if grp.get("gain_off") is not None:
                assert chunk == width
                go = grp["gain_off"]
                acc = (acc * lax.rsqrt(jnp.mean(acc * acc, axis=-1, keepdims=True) + EPS)
                       * g_ref[:, go:go + width])
            if grp.get("rope") is not None:
                c_ref, s1_ref, s2_ref = tabs[grp["rope"]]
                half = tab_halves[grp["rope"]]
                c, s1, s2 = c_ref[...], s1_ref[...], s2_ref[...]
                parts = []
                for gi in range(chunk // LANES):
                    xg = acc[:, gi * LANES:(gi + 1) * LANES]
                    parts.append(xg * c + pltpu.roll(xg, half, 1) * s1
                                 + pltpu.roll(xg, LANES - half, 1) * s2)
                acc = parts[0] if len(parts) == 1 else jnp.concatenate(parts, axis=1)
            if grp.get("scale") is not None:
                acc = acc * grp["scale"]
            if has_resid:
                acc = acc + r_ref[:, cc:cc + chunk]
            for k in range(len(grp["dtypes"])):
                outs[oi + k][:, cc:cc + chunk] = acc.astype(outs[oi + k].dtype)
        oi += len(grp["dtypes"])


def _proj(xs, w, groups, *, tabs=(), gain=None, resid=None, name="proj"):
    m = xs[0].shape[0]
    k, n = w.shape
    tm = min(m, 512)
    assert m % tm == 0 and n == sum(g["width"] for g in groups) and k == sum(x.shape[1] for x in xs)
    assert resid is None or len(groups) == 1
    in_specs = [pl.BlockSpec((tm, x.shape[1]), lambda i: (i, 0)) for x in xs]
    in_specs.append(pl.BlockSpec((k, n), lambda i: (0, 0)))
    args = list(xs) + [w]
    for c, s1, s2, _ in tabs:
        for t in (c, s1, s2):
            in_specs.append(pl.BlockSpec((tm, LANES), lambda i: (i, 0)))
            args.append(t)
    if gain is not None:
        in_specs.append(pl.BlockSpec((1, gain.shape[0]), lambda i: (0, 0)))
        args.append(gain.reshape(1, -1))
    if resid is not None:
        in_specs.append(pl.BlockSpec((tm, n), lambda i: (i, 0)))
        args.append(resid)
    out_specs, out_shape = [], []
    for g in groups:
        for dt in g["dtypes"]:
            out_specs.append(pl.BlockSpec((tm, g["width"]), lambda i: (i, 0)))
            out_shape.append(jax.ShapeDtypeStruct((m, g["width"]), dt))
    body = functools.partial(_proj_body, nx=len(xs), groups=groups, tab_halves=[t[3] for t in tabs],
                             has_gain=gain is not None, has_resid=resid is not None)
    return pl.pallas_call(
        body,
        grid=(m // tm,),
        in_specs=in_specs,
        out_specs=out_specs,
        out_shape=out_shape,
        compiler_params=_cparams(("parallel",)),
        name=name,
    )(*args)


def _dsa_body(qa_ref, qi_ref, wi_ref, ka_ref, va_ref, ki_ref, tri_ref, o_ref,
              key_scr, qs_scr, m_scr, l_scr, acc_scr, *, tq, tk, s_valid, q0, topk):
    i = pl.program_id(1)
    q_first = q0 + i * tq
    q_last = q_first + tq - 1
    adm_end = jnp.minimum((q_last // CHUNK + 1) * CHUNK, s_valid)
    nkb = (adm_end + tk - 1) // tk
    q_pos = q_first + lax.broadcasted_iota(jnp.int32, (tq, tk), 0)
    q_lim = jnp.minimum(((q_pos >> 6) + 1) * CHUNK, s_valid)
    col = lax.broadcasted_iota(jnp.int32, (tq, tk), 1)
    rep = H_A // KV_A

    w = wi_ref[0] * ((H_I * D_IDX) ** -0.5)
    hg = 4
    q_groups = [qi_ref[0, g * hg:(g + 1) * hg].reshape(hg * tq, D_IDX) for g in range(H_I // hg)]

    def p1(j, c):
        k0 = pl.multiple_of(j * tk, tk)
        kblk = ki_ref[0, pl.ds(k0, tk), :]
        sc = jnp.zeros((tq, tk), F32)
        for g in range(H_I // hg):
            rel = _dot_nt(q_groups[g], kblk)
            for hh in range(hg):
                h = g * hg + hh
                sc = sc + jnp.maximum(rel[hh * tq:(hh + 1) * tq], 0.0) * w[:, h:h + 1]
        bits = pltpu.bitcast(sc, jnp.int32)
        key = bits ^ ((bits >> 31) & INT_MAX)
        key = jnp.where(k0 + col < q_lim, key, INT_MIN)
        key_scr[:, pl.ds(k0, tk)] = key
        return c

    lax.fori_loop(0, nkb, p1, 0)

    def count_ge(v):
        def cb(j, c):
            k0 = pl.multiple_of(j * tk, tk)
            ge = jnp.where(key_scr[:, pl.ds(k0, tk)] >= v, 1.0, 0.0)
            part = ge[:, 0:LANES]
            for cc in range(1, tk // LANES):
                part = part + ge[:, cc * LANES:(cc + 1) * LANES]
            return c + part
        c = lax.fori_loop(0, nkb, cb, jnp.zeros((tq, LANES), F32))
        return jnp.sum(c, axis=1, keepdims=True)

    def bis(_, lohi):
        lo, hi = lohi
        mid = (lo >> 1) + (hi >> 1) + (lo & hi & 1)
        ok = count_ge(mid) >= float(topk)
        return jnp.where(ok, mid, lo), jnp.where(ok, hi, mid)

    thr, _ = lax.fori_loop(0, 32, bis, (jnp.full((tq, 1), INT_MIN, jnp.int32),
                                        jnp.full((tq, 1), INT_MAX, jnp.int32)))
    need = float(topk) - count_ge(thr + 1)

    qa = qa_ref[0]
    for g in range(KV_A):
        qs_scr[g] = jnp.concatenate(
            [qa[:, (g * rep + r) * DH:(g * rep + r + 1) * DH] for r in range(rep)], axis=0)
    m_scr[...] = jnp.full(m_scr.shape, NEG, F32)
    l_scr[...] = jnp.zeros(l_scr.shape, F32)
    acc_scr[...] = jnp.zeros(acc_scr.shape, F32)

    def p3(j, eqc):
        k0 = pl.multiple_of(j * tk, tk)
        key = key_scr[:, pl.ds(k0, tk)]
        eq = key == thr
        eqf = jnp.where(eq, 1.0, 0.0)
        prefix = jnp.dot(eqf.astype(BF16), tri_ref[...], preferred_element_type=F32) + eqc
        bias = jnp.where(key > thr, 0.0, jnp.where(eq, jnp.where(prefix < need, 0.0, NEG), NEG))
        bias = jnp.where(key == INT_MIN, NEG, bias)
        eqc = eqc + jnp.sum(eqf, axis=1, keepdims=True)
        for g in range(KV_A):
            kg = ka_ref[0, pl.ds(k0, tk), g * DH:(g + 1) * DH]
            vg = va_ref[0, pl.ds(k0, tk), g * DH:(g + 1) * DH]
            s = _dot_nt(qs_scr[g], kg)
            ps, alphas = [], []
            for r in range(rep):
                h = g * rep + r
                sr = s[r * tq:(r + 1) * tq] + bias
                m_prev = m_scr[h]
                m_new = jnp.maximum(m_prev, jnp.max(sr, axis=1, keepdims=True))
                alpha = jnp.exp2(m_prev - m_new)
                p = jnp.exp2(sr - jnp.concatenate([m_new] * (tk // LANES), axis=1))
                l_scr[h] = alpha * l_scr[h] + jnp.sum(p, axis=1, keepdims=True)
                m_scr[h] = m_new
                ps.append(p.astype(BF16))
                alphas.append(alpha)
            for r in range(rep):
                h = g * rep + r
                acc_scr[h] = acc_scr[h] * alphas[r] + jnp.dot(ps[r], vg, preferred_element_type=F32)
        return eqc

    lax.fori_loop(0, nkb, p3, jnp.zeros((tq, 1), F32))
    for h in range(H_A):
        o_ref[0, :, h * DH:(h + 1) * DH] = (acc_scr[h] / l_scr[h]).astype(o_ref.dtype)


def _dsa(qa, qi_hm, wi, ka, va, ki, *, tq, tk, s_valid, q0):
    b, t, _ = qa.shape
    s_pad = ka.shape[1]
    assert s_pad % tk == 0 and t % tq == 0 and tk >= TOPK_MAX
    topk = min(TOPK_MAX, s_valid // 4)
    tri = jnp.asarray(np.triu(np.ones((tk, tk), np.float32), 1), BF16)
    body = functools.partial(_dsa_body, tq=tq, tk=tk, s_valid=s_valid, q0=q0, topk=topk)
    rep = H_A // KV_A
    return pl.pallas_call(
        body,
        grid=(b, t // tq),
        in_specs=[pl.BlockSpec((1, tq, H_A * DH), lambda bi, i: (bi, i, 0)),
                  pl.BlockSpec((1, H_I, tq, D_IDX), lambda bi, i: (bi, 0, i, 0)),
                  pl.BlockSpec((1, tq, H_I), lambda bi, i: (bi, i, 0)),
                  pl.BlockSpec((1, s_pad, KV_A * DH), lambda bi, i: (bi, 0, 0)),
                  pl.BlockSpec((1, s_pad, KV_A * DH), lambda bi, i: (bi, 0, 0)),
                  pl.BlockSpec((1, s_pad, D_IDX), lambda bi, i: (bi, 0, 0)),
                  pl.BlockSpec((tk, tk), lambda bi, i: (0, 0))],
        out_specs=pl.BlockSpec((1, tq, H_A * DH), lambda bi, i: (bi, i, 0)),
        out_shape=jax.ShapeDtypeStruct((b, t, H_A * DH), BF16),
        scratch_shapes=[pltpu.VMEM((tq, s_pad), jnp.int32),
                        pltpu.VMEM((KV_A, rep * tq, DH), BF16),
                        pltpu.VMEM((H_A, tq, LANES), F32),
                        pltpu.VMEM((H_A, tq, LANES), F32),
                        pltpu.VMEM((H_A, tq, DH), F32)],
        compiler_params=_cparams(("parallel", "arbitrary")),
        name="dsa",
    )(qa, qi_hm, wi, ka, va, ki, tri)


def _sb_body(q_ref, k_ref, v_ref, low_ref, o_ref, *, tq, tk, s_pad, q0, hpg, nchunk):
    i = pl.program_id(2)
    tqc = tq // nchunk
    q_first = q0 + i * tq
    q_last = q_first + tq - 1
    nkb = jnp.minimum((q_last + tk - 1) // tk, s_pad // tk)
    n_full = jnp.minimum(q_first // tk, nkb)
    row = lax.broadcasted_iota(jnp.int32, (tqc, tk), 0)
    col = lax.broadcasted_iota(jnp.int32, (tqc, tk), 1)
    chains = [(g, c) for g in range(hpg) for c in range(nchunk)]

    def step(j, carry, masked):
        k0 = pl.multiple_of(j * tk, tk)
        zs = [_dot_nt(q_ref[0, c * tqc:(c + 1) * tqc, g * DH:(g + 1) * DH],
                      k_ref[0, pl.ds(k0, tk), g * DH:(g + 1) * DH]) for g, c in chains]
        lbs, lks, cats, causals = [], [], [], []
        for (g, c), z in zip(chains, zs):
            lb = jnp.minimum(z, 0.0) - jnp.log(1.0 + jnp.exp2(-jnp.abs(z))) * LOG2E
            lk = lb - z
            causal = None
            if masked:
                causal = k0 + col < q_first + c * tqc + row
                lk = jnp.where(causal, lk, 0.0)
            hi = lk.astype(BF16)
            lo = (lk - hi.astype(F32)).astype(BF16)
            lbs.append(lb)
            lks.append(lk)
            causals.append(causal)
            cats.append(jnp.concatenate([hi, lo], axis=1))
        sufs = [jnp.dot(cat, low_ref[...], preferred_element_type=F32) for cat in cats]
        out = []
        for n, (g, c) in enumerate(chains):
            run, acc = carry[2 * n], carry[2 * n + 1]
            wgt = jnp.exp2(lbs[n] + sufs[n] + run)
            if masked:
                wgt = jnp.where(causals[n], wgt, 0.0)
            acc = acc + jnp.dot(wgt.astype(BF16), v_ref[0, pl.ds(k0, tk), g * DH:(g + 1) * DH],
                                preferred_element_type=F32)
            run = run + jnp.sum(lks[n], axis=1, keepdims=True)
            out += [run, acc]
        return tuple(out)

    carry = []
    for _ in chains:
        carry += [jnp.zeros((tqc, 1), F32), jnp.zeros((tqc, DH), F32)]
    carry = tuple(carry)
    carry = lax.fori_loop(0, nkb - n_full, lambda s, c: step(nkb - 1 - s, c, True), carry)
    carry = lax.fori_loop(0, n_full, lambda s, c: step(n_full - 1 - s, c, False), carry)
    for n, (g, c) in enumerate(chains):
        o_ref[0, c * tqc:(c + 1) * tqc, g * DH:(g + 1) * DH] = carry[2 * n + 1].astype(o_ref.dtype)


def _stick_breaking(qb, kb, vb, *, tq, tk, q0, hpg, nchunk):
    b, t, _ = qb.shape
    s_pad = kb.shape[1]
    assert s_pad % tk == 0 and t % tq == 0 and H_B % hpg == 0 and tq % nchunk == 0
    low = np.tril(np.ones((tk, tk), np.float32), -1)
    low = jnp.asarray(np.concatenate([low, low], axis=0), BF16)
    body = functools.partial(_sb_body, tq=tq, tk=tk, s_pad=s_pad, q0=q0, hpg=hpg, nchunk=nchunk)
    wd = hpg * DH
    return pl.pallas_call(
        body,
        grid=(b, H_B // hpg, t // tq),
        in_specs=[pl.BlockSpec((1, tq, wd), lambda bi, h, i: (bi, i, h)),
                  pl.BlockSpec((1, s_pad, wd), lambda bi, h, i: (bi, 0, h)),
                  pl.BlockSpec((1, s_pad, wd), lambda bi, h, i: (bi, 0, h)),
                  pl.BlockSpec((2 * tk, tk), lambda bi, h, i: (0, 0))],
        out_specs=pl.BlockSpec((1, tq, wd), lambda bi, h, i: (bi, i, h)),
        out_shape=jax.ShapeDtypeStruct((b, t, H_B * DH), BF16),
        compiler_params=_cparams(("parallel", "parallel", "arbitrary")),
        name="stick_breaking",
    )(qb, kb, vb, low)


def _mla_body(qn_ref, qr_ref, kv_ref, kr_ref, o_ref, *, tq, tk, s_valid, q0, hpg, nchunk):
    i = pl.program_id(2)
    tqc = tq // nchunk
    q_first = q0 + i * tq
    q_last = q_first + tq - 1
    lim_first = jnp.minimum((q_first // CHUNK + 1) * CHUNK, s_valid)
    lim_last = jnp.minimum((q_last // CHUNK + 1) * CHUNK, s_valid)
    nkb = (lim_last + tk - 1) // tk
    n_full = lim_first // tk
    row = lax.broadcasted_iota(jnp.int32, (tqc, tk), 0)
    col = lax.broadcasted_iota(jnp.int32, (tqc, tk), 1)
    chains = [(g, c) for g in range(hpg) for c in range(nchunk)]
    hw = NOPE + VD

    def step(j, carry, masked):
        k0 = pl.multiple_of(j * tk, tk)
        kr = kr_ref[0, pl.ds(k0, tk), :]
        ss = []
        for g, c in chains:
            rows = slice(c * tqc, (c + 1) * tqc)
            qc = jnp.concatenate([qn_ref[0, rows, g * NOPE:(g + 1) * NOPE],
                                  qr_ref[0, rows, g * LANES:(g + 1) * LANES]], axis=1)
            kc = jnp.concatenate([kv_ref[0, pl.ds(k0, tk), g * hw:g * hw + NOPE], kr], axis=1)
            ss.append(_dot_nt(qc, kc))
        ps, stats = [], []
        for n, (g, c) in enumerate(chains):
            m_prev, l_prev = carry[3 * n], carry[3 * n + 1]
            s = ss[n]
            if masked:
                lim = jnp.minimum((((q_first + c * tqc + row) >> 6) + 1) * CHUNK, s_valid)
                s = jnp.where(k0 + col < lim, s, NEG)
            m_new = jnp.maximum(m_prev, jnp.max(s, axis=1, keepdims=True))
            alpha = jnp.exp2(m_prev - m_new)
            p = jnp.exp2(s - jnp.concatenate([m_new] * (tk // LANES), axis=1))
            stats.append((m_new, alpha, alpha * l_prev + jnp.sum(p, axis=1, keepdims=True)))
            ps.append(p.astype(BF16))
        out = []
        for n, (g, c) in enumerate(chains):
            m_new, alpha, l_new = stats[n]
            vb = kv_ref[0, pl.ds(k0, tk), g * hw + NOPE:(g + 1) * hw]
            acc = carry[3 * n + 2] * alpha + jnp.dot(ps[n], vb, preferred_element_type=F32)
            out += [m_new, l_new, acc]
        return tuple(out)

    carry = []
    for _ in chains:
        carry += [jnp.full((tqc, LANES), NEG, F32), jnp.zeros((tqc, LANES), F32), jnp.zeros((tqc, VD), F32)]
    carry = tuple(carry)
    carry = lax.fori_loop(0, n_full, lambda j, c: step(j, c, False), carry)
    carry = lax.fori_loop(n_full, nkb, lambda j, c: step(j, c, True), carry)
    for n, (g, c) in enumerate(chains):
        o_ref[0, c * tqc:(c + 1) * tqc, g * VD:(g + 1) * VD] = (
            carry[3 * n + 2] / carry[3 * n + 1]).astype(o_ref.dtype)


def _mla(qn, qr, kv, kr, *, tq, tk, s_valid, q0, hpg, nchunk):
    b, t, _ = qn.shape
    s_pad = kv.shape[1]
    assert s_pad % tk == 0 and t % tq == 0 and H_C % hpg == 0 and tq % nchunk == 0
    body = functools.partial(_mla_body, tq=tq, tk=tk, s_valid=s_valid, q0=q0, hpg=hpg, nchunk=nchunk)
    return pl.pallas_call(
        body,
        grid=(b, H_C // hpg, t // tq),
        in_specs=[pl.BlockSpec((1, tq, hpg * NOPE), lambda bi, h, i: (bi, i, h)),
                  pl.BlockSpec((1, tq, hpg * LANES), lambda bi, h, i: (bi, i, h)),
                  pl.BlockSpec((1, s_pad, hpg * (NOPE + VD)), lambda bi, h, i: (bi, 0, h)),
                  pl.BlockSpec((1, s_pad, LANES), lambda bi, h, i: (bi, 0, 0))],
        out_specs=pl.BlockSpec((1, tq, hpg * VD), lambda bi, h, i: (bi, i, h)),
        out_shape=jax.ShapeDtypeStruct((b, t, H_C * VD), BF16),
        compiler_params=_cparams(("parallel", "parallel", "arbitrary")),
        name="mla",
    )(qn, qr, kv, kr)


def _ffn_body(x_ref, g_ref, wg_ref, wv_ref, cwg_ref, cwv_ref, cbg_ref, cbv_ref, wd_ref, pg_ref, pv_ref,
              o_ref, sg_ref, sv_ref, h_scr, acc_scr, ug_buf, uv_buf, cg_scr, cv_scr, *, seg, carried):
    i = pl.program_id(0)
    f = pl.program_id(1)
    tm = x_ref.shape[0]

    @pl.when(f == 0)
    def _():
        x = x_ref[...]
        y = x * lax.rsqrt(jnp.mean(x * x, axis=-1, keepdims=True) + EPS)
        h_scr[...] = (y * g_ref[...]).astype(BF16)
        acc_scr[...] = x

    h = h_scr[...]
    halves = ((wg_ref, cwg_ref, cbg_ref, pg_ref, sg_ref, ug_buf, cg_scr),
              (wv_ref, cwv_ref, cbv_ref, pv_ref, sv_ref, uv_buf, cv_scr))
    ys = []
    for w_ref, cw_ref, cb_ref, p_ref, s_ref, buf, c_scr in halves:
        u = jnp.dot(h, w_ref[...], preferred_element_type=F32)
        cw = cw_ref[...]
        parts = []
        for sgi in range(tm // seg):
            us = u[sgi * seg:(sgi + 1) * seg]
            if carried:
                head = jnp.where(i == 0, p_ref[0], c_scr[f])
            else:
                head = p_ref[sgi]
            buf[0:SUBLANES] = head
            buf[SUBLANES:SUBLANES + seg] = us
            y = (cw[2:3] * us + cw[1:2] * buf[SUBLANES - 1:SUBLANES - 1 + seg]
                 + cw[0:1] * buf[SUBLANES - 2:SUBLANES - 2 + seg] + cb_ref[...])
            parts.append(y)
            last = us[seg - SUBLANES:seg]
            s_ref[sgi] = last
            if carried:
                c_scr[f] = last
        ys.append(parts[0] if len(parts) == 1 else jnp.concatenate(parts, axis=0))
    yg, yv = ys
    act = (yg / (1.0 + jnp.exp(-yg))) * yv
    acc_scr[...] += jnp.dot(act.astype(BF16), wd_ref[...], preferred_element_type=F32)

    @pl.when(f == pl.num_programs(1) - 1)
    def _():
        o_ref[...] = acc_scr[...]


def _ffn(x, g, w_up, conv_w, conv_b, w_down, prev8, *, seg, carried, tm, tf):
    m, d = x.shape
    nf = D_FF // tf
    nseg = tm // seg
    assert m % tm == 0 and D_FF % tf == 0 and tm % seg == 0
    sidx = (lambda i: 0) if carried else (lambda i: i)
    body = functools.partial(_ffn_body, seg=seg, carried=carried)
    cb = conv_b.reshape(1, 2 * D_FF)
    return pl.pallas_call(
        body,
        grid=(m // tm, nf),
        in_specs=[pl.BlockSpec((tm, d), lambda i, f: (i, 0)),
                  pl.BlockSpec((1, d), lambda i, f: (0, 0)),
                  pl.BlockSpec((d, tf), lambda i, f: (0, f)),
                  pl.BlockSpec((d, tf), lambda i, f: (0, nf + f)),
                  pl.BlockSpec((CONV_W, tf), lambda i, f: (0, f)),
                  pl.BlockSpec((CONV_W, tf), lambda i, f: (0, nf + f)),
                  pl.BlockSpec((1, tf), lambda i, f: (0, f)),
                  pl.BlockSpec((1, tf), lambda i, f: (0, nf + f)),
                  pl.BlockSpec((tf, d), lambda i, f: (f, 0)),
                  pl.BlockSpec((nseg, SUBLANES, tf), lambda i, f: (sidx(i), 0, f)),
                  pl.BlockSpec((nseg, SUBLANES, tf), lambda i, f: (sidx(i), 0, nf + f))],
        out_specs=[pl.BlockSpec((tm, d), lambda i, f: (i, 0)),
                   pl.BlockSpec((nseg, SUBLANES, tf), lambda i, f: (i, 0, f)),
                   pl.BlockSpec((nseg, SUBLANES, tf), lambda i, f: (i, 0, f))],
        out_shape=[jax.ShapeDtypeStruct((m, d), F32),
                   jax.ShapeDtypeStruct((m // seg, SUBLANES, D_FF), F32),
                   jax.ShapeDtypeStruct((m // seg, SUBLANES, D_FF), F32)],
        scratch_shapes=[pltpu.VMEM((tm, d), BF16),
                        pltpu.VMEM((tm, d), F32),
                        pltpu.VMEM((SUBLANES + seg, tf), F32),
                        pltpu.VMEM((SUBLANES + seg, tf), F32),
                        pltpu.VMEM((nf, SUBLANES, tf), F32),
                        pltpu.VMEM((nf, SUBLANES, tf), F32)],
        compiler_params=_cparams(("arbitrary", "arbitrary")),
        name="conv_ffn",
    )(x, g.reshape(1, d), w_up, w_up, conv_w, conv_w, cb, cb, w_down, prev8, prev8)


def _rope_tables(pos, head_dim, rot, lanes_valid=LANES):
    half = rot // 2
    freqs = ROPE_THETA ** (-jnp.arange(half, dtype=F32) / half)
    ang = pos.astype(F32)[:, None] * freqs[None, :]
    cos, sin = jnp.cos(ang), jnp.sin(ang)
    lane = np.arange(LANES)
    d = lane % head_dim
    first = (d < half) & (lane < lanes_valid)
    second = (d >= half) & (d < rot) & (lane < lanes_valid)
    fidx = np.clip(np.where(d < half, d, d - half), 0, half - 1)
    cosl, sinl = cos[:, fidx], sin[:, fidx]
    c = jnp.where(first | second, cosl, 1.0)
    s1 = jnp.where(second, sinl, 0.0)
    s2 = jnp.where(first, -sinl, 0.0)
    return c, s1, s2, half


def _prep_weights(w_in_ab, w_out_ab, w_dqkv, w_uq, w_ukv, w_o_mla, w_up, w_down):
    n_even, n_odd = w_in_ab.shape[0], w_dqkv.shape[0]
    w = {}
    n_a = H_A * DH + 2 * KV_A * DH + H_I * D_IDX
    w["in_a"] = w_in_ab[:, :, :n_a].astype(BF16)
    pad = jnp.zeros((n_even, D_MODEL, LANES - D_IDX - H_I), BF16)
    w["in_b"] = jnp.concatenate([w_in_ab[:, :, n_a:n_a + D_IDX + H_I].astype(BF16), pad,
                                 w_in_ab[:, :, n_a + D_IDX + H_I:].astype(BF16)], axis=-1)
    w["out_ab"] = w_out_ab.astype(BF16)
    w["dqkr"] = jnp.concatenate([w_dqkv.astype(BF16),
                                 jnp.zeros((n_odd, D_MODEL, LANES - ROPE_C), BF16)], axis=-1)
    uq = w_uq.reshape(n_odd, Q_RANK, H_C, NOPE + ROPE_C)
    uqn = uq[..., :NOPE].reshape(n_odd, Q_RANK, H_C * NOPE).astype(BF16)
    uqr = uq[..., NOPE:].astype(BF16)
    uqr = jnp.concatenate([uqr, jnp.zeros_like(uqr)], axis=-1).reshape(n_odd, Q_RANK, H_C * LANES)
    w["uq"] = jnp.concatenate([uqn, uqr], axis=-1)
    w["ukv"] = w_ukv.astype(BF16)
    w["o_mla"] = w_o_mla.astype(BF16)
    w["up"] = w_up.astype(BF16)
    w["down"] = w_down.astype(BF16)
    return w


def _with_past(past, new, s_pad):
    b, _, fdim = new.shape
    parts = [new.astype(BF16)] if past is None else [past.astype(BF16), new.astype(BF16)]
    n = sum(p.shape[1] for p in parts)
    if s_pad > n:
        parts.append(jnp.zeros((b, s_pad - n, fdim), BF16))
    return parts[0] if len(parts) == 1 else jnp.concatenate(parts, axis=1)


def _trunk(x, past, conv_state, w, p, cfg):
    b, t, d = x.shape
    m = b * t
    q0 = 0 if past is None else past[0].shape[2]
    s_valid = q0 + t
    tk = cfg["tk"]
    s_pad = -(-s_valid // tk) * tk
    pos = q0 + jnp.tile(jnp.arange(t), b)
    tab_a = _rope_tables(pos, DH, ROT_A)
    tab_i = _rope_tables(pos, D_IDX, ROT_I)
    tab_ki = _rope_tables(pos, D_IDX, ROT_I, lanes_valid=D_IDX)
    tab_kr = _rope_tables(pos, ROPE_C, ROPE_C, lanes_valid=ROPE_C)

    xf = x.reshape(m, d)
    ab_rows, mla_rows, conv_rows = [], [], []
    sc_ab = DH ** -0.5 * LOG2E
    sc_c = (NOPE + ROPE_C) ** -0.5 * LOG2E
    for l in range(DEPTH):
        h = _rmsnorm(xf, p["norm_mix"][l], BF16)
        if l % 2 == 0:
            e = l // 2
            lp = None if past is None else tuple(c[e] for c in past[:5])
            qa, ka32, ka16, va32, va16, qi = _proj(
                [h], w["in_a"][e],
                [dict(width=H_A * DH, dtypes=(BF16,), rope=0, scale=sc_ab),
                 dict(width=KV_A * DH, dtypes=(F32, BF16), rope=0),
                 dict(width=KV_A * DH, dtypes=(F32, BF16)),
                 dict(width=H_I * D_IDX, dtypes=(BF16,), rope=1)],
                tabs=(tab_a, tab_i), name="proj_in_a")
            kiwi, qb, kb32, kb16, vb32, vb16 = _proj(
                [h], w["in_b"][e],
                [dict(width=LANES, dtypes=(F32,), rope=0),
                 dict(width=H_B * DH, dtypes=(BF16,), scale=sc_ab),
                 dict(width=H_B * DH, dtypes=(F32, BF16)),
                 dict(width=H_B * DH, dtypes=(F32, BF16))],
                tabs=(tab_ki,), name="proj_in_b")
            ki32 = kiwi[:, :D_IDX]
            wi = kiwi[:, D_IDX:D_IDX + H_I]
            ab_rows.append((ka32.reshape(b, t, KV_A, DH), va32.reshape(b, t, KV_A, DH),
                            ki32.reshape(b, t, D_IDX), kb32.reshape(b, t, H_B, DH),
                            vb32.reshape(b, t, H_B, DH)))

            def full(idx, new16):
                pst = None if lp is None else lp[idx].reshape(b, q0, -1)
                return _with_past(pst, new16.reshape(b, t, -1), s_pad)

            qi_hm = qi.reshape(b, t, H_I, D_IDX).transpose(0, 2, 1, 3)
            oa = _dsa(qa.reshape(b, t, -1), qi_hm, wi.reshape(b, t, H_I), full(0, ka16), full(1, va16),
                      full(2, ki32), tq=cfg["tq_a"], tk=tk, s_valid=s_valid, q0=q0)
            ob = _stick_breaking(qb.reshape(b, t, -1), full(3, kb16), full(4, vb16),
                                 tq=cfg["tq_b"], tk=cfg["tk_b"], q0=q0, hpg=cfg["hpg_b"], nchunk=cfg["nc_b"])
            (xf,) = _proj([oa.reshape(m, -1), ob.reshape(m, -1)], w["out_ab"][e],
                          [dict(width=D_MODEL, dtypes=(F32,))], resid=xf, name="proj_out_ab")
        else:
            od = l // 2
            gains = jnp.concatenate([p["g_q"][od], p["g_kv"][od]])
            cq16, ckv32, ckv16, kr32w, kr16w = _proj(
                [h], w["dqkr"][od],
                [dict(width=Q_RANK, dtypes=(BF16,), gain_off=0),
                 dict(width=KV_RANK, dtypes=(F32, BF16), gain_off=Q_RANK),
                 dict(width=LANES, dtypes=(F32, BF16), rope=0)],
                tabs=(tab_kr,), gain=gains, name="proj_dqkr")
            mla_rows.append((ckv32.reshape(b, t, KV_RANK), kr32w[:, :ROPE_C].reshape(b, t, ROPE_C)))
            qn, qr = _proj([cq16], w["uq"][od],
                           [dict(width=H_C * NOPE, dtypes=(BF16,), scale=sc_c),
                            dict(width=H_C * LANES, dtypes=(BF16,), rope=0, scale=sc_c)],
                           tabs=(tab_kr,), name="proj_uq")
            pc = None if past is None else past[5][od]
            pk = None if past is None else jnp.pad(past[6][od], ((0, 0), (0, 0), (0, LANES - ROPE_C)))
            c_all = _with_past(pc, ckv16.reshape(b, t, KV_RANK), s_pad)
            kr_all = _with_past(pk, kr16w.reshape(b, t, LANES), s_pad)
            (kv,) = _proj([c_all.reshape(b * s_pad, KV_RANK)], w["ukv"][od],
                          [dict(width=H_C * (NOPE + VD), dtypes=(BF16,))], name="proj_ukv")
            o = _mla(qn.reshape(b, t, -1), qr.reshape(b, t, -1), kv.reshape(b, s_pad, -1), kr_all,
                     tq=cfg["tq_c"], tk=cfg["tk_c"], s_valid=s_valid, q0=q0, hpg=cfg["hpg_c"],
                     nchunk=cfg["nc_c"])
            (xf,) = _proj([o.reshape(m, -1)], w["o_mla"][od], [dict(width=D_MODEL, dtypes=(F32,))],
                          resid=xf, name="proj_o_mla")

        if conv_state is None:
            prev8 = jnp.zeros((b, SUBLANES, 2 * D_FF), F32)
        else:
            prev8 = jnp.concatenate([jnp.zeros((b, SUBLANES - (CONV_W - 1), 2 * D_FF), F32),
                                     conv_state[l]], axis=1)
        xf, sg, sv = _ffn(xf, p["norm_ffn"][l], w["up"][l], p["conv_w"][l], p["conv_b"][l], w["down"][l],
                          prev8, seg=cfg["seg"], carried=cfg["carried"], tm=cfg["tm_ffn"], tf=cfg["tf"])
        last = [s8.reshape(b, -1, SUBLANES, D_FF)[:, -1, SUBLANES - (CONV_W - 1):] for s8 in (sg, sv)]
        conv_rows.append(jnp.concatenate(last, axis=-1))

    y = _rmsnorm(xf, p["norm_final"], F32).reshape(b, t, d)
    new_ab = [jnp.stack([r[i] for r in ab_rows]) for i in range(5)]
    new_mla = [jnp.stack([r[i] for r in mla_rows]) for i in range(2)]
    return y, new_ab + new_mla + [jnp.stack(conv_rows)]


def kernel(x_prompt, x_sample, cache_k_a, cache_v_a, cache_idx_k, cache_k_b, cache_v_b, cache_ckv, cache_krope, state_conv, norm_mix, norm_ffn, norm_final, w_in_ab, w_out_ab, w_dqkv, g_q, g_kv, w_uq, w_ukv, w_o_mla, w_up, conv_w, conv_b, w_down):
    w = _prep_weights(w_in_ab, w_out_ab, w_dqkv, w_uq, w_ukv, w_o_mla, w_up, w_down)
    p = dict(norm_mix=norm_mix, norm_ffn=norm_ffn, norm_final=norm_final, g_q=g_q, g_kv=g_kv,
             conv_w=conv_w, conv_b=conv_b)
    cfg_p = dict(tk=512, tq_a=128, tq_b=256, tk_b=256, hpg_b=4, nc_b=1, tq_c=256, tk_c=512, hpg_c=4, nc_c=1,
                 seg=512, carried=True, tm_ffn=512, tf=512)
    t_s = x_sample.shape[1]
    cfg_s = dict(tk=256, tq_a=t_s, tq_b=t_s, tk_b=256, hpg_b=H_B, nc_b=1, tq_c=t_s, tk_c=256, hpg_c=8, nc_c=1,
                 seg=t_s, carried=False, tm_ffn=x_sample.shape[0] * t_s, tf=512)
    y_p, st_p = _trunk(x_prompt, None, None, w, p, cfg_p)
    past = (cache_k_a, cache_v_a, cache_idx_k, cache_k_b, cache_v_b, cache_ckv, cache_krope)
    y_s, st_s = _trunk(x_sample, past, state_conv, w, p, cfg_s)
    return (y_p, y_s, *st_p, *st_s)
```

```python
import functools

import numpy as np
import jax
import jax.numpy as jnp
from jax import lax
from jax.experimental import pallas as pl
from jax.experimental.pallas import tpu as pltpu

F32 = jnp.float32
BF16 = jnp.bfloat16

D_MODEL = 2048
DEPTH = 4
CHUNK = 64
ROPE_THETA = 500000.0
EPS = 1e-6
H_A, KV_A, DH = 8, 2, 128
ROT_A = DH // 4
H_I, D_IDX = 16, 64
ROT_I = D_IDX // 4
TOPK_MAX = 256
H_B = 8
H_C, Q_RANK, KV_RANK, NOPE, ROPE_C, VD = 16, 512, 512, 128, 64, 128
D_FF = 5632
CONV_W = 3

LANES = 128
SUBLANES = 8
VMEM_LIMIT = 56 * 1024 * 1024
NEG = -1e30
INT_MIN = np.int32(-2 ** 31)
INT_MAX = np.int32(2 ** 31 - 1)
LOG2E = float(np.log2(np.e))


def _cparams(sem):
    return pltpu.CompilerParams(dimension_semantics=sem, vmem_limit_bytes=VMEM_LIMIT)


def _dot_nt(a, b):
    return lax.dot_general(a, b, (((1,), (1,)), ((), ())), preferred_element_type=F32)


def _rms_body(x_ref, g_ref, o_ref):
    x = x_ref[...]
    y = x * lax.rsqrt(jnp.mean(x * x, axis=-1, keepdims=True) + EPS)
    o_ref[...] = (y * g_ref[...]).astype(o_ref.dtype)


def _rmsnorm(x, g, out_dtype):
    m, d = x.shape
    tm = min(m, 512)
    return pl.pallas_call(
        _rms_body,
        grid=(m // tm,),
        in_specs=[pl.BlockSpec((tm, d), lambda i: (i, 0)),
                  pl.BlockSpec((1, d), lambda i: (0, 0))],
        out_specs=pl.BlockSpec((tm, d), lambda i: (i, 0)),
        out_shape=jax.ShapeDtypeStruct((m, d), out_dtype),
        compiler_params=_cparams(("parallel",)),
        name="rmsnorm",
    )(x, g.reshape(1, d))


PROJ_CHUNK = 512


def _proj_body(*refs, nx, groups, tab_halves, has_gain, has_resid):
    it = iter(refs)
    x_refs = [next(it) for _ in range(nx)]
    w_ref = next(it)
    tabs = [(next(it), next(it), next(it)) for _ in tab_halves]
    g_ref = next(it) if has_gain else None
    r_ref = next(it) if has_resid else None
    outs = list(it)

    xs = [r[...] for r in x_refs]
    x = xs[0] if nx == 1 else jnp.concatenate(xs, axis=1)
    accs = []
    c0 = 0
    for grp in groups:
        width = grp["width"]
        chunk = min(width, PROJ_CHUNK)
        for cc in range(0, width, chunk):
            accs.append(jnp.dot(x, w_ref[:, c0 + cc:c0 + cc + chunk], preferred_element_type=F32))
        c0 += width
    accs = iter(accs)
    oi = 0
    for grp in groups:
        width = grp["width"]
        chunk = min(width, PROJ_CHUNK)
        for cc in range(0, width, chunk):
            acc = next(accs)
            if grp.get("gain_off") is not None:
                assert chunk == width
                go = grp["gain_off"]
                acc = (acc * lax.rsqrt(jnp.mean(acc * acc, axis=-1, keepdims=True) + EPS)
                       * g_ref[:, go:go + width])
            if grp.get("rope") is not None:
                c_ref, s1_ref, s2_ref = tabs[grp["rope"]]
                half = tab_halves[grp["rope"]]
                c, s1, s2 = c_ref[...], s1_ref[...], s2_ref[...]
                parts = []
                for gi in range(chunk // LANES):
                    xg = acc[:, gi * LANES:(gi + 1) * LANES]
                    parts.append(xg * c + pltpu.roll(xg, half, 1) * s1
                                 + pltpu.roll(xg, LANES - half, 1) * s2)
                acc = parts[0] if len(parts) == 1 else jnp.concatenate(parts, axis=1)
            if grp.get("scale") is not None:
                acc = acc * grp["scale"]
            if has_resid:
                acc = acc + r_ref[:, cc:cc + chunk]
            for k in range(len(grp["dtypes"])):
                o = outs[oi + k]
                if len(o.shape) == 3:
                    for gi in range(chunk // LANES):
                        o[:, cc // LANES + gi, :] = acc[:, gi * LANES:(gi + 1) * LANES].astype(o.dtype)
                else:
                    o[:, cc:cc + chunk] = acc.astype(o.dtype)
        oi += len(grp["dtypes"])


def _proj(xs, w, groups, *, tabs=(), gain=None, resid=None, name="proj"):
    m = xs[0].shape[0]
    k, n = w.shape
    tm = min(m, 512)
    assert m % tm == 0 and n == sum(g["width"] for g in groups) and k == sum(x.shape[1] for x in xs)
    assert resid is None or len(groups) == 1
    in_specs = [pl.BlockSpec((tm, x.shape[1]), lambda i: (i, 0)) for x in xs]
    in_specs.append(pl.BlockSpec((k, n), lambda i: (0, 0)))
    args = list(xs) + [w]
    for c, s1, s2, _ in tabs:
        for t in (c, s1, s2):
            in_specs.append(pl.BlockSpec((tm, LANES), lambda i: (i, 0)))
            args.append(t)
    if gain is not None:
        in_specs.append(pl.BlockSpec((1, gain.shape[0]), lambda i: (0, 0)))
        args.append(gain.reshape(1, -1))
    if resid is not None:
        in_specs.append(pl.BlockSpec((tm, n), lambda i: (i, 0)))
        args.append(resid)
    out_specs, out_shape = [], []
    for g in groups:
        for dt in g["dtypes"]:
            if g.get("split_heads") and dt == F32:
                nh = g["width"] // LANES
                out_specs.append(pl.BlockSpec((tm, nh, LANES), lambda i: (i, 0, 0)))
                out_shape.append(jax.ShapeDtypeStruct((m, nh, LANES), dt))
            else:
                out_specs.append(pl.BlockSpec((tm, g["width"]), lambda i: (i, 0)))
                out_shape.append(jax.ShapeDtypeStruct((m, g["width"]), dt))
    body = functools.partial(_proj_body, nx=len(xs), groups=groups, tab_halves=[t[3] for t in tabs],
                             has_gain=gain is not None, has_resid=resid is not None)
    return pl.pallas_call(
        body,
        grid=(m // tm,),
        in_specs=in_specs,
        out_specs=out_specs,
        out_shape=out_shape,
        compiler_params=_cparams(("parallel",)),
        name=name,
    )(*args)


def _dsa_body(qa_ref, qi_ref, wi_ref, ka_ref, va_ref, ki_ref, tri_ref, o_ref,
              key_scr, qs_scr, m_scr, l_scr, acc_scr, *, tq, tk, s_valid, q0, topk):
    i = pl.program_id(1)
    q_first = q0 + i * tq
    q_last = q_first + tq - 1
    adm_end = jnp.minimum((q_last // CHUNK + 1) * CHUNK, s_valid)
    nkb = (adm_end + tk - 1) // tk
    q_pos = q_first + lax.broadcasted_iota(jnp.int32, (tq, tk), 0)
    q_lim = jnp.minimum(((q_pos >> 6) + 1) * CHUNK, s_valid)
    col = lax.broadcasted_iota(jnp.int32, (tq, tk), 1)
    rep = H_A // KV_A

    w = wi_ref[0] * ((H_I * D_IDX) ** -0.5)
    pg = 4
    pairs = LANES // D_IDX
    qi = qi_ref[0]
    q_groups = [jnp.concatenate([qi[:, (g * pg + pp) * LANES:(g * pg + pp + 1) * LANES]
                                 for pp in range(pg)], axis=0) for g in range(H_I // (pairs * pg))]

    def to_key(x):
        bits = pltpu.bitcast(x, jnp.int32)
        return bits ^ ((bits >> 31) & INT_MAX)

    def p1(j, c):
        k0 = pl.multiple_of(j * tk, tk)
        sc = jnp.zeros((tq, tk), F32)
        for g, qg in enumerate(q_groups):
            for half in range(pairs):
                rel = _dot_nt(qg, ki_ref[0, pl.ds(k0, tk), half * LANES:(half + 1) * LANES])
                for pp in range(pg):
                    h = (g * pg + pp) * pairs + half
                    sc = sc + jnp.maximum(rel[pp * tq:(pp + 1) * tq], 0.0) * w[:, h:h + 1]
        key_scr[:, pl.ds(k0, tk)] = jnp.where(k0 + col < q_lim, to_key(sc), INT_MIN)
        return c

    lax.fori_loop(0, nkb, p1, 0)

    def count_ge(v):
        def cb(j, c):
            k0 = pl.multiple_of(j * tk, tk)
            ge = jnp.where(key_scr[:, pl.ds(k0, tk)] >= v, 1.0, 0.0)
            part = ge[:, 0:LANES]
            for cc in range(1, tk // LANES):
                part = part + ge[:, cc * LANES:(cc + 1) * LANES]
            return c + part
        c = lax.fori_loop(0, nkb, cb, jnp.zeros((tq, LANES), F32))
        return jnp.sum(c, axis=1, keepdims=True)

    def bis(_, lohi):
        lo, hi = lohi
        mid = (lo >> 1) + (hi >> 1) + (lo & hi & 1)
        ok = count_ge(mid) >= float(topk)
        return jnp.where(ok, mid, lo), jnp.where(ok, hi, mid)

    thr, _ = lax.fori_loop(0, 32, bis, (jnp.full((tq, 1), INT_MIN, jnp.int32),
                                        jnp.full((tq, 1), INT_MAX, jnp.int32)))
    need = float(topk) - count_ge(thr + 1)

    qa = qa_ref[0]
    for g in range(KV_A):
        qs_scr[g] = jnp.concatenate(
            [qa[:, (g * rep + r) * DH:(g * rep + r + 1) * DH] for r in range(rep)], axis=0)
    m_scr[...] = jnp.full(m_scr.shape, NEG, F32)
    l_scr[...] = jnp.zeros(l_scr.shape, F32)
    acc_scr[...] = jnp.zeros(acc_scr.shape, F32)

    def p3(j, eqc):
        k0 = pl.multiple_of(j * tk, tk)
        key = key_scr[:, pl.ds(k0, tk)]
        eq = key == thr
        eqf = jnp.where(eq, 1.0, 0.0)
        prefix = jnp.dot(eqf.astype(BF16), tri_ref[...], preferred_element_type=F32) + eqc
        bias = jnp.where(key > thr, 0.0, jnp.where(eq, jnp.where(prefix < need, 0.0, NEG), NEG))
        bias = jnp.where(key == INT_MIN, NEG, bias)
        eqc = eqc + jnp.sum(eqf, axis=1, keepdims=True)
        for g in range(KV_A):
            kg = ka_ref[0, pl.ds(k0, tk), g * DH:(g + 1) * DH]
            vg = va_ref[0, pl.ds(k0, tk), g * DH:(g + 1) * DH]
            s = _dot_nt(qs_scr[g], kg)
            ps, alphas = [], []
            for r in range(rep):
                h = g * rep + r
                sr = s[r * tq:(r + 1) * tq] + bias
                m_prev = m_scr[h]
                m_new = jnp.maximum(m_prev, jnp.max(sr, axis=1, keepdims=True))
                alpha = jnp.exp2(m_prev - m_new)
                p = jnp.exp2(sr - jnp.concatenate([m_new] * (tk // LANES), axis=1))
                l_scr[h] = alpha * l_scr[h] + jnp.sum(p, axis=1, keepdims=True)
                m_scr[h] = m_new
                ps.append(p.astype(BF16))
                alphas.append(alpha)
            for r in range(rep):
                h = g * rep + r
                acc_scr[h] = acc_scr[h] * alphas[r] + jnp.dot(ps[r], vg, preferred_element_type=F32)
        return eqc

    lax.fori_loop(0, nkb, p3, jnp.zeros((tq, 1), F32))
    for h in range(H_A):
        o_ref[0, :, h * DH:(h + 1) * DH] = (acc_scr[h] / l_scr[h]).astype(o_ref.dtype)


def _dsa(qa, qi, wi, ka, va, ki, *, tq, tk, s_valid, q0):
    b, t, _ = qa.shape
    s_pad = ka.shape[1]
    assert s_pad % tk == 0 and t % tq == 0 and tk >= TOPK_MAX
    topk = min(TOPK_MAX, s_valid // 4)
    tri = jnp.asarray(np.triu(np.ones((tk, tk), np.float32), 1), BF16)
    body = functools.partial(_dsa_body, tq=tq, tk=tk, s_valid=s_valid, q0=q0, topk=topk)
    rep = H_A // KV_A
    return pl.pallas_call(
        body,
        grid=(b, t // tq),
        in_specs=[pl.BlockSpec((1, tq, H_A * DH), lambda bi, i: (bi, i, 0)),
                  pl.BlockSpec((1, tq, H_I * D_IDX), lambda bi, i: (bi, i, 0)),
                  pl.BlockSpec((1, tq, H_I), lambda bi, i: (bi, i, 0)),
                  pl.BlockSpec((1, s_pad, KV_A * DH), lambda bi, i: (bi, 0, 0)),
                  pl.BlockSpec((1, s_pad, KV_A * DH), lambda bi, i: (bi, 0, 0)),
                  pl.BlockSpec((1, s_pad, 2 * LANES), lambda bi, i: (bi, 0, 0)),
                  pl.BlockSpec((tk, tk), lambda bi, i: (0, 0))],
        out_specs=pl.BlockSpec((1, tq, H_A * DH), lambda bi, i: (bi, i, 0)),
        out_shape=jax.ShapeDtypeStruct((b, t, H_A * DH), BF16),
        scratch_shapes=[pltpu.VMEM((tq, s_pad), jnp.int32),
                        pltpu.VMEM((KV_A, rep * tq, DH), BF16),
                        pltpu.VMEM((H_A, tq, LANES), F32),
                        pltpu.VMEM((H_A, tq, LANES), F32),
                        pltpu.VMEM((H_A, tq, DH), F32)],
        compiler_params=_cparams(("parallel", "arbitrary")),
        name="dsa",
    )(qa, qi, wi, ka, va, ki, tri)


def _sb_body(q_ref, k_ref, v_ref, low_ref, o_ref, *, tq, tk, s_pad, q0, hpg, nchunk):
    i = pl.program_id(2)
    tqc = tq // nchunk
    q_first = q0 + i * tq
    q_last = q_first + tq - 1
    nkb = jnp.minimum((q_last + tk - 1) // tk, s_pad // tk)
    n_full = jnp.minimum(q_first // tk, nkb)
    row = lax.broadcasted_iota(jnp.int32, (tqc, tk), 0)
    col = lax.broadcasted_iota(jnp.int32, (tqc, tk), 1)
    chains = [(g, c) for g in range(hpg) for c in range(nchunk)]

    def step(j, carry, masked):
        k0 = pl.multiple_of(j * tk, tk)
        zs = [_dot_nt(q_ref[0, c * tqc:(c + 1) * tqc, g * DH:(g + 1) * DH],
                      k_ref[0, pl.ds(k0, tk), g * DH:(g + 1) * DH]) for g, c in chains]
        lbs, lks, cats, causals = [], [], [], []
        for (g, c), z in zip(chains, zs):
            lb = jnp.minimum(z, 0.0) - jnp.log(1.0 + jnp.exp2(-jnp.abs(z))) * LOG2E
            lk = lb - z
            causal = None
            if masked:
                causal = k0 + col < q_first + c * tqc + row
                lk = jnp.where(causal, lk, 0.0)
            hi = lk.astype(BF16)
            lo = (lk - hi.astype(F32)).astype(BF16)
            lbs.append(lb)
            lks.append(lk)
            causals.append(causal)
            cats.append(jnp.concatenate([hi, lo], axis=1))
        sufs = [jnp.dot(cat, low_ref[...], preferred_element_type=F32) for cat in cats]
        out = []
        for n, (g, c) in enumerate(chains):
            run, acc = carry[2 * n], carry[2 * n + 1]
            wgt = jnp.exp2(lbs[n] + sufs[n] + run)
            if masked:
                wgt = jnp.where(causals[n], wgt, 0.0)
            acc = acc + jnp.dot(wgt.astype(BF16), v_ref[0, pl.ds(k0, tk), g * DH:(g + 1) * DH],
                                preferred_element_type=F32)
            run = run + jnp.sum(lks[n], axis=1, keepdims=True)
            out += [run, acc]
        return tuple(out)

    carry = []
    for _ in chains:
        carry += [jnp.zeros((tqc, 1), F32), jnp.zeros((tqc, DH), F32)]
    carry = tuple(carry)
    carry = lax.fori_loop(0, nkb - n_full, lambda s, c: step(nkb - 1 - s, c, True), carry)
    carry = lax.fori_loop(0, n_full, lambda s, c: step(n_full - 1 - s, c, False), carry)
    for n, (g, c) in enumerate(chains):
        o_ref[0, c * tqc:(c + 1) * tqc, g * DH:(g + 1) * DH] = carry[2 * n + 1].astype(o_ref.dtype)


def _stick_breaking(qb, kb, vb, *, tq, tk, q0, hpg, nchunk):
    b, t, _ = qb.shape
    s_pad = kb.shape[1]
    assert s_pad % tk == 0 and t % tq == 0 and H_B % hpg == 0 and tq % nchunk == 0
    low = np.tril(np.ones((tk, tk), np.float32), -1)
    low = jnp.asarray(np.concatenate([low, low], axis=0), BF16)
    body = functools.partial(_sb_body, tq=tq, tk=tk, s_pad=s_pad, q0=q0, hpg=hpg, nchunk=nchunk)
    wd = hpg * DH
    return pl.pallas_call(
        body,
        grid=(b, H_B // hpg, t // tq),
        in_specs=[pl.BlockSpec((1, tq, wd), lambda bi, h, i: (bi, i, h)),
                  pl.BlockSpec((1, s_pad, wd), lambda bi, h, i: (bi, 0, h)),
                  pl.BlockSpec((1, s_pad, wd), lambda bi, h, i: (bi, 0, h)),
                  pl.BlockSpec((2 * tk, tk), lambda bi, h, i: (0, 0))],
        out_specs=pl.BlockSpec((1, tq, wd), lambda bi, h, i: (bi, i, h)),
        out_shape=jax.ShapeDtypeStruct((b, t, H_B * DH), BF16),
        compiler_params=_cparams(("parallel", "parallel", "arbitrary")),
        name="stick_breaking",
    )(qb, kb, vb, low)


def _mla_body(qn_ref, qr_ref, kv_ref, kr_ref, o_ref, *, tq, tk, s_valid, q0, hpg, nchunk):
    i = pl.program_id(2)
    tqc = tq // nchunk
    q_first = q0 + i * tq
    q_last = q_first + tq - 1
    lim_first = jnp.minimum((q_first // CHUNK + 1) * CHUNK, s_valid)
    lim_last = jnp.minimum((q_last // CHUNK + 1) * CHUNK, s_valid)
    nkb = (lim_last + tk - 1) // tk
    n_full = lim_first // tk
    row = lax.broadcasted_iota(jnp.int32, (tqc, tk), 0)
    col = lax.broadcasted_iota(jnp.int32, (tqc, tk), 1)
    chains = [(g, c) for g in range(hpg) for c in range(nchunk)]
    hw = NOPE + VD

    def step(j, carry, masked):
        k0 = pl.multiple_of(j * tk, tk)
        kr = kr_ref[0, pl.ds(k0, tk), :]
        ss = []
        for g, c in chains:
            rows = slice(c * tqc, (c + 1) * tqc)
            qc = jnp.concatenate([qn_ref[0, rows, g * NOPE:(g + 1) * NOPE],
                                  qr_ref[0, rows, g * LANES:(g + 1) * LANES]], axis=1)
            kc = jnp.concatenate([kv_ref[0, pl.ds(k0, tk), g * hw:g * hw + NOPE], kr], axis=1)
            ss.append(_dot_nt(qc, kc))
        ps, stats = [], []
        for n, (g, c) in enumerate(chains):
            m_prev = carry[2 * n]
            s = ss[n]
            if masked:
                lim = jnp.minimum((((q_first + c * tqc + row) >> 6) + 1) * CHUNK, s_valid)
                s = jnp.where(k0 + col < lim, s, NEG)
            m_new = jnp.maximum(m_prev, jnp.max(s, axis=1, keepdims=True))
            alpha = jnp.exp2(m_prev - m_new)
            p = jnp.exp2(s - jnp.concatenate([m_new] * (tk // LANES), axis=1))
            stats.append((m_new, alpha))
            ps.append(p.astype(BF16))
        out = []
        for n, (g, c) in enumerate(chains):
            m_new, alpha = stats[n]
            vb = jnp.concatenate([kv_ref[0, pl.ds(k0, tk), g * hw + NOPE:(g + 1) * hw], ones_k], axis=1)
            acc = (carry[2 * n + 1] * jnp.concatenate([alpha, alpha], axis=1)
                   + jnp.dot(ps[n], vb, preferred_element_type=F32))
            out += [m_new, acc]
        return tuple(out)

    ones_k = jnp.ones((tk, LANES), BF16)
    carry = []
    for _ in chains:
        carry += [jnp.full((tqc, LANES), NEG, F32), jnp.zeros((tqc, VD + LANES), F32)]
    carry = tuple(carry)
    carry = lax.fori_loop(0, n_full, lambda j, c: step(j, c, False), carry)
    carry = lax.fori_loop(n_full, nkb, lambda j, c: step(j, c, True), carry)
    for n, (g, c) in enumerate(chains):
        acc = carry[2 * n + 1]
        o_ref[0, c * tqc:(c + 1) * tqc, g * VD:(g + 1) * VD] = (acc[:, :VD] / acc[:, VD:]).astype(o_ref.dtype)


def _mla(qn, qr, kv, kr, *, tq, tk, s_valid, q0, hpg, nchunk):
    b, t, _ = qn.shape
    s_pad = kv.shape[1]
    assert s_pad % tk == 0 and t % tq == 0 and H_C % hpg == 0 and tq % nchunk == 0
    body = functools.partial(_mla_body, tq=tq, tk=tk, s_valid=s_valid, q0=q0, hpg=hpg, nchunk=nchunk)
    return pl.pallas_call(
        body,
        grid=(b, H_C // hpg, t // tq),
        in_specs=[pl.BlockSpec((1, tq, hpg * NOPE), lambda bi, h, i: (bi, i, h)),
                  pl.BlockSpec((1, tq, hpg * LANES), lambda bi, h, i: (bi, i, h)),
                  pl.BlockSpec((1, s_pad, hpg * (NOPE + VD)), lambda bi, h, i: (bi, 0, h)),
                  pl.BlockSpec((1, s_pad, LANES), lambda bi, h, i: (bi, 0, 0))],
        out_specs=pl.BlockSpec((1, tq, hpg * VD), lambda bi, h, i: (bi, i, h)),
        out_shape=jax.ShapeDtypeStruct((b, t, H_C * VD), BF16),
        compiler_params=_cparams(("parallel", "parallel", "arbitrary")),
        name="mla",
    )(qn, qr, kv, kr)


def _ffn_body(x_ref, g_ref, wg_ref, wv_ref, cwg_ref, cwv_ref, cbg_ref, cbv_ref, wd_ref, pg_ref, pv_ref, gn_ref,
              o_ref, hn_ref, sg_ref, sv_ref, h_scr, acc_scr, ug_buf, uv_buf, cg_scr, cv_scr, *, seg, carried):
    i = pl.program_id(0)
    f = pl.program_id(1)
    tm = x_ref.shape[0]

    @pl.when(f == 0)
    def _():
        x = x_ref[...]
        y = x * lax.rsqrt(jnp.mean(x * x, axis=-1, keepdims=True) + EPS)
        h_scr[...] = (y * g_ref[...]).astype(BF16)
        acc_scr[...] = x

    h = h_scr[...]
    halves = ((wg_ref, cwg_ref, cbg_ref, pg_ref, sg_ref, ug_buf, cg_scr),
              (wv_ref, cwv_ref, cbv_ref, pv_ref, sv_ref, uv_buf, cv_scr))
    ys = []
    for w_ref, cw_ref, cb_ref, p_ref, s_ref, buf, c_scr in halves:
        u = jnp.dot(h, w_ref[...], preferred_element_type=F32)
        cw = cw_ref[...]
        parts = []
        for sgi in range(tm // seg):
            us = u[sgi * seg:(sgi + 1) * seg]
            if carried:
                head = jnp.where(i == 0, p_ref[0], c_scr[f])
            else:
                head = p_ref[sgi]
            buf[sgi, 0:SUBLANES] = head
            buf[sgi, SUBLANES:SUBLANES + seg] = us
            y = (cw[2:3] * us + cw[1:2] * buf[sgi, SUBLANES - 1:SUBLANES - 1 + seg]
                 + cw[0:1] * buf[sgi, SUBLANES - 2:SUBLANES - 2 + seg] + cb_ref[...])
            parts.append(y)
            last = us[seg - SUBLANES:seg]
            s_ref[sgi] = last
            if carried:
                c_scr[f] = last
        ys.append(parts[0] if len(parts) == 1 else jnp.concatenate(parts, axis=0))
    yg, yv = ys
    act = (yg / (1.0 + jnp.exp(-yg))) * yv
    acc_scr[...] += jnp.dot(act.astype(BF16), wd_ref[...], preferred_element_type=F32)

    @pl.when(f == pl.num_programs(1) - 1)
    def _():
        a = acc_scr[...]
        o_ref[...] = a
        y = a * lax.rsqrt(jnp.mean(a * a, axis=-1, keepdims=True) + EPS)
        hn_ref[...] = (y * gn_ref[...]).astype(hn_ref.dtype)


def _ffn(x, g, w_up, conv_w, conv_b, w_down, prev8, g_next, next_dtype, *, seg, carried, tm, tf):
    m, d = x.shape
    nf = D_FF // tf
    nseg = tm // seg
    assert m % tm == 0 and D_FF % tf == 0 and tm % seg == 0
    sidx = (lambda i: 0) if carried else (lambda i: i)
    body = functools.partial(_ffn_body, seg=seg, carried=carried)
    cb = conv_b.reshape(1, 2 * D_FF)
    return pl.pallas_call(
        body,
        grid=(m // tm, nf),
        in_specs=[pl.BlockSpec((tm, d), lambda i, f: (i, 0)),
                  pl.BlockSpec((1, d), lambda i, f: (0, 0)),
                  pl.BlockSpec((d, tf), lambda i, f: (0, f)),
                  pl.BlockSpec((d, tf), lambda i, f: (0, nf + f)),
                  pl.BlockSpec((CONV_W, tf), lambda i, f: (0, f)),
                  pl.BlockSpec((CONV_W, tf), lambda i, f: (0, nf + f)),
                  pl.BlockSpec((1, tf), lambda i, f: (0, f)),
                  pl.BlockSpec((1, tf), lambda i, f: (0, nf + f)),
                  pl.BlockSpec((tf, d), lambda i, f: (f, 0)),
                  pl.BlockSpec((nseg, SUBLANES, tf), lambda i, f: (sidx(i), 0, f)),
                  pl.BlockSpec((nseg, SUBLANES, tf), lambda i, f: (sidx(i), 0, nf + f)),
                  pl.BlockSpec((1, d), lambda i, f: (0, 0))],
        out_specs=[pl.BlockSpec((tm, d), lambda i, f: (i, 0)),
                   pl.BlockSpec((tm, d), lambda i, f: (i, 0)),
                   pl.BlockSpec((nseg, SUBLANES, tf), lambda i, f: (i, 0, f)),
                   pl.BlockSpec((nseg, SUBLANES, tf), lambda i, f: (i, 0, f))],
        out_shape=[jax.ShapeDtypeStruct((m, d), F32),
                   jax.ShapeDtypeStruct((m, d), next_dtype),
                   jax.ShapeDtypeStruct((m // seg, SUBLANES, D_FF), F32),
                   jax.ShapeDtypeStruct((m // seg, SUBLANES, D_FF), F32)],
        scratch_shapes=[pltpu.VMEM((tm, d), BF16),
                        pltpu.VMEM((tm, d), F32),
                        pltpu.VMEM((nseg, SUBLANES + seg, tf), F32),
                        pltpu.VMEM((nseg, SUBLANES + seg, tf), F32),
                        pltpu.VMEM((nf, SUBLANES, tf), F32),
                        pltpu.VMEM((nf, SUBLANES, tf), F32)],
        compiler_params=_cparams(("arbitrary", "arbitrary")),
        name="conv_ffn",
    )(x, g.reshape(1, d), w_up, w_up, conv_w, conv_w, cb, cb, w_down, prev8, prev8, g_next.reshape(1, d))


def _rope_tables(pos, head_dim, rot, lanes_valid=LANES):
    half = rot // 2
    freqs = ROPE_THETA ** (-jnp.arange(half, dtype=F32) / half)
    ang = pos.astype(F32)[:, None] * freqs[None, :]
    cos, sin = jnp.cos(ang), jnp.sin(ang)
    lane = np.arange(LANES)
    d = lane % head_dim
    first = (d < half) & (lane < lanes_valid)
    second = (d >= half) & (d < rot) & (lane < lanes_valid)
    fidx = np.clip(np.where(d < half, d, d - half), 0, half - 1)
    cosl, sinl = cos[:, fidx], sin[:, fidx]
    c = jnp.where(first | second, cosl, 1.0)
    s1 = jnp.where(second, sinl, 0.0)
    s2 = jnp.where(first, -sinl, 0.0)
    return c, s1, s2, half


def _prep_weights(w_in_ab, w_out_ab, w_dqkv, w_uq, w_ukv, w_o_mla, w_up, w_down):
    n_even, n_odd = w_in_ab.shape[0], w_dqkv.shape[0]
    w = {}
    n_a = H_A * DH + 2 * KV_A * DH + H_I * D_IDX
    w["in_a"] = w_in_ab[:, :, :n_a].astype(BF16)
    pad = jnp.zeros((n_even, D_MODEL, LANES - D_IDX - H_I), BF16)
    w["in_b"] = jnp.concatenate([w_in_ab[:, :, n_a:n_a + D_IDX + H_I].astype(BF16), pad,
                                 w_in_ab[:, :, n_a + D_IDX + H_I:].astype(BF16)], axis=-1)
    w["out_ab"] = w_out_ab.astype(BF16)
    w["dqkr"] = jnp.concatenate([w_dqkv.astype(BF16),
                                 jnp.zeros((n_odd, D_MODEL, LANES - ROPE_C), BF16)], axis=-1)
    uq = w_uq.reshape(n_odd, Q_RANK, H_C, NOPE + ROPE_C)
    uqn = uq[..., :NOPE].reshape(n_odd, Q_RANK, H_C * NOPE).astype(BF16)
    uqr = uq[..., NOPE:].astype(BF16)
    uqr = jnp.concatenate([uqr, jnp.zeros_like(uqr)], axis=-1).reshape(n_odd, Q_RANK, H_C * LANES)
    w["uq"] = jnp.concatenate([uqn, uqr], axis=-1)
    w["ukv"] = w_ukv.astype(BF16)
    w["o_mla"] = w_o_mla.astype(BF16)
    w["up"] = w_up.astype(BF16)
    w["down"] = w_down.astype(BF16)
    return w


def _with_past(past, new, s_pad):
    b, _, fdim = new.shape
    parts = [new.astype(BF16)] if past is None else [past.astype(BF16), new.astype(BF16)]
    n = sum(p.shape[1] for p in parts)
    if s_pad > n:
        parts.append(jnp.zeros((b, s_pad - n, fdim), BF16))
    return parts[0] if len(parts) == 1 else jnp.concatenate(parts, axis=1)


def _trunk(x, past, conv_state, w, p, cfg):
    b, t, d = x.shape
    m = b * t
    q0 = 0 if past is None else past[0].shape[2]
    s_valid = q0 + t
    tk = cfg["tk"]
    s_pad = -(-s_valid // tk) * tk
    pos = q0 + jnp.tile(jnp.arange(t), b)
    tab_a = _rope_tables(pos, DH, ROT_A)
    tab_i = _rope_tables(pos, D_IDX, ROT_I)
    tab_ki = _rope_tables(pos, D_IDX, ROT_I, lanes_valid=D_IDX)
    tab_kr = _rope_tables(pos, ROPE_C, ROPE_C, lanes_valid=ROPE_C)

    xf = x.reshape(m, d)
    ab_rows, mla_rows, conv_rows = [], [], []
    sc_ab = DH ** -0.5 * LOG2E
    sc_c = (NOPE + ROPE_C) ** -0.5 * LOG2E
    h = _rmsnorm(xf, p["norm_mix"][0], BF16)
    for l in range(DEPTH):
        if l % 2 == 0:
            e = l // 2
            lp = None if past is None else tuple(c[e] for c in past[:5])
            qa, ka32, ka16, va32, va16, qi = _proj(
                [h], w["in_a"][e],
                [dict(width=H_A * DH, dtypes=(BF16,), rope=0, scale=sc_ab),
                 dict(width=KV_A * DH, dtypes=(F32, BF16), rope=0, split_heads=True),
                 dict(width=KV_A * DH, dtypes=(F32, BF16), split_heads=True),
                 dict(width=H_I * D_IDX, dtypes=(BF16,), rope=1)],
                tabs=(tab_a, tab_i), name="proj_in_a")
            kiwi, qb, kb32, kb16, vb32, vb16 = _proj(
                [h], w["in_b"][e],
                [dict(width=LANES, dtypes=(F32,), rope=0),
                 dict(width=H_B * DH, dtypes=(BF16,), scale=sc_ab),
                 dict(width=H_B * DH, dtypes=(F32, BF16), split_heads=True),
                 dict(width=H_B * DH, dtypes=(F32, BF16), split_heads=True)],
                tabs=(tab_ki,), name="proj_in_b")
            ki32 = kiwi[:, :D_IDX]
            wi = kiwi[:, D_IDX:D_IDX + H_I]
            ab_rows.append((ka32.reshape(b, t, KV_A, DH), va32.reshape(b, t, KV_A, DH),
                            ki32.reshape(b, t, D_IDX), kb32.reshape(b, t, H_B, DH),
                            vb32.reshape(b, t, H_B, DH)))

            def full(idx, new16):
                pst = None if lp is None else lp[idx].reshape(b, q0, -1)
                return _with_past(pst, new16.reshape(b, t, -1), s_pad)

            ki_all = full(2, ki32)
            zk = jnp.zeros_like(ki_all)
            ki2 = jnp.concatenate([ki_all, zk, zk, ki_all], axis=-1)
            oa = _dsa(qa.reshape(b, t, -1), qi.reshape(b, t, -1), wi.reshape(b, t, H_I), full(0, ka16),
                      full(1, va16), ki2, tq=cfg["tq_a"], tk=tk, s_valid=s_valid, q0=q0)
            ob = _stick_breaking(qb.reshape(b, t, -1), full(3, kb16), full(4, vb16),
                                 tq=cfg["tq_b"], tk=cfg["tk_b"], q0=q0, hpg=cfg["hpg_b"], nchunk=cfg["nc_b"])
            (xf,) = _proj([oa.reshape(m, -1), ob.reshape(m, -1)], w["out_ab"][e],
                          [dict(width=D_MODEL, dtypes=(F32,))], resid=xf, name="proj_out_ab")
        else:
            od = l // 2
            gains = jnp.concatenate([p["g_q"][od], p["g_kv"][od]])
            cq16, ckv32, ckv16, kr32w, kr16w = _proj(
                [h], w["dqkr"][od],
                [dict(width=Q_RANK, dtypes=(BF16,), gain_off=0),
                 dict(width=KV_RANK, dtypes=(F32, BF16), gain_off=Q_RANK),
                 dict(width=LANES, dtypes=(F32, BF16), rope=0)],
                tabs=(tab_kr,), gain=gains, name="proj_dqkr")
            mla_rows.append((ckv32.reshape(b, t, KV_RANK), kr32w[:, :ROPE_C].reshape(b, t, ROPE_C)))
            qn, qr = _proj([cq16], w["uq"][od],
                           [dict(width=H_C * NOPE, dtypes=(BF16,), scale=sc_c),
                            dict(width=H_C * LANES, dtypes=(BF16,), rope=0, scale=sc_c)],
                           tabs=(tab_kr,), name="proj_uq")
            pc = None if past is None else past[5][od]
            pk = None if past is None else jnp.pad(past[6][od], ((0, 0), (0, 0), (0, LANES - ROPE_C)))
            c_all = _with_past(pc, ckv16.reshape(b, t, KV_RANK), s_pad)
            kr_all = _with_past(pk, kr16w.reshape(b, t, LANES), s_pad)
            (kv,) = _proj([c_all.reshape(b * s_pad, KV_RANK)], w["ukv"][od],
                          [dict(width=H_C * (NOPE + VD), dtypes=(BF16,))], name="proj_ukv")
            o = _mla(qn.reshape(b, t, -1), qr.reshape(b, t, -1), kv.reshape(b, s_pad, -1), kr_all,
                     tq=cfg["tq_c"], tk=cfg["tk_c"], s_valid=s_valid, q0=q0, hpg=cfg["hpg_c"],
                     nchunk=cfg["nc_c"])
            (xf,) = _proj([o.reshape(m, -1)], w["o_mla"][od], [dict(width=D_MODEL, dtypes=(F32,))],
                          resid=xf, name="proj_o_mla")

        if conv_state is None:
            prev8 = jnp.zeros((b, SUBLANES, 2 * D_FF), F32)
        else:
            prev8 = jnp.concatenate([jnp.zeros((b, SUBLANES - (CONV_W - 1), 2 * D_FF), F32),
                                     conv_state[l]], axis=1)
        final = l == DEPTH - 1
        g_next = p["norm_final"] if final else p["norm_mix"][l + 1]
        xf, h, sg, sv = _ffn(xf, p["norm_ffn"][l], w["up"][l], p["conv_w"][l], p["conv_b"][l], w["down"][l],
                             prev8, g_next, F32 if final else BF16, seg=cfg["seg"], carried=cfg["carried"],
                             tm=cfg["tm_ffn"], tf=cfg["tf"])
        last = [s8.reshape(b, -1, SUBLANES, D_FF)[:, -1, SUBLANES - (CONV_W - 1):] for s8 in (sg, sv)]
        conv_rows.append(jnp.concatenate(last, axis=-1))

    y = h.reshape(b, t, d)
    new_ab = [jnp.stack([r[i] for r in ab_rows]) for i in range(5)]
    new_mla = [jnp.stack([r[i] for r in mla_rows]) for i in range(2)]
    return y, new_ab + new_mla + [jnp.stack(conv_rows)]


def kernel(x_prompt, x_sample, cache_k_a, cache_v_a, cache_idx_k, cache_k_b, cache_v_b, cache_ckv, cache_krope, state_conv, norm_mix, norm_ffn, norm_final, w_in_ab, w_out_ab, w_dqkv, g_q, g_kv, w_uq, w_ukv, w_o_mla, w_up, conv_w, conv_b, w_down):
    w = _prep_weights(w_in_ab, w_out_ab, w_dqkv, w_uq, w_ukv, w_o_mla, w_up, w_down)
    p = dict(norm_mix=norm_mix, norm_ffn=norm_ffn, norm_final=norm_final, g_q=g_q, g_kv=g_kv,
             conv_w=conv_w, conv_b=conv_b)
    cfg_p = dict(tk=512, tq_a=128, tq_b=256, tk_b=256, hpg_b=4, nc_b=1, tq_c=256, tk_c=512, hpg_c=4, nc_c=1,
                 seg=512, carried=True, tm_ffn=512, tf=512)
    t_s = x_sample.shape[1]
    cfg_s = dict(tk=256, tq_a=t_s, tq_b=t_s, tk_b=256, hpg_b=H_B, nc_b=1, tq_c=t_s, tk_c=256, hpg_c=8, nc_c=1,
                 seg=t_s, carried=False, tm_ffn=x_sample.shape[0] * t_s, tf=512)
    y_p, st_p = _trunk(x_prompt, None, None, w, p, cfg_p)
    past = (cache_k_a, cache_v_a, cache_idx_k, cache_k_b, cache_v_b, cache_ckv, cache_krope)
    y_s, st_s = _trunk(x_sample, past, state_conv, w, p, cfg_s)
    return (y_p, y_s, *st_p, *st_s)
```

```python
import functools

import numpy as np
import jax
import jax.numpy as jnp
from jax import lax
from jax.experimental import pallas as pl
from jax.experimental.pallas import tpu as pltpu

F32 = jnp.float32
BF16 = jnp.bfloat16

D_MODEL = 2048
DEPTH = 4
CHUNK = 64
ROPE_THETA = 500000.0
EPS = 1e-6
H_A, KV_A, DH = 8, 2, 128
ROT_A = DH // 4
H_I, D_IDX = 16, 64
ROT_I = D_IDX // 4
TOPK_MAX = 256
H_B = 8
H_C, Q_RANK, KV_RANK, NOPE, ROPE_C, VD = 16, 512, 512, 128, 64, 128
D_FF = 5632
CONV_W = 3

LANES = 128
SUBLANES = 8
VMEM_LIMIT = 56 * 1024 * 1024
NEG = -1e30
INT_MIN = np.int32(-2 ** 31)
INT_MAX = np.int32(2 ** 31 - 1)
LOG2E = float(np.log2(np.e))


def _cparams(sem):
    return pltpu.CompilerParams(dimension_semantics=sem, vmem_limit_bytes=VMEM_LIMIT)


def _dot_nt(a, b):
    return lax.dot_general(a, b, (((1,), (1,)), ((), ())), preferred_element_type=F32)


def _rms_body(x_ref, g_ref, o_ref):
    x = x_ref[...]
    y = x * lax.rsqrt(jnp.mean(x * x, axis=-1, keepdims=True) + EPS)
    o_ref[...] = (y * g_ref[...]).astype(o_ref.dtype)


def _rmsnorm(x, g, out_dtype):
    m, d = x.shape
    tm = min(m, 512)
    return pl.pallas_call(
        _rms_body,
        grid=(m // tm,),
        in_specs=[pl.BlockSpec((tm, d), lambda i: (i, 0)),
                  pl.BlockSpec((1, d), lambda i: (0, 0))],
        out_specs=pl.BlockSpec((tm, d), lambda i: (i, 0)),
        out_shape=jax.ShapeDtypeStruct((m, d), out_dtype),
        compiler_params=_cparams(("parallel",)),
        name="rmsnorm",
    )(x, g.reshape(1, d))


PROJ_CHUNK = 512


def _proj_body(*refs, nx, groups, tab_halves, has_gain, has_resid):
    it = iter(refs)
    x_refs = [next(it) for _ in range(nx)]
    w_ref = next(it)
    tabs = [(next(it), next(it), next(it)) for _ in tab_halves]
    g_ref = next(it) if has_gain else None
    r_ref = next(it) if has_resid else None
    outs = list(it)

    xs = [r[...] for r in x_refs]
    x = xs[0] if nx == 1 else jnp.concatenate(xs, axis=1)
    accs = []
    c0 = 0
    for grp in groups:
        width = grp["width"]
        chunk = min(width, PROJ_CHUNK)
        for cc in range(0, width, chunk):
            accs.append(jnp.dot(x, w_ref[:, c0 + cc:c0 + cc + chunk], preferred_element_type=F32))
        c0 += width
    accs = iter(accs)
    oi = 0
    for grp in groups:
        width = grp["width"]
        chunk = min(width, PROJ_CHUNK)
        for cc in range(0, width, chunk):
            acc = next(accs)
            if grp.get("gain_off") is not None:
                assert chunk == width
                go = grp["gain_off"]
                acc = (acc * lax.rsqrt(jnp.mean(acc * acc, axis=-1, keepdims=True) + EPS)
                       * g_ref[:, go:go + width])
            if grp.get("rope") is not None:
                c_ref, s1_ref, s2_ref = tabs[grp["rope"]]
                half = tab_halves[grp["rope"]]
                c, s1, s2 = c_ref[...], s1_ref[...], s2_ref[...]
                parts = []
                for gi in range(chunk // LANES):
                    xg = acc[:, gi * LANES:(gi + 1) * LANES]
                    parts.append(xg * c + pltpu.roll(xg, half, 1) * s1
                                 + pltpu.roll(xg, LANES - half, 1) * s2)
                acc = parts[0] if len(parts) == 1 else jnp.concatenate(parts, axis=1)
            if grp.get("scale") is not None:
                acc = acc * grp["scale"]
            if has_resid:
                acc = acc + r_ref[:, cc:cc + chunk]
            for k in range(len(grp["dtypes"])):
                o = outs[oi + k]
                if len(o.shape) == 3:
                    for gi in range(chunk // LANES):
                        o[:, cc // LANES + gi, :] = acc[:, gi * LANES:(gi + 1) * LANES].astype(o.dtype)
                else:
                    o[:, cc:cc + chunk] = acc.astype(o.dtype)
        oi += len(grp["dtypes"])


def _proj(xs, w, groups, *, tabs=(), gain=None, resid=None, name="proj"):
    m = xs[0].shape[0]
    k, n = w.shape
    tm = min(m, 512)
    assert m % tm == 0 and n == sum(g["width"] for g in groups) and k == sum(x.shape[1] for x in xs)
    assert resid is None or len(groups) == 1
    in_specs = [pl.BlockSpec((tm, x.shape[1]), lambda i: (i, 0)) for x in xs]
    in_specs.append(pl.BlockSpec((k, n), lambda i: (0, 0)))
    args = list(xs) + [w]
    for c, s1, s2, _ in tabs:
        for t in (c, s1, s2):
            in_specs.append(pl.BlockSpec((tm, LANES), lambda i: (i, 0)))
            args.append(t)
    if gain is not None:
        in_specs.append(pl.BlockSpec((1, gain.shape[0]), lambda i: (0, 0)))
        args.append(gain.reshape(1, -1))
    if resid is not None:
        in_specs.append(pl.BlockSpec((tm, n), lambda i: (i, 0)))
        args.append(resid)
    out_specs, out_shape = [], []
    for g in groups:
        for dt in g["dtypes"]:
            if g.get("split_heads") and dt == F32:
                nh = g["width"] // LANES
                out_specs.append(pl.BlockSpec((tm, nh, LANES), lambda i: (i, 0, 0)))
                out_shape.append(jax.ShapeDtypeStruct((m, nh, LANES), dt))
            else:
                out_specs.append(pl.BlockSpec((tm, g["width"]), lambda i: (i, 0)))
                out_shape.append(jax.ShapeDtypeStruct((m, g["width"]), dt))
    body = functools.partial(_proj_body, nx=len(xs), groups=groups, tab_halves=[t[3] for t in tabs],
                             has_gain=gain is not None, has_resid=resid is not None)
    return pl.pallas_call(
        body,
        grid=(m // tm,),
        in_specs=in_specs,
        out_specs=out_specs,
        out_shape=out_shape,
        compiler_params=_cparams(("parallel",)),
        name=name,
    )(*args)


def _dsa_body(qa_ref, qi_ref, wi_ref, ka_ref, va_ref, ki_ref, tri_ref, o_ref,
              key_scr, qs_scr, m_scr, l_scr, acc_scr, *, tq, tk, s_valid, q0, topk):
    i = pl.program_id(1)
    q_first = q0 + i * tq
    q_last = q_first + tq - 1
    adm_end = jnp.minimum((q_last // CHUNK + 1) * CHUNK, s_valid)
    nkb = (adm_end + tk - 1) // tk
    q_pos = q_first + lax.broadcasted_iota(jnp.int32, (tq, tk), 0)
    q_lim = jnp.minimum(((q_pos >> 6) + 1) * CHUNK, s_valid)
    col = lax.broadcasted_iota(jnp.int32, (tq, tk), 1)
    rep = H_A // KV_A

    w = wi_ref[0] * ((H_I * D_IDX) ** -0.5)
    pg = 4
    pairs = LANES // D_IDX
    qi = qi_ref[0]
    q_groups = [jnp.concatenate([qi[:, (g * pg + pp) * LANES:(g * pg + pp + 1) * LANES]
                                 for pp in range(pg)], axis=0) for g in range(H_I // (pairs * pg))]

    def to_key(x):
        bits = pltpu.bitcast(x, jnp.int32)
        return bits ^ ((bits >> 31) & INT_MAX)

    def p1(j, c):
        k0 = pl.multiple_of(j * tk, tk)
        sc = jnp.zeros((tq, tk), F32)
        for g, qg in enumerate(q_groups):
            for half in range(pairs):
                rel = _dot_nt(qg, ki_ref[0, pl.ds(k0, tk), half * LANES:(half + 1) * LANES])
                for pp in range(pg):
                    h = (g * pg + pp) * pairs + half
                    sc = sc + jnp.maximum(rel[pp * tq:(pp + 1) * tq], 0.0) * w[:, h:h + 1]
        key_scr[:, pl.ds(k0, tk)] = jnp.where(k0 + col < q_lim, to_key(sc), INT_MIN)
        return c

    lax.fori_loop(0, nkb, p1, 0)

    def count_ge(v):
        def cb(j, c):
            k0 = pl.multiple_of(j * tk, tk)
            ge = jnp.where(key_scr[:, pl.ds(k0, tk)] >= v, 1.0, 0.0)
            part = ge[:, 0:LANES]
            for cc in range(1, tk // LANES):
                part = part + ge[:, cc * LANES:(cc + 1) * LANES]
            return c + part
        c = lax.fori_loop(0, nkb, cb, jnp.zeros((tq, LANES), F32))
        return jnp.sum(c, axis=1, keepdims=True)

    def bis(_, lohi):
        lo, hi = lohi
        mid = (lo >> 1) + (hi >> 1) + (lo & hi & 1)
        ok = count_ge(mid) >= float(topk)
        return jnp.where(ok, mid, lo), jnp.where(ok, hi, mid)

    thr, _ = lax.fori_loop(0, 32, bis, (jnp.full((tq, 1), INT_MIN, jnp.int32),
                                        jnp.full((tq, 1), INT_MAX, jnp.int32)))
    need = float(topk) - count_ge(thr + 1)

    qa = qa_ref[0]
    for g in range(KV_A):
        qs_scr[g] = jnp.concatenate(
            [qa[:, (g * rep + r) * DH:(g * rep + r + 1) * DH] for r in range(rep)], axis=0)
    m_scr[...] = jnp.full(m_scr.shape, NEG, F32)
    l_scr[...] = jnp.zeros(l_scr.shape, F32)
    acc_scr[...] = jnp.zeros(acc_scr.shape, F32)

    def p3(j, eqc):
        k0 = pl.multiple_of(j * tk, tk)
        key = key_scr[:, pl.ds(k0, tk)]
        eq = key == thr
        eqf = jnp.where(eq, 1.0, 0.0)
        prefix = jnp.dot(eqf.astype(BF16), tri_ref[...], preferred_element_type=F32) + eqc
        bias = jnp.where(key > thr, 0.0, jnp.where(eq, jnp.where(prefix < need, 0.0, NEG), NEG))
        bias = jnp.where(key == INT_MIN, NEG, bias)
        eqc = eqc + jnp.sum(eqf, axis=1, keepdims=True)
        for g in range(KV_A):
            kg = ka_ref[0, pl.ds(k0, tk), g * DH:(g + 1) * DH]
            vg = va_ref[0, pl.ds(k0, tk), g * DH:(g + 1) * DH]
            s = _dot_nt(qs_scr[g], kg)
            ps, alphas = [], []
            for r in range(rep):
                h = g * rep + r
                sr = s[r * tq:(r + 1) * tq] + bias
                m_prev = m_scr[h]
                m_new = jnp.maximum(m_prev, jnp.max(sr, axis=1, keepdims=True))
                alpha = jnp.exp2(m_prev - m_new)
                p = jnp.exp2(sr - jnp.concatenate([m_new] * (tk // LANES), axis=1))
                l_scr[h] = alpha * l_scr[h] + jnp.sum(p, axis=1, keepdims=True)
                m_scr[h] = m_new
                ps.append(p.astype(BF16))
                alphas.append(alpha)
            for r in range(rep):
                h = g * rep + r
                acc_scr[h] = acc_scr[h] * alphas[r] + jnp.dot(ps[r], vg, preferred_element_type=F32)
        return eqc

    lax.fori_loop(0, nkb, p3, jnp.zeros((tq, 1), F32))
    for h in range(H_A):
        o_ref[0, :, h * DH:(h + 1) * DH] = (acc_scr[h] / l_scr[h]).astype(o_ref.dtype)


def _dsa(qa, qi, wi, ka, va, ki, *, tq, tk, s_valid, q0):
    b, t, _ = qa.shape
    s_pad = ka.shape[1]
    assert s_pad % tk == 0 and t % tq == 0 and tk >= TOPK_MAX
    topk = min(TOPK_MAX, s_valid // 4)
    tri = jnp.asarray(np.triu(np.ones((tk, tk), np.float32), 1), BF16)
    body = functools.partial(_dsa_body, tq=tq, tk=tk, s_valid=s_valid, q0=q0, topk=topk)
    rep = H_A // KV_A
    return pl.pallas_call(
        body,
        grid=(b, t // tq),
        in_specs=[pl.BlockSpec((1, tq, H_A * DH), lambda bi, i: (bi, i, 0)),
                  pl.BlockSpec((1, tq, H_I * D_IDX), lambda bi, i: (bi, i, 0)),
                  pl.BlockSpec((1, tq, H_I), lambda bi, i: (bi, i, 0)),
                  pl.BlockSpec((1, s_pad, KV_A * DH), lambda bi, i: (bi, 0, 0)),
                  pl.BlockSpec((1, s_pad, KV_A * DH), lambda bi, i: (bi, 0, 0)),
                  pl.BlockSpec((1, s_pad, 2 * LANES), lambda bi, i: (bi, 0, 0)),
                  pl.BlockSpec((tk, tk), lambda bi, i: (0, 0))],
        out_specs=pl.BlockSpec((1, tq, H_A * DH), lambda bi, i: (bi, i, 0)),
        out_shape=jax.ShapeDtypeStruct((b, t, H_A * DH), BF16),
        scratch_shapes=[pltpu.VMEM((tq, s_pad), jnp.int32),
                        pltpu.VMEM((KV_A, rep * tq, DH), BF16),
                        pltpu.VMEM((H_A, tq, LANES), F32),
                        pltpu.VMEM((H_A, tq, LANES), F32),
                        pltpu.VMEM((H_A, tq, DH), F32)],
        compiler_params=_cparams(("parallel", "arbitrary")),
        name="dsa",
    )(qa, qi, wi, ka, va, ki, tri)


def _sb_body(q_ref, k_ref, v_ref, low_ref, o_ref, *, tq, tk, s_pad, q0, hpg, nchunk):
    i = pl.program_id(2)
    tqc = tq // nchunk
    q_first = q0 + i * tq
    q_last = q_first + tq - 1
    nkb = jnp.minimum((q_last + tk - 1) // tk, s_pad // tk)
    n_full = jnp.minimum(q_first // tk, nkb)
    row = lax.broadcasted_iota(jnp.int32, (tqc, tk), 0)
    col = lax.broadcasted_iota(jnp.int32, (tqc, tk), 1)
    chains = [(g, c) for g in range(hpg) for c in range(nchunk)]

    def step(j, carry, masked):
        k0 = pl.multiple_of(j * tk, tk)
        zs = [_dot_nt(q_ref[0, c * tqc:(c + 1) * tqc, g * DH:(g + 1) * DH],
                      k_ref[0, pl.ds(k0, tk), g * DH:(g + 1) * DH]) for g, c in chains]
        lbs, lks, cats, causals = [], [], [], []
        for (g, c), z in zip(chains, zs):
            lb = jnp.minimum(z, 0.0) - jnp.log(1.0 + jnp.exp2(-jnp.abs(z))) * LOG2E
            lk = lb - z
            causal = None
            if masked:
                causal = k0 + col < q_first + c * tqc + row
                lk = jnp.where(causal, lk, 0.0)
            hi = lk.astype(BF16)
            lo = (lk - hi.astype(F32)).astype(BF16)
            lbs.append(lb)
            lks.append(lk)
            causals.append(causal)
            cats.append(jnp.concatenate([hi, lo], axis=1))
        sufs = [jnp.dot(cat, low_ref[...], preferred_element_type=F32) for cat in cats]
        out = []
        for n, (g, c) in enumerate(chains):
            run, acc = carry[2 * n], carry[2 * n + 1]
            wgt = jnp.exp2(lbs[n] + sufs[n] + run)
            if masked:
                wgt = jnp.where(causals[n], wgt, 0.0)
            acc = acc + jnp.dot(wgt.astype(BF16), v_ref[0, pl.ds(k0, tk), g * DH:(g + 1) * DH],
                                preferred_element_type=F32)
            run = run + jnp.sum(lks[n], axis=1, keepdims=True)
            out += [run, acc]
        return tuple(out)

    carry = []
    for _ in chains:
        carry += [jnp.zeros((tqc, 1), F32), jnp.zeros((tqc, DH), F32)]
    carry = tuple(carry)
    carry = lax.fori_loop(0, nkb - n_full, lambda s, c: step(nkb - 1 - s, c, True), carry)
    carry = lax.fori_loop(0, n_full, lambda s, c: step(n_full - 1 - s, c, False), carry)
    for n, (g, c) in enumerate(chains):
        o_ref[0, c * tqc:(c + 1) * tqc, g * DH:(g + 1) * DH] = carry[2 * n + 1].astype(o_ref.dtype)


def _stick_breaking(qb, kb, vb, *, tq, tk, q0, hpg, nchunk):
    b, t, _ = qb.shape
    s_pad = kb.shape[1]
    assert s_pad % tk == 0 and t % tq == 0 and H_B % hpg == 0 and tq % nchunk == 0
    low = np.tril(np.ones((tk, tk), np.float32), -1)
    low = jnp.asarray(np.concatenate([low, low], axis=0), BF16)
    body = functools.partial(_sb_body, tq=tq, tk=tk, s_pad=s_pad, q0=q0, hpg=hpg, nchunk=nchunk)
    wd = hpg * DH
    return pl.pallas_call(
        body,
        grid=(b, H_B // hpg, t // tq),
        in_specs=[pl.BlockSpec((1, tq, wd), lambda bi, h, i: (bi, i, h)),
                  pl.BlockSpec((1, s_pad, wd), lambda bi, h, i: (bi, 0, h)),
                  pl.BlockSpec((1, s_pad, wd), lambda bi, h, i: (bi, 0, h)),
                  pl.BlockSpec((2 * tk, tk), lambda bi, h, i: (0, 0))],
        out_specs=pl.BlockSpec((1, tq, wd), lambda bi, h, i: (bi, i, h)),
        out_shape=jax.ShapeDtypeStruct((b, t, H_B * DH), BF16),
        compiler_params=_cparams(("parallel", "parallel", "arbitrary")),
        name="stick_breaking",
    )(qb, kb, vb, low)


def _mla_body(qn_ref, qr_ref, kv_ref, kr_ref, o_ref, *, tq, tk, s_valid, q0, hpg, nchunk):
    i = pl.program_id(2)
    tqc = tq // nchunk
    q_first = q0 + i * tq
    q_last = q_first + tq - 1
    lim_first = jnp.minimum((q_first // CHUNK + 1) * CHUNK, s_valid)
    lim_last = jnp.minimum((q_last // CHUNK + 1) * CHUNK, s_valid)
    nkb = (lim_last + tk - 1) // tk
    n_full = lim_first // tk
    row = lax.broadcasted_iota(jnp.int32, (tqc, tk), 0)
    col = lax.broadcasted_iota(jnp.int32, (tqc, tk), 1)
    chains = [(g, c) for g in range(hpg) for c in range(nchunk)]
    hw = NOPE + VD

    def step(j, carry, masked):
        k0 = pl.multiple_of(j * tk, tk)
        kr = kr_ref[0, pl.ds(k0, tk), :]
        ss = []
        for g, c in chains:
            rows = slice(c * tqc, (c + 1) * tqc)
            qc = jnp.concatenate([qn_ref[0, rows, g * NOPE:(g + 1) * NOPE],
                                  qr_ref[0, rows, g * LANES:(g + 1) * LANES]], axis=1)
            kc = jnp.concatenate([kv_ref[0, pl.ds(k0, tk), g * hw:g * hw + NOPE], kr], axis=1)
            ss.append(_dot_nt(qc, kc))
        ps, stats = [], []
        for n, (g, c) in enumerate(chains):
            m_prev = carry[2 * n]
            s = ss[n]
            if masked:
                lim = jnp.minimum((((q_first + c * tqc + row) >> 6) + 1) * CHUNK, s_valid)
                s = jnp.where(k0 + col < lim, s, NEG)
            m_new = jnp.maximum(m_prev, jnp.max(s, axis=1, keepdims=True))
            alpha = jnp.exp2(m_prev - m_new)
            p = jnp.exp2(s - jnp.concatenate([m_new] * (tk // LANES), axis=1))
            stats.append((m_new, alpha))
            ps.append(p.astype(BF16))
        out = []
        for n, (g, c) in enumerate(chains):
            m_new, alpha = stats[n]
            vb = jnp.concatenate([kv_ref[0, pl.ds(k0, tk), g * hw + NOPE:(g + 1) * hw], ones_k], axis=1)
            acc = (carry[2 * n + 1] * jnp.concatenate([alpha, alpha], axis=1)
                   + jnp.dot(ps[n], vb, preferred_element_type=F32))
            out += [m_new, acc]
        return tuple(out)

    ones_k = jnp.ones((tk, LANES), BF16)
    carry = []
    for _ in chains:
        carry += [jnp.full((tqc, LANES), NEG, F32), jnp.zeros((tqc, VD + LANES), F32)]
    carry = tuple(carry)
    carry = lax.fori_loop(0, n_full, lambda j, c: step(j, c, False), carry)
    carry = lax.fori_loop(n_full, nkb, lambda j, c: step(j, c, True), carry)
    for n, (g, c) in enumerate(chains):
        acc = carry[2 * n + 1]
        o_ref[0, c * tqc:(c + 1) * tqc, g * VD:(g + 1) * VD] = (acc[:, :VD] / acc[:, VD:]).astype(o_ref.dtype)


def _mla(qn, qr, kv, kr, *, tq, tk, s_valid, q0, hpg, nchunk):
    b, t, _ = qn.shape
    s_pad = kv.shape[1]
    assert s_pad % tk == 0 and t % tq == 0 and H_C % hpg == 0 and tq % nchunk == 0
    body = functools.partial(_mla_body, tq=tq, tk=tk, s_valid=s_valid, q0=q0, hpg=hpg, nchunk=nchunk)
    return pl.pallas_call(
        body,
        grid=(b, H_C // hpg, t // tq),
        in_specs=[pl.BlockSpec((1, tq, hpg * NOPE), lambda bi, h, i: (bi, i, h)),
                  pl.BlockSpec((1, tq, hpg * LANES), lambda bi, h, i: (bi, i, h)),
                  pl.BlockSpec((1, s_pad, hpg * (NOPE + VD)), lambda bi, h, i: (bi, 0, h)),
                  pl.BlockSpec((1, s_pad, LANES), lambda bi, h, i: (bi, 0, 0))],
        out_specs=pl.BlockSpec((1, tq, hpg * VD), lambda bi, h, i: (bi, i, h)),
        out_shape=jax.ShapeDtypeStruct((b, t, H_C * VD), BF16),
        compiler_params=_cparams(("parallel", "parallel", "arbitrary")),
        name="mla",
    )(qn, qr, kv, kr)


def _ffn_body(x_ref, g_ref, wg_ref, wv_ref, cwg_ref, cwv_ref, cbg_ref, cbv_ref, wd_ref, pg_ref, pv_ref, gn_ref,
              o_ref, hn_ref, sg_ref, sv_ref, h_scr, acc_scr, ug_buf, uv_buf, cg_scr, cv_scr, *, seg, carried):
    i = pl.program_id(0)
    f = pl.program_id(1)
    tm = x_ref.shape[0]

    @pl.when(f == 0)
    def _():
        x = x_ref[...]
        y = x * lax.rsqrt(jnp.mean(x * x, axis=-1, keepdims=True) + EPS)
        h_scr[...] = (y * g_ref[...]).astype(BF16)
        acc_scr[...] = x

    h = h_scr[...]
    halves = ((wg_ref, cwg_ref, cbg_ref, pg_ref, sg_ref, ug_buf, cg_scr),
              (wv_ref, cwv_ref, cbv_ref, pv_ref, sv_ref, uv_buf, cv_scr))
    ys = []
    for w_ref, cw_ref, cb_ref, p_ref, s_ref, buf, c_scr in halves:
        u = jnp.dot(h, w_ref[...], preferred_element_type=F32)
        cw = cw_ref[...]
        parts = []
        for sgi in range(tm // seg):
            us = u[sgi * seg:(sgi + 1) * seg]
            if carried:
                head = jnp.where(i == 0, p_ref[0], c_scr[f])
            else:
                head = p_ref[sgi]
            buf[sgi, 0:SUBLANES] = head
            buf[sgi, SUBLANES:SUBLANES + seg] = us
            y = (cw[2:3] * us + cw[1:2] * buf[sgi, SUBLANES - 1:SUBLANES - 1 + seg]
                 + cw[0:1] * buf[sgi, SUBLANES - 2:SUBLANES - 2 + seg] + cb_ref[...])
            parts.append(y)
            last = us[seg - SUBLANES:seg]
            s_ref[sgi] = last
            if carried:
                c_scr[f] = last
        ys.append(parts[0] if len(parts) == 1 else jnp.concatenate(parts, axis=0))
    yg, yv = ys
    act = (yg / (1.0 + jnp.exp(-yg))) * yv
    acc_scr[...] += jnp.dot(act.astype(BF16), wd_ref[...], preferred_element_type=F32)

    @pl.when(f == pl.num_programs(1) - 1)
    def _():
        a = acc_scr[...]
        o_ref[...] = a
        y = a * lax.rsqrt(jnp.mean(a * a, axis=-1, keepdims=True) + EPS)
        hn_ref[...] = (y * gn_ref[...]).astype(hn_ref.dtype)


def _ffn(x, g, w_up, conv_w, conv_b, w_down, prev8, g_next, next_dtype, *, seg, carried, tm, tf):
    m, d = x.shape
    nf = D_FF // tf
    nseg = tm // seg
    assert m % tm == 0 and D_FF % tf == 0 and tm % seg == 0
    sidx = (lambda i: 0) if carried else (lambda i: i)
    body = functools.partial(_ffn_body, seg=seg, carried=carried)
    cb = conv_b.reshape(1, 2 * D_FF)
    return pl.pallas_call(
        body,
        grid=(m // tm, nf),
        in_specs=[pl.BlockSpec((tm, d), lambda i, f: (i, 0)),
                  pl.BlockSpec((1, d), lambda i, f: (0, 0)),
                  pl.BlockSpec((d, tf), lambda i, f: (0, f)),
                  pl.BlockSpec((d, tf), lambda i, f: (0, nf + f)),
                  pl.BlockSpec((CONV_W, tf), lambda i, f: (0, f)),
                  pl.BlockSpec((CONV_W, tf), lambda i, f: (0, nf + f)),
                  pl.BlockSpec((1, tf), lambda i, f: (0, f)),
                  pl.BlockSpec((1, tf), lambda i, f: (0, nf + f)),
                  pl.BlockSpec((tf, d), lambda i, f: (f, 0)),
                  pl.BlockSpec((nseg, SUBLANES, tf), lambda i, f: (sidx(i), 0, f)),
                  pl.BlockSpec((nseg, SUBLANES, tf), lambda i, f: (sidx(i), 0, nf + f)),
                  pl.BlockSpec((1, d), lambda i, f: (0, 0))],
        out_specs=[pl.BlockSpec((tm, d), lambda i, f: (i, 0)),
                   pl.BlockSpec((tm, d), lambda i, f: (i, 0)),
                   pl.BlockSpec((nseg, SUBLANES, tf), lambda i, f: (i, 0, f)),
                   pl.BlockSpec((nseg, SUBLANES, tf), lambda i, f: (i, 0, f))],
        out_shape=[jax.ShapeDtypeStruct((m, d), F32),
                   jax.ShapeDtypeStruct((m, d), next_dtype),
                   jax.ShapeDtypeStruct((m // seg, SUBLANES, D_FF), F32),
                   jax.ShapeDtypeStruct((m // seg, SUBLANES, D_FF), F32)],
        scratch_shapes=[pltpu.VMEM((tm, d), BF16),
                        pltpu.VMEM((tm, d), F32),
                        pltpu.VMEM((nseg, SUBLANES + seg, tf), F32),
                        pltpu.VMEM((nseg, SUBLANES + seg, tf), F32),
                        pltpu.VMEM((nf, SUBLANES, tf), F32),
                        pltpu.VMEM((nf, SUBLANES, tf), F32)],
        compiler_params=_cparams(("arbitrary", "arbitrary")),
        name="conv_ffn",
    )(x, g.reshape(1, d), w_up, w_up, conv_w, conv_w, cb, cb, w_down, prev8, prev8, g_next.reshape(1, d))


def _rope_tables(pos, head_dim, rot, lanes_valid=LANES):
    half = rot // 2
    freqs = ROPE_THETA ** (-jnp.arange(half, dtype=F32) / half)
    ang = pos.astype(F32)[:, None] * freqs[None, :]
    cos, sin = jnp.cos(ang), jnp.sin(ang)
    lane = np.arange(LANES)
    d = lane % head_dim
    first = (d < half) & (lane < lanes_valid)
    second = (d >= half) & (d < rot) & (lane < lanes_valid)
    fidx = np.clip(np.where(d < half, d, d - half), 0, half - 1)
    cosl, sinl = cos[:, fidx], sin[:, fidx]
    c = jnp.where(first | second, cosl, 1.0)
    s1 = jnp.where(second, sinl, 0.0)
    s2 = jnp.where(first, -sinl, 0.0)
    return c, s1, s2, half


def _prep_weights(w_in_ab, w_out_ab, w_dqkv, w_uq, w_ukv, w_o_mla, w_up, w_down):
    n_even, n_odd = w_in_ab.shape[0], w_dqkv.shape[0]
    w = {}
    n_a = H_A * DH + 2 * KV_A * DH + H_I * D_IDX
    w["in_a"] = w_in_ab[:, :, :n_a].astype(BF16)
    pad = jnp.zeros((n_even, D_MODEL, LANES - D_IDX - H_I), BF16)
    w["in_b"] = jnp.concatenate([w_in_ab[:, :, n_a:n_a + D_IDX + H_I].astype(BF16), pad,
                                 w_in_ab[:, :, n_a + D_IDX + H_I:].astype(BF16)], axis=-1)
    w["out_ab"] = w_out_ab.astype(BF16)
    w["dqkr"] = jnp.concatenate([w_dqkv.astype(BF16),
                                 jnp.zeros((n_odd, D_MODEL, LANES - ROPE_C), BF16)], axis=-1)
    uq = w_uq.reshape(n_odd, Q_RANK, H_C, NOPE + ROPE_C)
    uqn = uq[..., :NOPE].reshape(n_odd, Q_RANK, H_C * NOPE).astype(BF16)
    uqr = uq[..., NOPE:].astype(BF16)
    uqr = jnp.concatenate([uqr, jnp.zeros_like(uqr)], axis=-1).reshape(n_odd, Q_RANK, H_C * LANES)
    w["uq"] = jnp.concatenate([uqn, uqr], axis=-1)
    w["ukv"] = w_ukv.astype(BF16)
    w["o_mla"] = w_o_mla.astype(BF16)
    w["up"] = w_up.astype(BF16)
    w["down"] = w_down.astype(BF16)
    return w


def _with_past(past, new, s_pad):
    b, _, fdim = new.shape
    parts = [new.astype(BF16)] if past is None else [past.astype(BF16), new.astype(BF16)]
    n = sum(p.shape[1] for p in parts)
    if s_pad > n:
        parts.append(jnp.zeros((b, s_pad - n, fdim), BF16))
    return parts[0] if len(parts) == 1 else jnp.concatenate(parts, axis=1)


def _trunk(x, past, conv_state, w, p, cfg):
    b, t, d = x.shape
    m = b * t
    q0 = 0 if past is None else past[0].shape[2]
    s_valid = q0 + t
    tk = cfg["tk"]
    s_pad = -(-s_valid // tk) * tk
    pos = q0 + jnp.tile(jnp.arange(t), b)
    tab_a = _rope_tables(pos, DH, ROT_A)
    tab_i = _rope_tables(pos, D_IDX, ROT_I)
    tab_ki = _rope_tables(pos, D_IDX, ROT_I, lanes_valid=D_IDX)
    tab_kr = _rope_tables(pos, ROPE_C, ROPE_C, lanes_valid=ROPE_C)

    xf = x.reshape(m, d)
    ab_rows, mla_rows, conv_rows = [], [], []
    sc_ab = DH ** -0.5 * LOG2E
    sc_c = (NOPE + ROPE_C) ** -0.5 * LOG2E
    h = _rmsnorm(xf, p["norm_mix"][0], BF16)
    for l in range(DEPTH):
        if l % 2 == 0:
            e = l // 2
            lp = None if past is None else tuple(c[e] for c in past[:5])
            qa, ka32, ka16, va32, va16, qi = _proj(
                [h], w["in_a"][e],
                [dict(width=H_A * DH, dtypes=(BF16,), rope=0, scale=sc_ab),
                 dict(width=KV_A * DH, dtypes=(F32, BF16), rope=0, split_heads=True),
                 dict(width=KV_A * DH, dtypes=(F32, BF16), split_heads=True),
                 dict(width=H_I * D_IDX, dtypes=(BF16,), rope=1)],
                tabs=(tab_a, tab_i), name="proj_in_a")
            kiwi, qb, kb32, kb16, vb32, vb16 = _proj(
                [h], w["in_b"][e],
                [dict(width=LANES, dtypes=(F32,), rope=0),
                 dict(width=H_B * DH, dtypes=(BF16,), scale=sc_ab),
                 dict(width=H_B * DH, dtypes=(F32, BF16), split_heads=True),
                 dict(width=H_B * DH, dtypes=(F32, BF16), split_heads=True)],
                tabs=(tab_ki,), name="proj_in_b")
            ki32 = kiwi[:, :D_IDX]
            wi = kiwi[:, D_IDX:D_IDX + H_I]
            ab_rows.append((ka32.reshape(b, t, KV_A, DH), va32.reshape(b, t, KV_A, DH),
                            ki32.reshape(b, t, D_IDX), kb32.reshape(b, t, H_B, DH),
                            vb32.reshape(b, t, H_B, DH)))

            def full(idx, new16):
                pst = None if lp is None else lp[idx].reshape(b, q0, -1)
                return _with_past(pst, new16.reshape(b, t, -1), s_pad)

            ki_all = full(2, ki32)
            zk = jnp.zeros_like(ki_all)
            ki2 = jnp.concatenate([ki_all, zk, zk, ki_all], axis=-1)
            oa = _dsa(qa.reshape(b, t, -1), qi.reshape(b, t, -1), wi.reshape(b, t, H_I), full(0, ka16),
                      full(1, va16), ki2, tq=cfg["tq_a"], tk=tk, s_valid=s_valid, q0=q0)
            ob = _stick_breaking(qb.reshape(b, t, -1), full(3, kb16), full(4, vb16),
                                 tq=cfg["tq_b"], tk=cfg["tk_b"], q0=q0, hpg=cfg["hpg_b"], nchunk=cfg["nc_b"])
            (xf,) = _proj([oa.reshape(m, -1), ob.reshape(m, -1)], w["out_ab"][e],
                          [dict(width=D_MODEL, dtypes=(F32,))], resid=xf, name="proj_out_ab")
        else:
            od = l // 2
            gains = jnp.concatenate([p["g_q"][od], p["g_kv"][od]])
            cq16, ckv32, ckv16, kr32w, kr16w = _proj(
                [h], w["dqkr"][od],
                [dict(width=Q_RANK, dtypes=(BF16,), gain_off=0),
                 dict(width=KV_RANK, dtypes=(F32, BF16), gain_off=Q_RANK),
                 dict(width=LANES, dtypes=(F32, BF16), rope=0)],
                tabs=(tab_kr,), gain=gains, name="proj_dqkr")
            mla_rows.append((ckv32.reshape(b, t, KV_RANK), kr32w[:, :ROPE_C].reshape(b, t, ROPE_C)))
            qn, qr = _proj([cq16], w["uq"][od],
                           [dict(width=H_C * NOPE, dtypes=(BF16,), scale=sc_c),
                            dict(width=H_C * LANES, dtypes=(BF16,), rope=0, scale=sc_c)],
                           tabs=(tab_kr,), name="proj_uq")
            pc = None if past is None else past[5][od]
            pk = None if past is None else jnp.pad(past[6][od], ((0, 0), (0, 0), (0, LANES - ROPE_C)))
            c_all = _with_past(pc, ckv16.reshape(b, t, KV_RANK), s_pad)
            kr_all = _with_past(pk, kr16w.reshape(b, t, LANES), s_pad)
            (kv,) = _proj([c_all.reshape(b * s_pad, KV_RANK)], w["ukv"][od],
                          [dict(width=H_C * (NOPE + VD), dtypes=(BF16,))], name="proj_ukv")
            o = _mla(qn.reshape(b, t, -1), qr.reshape(b, t, -1), kv.reshape(b, s_pad, -1), kr_all,
                     tq=cfg["tq_c"], tk=cfg["tk_c"], s_valid=s_valid, q0=q0, hpg=cfg["hpg_c"],
                     nchunk=cfg["nc_c"])
            (xf,) = _proj([o.reshape(m, -1)], w["o_mla"][od], [dict(width=D_MODEL, dtypes=(F32,))],
                          resid=xf, name="proj_o_mla")

        if conv_state is None:
            prev8 = jnp.zeros((b, SUBLANES, 2 * D_FF), F32)
        else:
            prev8 = jnp.concatenate([jnp.zeros((b, SUBLANES - (CONV_W - 1), 2 * D_FF), F32),
                                     conv_state[l]], axis=1)
        final = l == DEPTH - 1
        g_next = p["norm_final"] if final else p["norm_mix"][l + 1]
        xf, h, sg, sv = _ffn(xf, p["norm_ffn"][l], w["up"][l], p["conv_w"][l], p["conv_b"][l], w["down"][l],
                             prev8, g_next, F32 if final else BF16, seg=cfg["seg"], carried=cfg["carried"],
                             tm=cfg["tm_ffn"], tf=cfg["tf"])
        last = [s8.reshape(b, -1, SUBLANES, D_FF)[:, -1, SUBLANES - (CONV_W - 1):] for s8 in (sg, sv)]
        conv_rows.append(jnp.concatenate(last, axis=-1))

    y = h.reshape(b, t, d)
    new_ab = [jnp.stack([r[i] for r in ab_rows]) for i in range(5)]
    new_mla = [jnp.stack([r[i] for r in mla_rows]) for i in range(2)]
    return y, new_ab + new_mla + [jnp.stack(conv_rows)]


def kernel(x_prompt, x_sample, cache_k_a, cache_v_a, cache_idx_k, cache_k_b, cache_v_b, cache_ckv, cache_krope, state_conv, norm_mix, norm_ffn, norm_final, w_in_ab, w_out_ab, w_dqkv, g_q, g_kv, w_uq, w_ukv, w_o_mla, w_up, conv_w, conv_b, w_down):
    w = _prep_weights(w_in_ab, w_out_ab, w_dqkv, w_uq, w_ukv, w_o_mla, w_up, w_down)
    p = dict(norm_mix=norm_mix, norm_ffn=norm_ffn, norm_final=norm_final, g_q=g_q, g_kv=g_kv,
             conv_w=conv_w, conv_b=conv_b)
    cfg_p = dict(tk=512, tq_a=128, tq_b=256, tk_b=256, hpg_b=4, nc_b=1, tq_c=512, tk_c=1024, hpg_c=2, nc_c=2,
                 seg=512, carried=True, tm_ffn=512, tf=512)
    t_s = x_sample.shape[1]
    cfg_s = dict(tk=256, tq_a=t_s, tq_b=t_s, tk_b=256, hpg_b=H_B, nc_b=1, tq_c=t_s, tk_c=256, hpg_c=8, nc_c=1,
                 seg=t_s, carried=False, tm_ffn=x_sample.shape[0] * t_s, tf=512)
    y_p, st_p = _trunk(x_prompt, None, None, w, p, cfg_p)
    past = (cache_k_a, cache_v_a, cache_idx_k, cache_k_b, cache_v_b, cache_ckv, cache_krope)
    y_s, st_s = _trunk(x_sample, past, state_conv, w, p, cfg_s)
    return (y_p, y_s, *st_p, *st_s)
```

```python
import functools

import numpy as np
import jax
import jax.numpy as jnp
from jax import lax
from jax.experimental import pallas as pl
from jax.experimental.pallas import tpu as pltpu

F32 = jnp.float32
BF16 = jnp.bfloat16

D_MODEL = 2048
DEPTH = 4
CHUNK = 64
ROPE_THETA = 500000.0
EPS = 1e-6
H_A, KV_A, DH = 8, 2, 128
ROT_A = DH // 4
H_I, D_IDX = 16, 64
ROT_I = D_IDX // 4
TOPK_MAX = 256
H_B = 8
H_C, Q_RANK, KV_RANK, NOPE, ROPE_C, VD = 16, 512, 512, 128, 64, 128
D_FF = 5632
CONV_W = 3

LANES = 128
SUBLANES = 8
VMEM_LIMIT = 56 * 1024 * 1024
NEG = -1e30
INT_MIN = np.int32(-2 ** 31)
INT_MAX = np.int32(2 ** 31 - 1)
LOG2E = float(np.log2(np.e))


def _cparams(sem):
    return pltpu.CompilerParams(dimension_semantics=sem, vmem_limit_bytes=VMEM_LIMIT)


def _dot_nt(a, b):
    return lax.dot_general(a, b, (((1,), (1,)), ((), ())), preferred_element_type=F32)


def _rms_body(x_ref, g_ref, o_ref):
    x = x_ref[...]
    y = x * lax.rsqrt(jnp.mean(x * x, axis=-1, keepdims=True) + EPS)
    o_ref[...] = (y * g_ref[...]).astype(o_ref.dtype)


def _rmsnorm(x, g, out_dtype):
    m, d = x.shape
    tm = min(m, 512)
    return pl.pallas_call(
        _rms_body,
        grid=(m // tm,),
        in_specs=[pl.BlockSpec((tm, d), lambda i: (i, 0)),
                  pl.BlockSpec((1, d), lambda i: (0, 0))],
        out_specs=pl.BlockSpec((tm, d), lambda i: (i, 0)),
        out_shape=jax.ShapeDtypeStruct((m, d), out_dtype),
        compiler_params=_cparams(("parallel",)),
        name="rmsnorm",
    )(x, g.reshape(1, d))


PROJ_CHUNK = 512


def _proj_body(*refs, nx, groups, tab_halves, has_gain, has_resid):
    it = iter(refs)
    x_refs = [next(it) for _ in range(nx)]
    w_ref = next(it)
    tabs = [(next(it), next(it), next(it)) for _ in tab_halves]
    g_ref = next(it) if has_gain else None
    r_ref = next(it) if has_resid else None
    outs = list(it)

    xs = [r[...] for r in x_refs]
    x = xs[0] if nx == 1 else jnp.concatenate(xs, axis=1)
    accs = []
    c0 = 0
    for grp in groups:
        width = grp["width"]
        chunk = min(width, PROJ_CHUNK)
        for cc in range(0, width, chunk):
            accs.append(jnp.dot(x, w_ref[:, c0 + cc:c0 + cc + chunk], preferred_element_type=F32))
        c0 += width
    accs = iter(accs)
    oi = 0
    for grp in groups:
        width = grp["width"]
        chunk = min(width, PROJ_CHUNK)
        for cc in range(0, width, chunk):
            acc = next(accs)
            if grp.get("gain_off") is not None:
                assert chunk == width
                go = grp["gain_off"]
                acc = (acc * lax.rsqrt(jnp.mean(acc * acc, axis=-1, keepdims=True) + EPS)
                       * g_ref[:, go:go + width])
            if grp.get("rope") is not None:
                c_ref, s1_ref, s2_ref = tabs[grp["rope"]]
                half = tab_halves[grp["rope"]]
                c, s1, s2 = c_ref[...], s1_ref[...], s2_ref[...]
                parts = []
                for gi in range(chunk // LANES):
                    xg = acc[:, gi * LANES:(gi + 1) * LANES]
                    parts.append(xg * c + pltpu.roll(xg, half, 1) * s1
                                 + pltpu.roll(xg, LANES - half, 1) * s2)
                acc = parts[0] if len(parts) == 1 else jnp.concatenate(parts, axis=1)
            if grp.get("scale") is not None:
                acc = acc * grp["scale"]
            if has_resid:
                acc = acc + r_ref[:, cc:cc + chunk]
            for k in range(len(grp["dtypes"])):
                o = outs[oi + k]
                if len(o.shape) == 3:
                    for gi in range(chunk // LANES):
                        o[:, cc // LANES + gi, :] = acc[:, gi * LANES:(gi + 1) * LANES].astype(o.dtype)
                else:
                    o[:, cc:cc + chunk] = acc.astype(o.dtype)
        oi += len(grp["dtypes"])


def _proj(xs, w, groups, *, tabs=(), gain=None, resid=None, name="proj"):
    m = xs[0].shape[0]
    k, n = w.shape
    tm = min(m, 512)
    assert m % tm == 0 and n == sum(g["width"] for g in groups) and k == sum(x.shape[1] for x in xs)
    assert resid is None or len(groups) == 1
    in_specs = [pl.BlockSpec((tm, x.shape[1]), lambda i: (i, 0)) for x in xs]
    in_specs.append(pl.BlockSpec((k, n), lambda i: (0, 0)))
    args = list(xs) + [w]
    for c, s1, s2, _ in tabs:
        for t in (c, s1, s2):
            in_specs.append(pl.BlockSpec((tm, LANES), lambda i: (i, 0)))
            args.append(t)
    if gain is not None:
        in_specs.append(pl.BlockSpec((1, gain.shape[0]), lambda i: (0, 0)))
        args.append(gain.reshape(1, -1))
    if resid is not None:
        in_specs.append(pl.BlockSpec((tm, n), lambda i: (i, 0)))
        args.append(resid)
    out_specs, out_shape = [], []
    for g in groups:
        for dt in g["dtypes"]:
            if g.get("split_heads") and dt == F32:
                nh = g["width"] // LANES
                out_specs.append(pl.BlockSpec((tm, nh, LANES), lambda i: (i, 0, 0)))
                out_shape.append(jax.ShapeDtypeStruct((m, nh, LANES), dt))
            else:
                out_specs.append(pl.BlockSpec((tm, g["width"]), lambda i: (i, 0)))
                out_shape.append(jax.ShapeDtypeStruct((m, g["width"]), dt))
    body = functools.partial(_proj_body, nx=len(xs), groups=groups, tab_halves=[t[3] for t in tabs],
                             has_gain=gain is not None, has_resid=resid is not None)
    return pl.pallas_call(
        body,
        grid=(m // tm,),
        in_specs=in_specs,
        out_specs=out_specs,
        out_shape=out_shape,
        compiler_params=_cparams(("parallel",)),
        name=name,
    )(*args)


def _dsa_body(qa_ref, qi_ref, wi_ref, ka_ref, va_ref, ki_ref, tri_ref, o_ref,
              key_scr, qs_scr, m_scr, l_scr, acc_scr, *, tq, tk, s_valid, q0, topk):
    i = pl.program_id(1)
    q_first = q0 + i * tq
    q_last = q_first + tq - 1
    adm_end = jnp.minimum((q_last // CHUNK + 1) * CHUNK, s_valid)
    nkb = (adm_end + tk - 1) // tk
    q_pos = q_first + lax.broadcasted_iota(jnp.int32, (tq, tk), 0)
    q_lim = jnp.minimum(((q_pos >> 6) + 1) * CHUNK, s_valid)
    col = lax.broadcasted_iota(jnp.int32, (tq, tk), 1)
    rep = H_A // KV_A

    w = wi_ref[0] * ((H_I * D_IDX) ** -0.5)
    pg = 4
    pairs = LANES // D_IDX
    qi = qi_ref[0]
    q_groups = [jnp.concatenate([qi[:, (g * pg + pp) * LANES:(g * pg + pp + 1) * LANES]
                                 for pp in range(pg)], axis=0) for g in range(H_I // (pairs * pg))]

    def to_key(x):
        bits = pltpu.bitcast(x, jnp.int32)
        return bits ^ ((bits >> 31) & INT_MAX)

    def p1(j, c):
        k0 = pl.multiple_of(j * tk, tk)
        sc = jnp.zeros((tq, tk), F32)
        for g, qg in enumerate(q_groups):
            for half in range(pairs):
                rel = _dot_nt(qg, ki_ref[0, pl.ds(k0, tk), half * LANES:(half + 1) * LANES])
                for pp in range(pg):
                    h = (g * pg + pp) * pairs + half
                    sc = sc + jnp.maximum(rel[pp * tq:(pp + 1) * tq], 0.0) * w[:, h:h + 1]
        key_scr[:, pl.ds(k0, tk)] = jnp.where(k0 + col < q_lim, to_key(sc), INT_MIN)
        return c

    lax.fori_loop(0, nkb, p1, 0)

    def count_ge(v):
        def cb(j, c):
            k0 = pl.multiple_of(j * tk, tk)
            ge = jnp.where(key_scr[:, pl.ds(k0, tk)] >= v, 1.0, 0.0)
            part = ge[:, 0:LANES]
            for cc in range(1, tk // LANES):
                part = part + ge[:, cc * LANES:(cc + 1) * LANES]
            return c + part
        c = lax.fori_loop(0, nkb, cb, jnp.zeros((tq, LANES), F32))
        return jnp.sum(c, axis=1, keepdims=True)

    def bis(_, lohi):
        lo, hi = lohi
        mid = (lo >> 1) + (hi >> 1) + (lo & hi & 1)
        ok = count_ge(mid) >= float(topk)
        return jnp.where(ok, mid, lo), jnp.where(ok, hi, mid)

    thr, _ = lax.fori_loop(0, 32, bis, (jnp.full((tq, 1), INT_MIN, jnp.int32),
                                        jnp.full((tq, 1), INT_MAX, jnp.int32)))
    need = float(topk) - count_ge(thr + 1)

    qa = qa_ref[0]
    for g in range(KV_A):
        qs_scr[g] = jnp.concatenate(
            [qa[:, (g * rep + r) * DH:(g * rep + r + 1) * DH] for r in range(rep)], axis=0)
    m_scr[...] = jnp.full(m_scr.shape, NEG, F32)
    l_scr[...] = jnp.zeros(l_scr.shape, F32)
    acc_scr[...] = jnp.zeros(acc_scr.shape, F32)

    def p3(j, eqc):
        k0 = pl.multiple_of(j * tk, tk)
        key = key_scr[:, pl.ds(k0, tk)]
        eq = key == thr
        eqf = jnp.where(eq, 1.0, 0.0)
        prefix = jnp.dot(eqf.astype(BF16), tri_ref[...], preferred_element_type=F32) + eqc
        bias = jnp.where(key > thr, 0.0, jnp.where(eq, jnp.where(prefix < need, 0.0, NEG), NEG))
        bias = jnp.where(key == INT_MIN, NEG, bias)
        eqc = eqc + jnp.sum(eqf, axis=1, keepdims=True)
        for g in range(KV_A):
            kg = ka_ref[0, pl.ds(k0, tk), g * DH:(g + 1) * DH]
            vg = va_ref[0, pl.ds(k0, tk), g * DH:(g + 1) * DH]
            s = _dot_nt(qs_scr[g], kg)
            ps, alphas = [], []
            for r in range(rep):
                h = g * rep + r
                sr = s[r * tq:(r + 1) * tq] + bias
                m_prev = m_scr[h]
                m_new = jnp.maximum(m_prev, jnp.max(sr, axis=1, keepdims=True))
                alpha = jnp.exp2(m_prev - m_new)
                p = jnp.exp2(sr - jnp.concatenate([m_new] * (tk // LANES), axis=1))
                l_scr[h] = alpha * l_scr[h] + jnp.sum(p, axis=1, keepdims=True)
                m_scr[h] = m_new
                ps.append(p.astype(BF16))
                alphas.append(alpha)
            for r in range(rep):
                h = g * rep + r
                acc_scr[h] = acc_scr[h] * alphas[r] + jnp.dot(ps[r], vg, preferred_element_type=F32)
        return eqc

    lax.fori_loop(0, nkb, p3, jnp.zeros((tq, 1), F32))
    for h in range(H_A):
        o_ref[0, :, h * DH:(h + 1) * DH] = (acc_scr[h] / l_scr[h]).astype(o_ref.dtype)


def _dsa(qa, qi, wi, ka, va, ki, *, tq, tk, s_valid, q0):
    b, t, _ = qa.shape
    s_pad = ka.shape[1]
    assert s_pad % tk == 0 and t % tq == 0 and tk >= TOPK_MAX
    topk = min(TOPK_MAX, s_valid // 4)
    tri = jnp.asarray(np.triu(np.ones((tk, tk), np.float32), 1), BF16)
    body = functools.partial(_dsa_body, tq=tq, tk=tk, s_valid=s_valid, q0=q0, topk=topk)
    rep = H_A // KV_A
    return pl.pallas_call(
        body,
        grid=(b, t // tq),
        in_specs=[pl.BlockSpec((1, tq, H_A * DH), lambda bi, i: (bi, i, 0)),
                  pl.BlockSpec((1, tq, H_I * D_IDX), lambda bi, i: (bi, i, 0)),
                  pl.BlockSpec((1, tq, H_I), lambda bi, i: (bi, i, 0)),
                  pl.BlockSpec((1, s_pad, KV_A * DH), lambda bi, i: (bi, 0, 0)),
                  pl.BlockSpec((1, s_pad, KV_A * DH), lambda bi, i: (bi, 0, 0)),
                  pl.BlockSpec((1, s_pad, 2 * LANES), lambda bi, i: (bi, 0, 0)),
                  pl.BlockSpec((tk, tk), lambda bi, i: (0, 0))],
        out_specs=pl.BlockSpec((1, tq, H_A * DH), lambda bi, i: (bi, i, 0)),
        out_shape=jax.ShapeDtypeStruct((b, t, H_A * DH), BF16),
        scratch_shapes=[pltpu.VMEM((tq, s_pad), jnp.int32),
                        pltpu.VMEM((KV_A, rep * tq, DH), BF16),
                        pltpu.VMEM((H_A, tq, LANES), F32),
                        pltpu.VMEM((H_A, tq, LANES), F32),
                        pltpu.VMEM((H_A, tq, DH), F32)],
        compiler_params=_cparams(("parallel", "arbitrary")),
        name="dsa",
    )(qa, qi, wi, ka, va, ki, tri)


def _sb_body(q_ref, k_ref, v_ref, low_ref, o_ref, *, tq, tk, s_pad, q0, hpg, nchunk):
    i = pl.program_id(2)
    tqc = tq // nchunk
    q_first = q0 + i * tq
    q_last = q_first + tq - 1
    nkb = jnp.minimum((q_last + tk - 1) // tk, s_pad // tk)
    n_full = jnp.minimum(q_first // tk, nkb)
    row = lax.broadcasted_iota(jnp.int32, (tqc, tk), 0)
    col = lax.broadcasted_iota(jnp.int32, (tqc, tk), 1)
    chains = [(g, c) for g in range(hpg) for c in range(nchunk)]

    def step(j, carry, masked):
        k0 = pl.multiple_of(j * tk, tk)
        zs = [_dot_nt(q_ref[0, c * tqc:(c + 1) * tqc, g * DH:(g + 1) * DH],
                      k_ref[0, pl.ds(k0, tk), g * DH:(g + 1) * DH]) for g, c in chains]
        lbs, lks, cats, causals = [], [], [], []
        for (g, c), z in zip(chains, zs):
            lb = jnp.minimum(z, 0.0) - jnp.log(1.0 + jnp.exp2(-jnp.abs(z))) * LOG2E
            lk = lb - z
            causal = None
            if masked:
                causal = k0 + col < q_first + c * tqc + row
                lk = jnp.where(causal, lk, 0.0)
            hi = lk.astype(BF16)
            lo = (lk - hi.astype(F32)).astype(BF16)
            lbs.append(lb)
            lks.append(lk)
            causals.append(causal)
            cats.append(jnp.concatenate([hi, lo], axis=1))
        sufs = [jnp.dot(cat, low_ref[...], preferred_element_type=F32) for cat in cats]
        out = []
        for n, (g, c) in enumerate(chains):
            run, acc = carry[2 * n], carry[2 * n + 1]
            wgt = jnp.exp2(lbs[n] + sufs[n] + run)
            if masked:
                wgt = jnp.where(causals[n], wgt, 0.0)
            acc = acc + jnp.dot(wgt.astype(BF16), v_ref[0, pl.ds(k0, tk), g * DH:(g + 1) * DH],
                                preferred_element_type=F32)
            run = run + jnp.sum(lks[n], axis=1, keepdims=True)
            out += [run, acc]
        return tuple(out)

    carry = []
    for _ in chains:
        carry += [jnp.zeros((tqc, 1), F32), jnp.zeros((tqc, DH), F32)]
    carry = tuple(carry)
    carry = lax.fori_loop(0, nkb - n_full, lambda s, c: step(nkb - 1 - s, c, True), carry)
    carry = lax.fori_loop(0, n_full, lambda s, c: step(n_full - 1 - s, c, False), carry)
    for n, (g, c) in enumerate(chains):
        o_ref[0, c * tqc:(c + 1) * tqc, g * DH:(g + 1) * DH] = carry[2 * n + 1].astype(o_ref.dtype)


def _stick_breaking(qb, kb, vb, *, tq, tk, q0, hpg, nchunk):
    b, t, _ = qb.shape
    s_pad = kb.shape[1]
    assert s_pad % tk == 0 and t % tq == 0 and H_B % hpg == 0 and tq % nchunk == 0
    low = np.tril(np.ones((tk, tk), np.float32), -1)
    low = jnp.asarray(np.concatenate([low, low], axis=0), BF16)
    body = functools.partial(_sb_body, tq=tq, tk=tk, s_pad=s_pad, q0=q0, hpg=hpg, nchunk=nchunk)
    wd = hpg * DH
    return pl.pallas_call(
        body,
        grid=(b, H_B // hpg, t // tq),
        in_specs=[pl.BlockSpec((1, tq, wd), lambda bi, h, i: (bi, i, h)),
                  pl.BlockSpec((1, s_pad, wd), lambda bi, h, i: (bi, 0, h)),
                  pl.BlockSpec((1, s_pad, wd), lambda bi, h, i: (bi, 0, h)),
                  pl.BlockSpec((2 * tk, tk), lambda bi, h, i: (0, 0))],
        out_specs=pl.BlockSpec((1, tq, wd), lambda bi, h, i: (bi, i, h)),
        out_shape=jax.ShapeDtypeStruct((b, t, H_B * DH), BF16),
        compiler_params=_cparams(("parallel", "parallel", "arbitrary")),
        name="stick_breaking",
    )(qb, kb, vb, low)


def _mla_body(qn_ref, qr_ref, kv_ref, kr_ref, o_ref, *, tq, tk, s_valid, q0, hpg, nchunk):
    i = pl.program_id(2)
    tqc = tq // nchunk
    q_first = q0 + i * tq
    q_last = q_first + tq - 1
    lim_first = jnp.minimum((q_first // CHUNK + 1) * CHUNK, s_valid)
    lim_last = jnp.minimum((q_last // CHUNK + 1) * CHUNK, s_valid)
    nkb = (lim_last + tk - 1) // tk
    n_full = lim_first // tk
    row = lax.broadcasted_iota(jnp.int32, (tqc, tk), 0)
    col = lax.broadcasted_iota(jnp.int32, (tqc, tk), 1)
    chains = [(g, c) for g in range(hpg) for c in range(nchunk)]
    hw = NOPE + VD

    def step(j, carry, masked):
        k0 = pl.multiple_of(j * tk, tk)
        kr = kr_ref[0, pl.ds(k0, tk), :]
        ss = []
        for g, c in chains:
            rows = slice(c * tqc, (c + 1) * tqc)
            qc = jnp.concatenate([qn_ref[0, rows, g * NOPE:(g + 1) * NOPE],
                                  qr_ref[0, rows, g * LANES:(g + 1) * LANES]], axis=1)
            kc = jnp.concatenate([kv_ref[0, pl.ds(k0, tk), g * hw:g * hw + NOPE], kr], axis=1)
            ss.append(_dot_nt(qc, kc))
        ps, stats = [], []
        for n, (g, c) in enumerate(chains):
            m_prev = carry[2 * n]
            s = ss[n]
            if masked:
                lim = jnp.minimum((((q_first + c * tqc + row) >> 6) + 1) * CHUNK, s_valid)
                s = jnp.where(k0 + col < lim, s, NEG)
            m_new = jnp.maximum(m_prev, jnp.max(s, axis=1, keepdims=True))
            alpha = jnp.exp2(m_prev - m_new)
            p = jnp.exp2(s - jnp.concatenate([m_new] * (tk // LANES), axis=1))
            stats.append((m_new, alpha))
            ps.append(p.astype(BF16))
        out = []
        for n, (g, c) in enumerate(chains):
            m_new, alpha = stats[n]
            vb = jnp.concatenate([kv_ref[0, pl.ds(k0, tk), g * hw + NOPE:(g + 1) * hw], ones_k], axis=1)
            acc = (carry[2 * n + 1] * jnp.concatenate([alpha, alpha], axis=1)
                   + jnp.dot(ps[n], vb, preferred_element_type=F32))
            out += [m_new, acc]
        return tuple(out)

    ones_k = jnp.ones((tk, LANES), BF16)
    carry = []
    for _ in chains:
        carry += [jnp.full((tqc, LANES), NEG, F32), jnp.zeros((tqc, VD + LANES), F32)]
    carry = tuple(carry)
    carry = lax.fori_loop(0, n_full, lambda j, c: step(j, c, False), carry)
    carry = lax.fori_loop(n_full, nkb, lambda j, c: step(j, c, True), carry)
    for n, (g, c) in enumerate(chains):
        acc = carry[2 * n + 1]
        o_ref[0, c * tqc:(c + 1) * tqc, g * VD:(g + 1) * VD] = (acc[:, :VD] / acc[:, VD:]).astype(o_ref.dtype)


def _mla(qn, qr, kv, kr, *, tq, tk, s_valid, q0, hpg, nchunk):
    b, t, _ = qn.shape
    s_pad = kv.shape[1]
    assert s_pad % tk == 0 and t % tq == 0 and H_C % hpg == 0 and tq % nchunk == 0
    body = functools.partial(_mla_body, tq=tq, tk=tk, s_valid=s_valid, q0=q0, hpg=hpg, nchunk=nchunk)
    return pl.pallas_call(
        body,
        grid=(b, H_C // hpg, t // tq),
        in_specs=[pl.BlockSpec((1, tq, hpg * NOPE), lambda bi, h, i: (bi, i, h)),
                  pl.BlockSpec((1, tq, hpg * LANES), lambda bi, h, i: (bi, i, h)),
                  pl.BlockSpec((1, s_pad, hpg * (NOPE + VD)), lambda bi, h, i: (bi, 0, h)),
                  pl.BlockSpec((1, s_pad, LANES), lambda bi, h, i: (bi, 0, 0))],
        out_specs=pl.BlockSpec((1, tq, hpg * VD), lambda bi, h, i: (bi, i, h)),
        out_shape=jax.ShapeDtypeStruct((b, t, H_C * VD), BF16),
        compiler_params=_cparams(("parallel", "parallel", "arbitrary")),
        name="mla",
    )(qn, qr, kv, kr)


def _absorb_q_body(qn_ref, qr_ref, wk_ref, qa_ref, qrh_ref):
    qa_ref[0] = _dot_nt(qn_ref[...], wk_ref[...]).astype(qa_ref.dtype)
    qrh_ref[0] = qr_ref[...]


def _absorb_q(qn, qr, w_ukv):
    m = qn.shape[0]
    return pl.pallas_call(
        _absorb_q_body,
        grid=(H_C,),
        in_specs=[pl.BlockSpec((m, NOPE), lambda h: (0, h)),
                  pl.BlockSpec((m, LANES), lambda h: (0, h)),
                  pl.BlockSpec((KV_RANK, NOPE), lambda h: (0, 2 * h))],
        out_specs=[pl.BlockSpec((1, m, KV_RANK), lambda h: (h, 0, 0)),
                   pl.BlockSpec((1, m, LANES), lambda h: (h, 0, 0))],
        out_shape=[jax.ShapeDtypeStruct((H_C, m, KV_RANK), BF16),
                   jax.ShapeDtypeStruct((H_C, m, LANES), BF16)],
        compiler_params=_cparams(("parallel",)),
        name="mla_absorb_q",
    )(qn, qr, w_ukv)


def _mla_latent_body(qa_ref, qr_ref, c_ref, kr_ref, o_ref, *, t, tk, s_valid, q0, nsplit):
    hs = H_C // nsplit
    rows = hs * t
    lim_first = min((q0 // CHUNK + 1) * CHUNK, s_valid)
    lim_last = min(((q0 + t - 1) // CHUNK + 1) * CHUNK, s_valid)
    nkb = -(-lim_last // tk)
    n_full = lim_first // tk
    q_pos = q0 + lax.broadcasted_iota(jnp.int32, (hs, t, tk), 1).reshape(rows, tk)
    lim = jnp.minimum(((q_pos >> 6) + 1) * CHUNK, s_valid)
    col = lax.broadcasted_iota(jnp.int32, (rows, tk), 1)
    ones_k = jnp.ones((tk, LANES), BF16)
    qs = [jnp.concatenate([qa_ref[n * hs:(n + 1) * hs].reshape(rows, KV_RANK),
                           qr_ref[n * hs:(n + 1) * hs].reshape(rows, LANES)], axis=1) for n in range(nsplit)]

    def step(j, carry, masked):
        k0 = pl.multiple_of(j * tk, tk)
        cb = c_ref[0, pl.ds(k0, tk), :]
        kc = jnp.concatenate([cb, kr_ref[0, pl.ds(k0, tk), :]], axis=1)
        vc = jnp.concatenate([cb, ones_k], axis=1)
        ss = [_dot_nt(q, kc) for q in qs]
        ps, stats = [], []
        for n in range(nsplit):
            s = ss[n]
            if masked:
                s = jnp.where(k0 + col < lim, s, NEG)
            m_prev = carry[2 * n]
            m_new = jnp.maximum(m_prev, jnp.max(s, axis=1, keepdims=True))
            alpha = jnp.exp2(m_prev - m_new)
            ps.append(jnp.exp2(s - jnp.concatenate([m_new] * (tk // LANES), axis=1)).astype(BF16))
            stats.append((m_new, alpha))
        out = []
        for n in range(nsplit):
            m_new, alpha = stats[n]
            acc = (carry[2 * n + 1] * jnp.concatenate([alpha] * (KV_RANK // LANES + 1), axis=1)
                   + jnp.dot(ps[n], vc, preferred_element_type=F32))
            out += [m_new, acc]
        return tuple(out)

    carry = []
    for _ in range(nsplit):
        carry += [jnp.full((rows, LANES), NEG, F32), jnp.zeros((rows, KV_RANK + LANES), F32)]
    carry = tuple(carry)
    carry = lax.fori_loop(0, n_full, lambda j, c: step(j, c, False), carry)
    carry = lax.fori_loop(n_full, nkb, lambda j, c: step(j, c, True), carry)
    for n in range(nsplit):
        acc = carry[2 * n + 1]
        lat = acc[:, :KV_RANK] / jnp.concatenate([acc[:, KV_RANK:]] * (KV_RANK // LANES), axis=1)
        o_ref[0, n * hs:(n + 1) * hs] = lat.reshape(hs, t, KV_RANK).astype(o_ref.dtype)


def _mla_latent(qa_hm, qr_hm, c_all, kr_all, *, t, tk, s_valid, q0):
    b, s_pad, _ = c_all.shape
    assert s_pad % tk == 0
    body = functools.partial(_mla_latent_body, t=t, tk=tk, s_valid=s_valid, q0=q0, nsplit=2)
    return pl.pallas_call(
        body,
        grid=(b,),
        in_specs=[pl.BlockSpec((H_C, t, KV_RANK), lambda bi: (0, bi, 0)),
                  pl.BlockSpec((H_C, t, LANES), lambda bi: (0, bi, 0)),
                  pl.BlockSpec((1, s_pad, KV_RANK), lambda bi: (bi, 0, 0)),
                  pl.BlockSpec((1, s_pad, LANES), lambda bi: (bi, 0, 0))],
        out_specs=pl.BlockSpec((1, H_C, t, KV_RANK), lambda bi: (bi, 0, 0, 0)),
        out_shape=jax.ShapeDtypeStruct((b, H_C, t, KV_RANK), BF16),
        compiler_params=_cparams(("parallel",)),
        name="mla_latent",
    )(qa_hm, qr_hm, c_all, kr_all)


def _latent_out_body(x_ref, wv_ref, o_ref):
    nb, _, t, r = x_ref.shape
    o_ref[...] = jnp.dot(x_ref[...].reshape(nb * t, r), wv_ref[...],
                         preferred_element_type=F32).astype(o_ref.dtype)


def _latent_out(lat, w_ukv):
    b, _, t, _ = lat.shape
    return pl.pallas_call(
        _latent_out_body,
        grid=(H_C,),
        in_specs=[pl.BlockSpec((b, 1, t, KV_RANK), lambda h: (0, h, 0, 0)),
                  pl.BlockSpec((KV_RANK, VD), lambda h: (0, 2 * h + 1))],
        out_specs=pl.BlockSpec((b * t, VD), lambda h: (0, h)),
        out_shape=jax.ShapeDtypeStruct((b * t, H_C * VD), BF16),
        compiler_params=_cparams(("parallel",)),
        name="mla_latent_out",
    )(lat, w_ukv)


def _ffn_body(x_ref, g_ref, wg_ref, wv_ref, cwg_ref, cwv_ref, cbg_ref, cbv_ref, wd_ref, pg_ref, pv_ref, gn_ref,
              o_ref, hn_ref, sg_ref, sv_ref, h_scr, acc_scr, ug_buf, uv_buf, cg_scr, cv_scr, *, seg, carried):
    i = pl.program_id(0)
    f = pl.program_id(1)
    tm = x_ref.shape[0]

    @pl.when(f == 0)
    def _():
        x = x_ref[...]
        y = x * lax.rsqrt(jnp.mean(x * x, axis=-1, keepdims=True) + EPS)
        h_scr[...] = (y * g_ref[...]).astype(BF16)
        acc_scr[...] = x

    h = h_scr[...]
    halves = ((wg_ref, cwg_ref, cbg_ref, pg_ref, sg_ref, ug_buf, cg_scr),
              (wv_ref, cwv_ref, cbv_ref, pv_ref, sv_ref, uv_buf, cv_scr))
    ys = []
    for w_ref, cw_ref, cb_ref, p_ref, s_ref, buf, c_scr in halves:
        u = jnp.dot(h, w_ref[...], preferred_element_type=F32)
        cw = cw_ref[...]
        parts = []
        for sgi in range(tm // seg):
            us = u[sgi * seg:(sgi + 1) * seg]
            if carried:
                head = jnp.where(i == 0, p_ref[0], c_scr[f])
            else:
                head = p_ref[sgi]
            buf[sgi, 0:SUBLANES] = head
            buf[sgi, SUBLANES:SUBLANES + seg] = us
            y = (cw[2:3] * us + cw[1:2] * buf[sgi, SUBLANES - 1:SUBLANES - 1 + seg]
                 + cw[0:1] * buf[sgi, SUBLANES - 2:SUBLANES - 2 + seg] + cb_ref[...])
            parts.append(y)
            last = us[seg - SUBLANES:seg]
            s_ref[sgi] = last
            if carried:
                c_scr[f] = last
        ys.append(parts[0] if len(parts) == 1 else jnp.concatenate(parts, axis=0))
    yg, yv = ys
    act = (yg / (1.0 + jnp.exp(-yg))) * yv
    acc_scr[...] += jnp.dot(act.astype(BF16), wd_ref[...], preferred_element_type=F32)

    @pl.when(f == pl.num_programs(1) - 1)
    def _():
        a = acc_scr[...]
        o_ref[...] = a
        y = a * lax.rsqrt(jnp.mean(a * a, axis=-1, keepdims=True) + EPS)
        hn_ref[...] = (y * gn_ref[...]).astype(hn_ref.dtype)


def _ffn(x, g, w_up, conv_w, conv_b, w_down, prev8, g_next, next_dtype, *, seg, carried, tm, tf):
    m, d = x.shape
    nf = D_FF // tf
    nseg = tm // seg
    assert m % tm == 0 and D_FF % tf == 0 and tm % seg == 0
    sidx = (lambda i: 0) if carried else (lambda i: i)
    body = functools.partial(_ffn_body, seg=seg, carried=carried)
    cb = conv_b.reshape(1, 2 * D_FF)
    return pl.pallas_call(
        body,
        grid=(m // tm, nf),
        in_specs=[pl.BlockSpec((tm, d), lambda i, f: (i, 0)),
                  pl.BlockSpec((1, d), lambda i, f: (0, 0)),
                  pl.BlockSpec((d, tf), lambda i, f: (0, f)),
                  pl.BlockSpec((d, tf), lambda i, f: (0, nf + f)),
                  pl.BlockSpec((CONV_W, tf), lambda i, f: (0, f)),
                  pl.BlockSpec((CONV_W, tf), lambda i, f: (0, nf + f)),
                  pl.BlockSpec((1, tf), lambda i, f: (0, f)),
                  pl.BlockSpec((1, tf), lambda i, f: (0, nf + f)),
                  pl.BlockSpec((tf, d), lambda i, f: (f, 0)),
                  pl.BlockSpec((nseg, SUBLANES, tf), lambda i, f: (sidx(i), 0, f)),
                  pl.BlockSpec((nseg, SUBLANES, tf), lambda i, f: (sidx(i), 0, nf + f)),
                  pl.BlockSpec((1, d), lambda i, f: (0, 0))],
        out_specs=[pl.BlockSpec((tm, d), lambda i, f: (i, 0)),
                   pl.BlockSpec((tm, d), lambda i, f: (i, 0)),
                   pl.BlockSpec((nseg, SUBLANES, tf), lambda i, f: (i, 0, f)),
                   pl.BlockSpec((nseg, SUBLANES, tf), lambda i, f: (i, 0, f))],
        out_shape=[jax.ShapeDtypeStruct((m, d), F32),
                   jax.ShapeDtypeStruct((m, d), next_dtype),
                   jax.ShapeDtypeStruct((m // seg, SUBLANES, D_FF), F32),
                   jax.ShapeDtypeStruct((m // seg, SUBLANES, D_FF), F32)],
        scratch_shapes=[pltpu.VMEM((tm, d), BF16),
                        pltpu.VMEM((tm, d), F32),
                        pltpu.VMEM((nseg, SUBLANES + seg, tf), F32),
                        pltpu.VMEM((nseg, SUBLANES + seg, tf), F32),
                        pltpu.VMEM((nf, SUBLANES, tf), F32),
                        pltpu.VMEM((nf, SUBLANES, tf), F32)],
        compiler_params=_cparams(("arbitrary", "arbitrary")),
        name="conv_ffn",
    )(x, g.reshape(1, d), w_up, w_up, conv_w, conv_w, cb, cb, w_down, prev8, prev8, g_next.reshape(1, d))


def _rope_tables(pos, head_dim, rot, lanes_valid=LANES):
    half = rot // 2
    freqs = ROPE_THETA ** (-jnp.arange(half, dtype=F32) / half)
    ang = pos.astype(F32)[:, None] * freqs[None, :]
    cos, sin = jnp.cos(ang), jnp.sin(ang)
    lane = np.arange(LANES)
    d = lane % head_dim
    first = (d < half) & (lane < lanes_valid)
    second = (d >= half) & (d < rot) & (lane < lanes_valid)
    fidx = np.clip(np.where(d < half, d, d - half), 0, half - 1)
    cosl, sinl = cos[:, fidx], sin[:, fidx]
    c = jnp.where(first | second, cosl, 1.0)
    s1 = jnp.where(second, sinl, 0.0)
    s2 = jnp.where(first, -sinl, 0.0)
    return c, s1, s2, half


def _prep_weights(w_in_ab, w_out_ab, w_dqkv, w_uq, w_ukv, w_o_mla, w_up, w_down):
    n_even, n_odd = w_in_ab.shape[0], w_dqkv.shape[0]
    w = {}
    n_a = H_A * DH + 2 * KV_A * DH + H_I * D_IDX
    w["in_a"] = w_in_ab[:, :, :n_a].astype(BF16)
    pad = jnp.zeros((n_even, D_MODEL, LANES - D_IDX - H_I), BF16)
    w["in_b"] = jnp.concatenate([w_in_ab[:, :, n_a:n_a + D_IDX + H_I].astype(BF16), pad,
                                 w_in_ab[:, :, n_a + D_IDX + H_I:].astype(BF16)], axis=-1)
    w["out_ab"] = w_out_ab.astype(BF16)
    w["dqkr"] = jnp.concatenate([w_dqkv.astype(BF16),
                                 jnp.zeros((n_odd, D_MODEL, LANES - ROPE_C), BF16)], axis=-1)
    uq = w_uq.reshape(n_odd, Q_RANK, H_C, NOPE + ROPE_C)
    uqn = uq[..., :NOPE].reshape(n_odd, Q_RANK, H_C * NOPE).astype(BF16)
    uqr = uq[..., NOPE:].astype(BF16)
    uqr = jnp.concatenate([uqr, jnp.zeros_like(uqr)], axis=-1).reshape(n_odd, Q_RANK, H_C * LANES)
    w["uq"] = jnp.concatenate([uqn, uqr], axis=-1)
    w["ukv"] = w_ukv.astype(BF16)
    w["o_mla"] = w_o_mla.astype(BF16)
    w["up"] = w_up.astype(BF16)
    w["down"] = w_down.astype(BF16)
    return w


def _with_past(past, new, s_pad):
    b, _, fdim = new.shape
    parts = [new.astype(BF16)] if past is None else [past.astype(BF16), new.astype(BF16)]
    n = sum(p.shape[1] for p in parts)
    if s_pad > n:
        parts.append(jnp.zeros((b, s_pad - n, fdim), BF16))
    return parts[0] if len(parts) == 1 else jnp.concatenate(parts, axis=1)


def _append_body(past_ref, new_ref, o_ref, *, n_past_blocks, t_new):
    j = pl.program_id(1)

    @pl.when(j < n_past_blocks)
    def _():
        for h in range(past_ref.shape[2]):
            o_ref[0, :, h * LANES:(h + 1) * LANES] = past_ref[0, :, h, :].astype(o_ref.dtype)

    @pl.when(j >= n_past_blocks)
    def _():
        o_ref[0] = jnp.zeros(o_ref.shape[1:], o_ref.dtype)
        o_ref[0, 0:t_new, :] = new_ref[0]


def _append_cache(past, new, s_pad, tk):
    b, n_past, nh, _ = past.shape
    t_new = new.shape[1]
    assert n_past % tk == 0 and s_pad == n_past + tk and t_new <= tk
    npb = n_past // tk
    body = functools.partial(_append_body, n_past_blocks=npb, t_new=t_new)
    return pl.pallas_call(
        body,
        grid=(b, s_pad // tk),
        in_specs=[pl.BlockSpec((1, tk, nh, LANES), lambda bi, j: (bi, jnp.minimum(j, npb - 1), 0, 0)),
                  pl.BlockSpec((1, t_new, nh * LANES), lambda bi, j: (bi, 0, 0))],
        out_specs=pl.BlockSpec((1, tk, nh * LANES), lambda bi, j: (bi, j, 0)),
        out_shape=jax.ShapeDtypeStruct((b, s_pad, nh * LANES), BF16),
        compiler_params=_cparams(("parallel", "arbitrary")),
        name="append_cache",
    )(past, new)


def _trunk(x, past, conv_state, w, p, cfg):
    b, t, d = x.shape
    m = b * t
    q0 = 0 if past is None else past[0].shape[2]
    s_valid = q0 + t
    tk = cfg["tk"]
    s_pad = -(-s_valid // tk) * tk
    pos = q0 + jnp.tile(jnp.arange(t), b)
    tab_a = _rope_tables(pos, DH, ROT_A)
    tab_i = _rope_tables(pos, D_IDX, ROT_I)
    tab_ki = _rope_tables(pos, D_IDX, ROT_I, lanes_valid=D_IDX)
    tab_kr = _rope_tables(pos, ROPE_C, ROPE_C, lanes_valid=ROPE_C)

    xf = x.reshape(m, d)
    ab_rows, mla_rows, conv_rows = [], [], []
    sc_ab = DH ** -0.5 * LOG2E
    sc_c = (NOPE + ROPE_C) ** -0.5 * LOG2E
    h = _rmsnorm(xf, p["norm_mix"][0], BF16)
    for l in range(DEPTH):
        if l % 2 == 0:
            e = l // 2
            lp = None if past is None else tuple(c[e] for c in past[:5])
            qa, ka32, ka16, va32, va16, qi = _proj(
                [h], w["in_a"][e],
                [dict(width=H_A * DH, dtypes=(BF16,), rope=0, scale=sc_ab),
                 dict(width=KV_A * DH, dtypes=(F32, BF16), rope=0, split_heads=True),
                 dict(width=KV_A * DH, dtypes=(F32, BF16), split_heads=True),
                 dict(width=H_I * D_IDX, dtypes=(BF16,), rope=1)],
                tabs=(tab_a, tab_i), name="proj_in_a")
            kiwi, qb, kb32, kb16, vb32, vb16 = _proj(
                [h], w["in_b"][e],
                [dict(width=LANES, dtypes=(F32,), rope=0),
                 dict(width=H_B * DH, dtypes=(BF16,), scale=sc_ab),
                 dict(width=H_B * DH, dtypes=(F32, BF16), split_heads=True),
                 dict(width=H_B * DH, dtypes=(F32, BF16), split_heads=True)],
                tabs=(tab_ki,), name="proj_in_b")
            ki32 = kiwi[:, :D_IDX]
            wi = kiwi[:, D_IDX:D_IDX + H_I]
            ab_rows.append((ka32.reshape(b, t, KV_A, DH), va32.reshape(b, t, KV_A, DH),
                            ki32.reshape(b, t, D_IDX), kb32.reshape(b, t, H_B, DH),
                            vb32.reshape(b, t, H_B, DH)))

            def full(idx, new16):
                new16 = new16.reshape(b, t, -1)
                if lp is None:
                    return _with_past(None, new16, s_pad)
                if lp[idx].ndim == 4:
                    return _append_cache(lp[idx], new16.astype(BF16), s_pad, tk)
                return _with_past(lp[idx], new16, s_pad)

            ki_all = full(2, ki32)
            zk = jnp.zeros_like(ki_all)
            ki2 = jnp.concatenate([ki_all, zk, zk, ki_all], axis=-1)
            oa = _dsa(qa.reshape(b, t, -1), qi.reshape(b, t, -1), wi.reshape(b, t, H_I), full(0, ka16),
                      full(1, va16), ki2, tq=cfg["tq_a"], tk=tk, s_valid=s_valid, q0=q0)
            ob = _stick_breaking(qb.reshape(b, t, -1), full(3, kb16), full(4, vb16),
                                 tq=cfg["tq_b"], tk=cfg["tk_b"], q0=q0, hpg=cfg["hpg_b"], nchunk=cfg["nc_b"])
            (xf,) = _proj([oa.reshape(m, -1), ob.reshape(m, -1)], w["out_ab"][e],
                          [dict(width=D_MODEL, dtypes=(F32,))], resid=xf, name="proj_out_ab")
        else:
            od = l // 2
            gains = jnp.concatenate([p["g_q"][od], p["g_kv"][od]])
            cq16, ckv32, ckv16, kr32w, kr16w = _proj(
                [h], w["dqkr"][od],
                [dict(width=Q_RANK, dtypes=(BF16,), gain_off=0),
                 dict(width=KV_RANK, dtypes=(F32, BF16), gain_off=Q_RANK),
                 dict(width=LANES, dtypes=(F32, BF16), rope=0)],
                tabs=(tab_kr,), gain=gains, name="proj_dqkr")
            mla_rows.append((ckv32.reshape(b, t, KV_RANK), kr32w[:, :ROPE_C].reshape(b, t, ROPE_C)))
            qn, qr = _proj([cq16], w["uq"][od],
                           [dict(width=H_C * NOPE, dtypes=(BF16,), scale=sc_c),
                            dict(width=H_C * LANES, dtypes=(BF16,), rope=0, scale=sc_c)],
                           tabs=(tab_kr,), name="proj_uq")
            pc = None if past is None else past[5][od]
            pk = None if past is None else jnp.pad(past[6][od], ((0, 0), (0, 0), (0, LANES - ROPE_C)))
            c_all = _with_past(pc, ckv16.reshape(b, t, KV_RANK), s_pad)
            kr_all = _with_past(pk, kr16w.reshape(b, t, LANES), s_pad)
            if cfg["latent_c"]:
                qa_hm, qr_hm = _absorb_q(qn, qr, w["ukv"][od])
                lat = _mla_latent(qa_hm, qr_hm, c_all, kr_all, t=t, tk=cfg["tk_c"], s_valid=s_valid, q0=q0)
                o = _latent_out(lat, w["ukv"][od])
            else:
                (kv,) = _proj([c_all.reshape(b * s_pad, KV_RANK)], w["ukv"][od],
                              [dict(width=H_C * (NOPE + VD), dtypes=(BF16,))], name="proj_ukv")
                o = _mla(qn.reshape(b, t, -1), qr.reshape(b, t, -1), kv.reshape(b, s_pad, -1), kr_all,
                         tq=cfg["tq_c"], tk=cfg["tk_c"], s_valid=s_valid, q0=q0, hpg=cfg["hpg_c"],
                         nchunk=cfg["nc_c"])
            (xf,) = _proj([o.reshape(m, -1)], w["o_mla"][od], [dict(width=D_MODEL, dtypes=(F32,))],
                          resid=xf, name="proj_o_mla")

        if conv_state is None:
            prev8 = jnp.zeros((b, SUBLANES, 2 * D_FF), F32)
        else:
            prev8 = jnp.concatenate([jnp.zeros((b, SUBLANES - (CONV_W - 1), 2 * D_FF), F32),
                                     conv_state[l]], axis=1)
        final = l == DEPTH - 1
        g_next = p["norm_final"] if final else p["norm_mix"][l + 1]
        xf, h, sg, sv = _ffn(xf, p["norm_ffn"][l], w["up"][l], p["conv_w"][l], p["conv_b"][l], w["down"][l],
                             prev8, g_next, F32 if final else BF16, seg=cfg["seg"], carried=cfg["carried"],
                             tm=cfg["tm_ffn"], tf=cfg["tf"])
        last = [s8.reshape(b, -1, SUBLANES, D_FF)[:, -1, SUBLANES - (CONV_W - 1):] for s8 in (sg, sv)]
        conv_rows.append(jnp.concatenate(last, axis=-1))

    y = h.reshape(b, t, d)
    new_ab = [jnp.stack([r[i] for r in ab_rows]) for i in range(5)]
    new_mla = [jnp.stack([r[i] for r in mla_rows]) for i in range(2)]
    return y, new_ab + new_mla + [jnp.stack(conv_rows)]


def kernel(x_prompt, x_sample, cache_k_a, cache_v_a, cache_idx_k, cache_k_b, cache_v_b, cache_ckv, cache_krope, state_conv, norm_mix, norm_ffn, norm_final, w_in_ab, w_out_ab, w_dqkv, g_q, g_kv, w_uq, w_ukv, w_o_mla, w_up, conv_w, conv_b, w_down):
    w = _prep_weights(w_in_ab, w_out_ab, w_dqkv, w_uq, w_ukv, w_o_mla, w_up, w_down)
    p = dict(norm_mix=norm_mix, norm_ffn=norm_ffn, norm_final=norm_final, g_q=g_q, g_kv=g_kv,
             conv_w=conv_w, conv_b=conv_b)
    cfg_p = dict(tk=512, tq_a=128, tq_b=256, tk_b=256, hpg_b=4, nc_b=1, tq_c=512, tk_c=1024, hpg_c=2, nc_c=2,
                 latent_c=False, seg=512, carried=True, tm_ffn=512, tf=512)
    t_s = x_sample.shape[1]
    cfg_s = dict(tk=256, tq_a=t_s, tq_b=t_s, tk_b=256, hpg_b=H_B, nc_b=1, tq_c=t_s, tk_c=256, hpg_c=8, nc_c=1,
                 latent_c=True, seg=t_s, carried=False, tm_ffn=x_sample.shape[0] * t_s, tf=512)
    y_p, st_p = _trunk(x_prompt, None, None, w, p, cfg_p)
    past = (cache_k_a, cache_v_a, cache_idx_k, cache_k_b, cache_v_b, cache_ckv, cache_krope)
    y_s, st_s = _trunk(x_sample, past, state_conv, w, p, cfg_s)
    return (y_p, y_s, *st_p, *st_s)
```

```python
import functools

import numpy as np
import jax
import jax.numpy as jnp
from jax import lax
from jax.experimental import pallas as pl
from jax.experimental.pallas import tpu as pltpu

F32 = jnp.float32
BF16 = jnp.bfloat16

D_MODEL = 2048
DEPTH = 4
CHUNK = 64
ROPE_THETA = 500000.0
EPS = 1e-6
H_A, KV_A, DH = 8, 2, 128
ROT_A = DH // 4
H_I, D_IDX = 16, 64
ROT_I = D_IDX // 4
TOPK_MAX = 256
H_B = 8
H_C, Q_RANK, KV_RANK, NOPE, ROPE_C, VD = 16, 512, 512, 128, 64, 128
D_FF = 5632
CONV_W = 3

LANES = 128
SUBLANES = 8
VMEM_LIMIT = 56 * 1024 * 1024
NEG = -1e30
INT_MIN = np.int32(-2 ** 31)
INT_MAX = np.int32(2 ** 31 - 1)
LOG2E = float(np.log2(np.e))


def _cparams(sem):
    return pltpu.CompilerParams(dimension_semantics=sem, vmem_limit_bytes=VMEM_LIMIT)


def _dot_nt(a, b):
    return lax.dot_general(a, b, (((1,), (1,)), ((), ())), preferred_element_type=F32)


def _rms_body(x_ref, g_ref, o_ref):
    x = x_ref[...]
    y = x * lax.rsqrt(jnp.mean(x * x, axis=-1, keepdims=True) + EPS)
    o_ref[...] = (y * g_ref[...]).astype(o_ref.dtype)


def _rmsnorm(x, g, out_dtype):
    m, d = x.shape
    tm = min(m, 512)
    return pl.pallas_call(
        _rms_body,
        grid=(m // tm,),
        in_specs=[pl.BlockSpec((tm, d), lambda i: (i, 0)),
                  pl.BlockSpec((1, d), lambda i: (0, 0))],
        out_specs=pl.BlockSpec((tm, d), lambda i: (i, 0)),
        out_shape=jax.ShapeDtypeStruct((m, d), out_dtype),
        compiler_params=_cparams(("parallel",)),
        name="rmsnorm",
    )(x, g.reshape(1, d))


PROJ_CHUNK = 512


def _proj_body(*refs, nx, groups, tab_halves, has_gain, has_resid):
    it = iter(refs)
    x_refs = [next(it) for _ in range(nx)]
    w_ref = next(it)
    tabs = [(next(it), next(it), next(it)) for _ in tab_halves]
    g_ref = next(it) if has_gain else None
    r_ref = next(it) if has_resid else None
    outs = list(it)

    xs = [r[...] for r in x_refs]
    x = xs[0] if nx == 1 else jnp.concatenate(xs, axis=1)
    accs = []
    c0 = 0
    for grp in groups:
        width = grp["width"]
        chunk = min(width, PROJ_CHUNK)
        for cc in range(0, width, chunk):
            accs.append(jnp.dot(x, w_ref[:, c0 + cc:c0 + cc + chunk], preferred_element_type=F32))
        c0 += width
    accs = iter(accs)
    oi = 0
    for grp in groups:
        width = grp["width"]
        chunk = min(width, PROJ_CHUNK)
        for cc in range(0, width, chunk):
            acc = next(accs)
            if grp.get("gain_off") is not None:
                assert chunk == width
                go = grp["gain_off"]
                acc = (acc * lax.rsqrt(jnp.mean(acc * acc, axis=-1, keepdims=True) + EPS)
                       * g_ref[:, go:go + width])
            if grp.get("rope") is not None:
                c_ref, s1_ref, s2_ref = tabs[grp["rope"]]
                half = tab_halves[grp["rope"]]
                c, s1, s2 = c_ref[...], s1_ref[...], s2_ref[...]
                parts = []
                for gi in range(chunk // LANES):
                    xg = acc[:, gi * LANES:(gi + 1) * LANES]
                    parts.append(xg * c + pltpu.roll(xg, half, 1) * s1
                                 + pltpu.roll(xg, LANES - half, 1) * s2)
                acc = parts[0] if len(parts) == 1 else jnp.concatenate(parts, axis=1)
            if grp.get("scale") is not None:
                acc = acc * grp["scale"]
            if has_resid:
                acc = acc + r_ref[:, cc:cc + chunk]
            for k in range(len(grp["dtypes"])):
                o = outs[oi + k]
                if len(o.shape) == 3:
                    for gi in range(chunk // LANES):
                        o[:, cc // LANES + gi, :] = acc[:, gi * LANES:(gi + 1) * LANES].astype(o.dtype)
                else:
                    o[:, cc:cc + chunk] = acc.astype(o.dtype)
        oi += len(grp["dtypes"])


def _proj(xs, w, groups, *, tabs=(), gain=None, resid=None, name="proj"):
    m = xs[0].shape[0]
    k, n = w.shape
    tm = min(m, 512)
    assert m % tm == 0 and n == sum(g["width"] for g in groups) and k == sum(x.shape[1] for x in xs)
    assert resid is None or len(groups) == 1
    in_specs = [pl.BlockSpec((tm, x.shape[1]), lambda i: (i, 0)) for x in xs]
    in_specs.append(pl.BlockSpec((k, n), lambda i: (0, 0)))
    args = list(xs) + [w]
    for c, s1, s2, _ in tabs:
        for t in (c, s1, s2):
            in_specs.append(pl.BlockSpec((tm, LANES), lambda i: (i, 0)))
            args.append(t)
    if gain is not None:
        in_specs.append(pl.BlockSpec((1, gain.shape[0]), lambda i: (0, 0)))
        args.append(gain.reshape(1, -1))
    if resid is not None:
        in_specs.append(pl.BlockSpec((tm, n), lambda i: (i, 0)))
        args.append(resid)
    out_specs, out_shape = [], []
    for g in groups:
        for dt in g["dtypes"]:
            if g.get("split_heads") and dt == F32:
                nh = g["width"] // LANES
                out_specs.append(pl.BlockSpec((tm, nh, LANES), lambda i: (i, 0, 0)))
                out_shape.append(jax.ShapeDtypeStruct((m, nh, LANES), dt))
            else:
                out_specs.append(pl.BlockSpec((tm, g["width"]), lambda i: (i, 0)))
                out_shape.append(jax.ShapeDtypeStruct((m, g["width"]), dt))
    body = functools.partial(_proj_body, nx=len(xs), groups=groups, tab_halves=[t[3] for t in tabs],
                             has_gain=gain is not None, has_resid=resid is not None)
    return pl.pallas_call(
        body,
        grid=(m // tm,),
        in_specs=in_specs,
        out_specs=out_specs,
        out_shape=out_shape,
        compiler_params=_cparams(("parallel",)),
        name=name,
    )(*args)


def _dsa_body(qa_ref, qi_ref, wi_ref, ka_ref, va_ref, ki_ref, tri_ref, o_ref,
              key_scr, qs_scr, m_scr, l_scr, acc_scr, *, tq, tk, s_valid, q0, topk):
    i = pl.program_id(1)
    q_first = q0 + i * tq
    q_last = q_first + tq - 1
    adm_end = jnp.minimum((q_last // CHUNK + 1) * CHUNK, s_valid)
    nkb = (adm_end + tk - 1) // tk
    q_pos = q_first + lax.broadcasted_iota(jnp.int32, (tq, tk), 0)
    q_lim = jnp.minimum(((q_pos >> 6) + 1) * CHUNK, s_valid)
    col = lax.broadcasted_iota(jnp.int32, (tq, tk), 1)
    rep = H_A // KV_A

    w = wi_ref[0] * ((H_I * D_IDX) ** -0.5)
    pg = 4
    pairs = LANES // D_IDX
    qi = qi_ref[0]
    q_groups = [jnp.concatenate([qi[:, (g * pg + pp) * LANES:(g * pg + pp + 1) * LANES]
                                 for pp in range(pg)], axis=0) for g in range(H_I // (pairs * pg))]

    def to_key(x):
        bits = pltpu.bitcast(x, jnp.int32)
        return bits ^ ((bits >> 31) & INT_MAX)

    def p1(j, c):
        k0 = pl.multiple_of(j * tk, tk)
        sc = jnp.zeros((tq, tk), F32)
        for g, qg in enumerate(q_groups):
            for half in range(pairs):
                rel = _dot_nt(qg, ki_ref[0, pl.ds(k0, tk), half * LANES:(half + 1) * LANES])
                for pp in range(pg):
                    h = (g * pg + pp) * pairs + half
                    sc = sc + jnp.maximum(rel[pp * tq:(pp + 1) * tq], 0.0) * w[:, h:h + 1]
        key_scr[:, pl.ds(k0, tk)] = jnp.where(k0 + col < q_lim, to_key(sc), INT_MIN)
        return c

    lax.fori_loop(0, nkb, p1, 0)

    def count_ge(v):
        def cb(j, c):
            k0 = pl.multiple_of(j * tk, tk)
            ge = jnp.where(key_scr[:, pl.ds(k0, tk)] >= v, 1.0, 0.0)
            part = ge[:, 0:LANES]
            for cc in range(1, tk // LANES):
                part = part + ge[:, cc * LANES:(cc + 1) * LANES]
            return c + part
        c = lax.fori_loop(0, nkb, cb, jnp.zeros((tq, LANES), F32))
        return jnp.sum(c, axis=1, keepdims=True)

    def bis(_, lohi):
        lo, hi = lohi
        mid = (lo >> 1) + (hi >> 1) + (lo & hi & 1)
        ok = count_ge(mid) >= float(topk)
        return jnp.where(ok, mid, lo), jnp.where(ok, hi, mid)

    thr, _ = lax.fori_loop(0, 32, bis, (jnp.full((tq, 1), INT_MIN, jnp.int32),
                                        jnp.full((tq, 1), INT_MAX, jnp.int32)))
    need = float(topk) - count_ge(thr + 1)

    qa = qa_ref[0]
    for g in range(KV_A):
        qs_scr[g] = jnp.concatenate(
            [qa[:, (g * rep + r) * DH:(g * rep + r + 1) * DH] for r in range(rep)], axis=0)
    m_scr[...] = jnp.full(m_scr.shape, NEG, F32)
    l_scr[...] = jnp.zeros(l_scr.shape, F32)
    acc_scr[...] = jnp.zeros(acc_scr.shape, F32)

    def p3(j, eqc):
        k0 = pl.multiple_of(j * tk, tk)
        key = key_scr[:, pl.ds(k0, tk)]
        eq = key == thr
        eqf = jnp.where(eq, 1.0, 0.0)
        prefix = jnp.dot(eqf.astype(BF16), tri_ref[...], preferred_element_type=F32) + eqc
        bias = jnp.where(key > thr, 0.0, jnp.where(eq, jnp.where(prefix < need, 0.0, NEG), NEG))
        bias = jnp.where(key == INT_MIN, NEG, bias)
        eqc = eqc + jnp.sum(eqf, axis=1, keepdims=True)
        ss = [_dot_nt(qs_scr[g], ka_ref[0, pl.ds(k0, tk), g * DH:(g + 1) * DH])
              for g in range(KV_A)]
        for g in range(KV_A):
            vg = va_ref[0, pl.ds(k0, tk), g * DH:(g + 1) * DH]
            s = ss[g]
            ps, alphas = [], []
            for r in range(rep):
                h = g * rep + r
                sr = s[r * tq:(r + 1) * tq] + bias
                m_prev = m_scr[h]
                m_new = jnp.maximum(m_prev, jnp.max(sr, axis=1, keepdims=True))
                alpha = jnp.exp2(m_prev - m_new)
                p = jnp.exp2(sr - jnp.concatenate([m_new] * (tk // LANES), axis=1))
                l_scr[h] = alpha * l_scr[h] + jnp.sum(p, axis=1, keepdims=True)
                m_scr[h] = m_new
                ps.append(p.astype(BF16))
                alphas.append(alpha)
            for r in range(rep):
                h = g * rep + r
                acc_scr[h] = acc_scr[h] * alphas[r] + jnp.dot(ps[r], vg, preferred_element_type=F32)
        return eqc

    lax.fori_loop(0, nkb, p3, jnp.zeros((tq, 1), F32))
    for h in range(H_A):
        o_ref[0, :, h * DH:(h + 1) * DH] = (acc_scr[h] / l_scr[h]).astype(o_ref.dtype)


def _dsa(qa, qi, wi, ka, va, ki, *, tq, tk, s_valid, q0):
    b, t, _ = qa.shape
    s_pad = ka.shape[1]
    assert s_pad % tk == 0 and t % tq == 0 and tk >= TOPK_MAX
    topk = min(TOPK_MAX, s_valid // 4)
    tri = jnp.asarray(np.triu(np.ones((tk, tk), np.float32), 1), BF16)
    body = functools.partial(_dsa_body, tq=tq, tk=tk, s_valid=s_valid, q0=q0, topk=topk)
    rep = H_A // KV_A
    return pl.pallas_call(
        body,
        grid=(b, t // tq),
        in_specs=[pl.BlockSpec((1, tq, H_A * DH), lambda bi, i: (bi, i, 0)),
                  pl.BlockSpec((1, tq, H_I * D_IDX), lambda bi, i: (bi, i, 0)),
                  pl.BlockSpec((1, tq, H_I), lambda bi, i: (bi, i, 0)),
                  pl.BlockSpec((1, s_pad, KV_A * DH), lambda bi, i: (bi, 0, 0)),
                  pl.BlockSpec((1, s_pad, KV_A * DH), lambda bi, i: (bi, 0, 0)),
                  pl.BlockSpec((1, s_pad, 2 * LANES), lambda bi, i: (bi, 0, 0)),
                  pl.BlockSpec((tk, tk), lambda bi, i: (0, 0))],
        out_specs=pl.BlockSpec((1, tq, H_A * DH), lambda bi, i: (bi, i, 0)),
        out_shape=jax.ShapeDtypeStruct((b, t, H_A * DH), BF16),
        scratch_shapes=[pltpu.VMEM((tq, s_pad), jnp.int32),
                        pltpu.VMEM((KV_A, rep * tq, DH), BF16),
                        pltpu.VMEM((H_A, tq, LANES), F32),
                        pltpu.VMEM((H_A, tq, LANES), F32),
                        pltpu.VMEM((H_A, tq, DH), F32)],
        compiler_params=_cparams(("parallel", "arbitrary")),
        name="dsa",
    )(qa, qi, wi, ka, va, ki, tri)


def _sb_body(q_ref, k_ref, v_ref, low_ref, o_ref, *, tq, tk, s_pad, q0, hpg, nchunk):
    i = pl.program_id(2)
    tqc = tq // nchunk
    q_first = q0 + i * tq
    q_last = q_first + tq - 1
    nkb = jnp.minimum((q_last + tk - 1) // tk, s_pad // tk)
    n_full = jnp.minimum(q_first // tk, nkb)
    row = lax.broadcasted_iota(jnp.int32, (tqc, tk), 0)
    col = lax.broadcasted_iota(jnp.int32, (tqc, tk), 1)
    chains = [(g, c) for g in range(hpg) for c in range(nchunk)]

    def step(j, carry, masked):
        k0 = pl.multiple_of(j * tk, tk)
        zs = [_dot_nt(q_ref[0, c * tqc:(c + 1) * tqc, g * DH:(g + 1) * DH],
                      k_ref[0, pl.ds(k0, tk), g * DH:(g + 1) * DH]) for g, c in chains]
        lbs, lks, cats, causals = [], [], [], []
        for (g, c), z in zip(chains, zs):
            lb = jnp.minimum(z, 0.0) - jnp.log(1.0 + jnp.exp2(-jnp.abs(z))) * LOG2E
            lk = lb - z
            causal = None
            if masked:
                causal = k0 + col < q_first + c * tqc + row
                lk = jnp.where(causal, lk, 0.0)
            hi = lk.astype(BF16)
            lo = (lk - hi.astype(F32)).astype(BF16)
            lbs.append(lb)
            lks.append(lk)
            causals.append(causal)
            cats.append(jnp.concatenate([hi, lo], axis=1))
        sufs = [jnp.dot(cat, low_ref[...], preferred_element_type=F32) for cat in cats]
        out = []
        for n, (g, c) in enumerate(chains):
            run, acc = carry[2 * n], carry[2 * n + 1]
            wgt = jnp.exp2(lbs[n] + sufs[n] + run)
            if masked:
                wgt = jnp.where(causals[n], wgt, 0.0)
            acc = acc + jnp.dot(wgt.astype(BF16), v_ref[0, pl.ds(k0, tk), g * DH:(g + 1) * DH],
                                preferred_element_type=F32)
            run = run + jnp.sum(lks[n], axis=1, keepdims=True)
            out += [run, acc]
        return tuple(out)

    carry = []
    for _ in chains:
        carry += [jnp.zeros((tqc, 1), F32), jnp.zeros((tqc, DH), F32)]
    carry = tuple(carry)
    carry = lax.fori_loop(0, nkb - n_full, lambda s, c: step(nkb - 1 - s, c, True), carry)
    carry = lax.fori_loop(0, n_full, lambda s, c: step(n_full - 1 - s, c, False), carry)
    for n, (g, c) in enumerate(chains):
        o_ref[0, c * tqc:(c + 1) * tqc, g * DH:(g + 1) * DH] = carry[2 * n + 1].astype(o_ref.dtype)


def _stick_breaking(qb, kb, vb, *, tq, tk, q0, hpg, nchunk):
    b, t, _ = qb.shape
    s_pad = kb.shape[1]
    assert s_pad % tk == 0 and t % tq == 0 and H_B % hpg == 0 and tq % nchunk == 0
    low = np.tril(np.ones((tk, tk), np.float32), -1)
    low = jnp.asarray(np.concatenate([low, low], axis=0), BF16)
    body = functools.partial(_sb_body, tq=tq, tk=tk, s_pad=s_pad, q0=q0, hpg=hpg, nchunk=nchunk)
    wd = hpg * DH
    return pl.pallas_call(
        body,
        grid=(b, H_B // hpg, t // tq),
        in_specs=[pl.BlockSpec((1, tq, wd), lambda bi, h, i: (bi, i, h)),
                  pl.BlockSpec((1, s_pad, wd), lambda bi, h, i: (bi, 0, h)),
                  pl.BlockSpec((1, s_pad, wd), lambda bi, h, i: (bi, 0, h)),
                  pl.BlockSpec((2 * tk, tk), lambda bi, h, i: (0, 0))],
        out_specs=pl.BlockSpec((1, tq, wd), lambda bi, h, i: (bi, i, h)),
        out_shape=jax.ShapeDtypeStruct((b, t, H_B * DH), BF16),
        compiler_params=_cparams(("parallel", "parallel", "arbitrary")),
        name="stick_breaking",
    )(qb, kb, vb, low)


def _mla_body(qn_ref, qr_ref, kv_ref, kr_ref, o_ref, *, tq, tk, s_valid, q0, hpg, nchunk):
    i = pl.program_id(2)
    tqc = tq // nchunk
    q_first = q0 + i * tq
    q_last = q_first + tq - 1
    lim_first = jnp.minimum((q_first // CHUNK + 1) * CHUNK, s_valid)
    lim_last = jnp.minimum((q_last // CHUNK + 1) * CHUNK, s_valid)
    nkb = (lim_last + tk - 1) // tk
    n_full = lim_first // tk
    row = lax.broadcasted_iota(jnp.int32, (tqc, tk), 0)
    col = lax.broadcasted_iota(jnp.int32, (tqc, tk), 1)
    chains = [(g, c) for g in range(hpg) for c in range(nchunk)]
    hw = NOPE + VD

    def step(j, carry, masked):
        k0 = pl.multiple_of(j * tk, tk)
        kr = kr_ref[0, pl.ds(k0, tk), :]
        ss = []
        for g, c in chains:
            rows = slice(c * tqc, (c + 1) * tqc)
            qc = jnp.concatenate([qn_ref[0, rows, g * NOPE:(g + 1) * NOPE],
                                  qr_ref[0, rows, g * LANES:(g + 1) * LANES]], axis=1)
            kc = jnp.concatenate([kv_ref[0, pl.ds(k0, tk), g * hw:g * hw + NOPE], kr], axis=1)
            ss.append(_dot_nt(qc, kc))
        ps, stats = [], []
        for n, (g, c) in enumerate(chains):
            m_prev = carry[2 * n]
            s = ss[n]
            if masked:
                lim = jnp.minimum((((q_first + c * tqc + row) >> 6) + 1) * CHUNK, s_valid)
                s = jnp.where(k0 + col < lim, s, NEG)
            m_new = jnp.maximum(m_prev, jnp.max(s, axis=1, keepdims=True))
            alpha = jnp.exp2(m_prev - m_new)
            p = jnp.exp2(s - jnp.concatenate([m_new] * (tk // LANES), axis=1))
            stats.append((m_new, alpha))
            ps.append(p.astype(BF16))
        out = []
        for n, (g, c) in enumerate(chains):
            m_new, alpha = stats[n]
            vb = jnp.concatenate([kv_ref[0, pl.ds(k0, tk), g * hw + NOPE:(g + 1) * hw], ones_k], axis=1)
            acc = (carry[2 * n + 1] * jnp.concatenate([alpha, alpha], axis=1)
                   + jnp.dot(ps[n], vb, preferred_element_type=F32))
            out += [m_new, acc]
        return tuple(out)

    ones_k = jnp.ones((tk, LANES), BF16)
    carry = []
    for _ in chains:
        carry += [jnp.full((tqc, LANES), NEG, F32), jnp.zeros((tqc, VD + LANES), F32)]
    carry = tuple(carry)
    carry = lax.fori_loop(0, n_full, lambda j, c: step(j, c, False), carry)
    carry = lax.fori_loop(n_full, nkb, lambda j, c: step(j, c, True), carry)
    for n, (g, c) in enumerate(chains):
        acc = carry[2 * n + 1]
        o_ref[0, c * tqc:(c + 1) * tqc, g * VD:(g + 1) * VD] = (acc[:, :VD] / acc[:, VD:]).astype(o_ref.dtype)


def _mla(qn, qr, kv, kr, *, tq, tk, s_valid, q0, hpg, nchunk):
    b, t, _ = qn.shape
    s_pad = kv.shape[1]
    assert s_pad % tk == 0 and t % tq == 0 and H_C % hpg == 0 and tq % nchunk == 0
    body = functools.partial(_mla_body, tq=tq, tk=tk, s_valid=s_valid, q0=q0, hpg=hpg, nchunk=nchunk)
    return pl.pallas_call(
        body,
        grid=(b, H_C // hpg, t // tq),
        in_specs=[pl.BlockSpec((1, tq, hpg * NOPE), lambda bi, h, i: (bi, i, h)),
                  pl.BlockSpec((1, tq, hpg * LANES), lambda bi, h, i: (bi, i, h)),
                  pl.BlockSpec((1, s_pad, hpg * (NOPE + VD)), lambda bi, h, i: (bi, 0, h)),
                  pl.BlockSpec((1, s_pad, LANES), lambda bi, h, i: (bi, 0, 0))],
        out_specs=pl.BlockSpec((1, tq, hpg * VD), lambda bi, h, i: (bi, i, h)),
        out_shape=jax.ShapeDtypeStruct((b, t, H_C * VD), BF16),
        compiler_params=_cparams(("parallel", "parallel", "arbitrary")),
        name="mla",
    )(qn, qr, kv, kr)


def _absorb_q_body(qn_ref, qr_ref, wk_ref, qa_ref, qrh_ref):
    qa_ref[0] = _dot_nt(qn_ref[...], wk_ref[...]).astype(qa_ref.dtype)
    qrh_ref[0] = qr_ref[...]


def _absorb_q(qn, qr, w_ukv):
    m = qn.shape[0]
    return pl.pallas_call(
        _absorb_q_body,
        grid=(H_C,),
        in_specs=[pl.BlockSpec((m, NOPE), lambda h: (0, h)),
                  pl.BlockSpec((m, LANES), lambda h: (0, h)),
                  pl.BlockSpec((KV_RANK, NOPE), lambda h: (0, 2 * h))],
        out_specs=[pl.BlockSpec((1, m, KV_RANK), lambda h: (h, 0, 0)),
                   pl.BlockSpec((1, m, LANES), lambda h: (h, 0, 0))],
        out_shape=[jax.ShapeDtypeStruct((H_C, m, KV_RANK), BF16),
                   jax.ShapeDtypeStruct((H_C, m, LANES), BF16)],
        compiler_params=_cparams(("parallel",)),
        name="mla_absorb_q",
    )(qn, qr, w_ukv)


def _mla_latent_body(qa_ref, qr_ref, c_ref, kr_ref, o_ref, *, t, tk, s_valid, q0, nsplit):
    hs = H_C // nsplit
    rows = hs * t
    lim_first = min((q0 // CHUNK + 1) * CHUNK, s_valid)
    lim_last = min(((q0 + t - 1) // CHUNK + 1) * CHUNK, s_valid)
    nkb = -(-lim_last // tk)
    n_full = lim_first // tk
    q_pos = q0 + lax.broadcasted_iota(jnp.int32, (hs, t, tk), 1).reshape(rows, tk)
    lim = jnp.minimum(((q_pos >> 6) + 1) * CHUNK, s_valid)
    col = lax.broadcasted_iota(jnp.int32, (rows, tk), 1)
    ones_k = jnp.ones((tk, LANES), BF16)
    qs = [jnp.concatenate([qa_ref[n * hs:(n + 1) * hs].reshape(rows, KV_RANK),
                           qr_ref[n * hs:(n + 1) * hs].reshape(rows, LANES)], axis=1) for n in range(nsplit)]

    def step(j, carry, masked):
        k0 = pl.multiple_of(j * tk, tk)
        cb = c_ref[0, pl.ds(k0, tk), :]
        kc = jnp.concatenate([cb, kr_ref[0, pl.ds(k0, tk), :]], axis=1)
        vc = jnp.concatenate([cb, ones_k], axis=1)
        ss = [_dot_nt(q, kc) for q in qs]
        ps, stats = [], []
        for n in range(nsplit):
            s = ss[n]
            if masked:
                s = jnp.where(k0 + col < lim, s, NEG)
            m_prev = carry[2 * n]
            m_new = jnp.maximum(m_prev, jnp.max(s, axis=1, keepdims=True))
            alpha = jnp.exp2(m_prev - m_new)
            ps.append(jnp.exp2(s - jnp.concatenate([m_new] * (tk // LANES), axis=1)).astype(BF16))
            stats.append((m_new, alpha))
        out = []
        for n in range(nsplit):
            m_new, alpha = stats[n]
            acc = (carry[2 * n + 1] * jnp.concatenate([alpha] * (KV_RANK // LANES + 1), axis=1)
                   + jnp.dot(ps[n], vc, preferred_element_type=F32))
            out += [m_new, acc]
        return tuple(out)

    carry = []
    for _ in range(nsplit):
        carry += [jnp.full((rows, LANES), NEG, F32), jnp.zeros((rows, KV_RANK + LANES), F32)]
    carry = tuple(carry)
    carry = lax.fori_loop(0, n_full, lambda j, c: step(j, c, False), carry)
    carry = lax.fori_loop(n_full, nkb, lambda j, c: step(j, c, True), carry)
    for n in range(nsplit):
        acc = carry[2 * n + 1]
        lat = acc[:, :KV_RANK] / jnp.concatenate([acc[:, KV_RANK:]] * (KV_RANK // LANES), axis=1)
        o_ref[0, n * hs:(n + 1) * hs] = lat.reshape(hs, t, KV_RANK).astype(o_ref.dtype)


def _mla_latent(qa_hm, qr_hm, c_all, kr_all, *, t, tk, s_valid, q0):
    b, s_pad, _ = c_all.shape
    assert s_pad % tk == 0
    body = functools.partial(_mla_latent_body, t=t, tk=tk, s_valid=s_valid, q0=q0, nsplit=2)
    return pl.pallas_call(
        body,
        grid=(b,),
        in_specs=[pl.BlockSpec((H_C, t, KV_RANK), lambda bi: (0, bi, 0)),
                  pl.BlockSpec((H_C, t, LANES), lambda bi: (0, bi, 0)),
                  pl.BlockSpec((1, s_pad, KV_RANK), lambda bi: (bi, 0, 0)),
                  pl.BlockSpec((1, s_pad, LANES), lambda bi: (bi, 0, 0))],
        out_specs=pl.BlockSpec((1, H_C, t, KV_RANK), lambda bi: (bi, 0, 0, 0)),
        out_shape=jax.ShapeDtypeStruct((b, H_C, t, KV_RANK), BF16),
        compiler_params=_cparams(("parallel",)),
        name="mla_latent",
    )(qa_hm, qr_hm, c_all, kr_all)


def _latent_out_body(x_ref, wv_ref, o_ref):
    nb, _, t, r = x_ref.shape
    o_ref[...] = jnp.dot(x_ref[...].reshape(nb * t, r), wv_ref[...],
                         preferred_element_type=F32).astype(o_ref.dtype)


def _latent_out(lat, w_ukv):
    b, _, t, _ = lat.shape
    return pl.pallas_call(
        _latent_out_body,
        grid=(H_C,),
        in_specs=[pl.BlockSpec((b, 1, t, KV_RANK), lambda h: (0, h, 0, 0)),
                  pl.BlockSpec((KV_RANK, VD), lambda h: (0, 2 * h + 1))],
        out_specs=pl.BlockSpec((b * t, VD), lambda h: (0, h)),
        out_shape=jax.ShapeDtypeStruct((b * t, H_C * VD), BF16),
        compiler_params=_cparams(("parallel",)),
        name="mla_latent_out",
    )(lat, w_ukv)


def _ffn_body(x_ref, g_ref, wg_ref, wv_ref, cwg_ref, cwv_ref, cbg_ref, cbv_ref, wd_ref, pg_ref, pv_ref, gn_ref,
              o_ref, hn_ref, sg_ref, sv_ref, h_scr, acc_scr, ug_buf, uv_buf, cg_scr, cv_scr, *, seg, carried):
    i = pl.program_id(0)
    f = pl.program_id(1)
    tm = x_ref.shape[0]

    @pl.when(f == 0)
    def _():
        x = x_ref[...]
        y = x * lax.rsqrt(jnp.mean(x * x, axis=-1, keepdims=True) + EPS)
        h_scr[...] = (y * g_ref[...]).astype(BF16)
        acc_scr[...] = x

    h = h_scr[...]
    halves = ((wg_ref, cwg_ref, cbg_ref, pg_ref, sg_ref, ug_buf, cg_scr),
              (wv_ref, cwv_ref, cbv_ref, pv_ref, sv_ref, uv_buf, cv_scr))
    ys = []
    for w_ref, cw_ref, cb_ref, p_ref, s_ref, buf, c_scr in halves:
        u = jnp.dot(h, w_ref[...], preferred_element_type=F32)
        cw = cw_ref[...]
        parts = []
        for sgi in range(tm // seg):
            us = u[sgi * seg:(sgi + 1) * seg]
            if carried:
                head = jnp.where(i == 0, p_ref[0], c_scr[f])
            else:
                head = p_ref[sgi]
            buf[sgi, 0:SUBLANES] = head
            buf[sgi, SUBLANES:SUBLANES + seg] = us
            y = (cw[2:3] * us + cw[1:2] * buf[sgi, SUBLANES - 1:SUBLANES - 1 + seg]
                 + cw[0:1] * buf[sgi, SUBLANES - 2:SUBLANES - 2 + seg] + cb_ref[...])
            parts.append(y)
            last = us[seg - SUBLANES:seg]
            s_ref[sgi] = last
            if carried:
                c_scr[f] = last
        ys.append(parts[0] if len(parts) == 1 else jnp.concatenate(parts, axis=0))
    yg, yv = ys
    act = (yg / (1.0 + jnp.exp(-yg))) * yv
    acc_scr[...] += jnp.dot(act.astype(BF16), wd_ref[...], preferred_element_type=F32)

    @pl.when(f == pl.num_programs(1) - 1)
    def _():
        a = acc_scr[...]
        o_ref[...] = a
        y = a * lax.rsqrt(jnp.mean(a * a, axis=-1, keepdims=True) + EPS)
        hn_ref[...] = (y * gn_ref[...]).astype(hn_ref.dtype)


def _ffn(x, g, w_up, conv_w, conv_b, w_down, prev8, g_next, next_dtype, *, seg, carried, tm, tf):
    m, d = x.shape
    nf = D_FF // tf
    nseg = tm // seg
    assert m % tm == 0 and D_FF % tf == 0 and tm % seg == 0
    sidx = (lambda i: 0) if carried else (lambda i: i)
    body = functools.partial(_ffn_body, seg=seg, carried=carried)
    cb = conv_b.reshape(1, 2 * D_FF)
    return pl.pallas_call(
        body,
        grid=(m // tm, nf),
        in_specs=[pl.BlockSpec((tm, d), lambda i, f: (i, 0)),
                  pl.BlockSpec((1, d), lambda i, f: (0, 0)),
                  pl.BlockSpec((d, tf), lambda i, f: (0, f)),
                  pl.BlockSpec((d, tf), lambda i, f: (0, nf + f)),
                  pl.BlockSpec((CONV_W, tf), lambda i, f: (0, f)),
                  pl.BlockSpec((CONV_W, tf), lambda i, f: (0, nf + f)),
                  pl.BlockSpec((1, tf), lambda i, f: (0, f)),
                  pl.BlockSpec((1, tf), lambda i, f: (0, nf + f)),
                  pl.BlockSpec((tf, d), lambda i, f: (f, 0)),
                  pl.BlockSpec((nseg, SUBLANES, tf), lambda i, f: (sidx(i), 0, f)),
                  pl.BlockSpec((nseg, SUBLANES, tf), lambda i, f: (sidx(i), 0, nf + f)),
                  pl.BlockSpec((1, d), lambda i, f: (0, 0))],
        out_specs=[pl.BlockSpec((tm, d), lambda i, f: (i, 0)),
                   pl.BlockSpec((tm, d), lambda i, f: (i, 0)),
                   pl.BlockSpec((nseg, SUBLANES, tf), lambda i, f: (i, 0, f)),
                   pl.BlockSpec((nseg, SUBLANES, tf), lambda i, f: (i, 0, f))],
        out_shape=[jax.ShapeDtypeStruct((m, d), F32),
                   jax.ShapeDtypeStruct((m, d), next_dtype),
                   jax.ShapeDtypeStruct((m // seg, SUBLANES, D_FF), F32),
                   jax.ShapeDtypeStruct((m // seg, SUBLANES, D_FF), F32)],
        scratch_shapes=[pltpu.VMEM((tm, d), BF16),
                        pltpu.VMEM((tm, d), F32),
                        pltpu.VMEM((nseg, SUBLANES + seg, tf), F32),
                        pltpu.VMEM((nseg, SUBLANES + seg, tf), F32),
                        pltpu.VMEM((nf, SUBLANES, tf), F32),
                        pltpu.VMEM((nf, SUBLANES, tf), F32)],
        compiler_params=_cparams(("arbitrary", "arbitrary")),
        name="conv_ffn",
    )(x, g.reshape(1, d), w_up, w_up, conv_w, conv_w, cb, cb, w_down, prev8, prev8, g_next.reshape(1, d))


def _rope_tables(pos, head_dim, rot, lanes_valid=LANES):
    half = rot // 2
    freqs = ROPE_THETA ** (-jnp.arange(half, dtype=F32) / half)
    ang = pos.astype(F32)[:, None] * freqs[None, :]
    cos, sin = jnp.cos(ang), jnp.sin(ang)
    lane = np.arange(LANES)
    d = lane % head_dim
    first = (d < half) & (lane < lanes_valid)
    second = (d >= half) & (d < rot) & (lane < lanes_valid)
    fidx = np.clip(np.where(d < half, d, d - half), 0, half - 1)
    cosl, sinl = cos[:, fidx], sin[:, fidx]
    c = jnp.where(first | second, cosl, 1.0)
    s1 = jnp.where(second, sinl, 0.0)
    s2 = jnp.where(first, -sinl, 0.0)
    return c, s1, s2, half


def _prep_weights(w_in_ab, w_out_ab, w_dqkv, w_uq, w_ukv, w_o_mla, w_up, w_down):
    n_even, n_odd = w_in_ab.shape[0], w_dqkv.shape[0]
    w = {}
    n_a = H_A * DH + 2 * KV_A * DH + H_I * D_IDX
    w["in_a"] = w_in_ab[:, :, :n_a].astype(BF16)
    pad = jnp.zeros((n_even, D_MODEL, LANES - D_IDX - H_I), BF16)
    w["in_b"] = jnp.concatenate([w_in_ab[:, :, n_a:n_a + D_IDX + H_I].astype(BF16), pad,
                                 w_in_ab[:, :, n_a + D_IDX + H_I:].astype(BF16)], axis=-1)
    w["out_ab"] = w_out_ab.astype(BF16)
    w["dqkr"] = jnp.concatenate([w_dqkv.astype(BF16),
                                 jnp.zeros((n_odd, D_MODEL, LANES - ROPE_C), BF16)], axis=-1)
    uq = w_uq.reshape(n_odd, Q_RANK, H_C, NOPE + ROPE_C)
    uqn = uq[..., :NOPE].reshape(n_odd, Q_RANK, H_C * NOPE).astype(BF16)
    uqr = uq[..., NOPE:].astype(BF16)
    uqr = jnp.concatenate([uqr, jnp.zeros_like(uqr)], axis=-1).reshape(n_odd, Q_RANK, H_C * LANES)
    w["uq"] = jnp.concatenate([uqn, uqr], axis=-1)
    w["ukv"] = w_ukv.astype(BF16)
    w["o_mla"] = w_o_mla.astype(BF16)
    w["up"] = w_up.astype(BF16)
    w["down"] = w_down.astype(BF16)
    return w


def _with_past(past, new, s_pad):
    b, _, fdim = new.shape
    parts = [new.astype(BF16)] if past is None else [past.astype(BF16), new.astype(BF16)]
    n = sum(p.shape[1] for p in parts)
    if s_pad > n:
        parts.append(jnp.zeros((b, s_pad - n, fdim), BF16))
    return parts[0] if len(parts) == 1 else jnp.concatenate(parts, axis=1)


def _trunk(x, past, conv_state, w, p, cfg):
    b, t, d = x.shape
    m = b * t
    q0 = 0 if past is None else past[0].shape[2]
    s_valid = q0 + t
    tk = cfg["tk"]
    s_pad = -(-s_valid // tk) * tk
    pos = q0 + jnp.tile(jnp.arange(t), b)
    tab_a = _rope_tables(pos, DH, ROT_A)
    tab_i = _rope_tables(pos, D_IDX, ROT_I)
    tab_ki = _rope_tables(pos, D_IDX, ROT_I, lanes_valid=D_IDX)
    tab_kr = _rope_tables(pos, ROPE_C, ROPE_C, lanes_valid=ROPE_C)

    xf = x.reshape(m, d)
    ab_rows, mla_rows, conv_rows = [], [], []
    sc_ab = DH ** -0.5 * LOG2E
    sc_c = (NOPE + ROPE_C) ** -0.5 * LOG2E
    h = _rmsnorm(xf, p["norm_mix"][0], BF16)
    for l in range(DEPTH):
        if l % 2 == 0:
            e = l // 2
            lp = None if past is None else tuple(c[e] for c in past[:5])
            qa, ka32, ka16, va32, va16, qi = _proj(
                [h], w["in_a"][e],
                [dict(width=H_A * DH, dtypes=(BF16,), rope=0, scale=sc_ab),
                 dict(width=KV_A * DH, dtypes=(F32, BF16), rope=0, split_heads=True),
                 dict(width=KV_A * DH, dtypes=(F32, BF16), split_heads=True),
                 dict(width=H_I * D_IDX, dtypes=(BF16,), rope=1)],
                tabs=(tab_a, tab_i), name="proj_in_a")
            kiwi, qb, kb32, kb16, vb32, vb16 = _proj(
                [h], w["in_b"][e],
                [dict(width=LANES, dtypes=(F32,), rope=0),
                 dict(width=H_B * DH, dtypes=(BF16,), scale=sc_ab),
                 dict(width=H_B * DH, dtypes=(F32, BF16), split_heads=True),
                 dict(width=H_B * DH, dtypes=(F32, BF16), split_heads=True)],
                tabs=(tab_ki,), name="proj_in_b")
            ki32 = kiwi[:, :D_IDX]
            wi = kiwi[:, D_IDX:D_IDX + H_I]
            ab_rows.append((ka32.reshape(b, t, KV_A, DH), va32.reshape(b, t, KV_A, DH),
                            ki32.reshape(b, t, D_IDX), kb32.reshape(b, t, H_B, DH),
                            vb32.reshape(b, t, H_B, DH)))

            def full(idx, new16):
                pst = None if lp is None else lp[idx].reshape(b, q0, -1)
                return _with_past(pst, new16.reshape(b, t, -1), s_pad)

            ki_all = full(2, ki32)
            zk = jnp.zeros_like(ki_all)
            ki2 = jnp.concatenate([ki_all, zk, zk, ki_all], axis=-1)
            oa = _dsa(qa.reshape(b, t, -1), qi.reshape(b, t, -1), wi.reshape(b, t, H_I), full(0, ka16),
                      full(1, va16), ki2, tq=cfg["tq_a"], tk=tk, s_valid=s_valid, q0=q0)
            ob = _stick_breaking(qb.reshape(b, t, -1), full(3, kb16), full(4, vb16),
                                 tq=cfg["tq_b"], tk=cfg["tk_b"], q0=q0, hpg=cfg["hpg_b"], nchunk=cfg["nc_b"])
            (xf,) = _proj([oa.reshape(m, -1), ob.reshape(m, -1)], w["out_ab"][e],
                          [dict(width=D_MODEL, dtypes=(F32,))], resid=xf, name="proj_out_ab")
        else:
            od = l // 2
            gains = jnp.concatenate([p["g_q"][od], p["g_kv"][od]])
            cq16, ckv32, ckv16, kr32w, kr16w = _proj(
                [h], w["dqkr"][od],
                [dict(width=Q_RANK, dtypes=(BF16,), gain_off=0),
                 dict(width=KV_RANK, dtypes=(F32, BF16), gain_off=Q_RANK),
                 dict(width=LANES, dtypes=(F32, BF16), rope=0)],
                tabs=(tab_kr,), gain=gains, name="proj_dqkr")
            mla_rows.append((ckv32.reshape(b, t, KV_RANK), kr32w[:, :ROPE_C].reshape(b, t, ROPE_C)))
            qn, qr = _proj([cq16], w["uq"][od],
                           [dict(width=H_C * NOPE, dtypes=(BF16,), scale=sc_c),
                            dict(width=H_C * LANES, dtypes=(BF16,), rope=0, scale=sc_c)],
                           tabs=(tab_kr,), name="proj_uq")
            pc = None if past is None else past[5][od]
            pk = None if past is None else jnp.pad(past[6][od], ((0, 0), (0, 0), (0, LANES - ROPE_C)))
            c_all = _with_past(pc, ckv16.reshape(b, t, KV_RANK), s_pad)
            kr_all = _with_past(pk, kr16w.reshape(b, t, LANES), s_pad)
            if cfg["latent_c"]:
                qa_hm, qr_hm = _absorb_q(qn, qr, w["ukv"][od])
                lat = _mla_latent(qa_hm, qr_hm, c_all, kr_all, t=t, tk=cfg["tk_c"], s_valid=s_valid, q0=q0)
                o = _latent_out(lat, w["ukv"][od])
            else:
                (kv,) = _proj([c_all.reshape(b * s_pad, KV_RANK)], w["ukv"][od],
                              [dict(width=H_C * (NOPE + VD), dtypes=(BF16,))], name="proj_ukv")
                o = _mla(qn.reshape(b, t, -1), qr.reshape(b, t, -1), kv.reshape(b, s_pad, -1), kr_all,
                         tq=cfg["tq_c"], tk=cfg["tk_c"], s_valid=s_valid, q0=q0, hpg=cfg["hpg_c"],
                         nchunk=cfg["nc_c"])
            (xf,) = _proj([o.reshape(m, -1)], w["o_mla"][od], [dict(width=D_MODEL, dtypes=(F32,))],
                          resid=xf, name="proj_o_mla")

        if conv_state is None:
            prev8 = jnp.zeros((b, SUBLANES, 2 * D_FF), F32)
        else:
            prev8 = jnp.concatenate([jnp.zeros((b, SUBLANES - (CONV_W - 1), 2 * D_FF), F32),
                                     conv_state[l]], axis=1)
        final = l == DEPTH - 1
        g_next = p["norm_final"] if final else p["norm_mix"][l + 1]
        xf, h, sg, sv = _ffn(xf, p["norm_ffn"][l], w["up"][l], p["conv_w"][l], p["conv_b"][l], w["down"][l],
                             prev8, g_next, F32 if final else BF16, seg=cfg["seg"], carried=cfg["carried"],
                             tm=cfg["tm_ffn"], tf=cfg["tf"])
        last = [s8.reshape(b, -1, SUBLANES, D_FF)[:, -1, SUBLANES - (CONV_W - 1):] for s8 in (sg, sv)]
        conv_rows.append(jnp.concatenate(last, axis=-1))

    y = h.reshape(b, t, d)
    new_ab = [jnp.stack([r[i] for r in ab_rows]) for i in range(5)]
    new_mla = [jnp.stack([r[i] for r in mla_rows]) for i in range(2)]
    return y, new_ab + new_mla + [jnp.stack(conv_rows)]


def kernel(x_prompt, x_sample, cache_k_a, cache_v_a, cache_idx_k, cache_k_b, cache_v_b, cache_ckv, cache_krope, state_conv, norm_mix, norm_ffn, norm_final, w_in_ab, w_out_ab, w_dqkv, g_q, g_kv, w_uq, w_ukv, w_o_mla, w_up, conv_w, conv_b, w_down):
    w = _prep_weights(w_in_ab, w_out_ab, w_dqkv, w_uq, w_ukv, w_o_mla, w_up, w_down)
    p = dict(norm_mix=norm_mix, norm_ffn=norm_ffn, norm_final=norm_final, g_q=g_q, g_kv=g_kv,
             conv_w=conv_w, conv_b=conv_b)
    cfg_p = dict(tk=512, tq_a=128, tq_b=256, tk_b=256, hpg_b=4, nc_b=1, tq_c=512, tk_c=1024, hpg_c=2, nc_c=2,
                 latent_c=False, seg=512, carried=True, tm_ffn=512, tf=512)
    t_s = x_sample.shape[1]
    cfg_s = dict(tk=256, tq_a=t_s, tq_b=t_s, tk_b=256, hpg_b=H_B, nc_b=1, tq_c=t_s, tk_c=256, hpg_c=8, nc_c=1,
                 latent_c=True, seg=t_s, carried=False, tm_ffn=x_sample.shape[0] * t_s, tf=512)
    y_p, st_p = _trunk(x_prompt, None, None, w, p, cfg_p)
    past = (cache_k_a, cache_v_a, cache_idx_k, cache_k_b, cache_v_b, cache_ckv, cache_krope)
    y_s, st_s = _trunk(x_sample, past, state_conv, w, p, cfg_s)
    return (y_p, y_s, *st_p, *st_s)
```

```python
import functools

import numpy as np
import jax
import jax.numpy as jnp
from jax import lax
from jax.experimental import pallas as pl
from jax.experimental.pallas import tpu as pltpu

F32 = jnp.float32
BF16 = jnp.bfloat16

D_MODEL = 2048
DEPTH = 4
CHUNK = 64
ROPE_THETA = 500000.0
EPS = 1e-6
H_A, KV_A, DH = 8, 2, 128
ROT_A = DH // 4
H_I, D_IDX = 16, 64
ROT_I = D_IDX // 4
TOPK_MAX = 256
H_B = 8
H_C, Q_RANK, KV_RANK, NOPE, ROPE_C, VD = 16, 512, 512, 128, 64, 128
D_FF = 5632
CONV_W = 3

LANES = 128
SUBLANES = 8
VMEM_LIMIT = 56 * 1024 * 1024
NEG = -1e30
INT_MIN = np.int32(-2 ** 31)
INT_MAX = np.int32(2 ** 31 - 1)
LOG2E = float(np.log2(np.e))


def _cparams(sem):
    return pltpu.CompilerParams(dimension_semantics=sem, vmem_limit_bytes=VMEM_LIMIT)


def _dot_nt(a, b):
    return lax.dot_general(a, b, (((1,), (1,)), ((), ())), preferred_element_type=F32)


def _rms_body(x_ref, g_ref, o_ref):
    x = x_ref[...]
    y = x * lax.rsqrt(jnp.mean(x * x, axis=-1, keepdims=True) + EPS)
    o_ref[...] = (y * g_ref[...]).astype(o_ref.dtype)


def _rmsnorm(x, g, out_dtype):
    m, d = x.shape
    tm = min(m, 512)
    return pl.pallas_call(
        _rms_body,
        grid=(m // tm,),
        in_specs=[pl.BlockSpec((tm, d), lambda i: (i, 0)),
                  pl.BlockSpec((1, d), lambda i: (0, 0))],
        out_specs=pl.BlockSpec((tm, d), lambda i: (i, 0)),
        out_shape=jax.ShapeDtypeStruct((m, d), out_dtype),
        compiler_params=_cparams(("parallel",)),
        name="rmsnorm",
    )(x, g.reshape(1, d))


PROJ_CHUNK = 512
COUNT_ROWS = 64


def _proj_body(*refs, nx, groups, tab_halves, has_gain, has_resid):
    it = iter(refs)
    x_refs = [next(it) for _ in range(nx)]
    w_ref = next(it)
    tabs = [(next(it), next(it), next(it)) for _ in tab_halves]
    g_ref = next(it) if has_gain else None
    r_ref = next(it) if has_resid else None
    outs = list(it)

    xs = [r[...] for r in x_refs]
    x = xs[0] if nx == 1 else jnp.concatenate(xs, axis=1)
    accs = []
    c0 = 0
    for grp in groups:
        width = grp["width"]
        chunk = min(width, PROJ_CHUNK)
        for cc in range(0, width, chunk):
            accs.append(jnp.dot(x, w_ref[:, c0 + cc:c0 + cc + chunk], preferred_element_type=F32))
        c0 += width
    accs = iter(accs)
    oi = 0
    for grp in groups:
        width = grp["width"]
        chunk = min(width, PROJ_CHUNK)
        for cc in range(0, width, chunk):
            acc = next(accs)
            if grp.get("gain_off") is not None:
                assert chunk == width
                go = grp["gain_off"]
                acc = (acc * lax.rsqrt(jnp.mean(acc * acc, axis=-1, keepdims=True) + EPS)
                       * g_ref[:, go:go + width])
            if grp.get("rope") is not None:
                c_ref, s1_ref, s2_ref = tabs[grp["rope"]]
                half = tab_halves[grp["rope"]]
                c, s1, s2 = c_ref[...], s1_ref[...], s2_ref[...]
                parts = []
                for gi in range(chunk // LANES):
                    xg = acc[:, gi * LANES:(gi + 1) * LANES]
                    parts.append(xg * c + pltpu.roll(xg, half, 1) * s1
                                 + pltpu.roll(xg, LANES - half, 1) * s2)
                acc = parts[0] if len(parts) == 1 else jnp.concatenate(parts, axis=1)
            if grp.get("scale") is not None:
                acc = acc * grp["scale"]
            if has_resid:
                acc = acc + r_ref[:, cc:cc + chunk]
            for k in range(len(grp["dtypes"])):
                o = outs[oi + k]
                if len(o.shape) == 3:
                    for gi in range(chunk // LANES):
                        o[:, cc // LANES + gi, :] = acc[:, gi * LANES:(gi + 1) * LANES].astype(o.dtype)
                else:
                    o[:, cc:cc + chunk] = acc.astype(o.dtype)
        oi += len(grp["dtypes"])


def _proj(xs, w, groups, *, tabs=(), gain=None, resid=None, name="proj"):
    m = xs[0].shape[0]
    k, n = w.shape
    tm = min(m, 512)
    assert m % tm == 0 and n == sum(g["width"] for g in groups) and k == sum(x.shape[1] for x in xs)
    assert resid is None or len(groups) == 1
    in_specs = [pl.BlockSpec((tm, x.shape[1]), lambda i: (i, 0)) for x in xs]
    in_specs.append(pl.BlockSpec((k, n), lambda i: (0, 0)))
    args = list(xs) + [w]
    for c, s1, s2, _ in tabs:
        for t in (c, s1, s2):
            in_specs.append(pl.BlockSpec((tm, LANES), lambda i: (i, 0)))
            args.append(t)
    if gain is not None:
        in_specs.append(pl.BlockSpec((1, gain.shape[0]), lambda i: (0, 0)))
        args.append(gain.reshape(1, -1))
    if resid is not None:
        in_specs.append(pl.BlockSpec((tm, n), lambda i: (i, 0)))
        args.append(resid)
    out_specs, out_shape = [], []
    for g in groups:
        for dt in g["dtypes"]:
            if g.get("split_heads") and dt == F32:
                nh = g["width"] // LANES
                out_specs.append(pl.BlockSpec((tm, nh, LANES), lambda i: (i, 0, 0)))
                out_shape.append(jax.ShapeDtypeStruct((m, nh, LANES), dt))
            else:
                out_specs.append(pl.BlockSpec((tm, g["width"]), lambda i: (i, 0)))
                out_shape.append(jax.ShapeDtypeStruct((m, g["width"]), dt))
    body = functools.partial(_proj_body, nx=len(xs), groups=groups, tab_halves=[t[3] for t in tabs],
                             has_gain=gain is not None, has_resid=resid is not None)
    return pl.pallas_call(
        body,
        grid=(m // tm,),
        in_specs=in_specs,
        out_specs=out_specs,
        out_shape=out_shape,
        compiler_params=_cparams(("parallel",)),
        name=name,
    )(*args)


def _dsa_body(qa_ref, qi_ref, wi_ref, ka_ref, va_ref, ki_ref, tri_ref, o_ref,
              key_scr, qs_scr, m_scr, l_scr, acc_scr, *, tq, tk, s_valid, q0, topk):
    i = pl.program_id(1)
    q_first = q0 + i * tq
    q_last = q_first + tq - 1
    adm_end = jnp.minimum((q_last // CHUNK + 1) * CHUNK, s_valid)
    nkb = (adm_end + tk - 1) // tk
    q_pos = q_first + lax.broadcasted_iota(jnp.int32, (tq, tk), 0)
    q_lim = jnp.minimum(((q_pos >> 6) + 1) * CHUNK, s_valid)
    col = lax.broadcasted_iota(jnp.int32, (tq, tk), 1)
    rep = H_A // KV_A

    w = wi_ref[0] * ((H_I * D_IDX) ** -0.5)
    pg = 4
    pairs = LANES // D_IDX
    qi = qi_ref[0]
    q_groups = [jnp.concatenate([qi[:, (g * pg + pp) * LANES:(g * pg + pp + 1) * LANES]
                                 for pp in range(pg)], axis=0) for g in range(H_I // (pairs * pg))]

    def to_key(x):
        bits = pltpu.bitcast(x, jnp.int32)
        return bits ^ ((bits >> 31) & INT_MAX)

    transposed = key_scr.shape[1] == tq

    if transposed:
        wt = w
        krow = lax.broadcasted_iota(jnp.int32, (tk, tq), 0)
        q_lim_t = jnp.minimum((((q_first + lax.broadcasted_iota(jnp.int32, (tk, tq), 1)) >> 6) + 1) * CHUNK,
                              s_valid)

    def p1(j, c):
        k0 = pl.multiple_of(j * tk, tk)
        if transposed:
            sc = jnp.zeros((tk, tq), F32)
            for g, qg in enumerate(q_groups):
                for half in range(pairs):
                    rel = _dot_nt(ki_ref[0, pl.ds(k0, tk), half * LANES:(half + 1) * LANES], qg)
                    for pp in range(pg):
                        h = (g * pg + pp) * pairs + half
                        sc = sc + jnp.maximum(rel[:, pp * tq:(pp + 1) * tq], 0.0) * wt[h:h + 1, :]
            key_scr[pl.ds(k0, tk), :] = jnp.where(k0 + krow < q_lim_t, to_key(sc), INT_MIN)
            return c
        sc = jnp.zeros((tq, tk), F32)
        for g, qg in enumerate(q_groups):
            for half in range(pairs):
                rel = _dot_nt(qg, ki_ref[0, pl.ds(k0, tk), half * LANES:(half + 1) * LANES])
                for pp in range(pg):
                    h = (g * pg + pp) * pairs + half
                    sc = sc + jnp.maximum(rel[pp * tq:(pp + 1) * tq], 0.0) * w[:, h:h + 1]
        key_scr[:, pl.ds(k0, tk)] = jnp.where(k0 + col < q_lim, to_key(sc), INT_MIN)
        return c

    lax.fori_loop(0, nkb, p1, 0)

    def count_ge(v):
        def cb(j, c):
            k0 = pl.multiple_of(j * tk, tk)
            if transposed:
                ge = jnp.where(key_scr[pl.ds(k0, tk), :] >= v, 1.0, 0.0)
                return c + jnp.sum(ge.reshape(tk // COUNT_ROWS, COUNT_ROWS, tq), axis=0)
            ge = jnp.where(key_scr[:, pl.ds(k0, tk)] >= v, 1.0, 0.0)
            part = ge[:, 0:LANES]
            for cc in range(1, tk // LANES):
                part = part + ge[:, cc * LANES:(cc + 1) * LANES]
            return c + part
        if transposed:
            c = lax.fori_loop(0, nkb, cb, jnp.zeros((COUNT_ROWS, tq), F32))
            return jnp.sum(c, axis=0, keepdims=True)
        c = lax.fori_loop(0, nkb, cb, jnp.zeros((tq, LANES), F32))
        return jnp.sum(c, axis=1, keepdims=True)

    def bis(_, lohi):
        lo, hi = lohi
        mid = (lo >> 1) + (hi >> 1) + (lo & hi & 1)
        ok = count_ge(mid) >= float(topk)
        return jnp.where(ok, mid, lo), jnp.where(ok, hi, mid)

    vshape = (1, tq) if transposed else (tq, 1)
    thr, _ = lax.fori_loop(0, 32, bis, (jnp.full(vshape, INT_MIN, jnp.int32),
                                        jnp.full(vshape, INT_MAX, jnp.int32)))
    need = float(topk) - count_ge(thr + 1)
    if transposed:
        thr = pltpu.bitcast(jnp.transpose(pltpu.bitcast(jnp.broadcast_to(thr, (LANES, tq)), F32)), jnp.int32)
        need = jnp.transpose(jnp.broadcast_to(need, (LANES, tq)))
        thr = jnp.concatenate([thr] * (tk // LANES), axis=1)
        need = jnp.concatenate([need] * (tk // LANES), axis=1)

    qa = qa_ref[0]
    for g in range(KV_A):
        qs_scr[g] = jnp.concatenate(
            [qa[:, (g * rep + r) * DH:(g * rep + r + 1) * DH] for r in range(rep)], axis=0)
    m_scr[...] = jnp.full(m_scr.shape, NEG, F32)
    l_scr[...] = jnp.zeros(l_scr.shape, F32)
    acc_scr[...] = jnp.zeros(acc_scr.shape, F32)

    def p3(j, eqc):
        k0 = pl.multiple_of(j * tk, tk)
        if transposed:
            key = pltpu.bitcast(jnp.transpose(pltpu.bitcast(key_scr[pl.ds(k0, tk), :], F32)), jnp.int32)
        else:
            key = key_scr[:, pl.ds(k0, tk)]
        eq = key == thr
        eqf = jnp.where(eq, 1.0, 0.0)
        prefix = jnp.dot(eqf.astype(BF16), tri_ref[...], preferred_element_type=F32) + eqc
        bias = jnp.where(key > thr, 0.0, jnp.where(eq, jnp.where(prefix < need, 0.0, NEG), NEG))
        bias = jnp.where(key == INT_MIN, NEG, bias)
        eqc = eqc + jnp.sum(eqf, axis=1, keepdims=True)
        ss = [_dot_nt(qs_scr[g], ka_ref[0, pl.ds(k0, tk), g * DH:(g + 1) * DH])
              for g in range(KV_A)]
        for g in range(KV_A):
            vg = va_ref[0, pl.ds(k0, tk), g * DH:(g + 1) * DH]
            s = ss[g]
            ps, alphas = [], []
            for r in range(rep):
                h = g * rep + r
                sr = s[r * tq:(r + 1) * tq] + bias
                m_prev = m_scr[h]
                m_new = jnp.maximum(m_prev, jnp.max(sr, axis=1, keepdims=True))
                alpha = jnp.exp2(m_prev - m_new)
                p = jnp.exp2(sr - jnp.concatenate([m_new] * (tk // LANES), axis=1))
                l_scr[h] = alpha * l_scr[h] + jnp.sum(p, axis=1, keepdims=True)
                m_scr[h] = m_new
                ps.append(p.astype(BF16))
                alphas.append(alpha)
            for r in range(rep):
                h = g * rep + r
                acc_scr[h] = acc_scr[h] * alphas[r] + jnp.dot(ps[r], vg, preferred_element_type=F32)
        return eqc

    lax.fori_loop(0, nkb, p3, jnp.zeros((tq, 1), F32))
    for h in range(H_A):
        o_ref[0, :, h * DH:(h + 1) * DH] = (acc_scr[h] / l_scr[h]).astype(o_ref.dtype)


def _dsa(qa, qi, wi, ka, va, ki, *, tq, tk, s_valid, q0):
    b, t, _ = qa.shape
    s_pad = ka.shape[1]
    assert s_pad % tk == 0 and t % tq == 0 and tk >= TOPK_MAX
    topk = min(TOPK_MAX, s_valid // 4)
    tri = jnp.asarray(np.triu(np.ones((tk, tk), np.float32), 1), BF16)
    body = functools.partial(_dsa_body, tq=tq, tk=tk, s_valid=s_valid, q0=q0, topk=topk)
    rep = H_A // KV_A
    transposed = tq % LANES == 0
    if transposed:
        wi = jnp.transpose(wi, (0, 2, 1))
        wi_spec = pl.BlockSpec((1, H_I, tq), lambda bi, i: (bi, 0, i))
    else:
        wi_spec = pl.BlockSpec((1, tq, H_I), lambda bi, i: (bi, i, 0))
    return pl.pallas_call(
        body,
        grid=(b, t // tq),
        in_specs=[pl.BlockSpec((1, tq, H_A * DH), lambda bi, i: (bi, i, 0)),
                  pl.BlockSpec((1, tq, H_I * D_IDX), lambda bi, i: (bi, i, 0)),
                  wi_spec,
                  pl.BlockSpec((1, s_pad, KV_A * DH), lambda bi, i: (bi, 0, 0)),
                  pl.BlockSpec((1, s_pad, KV_A * DH), lambda bi, i: (bi, 0, 0)),
                  pl.BlockSpec((1, s_pad, 2 * LANES), lambda bi, i: (bi, 0, 0)),
                  pl.BlockSpec((tk, tk), lambda bi, i: (0, 0))],
        out_specs=pl.BlockSpec((1, tq, H_A * DH), lambda bi, i: (bi, i, 0)),
        out_shape=jax.ShapeDtypeStruct((b, t, H_A * DH), BF16),
        scratch_shapes=[pltpu.VMEM((s_pad, tq) if transposed else (tq, s_pad), jnp.int32),
                        pltpu.VMEM((KV_A, rep * tq, DH), BF16),
                        pltpu.VMEM((H_A, tq, LANES), F32),
                        pltpu.VMEM((H_A, tq, LANES), F32),
                        pltpu.VMEM((H_A, tq, DH), F32)],
        compiler_params=_cparams(("parallel", "arbitrary")),
        name="dsa",
    )(qa, qi, wi, ka, va, ki, tri)


def _sb_body(q_ref, k_ref, v_ref, low_ref, o_ref, *, tq, tk, s_pad, q0, hpg, nchunk):
    i = pl.program_id(2)
    tqc = tq // nchunk
    q_first = q0 + i * tq
    q_last = q_first + tq - 1
    nkb = jnp.minimum((q_last + tk - 1) // tk, s_pad // tk)
    n_full = jnp.minimum(q_first // tk, nkb)
    row = lax.broadcasted_iota(jnp.int32, (tqc, tk), 0)
    col = lax.broadcasted_iota(jnp.int32, (tqc, tk), 1)
    chains = [(g, c) for g in range(hpg) for c in range(nchunk)]

    def step(j, carry, masked):
        k0 = pl.multiple_of(j * tk, tk)
        zs = [_dot_nt(q_ref[0, c * tqc:(c + 1) * tqc, g * DH:(g + 1) * DH],
                      k_ref[0, pl.ds(k0, tk), g * DH:(g + 1) * DH]) for g, c in chains]
        lbs, lks, sufs, causals = [], [], [], []
        for (g, c), z in zip(chains, zs):
            lb = jnp.minimum(z, 0.0) - jnp.log(1.0 + jnp.exp2(-jnp.abs(z))) * LOG2E
            lk = lb - z
            causal = None
            if masked:
                causal = k0 + col < q_first + c * tqc + row
                lk = jnp.where(causal, lk, 0.0)
            hi = lk.astype(BF16)
            lo = (lk - hi.astype(F32)).astype(BF16)
            lbs.append(lb)
            lks.append(lk)
            causals.append(causal)
            sufs.append(jnp.dot(jnp.concatenate([hi, lo], axis=1), low_ref[...], preferred_element_type=F32))
        out = []
        for n, (g, c) in enumerate(chains):
            run, acc = carry[2 * n], carry[2 * n + 1]
            wgt = jnp.exp2(lbs[n] + sufs[n] + run)
            if masked:
                wgt = jnp.where(causals[n], wgt, 0.0)
            acc = acc + jnp.dot(wgt.astype(BF16), v_ref[0, pl.ds(k0, tk), g * DH:(g + 1) * DH],
                                preferred_element_type=F32)
            run = run + jnp.sum(lks[n], axis=1, keepdims=True)
            out += [run, acc]
        return tuple(out)

    carry = []
    for _ in chains:
        carry += [jnp.zeros((tqc, 1), F32), jnp.zeros((tqc, DH), F32)]
    carry = tuple(carry)
    carry = lax.fori_loop(0, nkb - n_full, lambda s, c: step(nkb - 1 - s, c, True), carry)
    carry = lax.fori_loop(0, n_full, lambda s, c: step(n_full - 1 - s, c, False), carry)
    for n, (g, c) in enumerate(chains):
        o_ref[0, c * tqc:(c + 1) * tqc, g * DH:(g + 1) * DH] = carry[2 * n + 1].astype(o_ref.dtype)


def _stick_breaking(qb, kb, vb, *, tq, tk, q0, hpg, nchunk):
    b, t, _ = qb.shape
    s_pad = kb.shape[1]
    assert s_pad % tk == 0 and t % tq == 0 and H_B % hpg == 0 and tq % nchunk == 0
    low = np.tril(np.ones((tk, tk), np.float32), -1)
    low = jnp.asarray(np.concatenate([low, low], axis=0), BF16)
    body = functools.partial(_sb_body, tq=tq, tk=tk, s_pad=s_pad, q0=q0, hpg=hpg, nchunk=nchunk)
    wd = hpg * DH
    return pl.pallas_call(
        body,
        grid=(b, H_B // hpg, t // tq),
        in_specs=[pl.BlockSpec((1, tq, wd), lambda bi, h, i: (bi, i, h)),
                  pl.BlockSpec((1, s_pad, wd), lambda bi, h, i: (bi, 0, h)),
                  pl.BlockSpec((1, s_pad, wd), lambda bi, h, i: (bi, 0, h)),
                  pl.BlockSpec((2 * tk, tk), lambda bi, h, i: (0, 0))],
        out_specs=pl.BlockSpec((1, tq, wd), lambda bi, h, i: (bi, i, h)),
        out_shape=jax.ShapeDtypeStruct((b, t, H_B * DH), BF16),
        compiler_params=_cparams(("parallel", "parallel", "arbitrary")),
        name="stick_breaking",
    )(qb, kb, vb, low)


def _mla_body(qn_ref, qr_ref, kv_ref, kr_ref, o_ref, *, tq, tk, s_valid, q0, hpg, nchunk):
    i = pl.program_id(2)
    tqc = tq // nchunk
    q_first = q0 + i * tq
    q_last = q_first + tq - 1
    lim_first = jnp.minimum((q_first // CHUNK + 1) * CHUNK, s_valid)
    lim_last = jnp.minimum((q_last // CHUNK + 1) * CHUNK, s_valid)
    nkb = (lim_last + tk - 1) // tk
    n_full = lim_first // tk
    row = lax.broadcasted_iota(jnp.int32, (tqc, tk), 0)
    col = lax.broadcasted_iota(jnp.int32, (tqc, tk), 1)
    chains = [(g, c) for g in range(hpg) for c in range(nchunk)]
    hw = NOPE + VD

    def step(j, carry, masked):
        k0 = pl.multiple_of(j * tk, tk)
        kr = kr_ref[0, pl.ds(k0, tk), :]
        ss = []
        for g, c in chains:
            rows = slice(c * tqc, (c + 1) * tqc)
            qc = jnp.concatenate([qn_ref[0, rows, g * NOPE:(g + 1) * NOPE],
                                  qr_ref[0, rows, g * LANES:(g + 1) * LANES]], axis=1)
            kc = jnp.concatenate([kv_ref[0, pl.ds(k0, tk), g * hw:g * hw + NOPE], kr], axis=1)
            ss.append(_dot_nt(qc, kc))
        ps, stats = [], []
        for n, (g, c) in enumerate(chains):
            m_prev = carry[2 * n]
            s = ss[n]
            if masked:
                lim = jnp.minimum((((q_first + c * tqc + row) >> 6) + 1) * CHUNK, s_valid)
                s = jnp.where(k0 + col < lim, s, NEG)
            m_new = jnp.maximum(m_prev, jnp.max(s, axis=1, keepdims=True))
            alpha = jnp.exp2(m_prev - m_new)
            p = jnp.exp2(s - jnp.concatenate([m_new] * (tk // LANES), axis=1))
            stats.append((m_new, alpha))
            ps.append(p.astype(BF16))
        out = []
        for n, (g, c) in enumerate(chains):
            m_new, alpha = stats[n]
            vb = jnp.concatenate([kv_ref[0, pl.ds(k0, tk), g * hw + NOPE:(g + 1) * hw], ones_k], axis=1)
            acc = (carry[2 * n + 1] * jnp.concatenate([alpha, alpha], axis=1)
                   + jnp.dot(ps[n], vb, preferred_element_type=F32))
            out += [m_new, acc]
        return tuple(out)

    ones_k = jnp.ones((tk, LANES), BF16)
    carry = []
    for _ in chains:
        carry += [jnp.full((tqc, LANES), NEG, F32), jnp.zeros((tqc, VD + LANES), F32)]
    carry = tuple(carry)
    carry = lax.fori_loop(0, n_full, lambda j, c: step(j, c, False), carry)
    carry = lax.fori_loop(n_full, nkb, lambda j, c: step(j, c, True), carry)
    for n, (g, c) in enumerate(chains):
        acc = carry[2 * n + 1]
        o_ref[0, c * tqc:(c + 1) * tqc, g * VD:(g + 1) * VD] = (acc[:, :VD] / acc[:, VD:]).astype(o_ref.dtype)


def _mla(qn, qr, kv, kr, *, tq, tk, s_valid, q0, hpg, nchunk):
    b, t, _ = qn.shape
    s_pad = kv.shape[1]
    assert s_pad % tk == 0 and t % tq == 0 and H_C % hpg == 0 and tq % nchunk == 0
    body = functools.partial(_mla_body, tq=tq, tk=tk, s_valid=s_valid, q0=q0, hpg=hpg, nchunk=nchunk)
    return pl.pallas_call(
        body,
        grid=(b, H_C // hpg, t // tq),
        in_specs=[pl.BlockSpec((1, tq, hpg * NOPE), lambda bi, h, i: (bi, i, h)),
                  pl.BlockSpec((1, tq, hpg * LANES), lambda bi, h, i: (bi, i, h)),
                  pl.BlockSpec((1, s_pad, hpg * (NOPE + VD)), lambda bi, h, i: (bi, 0, h)),
                  pl.BlockSpec((1, s_pad, LANES), lambda bi, h, i: (bi, 0, 0))],
        out_specs=pl.BlockSpec((1, tq, hpg * VD), lambda bi, h, i: (bi, i, h)),
        out_shape=jax.ShapeDtypeStruct((b, t, H_C * VD), BF16),
        compiler_params=_cparams(("parallel", "parallel", "arbitrary")),
        name="mla",
    )(qn, qr, kv, kr)


def _absorb_q_body(qn_ref, qr_ref, wk_ref, qa_ref, qrh_ref):
    qa_ref[0] = _dot_nt(qn_ref[...], wk_ref[...]).astype(qa_ref.dtype)
    qrh_ref[0] = qr_ref[...]


def _absorb_q(qn, qr, w_ukv):
    m = qn.shape[0]
    return pl.pallas_call(
        _absorb_q_body,
        grid=(H_C,),
        in_specs=[pl.BlockSpec((m, NOPE), lambda h: (0, h)),
                  pl.BlockSpec((m, LANES), lambda h: (0, h)),
                  pl.BlockSpec((KV_RANK, NOPE), lambda h: (0, 2 * h))],
        out_specs=[pl.BlockSpec((1, m, KV_RANK), lambda h: (h, 0, 0)),
                   pl.BlockSpec((1, m, LANES), lambda h: (h, 0, 0))],
        out_shape=[jax.ShapeDtypeStruct((H_C, m, KV_RANK), BF16),
                   jax.ShapeDtypeStruct((H_C, m, LANES), BF16)],
        compiler_params=_cparams(("parallel",)),
        name="mla_absorb_q",
    )(qn, qr, w_ukv)


def _mla_latent_body(qa_ref, qr_ref, c_ref, kr_ref, o_ref, *, t, tk, s_valid, q0, nsplit):
    hs = H_C // nsplit
    rows = hs * t
    lim_first = min((q0 // CHUNK + 1) * CHUNK, s_valid)
    lim_last = min(((q0 + t - 1) // CHUNK + 1) * CHUNK, s_valid)
    nkb = -(-lim_last // tk)
    n_full = lim_first // tk
    q_pos = q0 + lax.broadcasted_iota(jnp.int32, (hs, t, tk), 1).reshape(rows, tk)
    lim = jnp.minimum(((q_pos >> 6) + 1) * CHUNK, s_valid)
    col = lax.broadcasted_iota(jnp.int32, (rows, tk), 1)
    ones_k = jnp.ones((tk, LANES), BF16)
    qs = [jnp.concatenate([qa_ref[n * hs:(n + 1) * hs].reshape(rows, KV_RANK),
                           qr_ref[n * hs:(n + 1) * hs].reshape(rows, LANES)], axis=1) for n in range(nsplit)]

    def step(j, carry, masked):
        k0 = pl.multiple_of(j * tk, tk)
        cb = c_ref[0, pl.ds(k0, tk), :]
        kc = jnp.concatenate([cb, kr_ref[0, pl.ds(k0, tk), :]], axis=1)
        vc = jnp.concatenate([cb, ones_k], axis=1)
        ss = [_dot_nt(q, kc) for q in qs]
        ps, stats = [], []
        for n in range(nsplit):
            s = ss[n]
            if masked:
                s = jnp.where(k0 + col < lim, s, NEG)
            m_prev = carry[2 * n]
            m_new = jnp.maximum(m_prev, jnp.max(s, axis=1, keepdims=True))
            alpha = jnp.exp2(m_prev - m_new)
            ps.append(jnp.exp2(s - jnp.concatenate([m_new] * (tk // LANES), axis=1)).astype(BF16))
            stats.append((m_new, alpha))
        out = []
        for n in range(nsplit):
            m_new, alpha = stats[n]
            acc = (carry[2 * n + 1] * jnp.concatenate([alpha] * (KV_RANK // LANES + 1), axis=1)
                   + jnp.dot(ps[n], vc, preferred_element_type=F32))
            out += [m_new, acc]
        return tuple(out)

    carry = []
    for _ in range(nsplit):
        carry += [jnp.full((rows, LANES), NEG, F32), jnp.zeros((rows, KV_RANK + LANES), F32)]
    carry = tuple(carry)
    carry = lax.fori_loop(0, n_full, lambda j, c: step(j, c, False), carry)
    carry = lax.fori_loop(n_full, nkb, lambda j, c: step(j, c, True), carry)
    for n in range(nsplit):
        acc = carry[2 * n + 1]
        lat = acc[:, :KV_RANK] / jnp.concatenate([acc[:, KV_RANK:]] * (KV_RANK // LANES), axis=1)
        o_ref[0, n * hs:(n + 1) * hs] = lat.reshape(hs, t, KV_RANK).astype(o_ref.dtype)


def _mla_latent(qa_hm, qr_hm, c_all, kr_all, *, t, tk, s_valid, q0):
    b, s_pad, _ = c_all.shape
    assert s_pad % tk == 0
    body = functools.partial(_mla_latent_body, t=t, tk=tk, s_valid=s_valid, q0=q0, nsplit=2)
    return pl.pallas_call(
        body,
        grid=(b,),
        in_specs=[pl.BlockSpec((H_C, t, KV_RANK), lambda bi: (0, bi, 0)),
                  pl.BlockSpec((H_C, t, LANES), lambda bi: (0, bi, 0)),
                  pl.BlockSpec((1, s_pad, KV_RANK), lambda bi: (bi, 0, 0)),
                  pl.BlockSpec((1, s_pad, LANES), lambda bi: (bi, 0, 0))],
        out_specs=pl.BlockSpec((1, H_C, t, KV_RANK), lambda bi: (bi, 0, 0, 0)),
        out_shape=jax.ShapeDtypeStruct((b, H_C, t, KV_RANK), BF16),
        compiler_params=_cparams(("parallel",)),
        name="mla_latent",
    )(qa_hm, qr_hm, c_all, kr_all)


def _latent_out_body(x_ref, wv_ref, o_ref):
    nb, _, t, r = x_ref.shape
    o_ref[...] = jnp.dot(x_ref[...].reshape(nb * t, r), wv_ref[...],
                         preferred_element_type=F32).astype(o_ref.dtype)


def _latent_out(lat, w_ukv):
    b, _, t, _ = lat.shape
    return pl.pallas_call(
        _latent_out_body,
        grid=(H_C,),
        in_specs=[pl.BlockSpec((b, 1, t, KV_RANK), lambda h: (0, h, 0, 0)),
                  pl.BlockSpec((KV_RANK, VD), lambda h: (0, 2 * h + 1))],
        out_specs=pl.BlockSpec((b * t, VD), lambda h: (0, h)),
        out_shape=jax.ShapeDtypeStruct((b * t, H_C * VD), BF16),
        compiler_params=_cparams(("parallel",)),
        name="mla_latent_out",
    )(lat, w_ukv)


def _ffn_body(x_ref, g_ref, wg_ref, wv_ref, cwg_ref, cwv_ref, cbg_ref, cbv_ref, wd_ref, pg_ref, pv_ref, gn_ref,
              o_ref, hn_ref, sg_ref, sv_ref, h_scr, acc_scr, ug_buf, uv_buf, cg_scr, cv_scr, *, seg, carried):
    i = pl.program_id(0)
    f = pl.program_id(1)
    tm = x_ref.shape[0]

    @pl.when(f == 0)
    def _():
        x = x_ref[...]
        y = x * lax.rsqrt(jnp.mean(x * x, axis=-1, keepdims=True) + EPS)
        h_scr[...] = (y * g_ref[...]).astype(BF16)
        acc_scr[...] = x

    h = h_scr[...]
    halves = ((wg_ref, cwg_ref, cbg_ref, pg_ref, sg_ref, ug_buf, cg_scr),
              (wv_ref, cwv_ref, cbv_ref, pv_ref, sv_ref, uv_buf, cv_scr))
    ys = []
    for w_ref, cw_ref, cb_ref, p_ref, s_ref, buf, c_scr in halves:
        u = jnp.dot(h, w_ref[...], preferred_element_type=F32)
        cw = cw_ref[...]
        parts = []
        for sgi in range(tm // seg):
            us = u[sgi * seg:(sgi + 1) * seg]
            if carried:
                head = jnp.where(i == 0, p_ref[0], c_scr[f])
            else:
                head = p_ref[sgi]
            buf[sgi, 0:SUBLANES] = head
            buf[sgi, SUBLANES:SUBLANES + seg] = us
            y = (cw[2:3] * us + cw[1:2] * buf[sgi, SUBLANES - 1:SUBLANES - 1 + seg]
                 + cw[0:1] * buf[sgi, SUBLANES - 2:SUBLANES - 2 + seg] + cb_ref[...])
            parts.append(y)
            last = us[seg - SUBLANES:seg]
            s_ref[sgi] = last
            if carried:
                c_scr[f] = last
        ys.append(parts[0] if len(parts) == 1 else jnp.concatenate(parts, axis=0))
    yg, yv = ys
    act = (yg / (1.0 + jnp.exp(-yg))) * yv
    acc_scr[...] += jnp.dot(act.astype(BF16), wd_ref[...], preferred_element_type=F32)

    @pl.when(f == pl.num_programs(1) - 1)
    def _():
        a = acc_scr[...]
        o_ref[...] = a
        y = a * lax.rsqrt(jnp.mean(a * a, axis=-1, keepdims=True) + EPS)
        hn_ref[...] = (y * gn_ref[...]).astype(hn_ref.dtype)


def _ffn(x, g, w_up, conv_w, conv_b, w_down, prev8, g_next, next_dtype, *, seg, carried, tm, tf):
    m, d = x.shape
    nf = D_FF // tf
    nseg = tm // seg
    assert m % tm == 0 and D_FF % tf == 0 and tm % seg == 0
    sidx = (lambda i: 0) if carried else (lambda i: i)
    body = functools.partial(_ffn_body, seg=seg, carried=carried)
    cb = conv_b.reshape(1, 2 * D_FF)
    return pl.pallas_call(
        body,
        grid=(m // tm, nf),
        in_specs=[pl.BlockSpec((tm, d), lambda i, f: (i, 0)),
                  pl.BlockSpec((1, d), lambda i, f: (0, 0)),
                  pl.BlockSpec((d, tf), lambda i, f: (0, f)),
                  pl.BlockSpec((d, tf), lambda i, f: (0, nf + f)),
                  pl.BlockSpec((CONV_W, tf), lambda i, f: (0, f)),
                  pl.BlockSpec((CONV_W, tf), lambda i, f: (0, nf + f)),
                  pl.BlockSpec((1, tf), lambda i, f: (0, f)),
                  pl.BlockSpec((1, tf), lambda i, f: (0, nf + f)),
                  pl.BlockSpec((tf, d), lambda i, f: (f, 0)),
                  pl.BlockSpec((nseg, SUBLANES, tf), lambda i, f: (sidx(i), 0, f)),
                  pl.BlockSpec((nseg, SUBLANES, tf), lambda i, f: (sidx(i), 0, nf + f)),
                  pl.BlockSpec((1, d), lambda i, f: (0, 0))],
        out_specs=[pl.BlockSpec((tm, d), lambda i, f: (i, 0)),
                   pl.BlockSpec((tm, d), lambda i, f: (i, 0)),
                   pl.BlockSpec((nseg, SUBLANES, tf), lambda i, f: (i, 0, f)),
                   pl.BlockSpec((nseg, SUBLANES, tf), lambda i, f: (i, 0, f))],
        out_shape=[jax.ShapeDtypeStruct((m, d), F32),
                   jax.ShapeDtypeStruct((m, d), next_dtype),
                   jax.ShapeDtypeStruct((m // seg, SUBLANES, D_FF), F32),
                   jax.ShapeDtypeStruct((m // seg, SUBLANES, D_FF), F32)],
        scratch_shapes=[pltpu.VMEM((tm, d), BF16),
                        pltpu.VMEM((tm, d), F32),
                        pltpu.VMEM((nseg, SUBLANES + seg, tf), F32),
                        pltpu.VMEM((nseg, SUBLANES + seg, tf), F32),
                        pltpu.VMEM((nf, SUBLANES, tf), F32),
                        pltpu.VMEM((nf, SUBLANES, tf), F32)],
        compiler_params=_cparams(("arbitrary", "arbitrary")),
        name="conv_ffn",
    )(x, g.reshape(1, d), w_up, w_up, conv_w, conv_w, cb, cb, w_down, prev8, prev8, g_next.reshape(1, d))


def _rope_tables(pos, head_dim, rot, lanes_valid=LANES):
    half = rot // 2
    freqs = ROPE_THETA ** (-jnp.arange(half, dtype=F32) / half)
    ang = pos.astype(F32)[:, None] * freqs[None, :]
    cos, sin = jnp.cos(ang), jnp.sin(ang)
    lane = np.arange(LANES)
    d = lane % head_dim
    first = (d < half) & (lane < lanes_valid)
    second = (d >= half) & (d < rot) & (lane < lanes_valid)
    fidx = np.clip(np.where(d < half, d, d - half), 0, half - 1)
    cosl, sinl = cos[:, fidx], sin[:, fidx]
    c = jnp.where(first | second, cosl, 1.0)
    s1 = jnp.where(second, sinl, 0.0)
    s2 = jnp.where(first, -sinl, 0.0)
    return c, s1, s2, half


def _prep_weights(w_in_ab, w_out_ab, w_dqkv, w_uq, w_ukv, w_o_mla, w_up, w_down):
    n_even, n_odd = w_in_ab.shape[0], w_dqkv.shape[0]
    w = {}
    n_a = H_A * DH + 2 * KV_A * DH + H_I * D_IDX
    w["in_a"] = w_in_ab[:, :, :n_a].astype(BF16)
    pad = jnp.zeros((n_even, D_MODEL, LANES - D_IDX - H_I), BF16)
    w["in_b"] = jnp.concatenate([w_in_ab[:, :, n_a:n_a + D_IDX + H_I].astype(BF16), pad,
                                 w_in_ab[:, :, n_a + D_IDX + H_I:].astype(BF16)], axis=-1)
    w["out_ab"] = w_out_ab.astype(BF16)
    w["dqkr"] = jnp.concatenate([w_dqkv.astype(BF16),
                                 jnp.zeros((n_odd, D_MODEL, LANES - ROPE_C), BF16)], axis=-1)
    uq = w_uq.reshape(n_odd, Q_RANK, H_C, NOPE + ROPE_C)
    uqn = uq[..., :NOPE].reshape(n_odd, Q_RANK, H_C * NOPE).astype(BF16)
    uqr = uq[..., NOPE:].astype(BF16)
    uqr = jnp.concatenate([uqr, jnp.zeros_like(uqr)], axis=-1).reshape(n_odd, Q_RANK, H_C * LANES)
    w["uq"] = jnp.concatenate([uqn, uqr], axis=-1)
    w["ukv"] = w_ukv.astype(BF16)
    w["o_mla"] = w_o_mla.astype(BF16)
    w["up"] = w_up.astype(BF16)
    w["down"] = w_down.astype(BF16)
    return w


def _with_past(past, new, s_pad):
    b, _, fdim = new.shape
    parts = [new.astype(BF16)] if past is None else [past.astype(BF16), new.astype(BF16)]
    n = sum(p.shape[1] for p in parts)
    if s_pad > n:
        parts.append(jnp.zeros((b, s_pad - n, fdim), BF16))
    return parts[0] if len(parts) == 1 else jnp.concatenate(parts, axis=1)


def _trunk(x, past, conv_state, w, p, cfg):
    b, t, d = x.shape
    m = b * t
    q0 = 0 if past is None else past[0].shape[2]
    s_valid = q0 + t
    tk = cfg["tk"]
    s_pad = -(-s_valid // tk) * tk
    pos = q0 + jnp.tile(jnp.arange(t), b)
    tab_a = _rope_tables(pos, DH, ROT_A)
    tab_i = _rope_tables(pos, D_IDX, ROT_I)
    tab_ki = _rope_tables(pos, D_IDX, ROT_I, lanes_valid=D_IDX)
    tab_kr = _rope_tables(pos, ROPE_C, ROPE_C, lanes_valid=ROPE_C)

    xf = x.reshape(m, d)
    ab_rows, mla_rows, conv_rows = [], [], []
    sc_ab = DH ** -0.5 * LOG2E
    sc_c = (NOPE + ROPE_C) ** -0.5 * LOG2E
    h = _rmsnorm(xf, p["norm_mix"][0], BF16)
    for l in range(DEPTH):
        if l % 2 == 0:
            e = l // 2
            lp = None if past is None else tuple(c[e] for c in past[:5])
            qa, ka32, ka16, va32, va16, qi = _proj(
                [h], w["in_a"][e],
                [dict(width=H_A * DH, dtypes=(BF16,), rope=0, scale=sc_ab),
                 dict(width=KV_A * DH, dtypes=(F32, BF16), rope=0, split_heads=True),
                 dict(width=KV_A * DH, dtypes=(F32, BF16), split_heads=True),
                 dict(width=H_I * D_IDX, dtypes=(BF16,), rope=1)],
                tabs=(tab_a, tab_i), name="proj_in_a")
            kiwi, qb, kb32, kb16, vb32, vb16 = _proj(
                [h], w["in_b"][e],
                [dict(width=LANES, dtypes=(F32,), rope=0),
                 dict(width=H_B * DH, dtypes=(BF16,), scale=sc_ab),
                 dict(width=H_B * DH, dtypes=(F32, BF16), split_heads=True),
                 dict(width=H_B * DH, dtypes=(F32, BF16), split_heads=True)],
                tabs=(tab_ki,), name="proj_in_b")
            ki32 = kiwi[:, :D_IDX]
            wi = kiwi[:, D_IDX:D_IDX + H_I]
            ab_rows.append((ka32.reshape(b, t, KV_A, DH), va32.reshape(b, t, KV_A, DH),
                            ki32.reshape(b, t, D_IDX), kb32.reshape(b, t, H_B, DH),
                            vb32.reshape(b, t, H_B, DH)))

            def full(idx, new16):
                pst = None if lp is None else lp[idx].reshape(b, q0, -1)
                return _with_past(pst, new16.reshape(b, t, -1), s_pad)

            ki_all = full(2, ki32)
            zk = jnp.zeros_like(ki_all)
            ki2 = jnp.concatenate([ki_all, zk, zk, ki_all], axis=-1)
            oa = _dsa(qa.reshape(b, t, -1), qi.reshape(b, t, -1), wi.reshape(b, t, H_I), full(0, ka16),
                      full(1, va16), ki2, tq=cfg["tq_a"], tk=tk, s_valid=s_valid, q0=q0)
            ob = _stick_breaking(qb.reshape(b, t, -1), full(3, kb16), full(4, vb16),
                                 tq=cfg["tq_b"], tk=cfg["tk_b"], q0=q0, hpg=cfg["hpg_b"], nchunk=cfg["nc_b"])
            (xf,) = _proj([oa.reshape(m, -1), ob.reshape(m, -1)], w["out_ab"][e],
                          [dict(width=D_MODEL, dtypes=(F32,))], resid=xf, name="proj_out_ab")
        else:
            od = l // 2
            gains = jnp.concatenate([p["g_q"][od], p["g_kv"][od]])
            cq16, ckv32, ckv16, kr32w, kr16w = _proj(
                [h], w["dqkr"][od],
                [dict(width=Q_RANK, dtypes=(BF16,), gain_off=0),
                 dict(width=KV_RANK, dtypes=(F32, BF16), gain_off=Q_RANK),
                 dict(width=LANES, dtypes=(F32, BF16), rope=0)],
                tabs=(tab_kr,), gain=gains, name="proj_dqkr")
            mla_rows.append((ckv32.reshape(b, t, KV_RANK), kr32w[:, :ROPE_C].reshape(b, t, ROPE_C)))
            qn, qr = _proj([cq16], w["uq"][od],
                           [dict(width=H_C * NOPE, dtypes=(BF16,), scale=sc_c),
                            dict(width=H_C * LANES, dtypes=(BF16,), rope=0, scale=sc_c)],
                           tabs=(tab_kr,), name="proj_uq")
            pc = None if past is None else past[5][od]
            pk = None if past is None else jnp.pad(past[6][od], ((0, 0), (0, 0), (0, LANES - ROPE_C)))
            c_all = _with_past(pc, ckv16.reshape(b, t, KV_RANK), s_pad)
            kr_all = _with_past(pk, kr16w.reshape(b, t, LANES), s_pad)
            if cfg["latent_c"]:
                qa_hm, qr_hm = _absorb_q(qn, qr, w["ukv"][od])
                lat = _mla_latent(qa_hm, qr_hm, c_all, kr_all, t=t, tk=cfg["tk_c"], s_valid=s_valid, q0=q0)
                o = _latent_out(lat, w["ukv"][od])
            else:
                (kv,) = _proj([c_all.reshape(b * s_pad, KV_RANK)], w["ukv"][od],
                              [dict(width=H_C * (NOPE + VD), dtypes=(BF16,))], name="proj_ukv")
                o = _mla(qn.reshape(b, t, -1), qr.reshape(b, t, -1), kv.reshape(b, s_pad, -1), kr_all,
                         tq=cfg["tq_c"], tk=cfg["tk_c"], s_valid=s_valid, q0=q0, hpg=cfg["hpg_c"],
                         nchunk=cfg["nc_c"])
            (xf,) = _proj([o.reshape(m, -1)], w["o_mla"][od], [dict(width=D_MODEL, dtypes=(F32,))],
                          resid=xf, name="proj_o_mla")

        if conv_state is None:
            prev8 = jnp.zeros((b, SUBLANES, 2 * D_FF), F32)
        else:
            prev8 = jnp.concatenate([jnp.zeros((b, SUBLANES - (CONV_W - 1), 2 * D_FF), F32),
                                     conv_state[l]], axis=1)
        final = l == DEPTH - 1
        g_next = p["norm_final"] if final else p["norm_mix"][l + 1]
        xf, h, sg, sv = _ffn(xf, p["norm_ffn"][l], w["up"][l], p["conv_w"][l], p["conv_b"][l], w["down"][l],
                             prev8, g_next, F32 if final else BF16, seg=cfg["seg"], carried=cfg["carried"],
                             tm=cfg["tm_ffn"], tf=cfg["tf"])
        last = [s8.reshape(b, -1, SUBLANES, D_FF)[:, -1, SUBLANES - (CONV_W - 1):] for s8 in (sg, sv)]
        conv_rows.append(jnp.concatenate(last, axis=-1))

    y = h.reshape(b, t, d)
    new_ab = [jnp.stack([r[i] for r in ab_rows]) for i in range(5)]
    new_mla = [jnp.stack([r[i] for r in mla_rows]) for i in range(2)]
    return y, new_ab + new_mla + [jnp.stack(conv_rows)]


def kernel(x_prompt, x_sample, cache_k_a, cache_v_a, cache_idx_k, cache_k_b, cache_v_b, cache_ckv, cache_krope, state_conv, norm_mix, norm_ffn, norm_final, w_in_ab, w_out_ab, w_dqkv, g_q, g_kv, w_uq, w_ukv, w_o_mla, w_up, conv_w, conv_b, w_down):
    w = _prep_weights(w_in_ab, w_out_ab, w_dqkv, w_uq, w_ukv, w_o_mla, w_up, w_down)
    p = dict(norm_mix=norm_mix, norm_ffn=norm_ffn, norm_final=norm_final, g_q=g_q, g_kv=g_kv,
             conv_w=conv_w, conv_b=conv_b)
    cfg_p = dict(tk=512, tq_a=128, tq_b=256, tk_b=256, hpg_b=4, nc_b=1, tq_c=512, tk_c=1024, hpg_c=2, nc_c=2,
                 latent_c=False, seg=512, carried=True, tm_ffn=512, tf=512)
    t_s = x_sample.shape[1]
    cfg_s = dict(tk=256, tq_a=t_s, tq_b=t_s, tk_b=256, hpg_b=H_B, nc_b=1, tq_c=t_s, tk_c=256, hpg_c=8, nc_c=1,
                 latent_c=True, seg=t_s, carried=False, tm_ffn=x_sample.shape[0] * t_s, tf=512)
    y_p, st_p = _trunk(x_prompt, None, None, w, p, cfg_p)
    past = (cache_k_a, cache_v_a, cache_idx_k, cache_k_b, cache_v_b, cache_ckv, cache_krope)
    y_s, st_s = _trunk(x_sample, past, state_conv, w, p, cfg_s)
    return (y_p, y_s, *st_p, *st_s)
```

```python
import functools

import numpy as np
import jax
import jax.numpy as jnp
from jax import lax
from jax.experimental import pallas as pl
from jax.experimental.pallas import tpu as pltpu

F32 = jnp.float32
BF16 = jnp.bfloat16

D_MODEL = 2048
DEPTH = 4
CHUNK = 64
ROPE_THETA = 500000.0
EPS = 1e-6
H_A, KV_A, DH = 8, 2, 128
ROT_A = DH // 4
H_I, D_IDX = 16, 64
ROT_I = D_IDX // 4
TOPK_MAX = 256
H_B = 8
H_C, Q_RANK, KV_RANK, NOPE, ROPE_C, VD = 16, 512, 512, 128, 64, 128
D_FF = 5632
CONV_W = 3

LANES = 128
SUBLANES = 8
VMEM_LIMIT = 56 * 1024 * 1024
NEG = -1e30
INT_MIN = np.int32(-2 ** 31)
INT_MAX = np.int32(2 ** 31 - 1)
LOG2E = float(np.log2(np.e))


def _cparams(sem):
    return pltpu.CompilerParams(dimension_semantics=sem, vmem_limit_bytes=VMEM_LIMIT)


def _dot_nt(a, b):
    return lax.dot_general(a, b, (((1,), (1,)), ((), ())), preferred_element_type=F32)


def _rms_body(x_ref, g_ref, o_ref):
    x = x_ref[...]
    y = x * lax.rsqrt(jnp.mean(x * x, axis=-1, keepdims=True) + EPS)
    o_ref[...] = (y * g_ref[...]).astype(o_ref.dtype)


def _rmsnorm(x, g, out_dtype):
    m, d = x.shape
    tm = min(m, 512)
    return pl.pallas_call(
        _rms_body,
        grid=(m // tm,),
        in_specs=[pl.BlockSpec((tm, d), lambda i: (i, 0)),
                  pl.BlockSpec((1, d), lambda i: (0, 0))],
        out_specs=pl.BlockSpec((tm, d), lambda i: (i, 0)),
        out_shape=jax.ShapeDtypeStruct((m, d), out_dtype),
        compiler_params=_cparams(("parallel",)),
        name="rmsnorm",
    )(x, g.reshape(1, d))


PROJ_CHUNK = 512
COUNT_ROWS = 64


def _proj_body(*refs, nx, groups, tab_halves, has_gain, has_resid):
    it = iter(refs)
    x_refs = [next(it) for _ in range(nx)]
    w_ref = next(it)
    tabs = [(next(it), next(it), next(it)) for _ in tab_halves]
    g_ref = next(it) if has_gain else None
    r_ref = next(it) if has_resid else None
    outs = list(it)

    xs = [r[...] for r in x_refs]
    x = xs[0] if nx == 1 else jnp.concatenate(xs, axis=1)
    accs = []
    c0 = 0
    for grp in groups:
        width = grp["width"]
        chunk = min(width, PROJ_CHUNK)
        for cc in range(0, width, chunk):
            accs.append(jnp.dot(x, w_ref[:, c0 + cc:c0 + cc + chunk], preferred_element_type=F32))
        c0 += width
    accs = iter(accs)
    oi = 0
    for grp in groups:
        width = grp["width"]
        chunk = min(width, PROJ_CHUNK)
        for cc in range(0, width, chunk):
            acc = next(accs)
            if grp.get("gain_off") is not None:
                assert chunk == width
                go = grp["gain_off"]
                acc = (acc * lax.rsqrt(jnp.mean(acc * acc, axis=-1, keepdims=True) + EPS)
                       * g_ref[:, go:go + width])
            if grp.get("rope") is not None:
                c_ref, s1_ref, s2_ref = tabs[grp["rope"]]
                half = tab_halves[grp["rope"]]
                c, s1, s2 = c_ref[...], s1_ref[...], s2_ref[...]
                parts = []
                for gi in range(chunk // LANES):
                    xg = acc[:, gi * LANES:(gi + 1) * LANES]
                    parts.append(xg * c + pltpu.roll(xg, half, 1) * s1
                                 + pltpu.roll(xg, LANES - half, 1) * s2)
                acc = parts[0] if len(parts) == 1 else jnp.concatenate(parts, axis=1)
            if grp.get("scale") is not None:
                acc = acc * grp["scale"]
            if has_resid:
                acc = acc + r_ref[:, cc:cc + chunk]
            for k in range(len(grp["dtypes"])):
                o = outs[oi + k]
                if len(o.shape) == 3:
                    for gi in range(chunk // LANES):
                        o[:, cc // LANES + gi, :] = acc[:, gi * LANES:(gi + 1) * LANES].astype(o.dtype)
                else:
                    o[:, cc:cc + chunk] = acc.astype(o.dtype)
        oi += len(grp["dtypes"])


def _proj(xs, w, groups, *, tabs=(), gain=None, resid=None, name="proj"):
    m = xs[0].shape[0]
    k, n = w.shape
    tm = min(m, 512)
    assert m % tm == 0 and n == sum(g["width"] for g in groups) and k == sum(x.shape[1] for x in xs)
    assert resid is None or len(groups) == 1
    in_specs = [pl.BlockSpec((tm, x.shape[1]), lambda i: (i, 0)) for x in xs]
    in_specs.append(pl.BlockSpec((k, n), lambda i: (0, 0)))
    args = list(xs) + [w]
    for c, s1, s2, _ in tabs:
        for t in (c, s1, s2):
            in_specs.append(pl.BlockSpec((tm, LANES), lambda i: (i, 0)))
            args.append(t)
    if gain is not None:
        in_specs.append(pl.BlockSpec((1, gain.shape[0]), lambda i: (0, 0)))
        args.append(gain.reshape(1, -1))
    if resid is not None:
        in_specs.append(pl.BlockSpec((tm, n), lambda i: (i, 0)))
        args.append(resid)
    out_specs, out_shape = [], []
    for g in groups:
        for dt in g["dtypes"]:
            if g.get("split_heads") and dt == F32:
                nh = g["width"] // LANES
                out_specs.append(pl.BlockSpec((tm, nh, LANES), lambda i: (i, 0, 0)))
                out_shape.append(jax.ShapeDtypeStruct((m, nh, LANES), dt))
            else:
                out_specs.append(pl.BlockSpec((tm, g["width"]), lambda i: (i, 0)))
                out_shape.append(jax.ShapeDtypeStruct((m, g["width"]), dt))
    body = functools.partial(_proj_body, nx=len(xs), groups=groups, tab_halves=[t[3] for t in tabs],
                             has_gain=gain is not None, has_resid=resid is not None)
    return pl.pallas_call(
        body,
        grid=(m // tm,),
        in_specs=in_specs,
        out_specs=out_specs,
        out_shape=out_shape,
        compiler_params=_cparams(("parallel",)),
        name=name,
    )(*args)


def _dsa_body(qa_ref, qi_ref, wi_ref, ka_ref, va_ref, ki_ref, tri_ref, o_ref,
              key_scr, qs_scr, m_scr, l_scr, acc_scr, *, tq, tk, s_valid, q0, topk):
    i = pl.program_id(1)
    q_first = q0 + i * tq
    q_last = q_first + tq - 1
    adm_end = jnp.minimum((q_last // CHUNK + 1) * CHUNK, s_valid)
    nkb = (adm_end + tk - 1) // tk
    q_pos = q_first + lax.broadcasted_iota(jnp.int32, (tq, tk), 0)
    q_lim = jnp.minimum(((q_pos >> 6) + 1) * CHUNK, s_valid)
    col = lax.broadcasted_iota(jnp.int32, (tq, tk), 1)
    rep = H_A // KV_A

    w = wi_ref[0] * ((H_I * D_IDX) ** -0.5)
    pg = 4
    pairs = LANES // D_IDX
    qi = qi_ref[0]
    q_groups = [jnp.concatenate([qi[:, (g * pg + pp) * LANES:(g * pg + pp + 1) * LANES]
                                 for pp in range(pg)], axis=0) for g in range(H_I // (pairs * pg))]

    def to_key(x):
        bits = pltpu.bitcast(x, jnp.int32)
        return bits ^ ((bits >> 31) & INT_MAX)

    transposed = key_scr.shape[1] == tq

    if transposed:
        wt = w
        krow = lax.broadcasted_iota(jnp.int32, (tk, tq), 0)
        q_lim_t = jnp.minimum((((q_first + lax.broadcasted_iota(jnp.int32, (tk, tq), 1)) >> 6) + 1) * CHUNK,
                              s_valid)

    def p1(j, c):
        k0 = pl.multiple_of(j * tk, tk)
        if transposed:
            sc = jnp.zeros((tk, tq), F32)
            for g, qg in enumerate(q_groups):
                for half in range(pairs):
                    rel = _dot_nt(ki_ref[0, pl.ds(k0, tk), half * LANES:(half + 1) * LANES], qg)
                    for pp in range(pg):
                        h = (g * pg + pp) * pairs + half
                        sc = sc + jnp.maximum(rel[:, pp * tq:(pp + 1) * tq], 0.0) * wt[h:h + 1, :]
            key_scr[pl.ds(k0, tk), :] = jnp.where(k0 + krow < q_lim_t, to_key(sc), INT_MIN)
            return c
        sc = jnp.zeros((tq, tk), F32)
        for g, qg in enumerate(q_groups):
            for half in range(pairs):
                rel = _dot_nt(qg, ki_ref[0, pl.ds(k0, tk), half * LANES:(half + 1) * LANES])
                for pp in range(pg):
                    h = (g * pg + pp) * pairs + half
                    sc = sc + jnp.maximum(rel[pp * tq:(pp + 1) * tq], 0.0) * w[:, h:h + 1]
        key_scr[:, pl.ds(k0, tk)] = jnp.where(k0 + col < q_lim, to_key(sc), INT_MIN)
        return c

    lax.fori_loop(0, nkb, p1, 0)

    def count_ge(v):
        def cb(j, c):
            k0 = pl.multiple_of(j * tk, tk)
            if transposed:
                ge = jnp.where(key_scr[pl.ds(k0, tk), :] >= v, 1.0, 0.0)
                return c + jnp.sum(ge.reshape(tk // COUNT_ROWS, COUNT_ROWS, tq), axis=0)
            ge = jnp.where(key_scr[:, pl.ds(k0, tk)] >= v, 1.0, 0.0)
            part = ge[:, 0:LANES]
            for cc in range(1, tk // LANES):
                part = part + ge[:, cc * LANES:(cc + 1) * LANES]
            return c + part
        if transposed:
            c = lax.fori_loop(0, nkb, cb, jnp.zeros((COUNT_ROWS, tq), F32))
            return jnp.sum(c, axis=0, keepdims=True)
        c = lax.fori_loop(0, nkb, cb, jnp.zeros((tq, LANES), F32))
        return jnp.sum(c, axis=1, keepdims=True)

    def bis(_, lohi):
        lo, hi = lohi
        mid = (lo >> 1) + (hi >> 1) + (lo & hi & 1)
        ok = count_ge(mid) >= float(topk)
        return jnp.where(ok, mid, lo), jnp.where(ok, hi, mid)

    vshape = (1, tq) if transposed else (tq, 1)
    thr, _ = lax.fori_loop(0, 32, bis, (jnp.full(vshape, INT_MIN, jnp.int32),
                                        jnp.full(vshape, INT_MAX, jnp.int32)))
    need = float(topk) - count_ge(thr + 1)
    if transposed:
        thr = pltpu.bitcast(jnp.transpose(pltpu.bitcast(jnp.broadcast_to(thr, (LANES, tq)), F32)), jnp.int32)
        need = jnp.transpose(jnp.broadcast_to(need, (LANES, tq)))
        thr = jnp.concatenate([thr] * (tk // LANES), axis=1)
        need = jnp.concatenate([need] * (tk // LANES), axis=1)

    qa = qa_ref[0]
    for g in range(KV_A):
        qs_scr[g] = jnp.concatenate(
            [qa[:, (g * rep + r) * DH:(g * rep + r + 1) * DH] for r in range(rep)], axis=0)
    m_scr[...] = jnp.full(m_scr.shape, NEG, F32)
    l_scr[...] = jnp.zeros(l_scr.shape, F32)
    acc_scr[...] = jnp.zeros(acc_scr.shape, F32)

    def p3(j, eqc):
        k0 = pl.multiple_of(j * tk, tk)
        if transposed:
            key = pltpu.bitcast(jnp.transpose(pltpu.bitcast(key_scr[pl.ds(k0, tk), :], F32)), jnp.int32)
        else:
            key = key_scr[:, pl.ds(k0, tk)]
        eq = key == thr
        eqf = jnp.where(eq, 1.0, 0.0)
        prefix = jnp.dot(eqf.astype(BF16), tri_ref[...], preferred_element_type=F32) + eqc
        bias = jnp.where(key > thr, 0.0, jnp.where(eq, jnp.where(prefix < need, 0.0, NEG), NEG))
        bias = jnp.where(key == INT_MIN, NEG, bias)
        eqc = eqc + jnp.sum(eqf, axis=1, keepdims=True)
        ss = [_dot_nt(qs_scr[g], ka_ref[0, pl.ds(k0, tk), g * DH:(g + 1) * DH])
              for g in range(KV_A)]
        for g in range(KV_A):
            vg = va_ref[0, pl.ds(k0, tk), g * DH:(g + 1) * DH]
            s = ss[g]
            ps, alphas = [], []
            for r in range(rep):
                h = g * rep + r
                sr = s[r * tq:(r + 1) * tq] + bias
                m_prev = m_scr[h]
                m_new = jnp.maximum(m_prev, jnp.max(sr, axis=1, keepdims=True))
                alpha = jnp.exp2(m_prev - m_new)
                p = jnp.exp2(sr - jnp.concatenate([m_new] * (tk // LANES), axis=1))
                l_scr[h] = alpha * l_scr[h] + jnp.sum(p, axis=1, keepdims=True)
                m_scr[h] = m_new
                ps.append(p.astype(BF16))
                alphas.append(alpha)
            for r in range(rep):
                h = g * rep + r
                acc_scr[h] = acc_scr[h] * alphas[r] + jnp.dot(ps[r], vg, preferred_element_type=F32)
        return eqc

    lax.fori_loop(0, nkb, p3, jnp.zeros((tq, 1), F32))
    for h in range(H_A):
        o_ref[0, :, h * DH:(h + 1) * DH] = (acc_scr[h] / l_scr[h]).astype(o_ref.dtype)


def _dsa(qa, qi, wi, ka, va, ki, *, tq, tk, s_valid, q0):
    b, t, _ = qa.shape
    s_pad = ka.shape[1]
    assert s_pad % tk == 0 and t % tq == 0 and tk >= TOPK_MAX
    topk = min(TOPK_MAX, s_valid // 4)
    tri = jnp.asarray(np.triu(np.ones((tk, tk), np.float32), 1), BF16)
    body = functools.partial(_dsa_body, tq=tq, tk=tk, s_valid=s_valid, q0=q0, topk=topk)
    rep = H_A // KV_A
    transposed = tq % LANES == 0
    if transposed:
        wi = jnp.transpose(wi, (0, 2, 1))
        wi_spec = pl.BlockSpec((1, H_I, tq), lambda bi, i: (bi, 0, i))
    else:
        wi_spec = pl.BlockSpec((1, tq, H_I), lambda bi, i: (bi, i, 0))
    return pl.pallas_call(
        body,
        grid=(b, t // tq),
        in_specs=[pl.BlockSpec((1, tq, H_A * DH), lambda bi, i: (bi, i, 0)),
                  pl.BlockSpec((1, tq, H_I * D_IDX), lambda bi, i: (bi, i, 0)),
                  wi_spec,
                  pl.BlockSpec((1, s_pad, KV_A * DH), lambda bi, i: (bi, 0, 0)),
                  pl.BlockSpec((1, s_pad, KV_A * DH), lambda bi, i: (bi, 0, 0)),
                  pl.BlockSpec((1, s_pad, 2 * LANES), lambda bi, i: (bi, 0, 0)),
                  pl.BlockSpec((tk, tk), lambda bi, i: (0, 0))],
        out_specs=pl.BlockSpec((1, tq, H_A * DH), lambda bi, i: (bi, i, 0)),
        out_shape=jax.ShapeDtypeStruct((b, t, H_A * DH), BF16),
        scratch_shapes=[pltpu.VMEM((s_pad, tq) if transposed else (tq, s_pad), jnp.int32),
                        pltpu.VMEM((KV_A, rep * tq, DH), BF16),
                        pltpu.VMEM((H_A, tq, LANES), F32),
                        pltpu.VMEM((H_A, tq, LANES), F32),
                        pltpu.VMEM((H_A, tq, DH), F32)],
        compiler_params=_cparams(("parallel", "arbitrary")),
        name="dsa",
    )(qa, qi, wi, ka, va, ki, tri)


def _sb_body(q_ref, k_ref, v_ref, low_ref, *rest, tq, tk, s_pad, q0, hpg, nchunk, past_heads):
    if past_heads:
        kp_ref, vp_ref, o_ref = rest
    else:
        (o_ref,) = rest
    i = pl.program_id(2)
    tqc = tq // nchunk
    q_first = q0 + i * tq
    q_last = q_first + tq - 1
    nkb = jnp.minimum((q_last + tk - 1) // tk, s_pad // tk)
    n_full = jnp.minimum(q_first // tk, nkb)
    row = lax.broadcasted_iota(jnp.int32, (tqc, tk), 0)
    col = lax.broadcasted_iota(jnp.int32, (tqc, tk), 1)
    chains = [(g, c) for g in range(hpg) for c in range(nchunk)]

    def step(j, carry, masked):
        k0 = pl.multiple_of(j * tk, tk)

        def load(new_ref, past_ref, g):
            if not past_heads:
                return new_ref[0, pl.ds(k0, tk), g * DH:(g + 1) * DH]
            if masked:
                return new_ref[0, :, g * DH:(g + 1) * DH]
            return past_ref[0, pl.ds(k0 * past_heads + g, tk, stride=past_heads), :].astype(BF16)

        zs = [_dot_nt(q_ref[0, c * tqc:(c + 1) * tqc, g * DH:(g + 1) * DH],
                      load(k_ref, kp_ref if past_heads else None, g)) for g, c in chains]
        lbs, lks, sufs, causals = [], [], [], []
        for (g, c), z in zip(chains, zs):
            lb = jnp.minimum(z, 0.0) - jnp.log(1.0 + jnp.exp2(-jnp.abs(z))) * LOG2E
            lk = lb - z
            causal = None
            if masked:
                causal = k0 + col < q_first + c * tqc + row
                lk = jnp.where(causal, lk, 0.0)
            hi = lk.astype(BF16)
            lo = (lk - hi.astype(F32)).astype(BF16)
            lbs.append(lb)
            lks.append(lk)
            causals.append(causal)
            sufs.append(jnp.dot(jnp.concatenate([hi, lo], axis=1), low_ref[...], preferred_element_type=F32))
        out = []
        for n, (g, c) in enumerate(chains):
            run, acc = carry[2 * n], carry[2 * n + 1]
            wgt = jnp.exp2(lbs[n] + sufs[n] + run)
            if masked:
                wgt = jnp.where(causals[n], wgt, 0.0)
            acc = acc + jnp.dot(wgt.astype(BF16), load(v_ref, vp_ref if past_heads else None, g),
                                preferred_element_type=F32)
            run = run + jnp.sum(lks[n], axis=1, keepdims=True)
            out += [run, acc]
        return tuple(out)

    carry = []
    for _ in chains:
        carry += [jnp.zeros((tqc, 1), F32), jnp.zeros((tqc, DH), F32)]
    carry = tuple(carry)
    carry = lax.fori_loop(0, nkb - n_full, lambda s, c: step(nkb - 1 - s, c, True), carry)
    carry = lax.fori_loop(0, n_full, lambda s, c: step(n_full - 1 - s, c, False), carry)
    for n, (g, c) in enumerate(chains):
        o_ref[0, c * tqc:(c + 1) * tqc, g * DH:(g + 1) * DH] = carry[2 * n + 1].astype(o_ref.dtype)


def _stick_breaking(qb, kb, vb, *, tq, tk, q0, hpg, nchunk, past=None):
    b, t, _ = qb.shape
    low = np.tril(np.ones((tk, tk), np.float32), -1)
    low = jnp.asarray(np.concatenate([low, low], axis=0), BF16)
    wd = hpg * DH
    if past is None:
        s_pad, extra, extra_specs, past_heads = kb.shape[1], [], [], 0
    else:
        assert hpg == H_B and t == tq and q0 % tk == 0 and t <= tk and past[0].shape[1:] == (q0, H_B, DH)
        s_pad, past_heads = q0 + tk, H_B
        kb, vb = (jnp.pad(a, ((0, 0), (0, tk - t), (0, 0))) for a in (kb, vb))
        extra = [c.reshape(b, q0 * H_B, DH) for c in past]
        extra_specs = [pl.BlockSpec((1, q0 * H_B, DH), lambda bi, h, i: (bi, 0, 0))] * 2
    s_new = kb.shape[1]
    assert s_pad % tk == 0 and t % tq == 0 and H_B % hpg == 0 and tq % nchunk == 0
    body = functools.partial(_sb_body, tq=tq, tk=tk, s_pad=s_pad, q0=q0, hpg=hpg, nchunk=nchunk,
                             past_heads=past_heads)
    return pl.pallas_call(
        body,
        grid=(b, H_B // hpg, t // tq),
        in_specs=[pl.BlockSpec((1, tq, wd), lambda bi, h, i: (bi, i, h)),
                  pl.BlockSpec((1, s_new, wd), lambda bi, h, i: (bi, 0, h)),
                  pl.BlockSpec((1, s_new, wd), lambda bi, h, i: (bi, 0, h)),
                  pl.BlockSpec((2 * tk, tk), lambda bi, h, i: (0, 0))] + extra_specs,
        out_specs=pl.BlockSpec((1, tq, wd), lambda bi, h, i: (bi, i, h)),
        out_shape=jax.ShapeDtypeStruct((b, t, H_B * DH), BF16),
        compiler_params=_cparams(("parallel", "parallel", "arbitrary")),
        name="stick_breaking",
    )(qb, kb, vb, low, *extra)


def _mla_body(qn_ref, qr_ref, kv_ref, kr_ref, o_ref, *, tq, tk, s_valid, q0, hpg, nchunk):
    i = pl.program_id(2)
    tqc = tq // nchunk
    q_first = q0 + i * tq
    q_last = q_first + tq - 1
    lim_first = jnp.minimum((q_first // CHUNK + 1) * CHUNK, s_valid)
    lim_last = jnp.minimum((q_last // CHUNK + 1) * CHUNK, s_valid)
    nkb = (lim_last + tk - 1) // tk
    n_full = lim_first // tk
    row = lax.broadcasted_iota(jnp.int32, (tqc, tk), 0)
    col = lax.broadcasted_iota(jnp.int32, (tqc, tk), 1)
    chains = [(g, c) for g in range(hpg) for c in range(nchunk)]
    hw = NOPE + VD

    def step(j, carry, masked):
        k0 = pl.multiple_of(j * tk, tk)
        kr = kr_ref[0, pl.ds(k0, tk), :]
        ss = []
        for g, c in chains:
            rows = slice(c * tqc, (c + 1) * tqc)
            qc = jnp.concatenate([qn_ref[0, rows, g * NOPE:(g + 1) * NOPE],
                                  qr_ref[0, rows, g * LANES:(g + 1) * LANES]], axis=1)
            kc = jnp.concatenate([kv_ref[0, pl.ds(k0, tk), g * hw:g * hw + NOPE], kr], axis=1)
            ss.append(_dot_nt(qc, kc))
        ps, stats = [], []
        for n, (g, c) in enumerate(chains):
            m_prev = carry[2 * n]
            s = ss[n]
            if masked:
                lim = jnp.minimum((((q_first + c * tqc + row) >> 6) + 1) * CHUNK, s_valid)
                s = jnp.where(k0 + col < lim, s, NEG)
            m_new = jnp.maximum(m_prev, jnp.max(s, axis=1, keepdims=True))
            alpha = jnp.exp2(m_prev - m_new)
            p = jnp.exp2(s - jnp.concatenate([m_new] * (tk // LANES), axis=1))
            stats.append((m_new, alpha))
            ps.append(p.astype(BF16))
        out = []
        for n, (g, c) in enumerate(chains):
            m_new, alpha = stats[n]
            vb = jnp.concatenate([kv_ref[0, pl.ds(k0, tk), g * hw + NOPE:(g + 1) * hw], ones_k], axis=1)
            acc = (carry[2 * n + 1] * jnp.concatenate([alpha, alpha], axis=1)
                   + jnp.dot(ps[n], vb, preferred_element_type=F32))
            out += [m_new, acc]
        return tuple(out)

    ones_k = jnp.ones((tk, LANES), BF16)
    carry = []
    for _ in chains:
        carry += [jnp.full((tqc, LANES), NEG, F32), jnp.zeros((tqc, VD + LANES), F32)]
    carry = tuple(carry)
    carry = lax.fori_loop(0, n_full, lambda j, c: step(j, c, False), carry)
    carry = lax.fori_loop(n_full, nkb, lambda j, c: step(j, c, True), carry)
    for n, (g, c) in enumerate(chains):
        acc = carry[2 * n + 1]
        o_ref[0, c * tqc:(c + 1) * tqc, g * VD:(g + 1) * VD] = (acc[:, :VD] / acc[:, VD:]).astype(o_ref.dtype)


def _mla(qn, qr, kv, kr, *, tq, tk, s_valid, q0, hpg, nchunk):
    b, t, _ = qn.shape
    s_pad = kv.shape[1]
    assert s_pad % tk == 0 and t % tq == 0 and H_C % hpg == 0 and tq % nchunk == 0
    body = functools.partial(_mla_body, tq=tq, tk=tk, s_valid=s_valid, q0=q0, hpg=hpg, nchunk=nchunk)
    return pl.pallas_call(
        body,
        grid=(b, H_C // hpg, t // tq),
        in_specs=[pl.BlockSpec((1, tq, hpg * NOPE), lambda bi, h, i: (bi, i, h)),
                  pl.BlockSpec((1, tq, hpg * LANES), lambda bi, h, i: (bi, i, h)),
                  pl.BlockSpec((1, s_pad, hpg * (NOPE + VD)), lambda bi, h, i: (bi, 0, h)),
                  pl.BlockSpec((1, s_pad, LANES), lambda bi, h, i: (bi, 0, 0))],
        out_specs=pl.BlockSpec((1, tq, hpg * VD), lambda bi, h, i: (bi, i, h)),
        out_shape=jax.ShapeDtypeStruct((b, t, H_C * VD), BF16),
        compiler_params=_cparams(("parallel", "parallel", "arbitrary")),
        name="mla",
    )(qn, qr, kv, kr)


def _absorb_q_body(qn_ref, qr_ref, wk_ref, qa_ref, qrh_ref):
    qa_ref[0] = _dot_nt(qn_ref[...], wk_ref[...]).astype(qa_ref.dtype)
    qrh_ref[0] = qr_ref[...]


def _absorb_q(qn, qr, w_ukv):
    m = qn.shape[0]
    return pl.pallas_call(
        _absorb_q_body,
        grid=(H_C,),
        in_specs=[pl.BlockSpec((m, NOPE), lambda h: (0, h)),
                  pl.BlockSpec((m, LANES), lambda h: (0, h)),
                  pl.BlockSpec((KV_RANK, NOPE), lambda h: (0, 2 * h))],
        out_specs=[pl.BlockSpec((1, m, KV_RANK), lambda h: (h, 0, 0)),
                   pl.BlockSpec((1, m, LANES), lambda h: (h, 0, 0))],
        out_shape=[jax.ShapeDtypeStruct((H_C, m, KV_RANK), BF16),
                   jax.ShapeDtypeStruct((H_C, m, LANES), BF16)],
        compiler_params=_cparams(("parallel",)),
        name="mla_absorb_q",
    )(qn, qr, w_ukv)


def _mla_latent_body(qa_ref, qr_ref, c_ref, kr_ref, o_ref, *, t, tk, s_valid, q0, nsplit):
    hs = H_C // nsplit
    rows = hs * t
    lim_first = min((q0 // CHUNK + 1) * CHUNK, s_valid)
    lim_last = min(((q0 + t - 1) // CHUNK + 1) * CHUNK, s_valid)
    nkb = -(-lim_last // tk)
    n_full = lim_first // tk
    q_pos = q0 + lax.broadcasted_iota(jnp.int32, (hs, t, tk), 1).reshape(rows, tk)
    lim = jnp.minimum(((q_pos >> 6) + 1) * CHUNK, s_valid)
    col = lax.broadcasted_iota(jnp.int32, (rows, tk), 1)
    ones_k = jnp.ones((tk, LANES), BF16)
    qs = [jnp.concatenate([qa_ref[n * hs:(n + 1) * hs].reshape(rows, KV_RANK),
                           qr_ref[n * hs:(n + 1) * hs].reshape(rows, LANES)], axis=1) for n in range(nsplit)]

    def step(j, carry, masked):
        k0 = pl.multiple_of(j * tk, tk)
        cb = c_ref[0, pl.ds(k0, tk), :]
        kc = jnp.concatenate([cb, kr_ref[0, pl.ds(k0, tk), :]], axis=1)
        vc = jnp.concatenate([cb, ones_k], axis=1)
        ss = [_dot_nt(q, kc) for q in qs]
        ps, stats = [], []
        for n in range(nsplit):
            s = ss[n]
            if masked:
                s = jnp.where(k0 + col < lim, s, NEG)
            m_prev = carry[2 * n]
            m_new = jnp.maximum(m_prev, jnp.max(s, axis=1, keepdims=True))
            alpha = jnp.exp2(m_prev - m_new)
            ps.append(jnp.exp2(s - jnp.concatenate([m_new] * (tk // LANES), axis=1)).astype(BF16))
            stats.append((m_new, alpha))
        out = []
        for n in range(nsplit):
            m_new, alpha = stats[n]
            acc = (carry[2 * n + 1] * jnp.concatenate([alpha] * (KV_RANK // LANES + 1), axis=1)
                   + jnp.dot(ps[n], vc, preferred_element_type=F32))
            out += [m_new, acc]
        return tuple(out)

    carry = []
    for _ in range(nsplit):
        carry += [jnp.full((rows, LANES), NEG, F32), jnp.zeros((rows, KV_RANK + LANES), F32)]
    carry = tuple(carry)
    carry = lax.fori_loop(0, n_full, lambda j, c: step(j, c, False), carry)
    carry = lax.fori_loop(n_full, nkb, lambda j, c: step(j, c, True), carry)
    for n in range(nsplit):
        acc = carry[2 * n + 1]
        lat = acc[:, :KV_RANK] / jnp.concatenate([acc[:, KV_RANK:]] * (KV_RANK // LANES), axis=1)
        o_ref[0, n * hs:(n + 1) * hs] = lat.reshape(hs, t, KV_RANK).astype(o_ref.dtype)


def _mla_latent(qa_hm, qr_hm, c_all, kr_all, *, t, tk, s_valid, q0):
    b, s_pad, _ = c_all.shape
    assert s_pad % tk == 0
    body = functools.partial(_mla_latent_body, t=t, tk=tk, s_valid=s_valid, q0=q0, nsplit=2)
    return pl.pallas_call(
        body,
        grid=(b,),
        in_specs=[pl.BlockSpec((H_C, t, KV_RANK), lambda bi: (0, bi, 0)),
                  pl.BlockSpec((H_C, t, LANES), lambda bi: (0, bi, 0)),
                  pl.BlockSpec((1, s_pad, KV_RANK), lambda bi: (bi, 0, 0)),
                  pl.BlockSpec((1, s_pad, LANES), lambda bi: (bi, 0, 0))],
        out_specs=pl.BlockSpec((1, H_C, t, KV_RANK), lambda bi: (bi, 0, 0, 0)),
        out_shape=jax.ShapeDtypeStruct((b, H_C, t, KV_RANK), BF16),
        compiler_params=_cparams(("parallel",)),
        name="mla_latent",
    )(qa_hm, qr_hm, c_all, kr_all)


def _latent_out_body(x_ref, wv_ref, o_ref):
    nb, _, t, r = x_ref.shape
    o_ref[...] = jnp.dot(x_ref[...].reshape(nb * t, r), wv_ref[...],
                         preferred_element_type=F32).astype(o_ref.dtype)


def _latent_out(lat, w_ukv):
    b, _, t, _ = lat.shape
    return pl.pallas_call(
        _latent_out_body,
        grid=(H_C,),
        in_specs=[pl.BlockSpec((b, 1, t, KV_RANK), lambda h: (0, h, 0, 0)),
                  pl.BlockSpec((KV_RANK, VD), lambda h: (0, 2 * h + 1))],
        out_specs=pl.BlockSpec((b * t, VD), lambda h: (0, h)),
        out_shape=jax.ShapeDtypeStruct((b * t, H_C * VD), BF16),
        compiler_params=_cparams(("parallel",)),
        name="mla_latent_out",
    )(lat, w_ukv)


def _ffn_body(x_ref, g_ref, wg_ref, wv_ref, cwg_ref, cwv_ref, cbg_ref, cbv_ref, wd_ref, pg_ref, pv_ref, gn_ref,
              o_ref, hn_ref, sg_ref, sv_ref, h_scr, acc_scr, ug_buf, uv_buf, cg_scr, cv_scr, *, seg, carried):
    i = pl.program_id(0)
    f = pl.program_id(1)
    tm = x_ref.shape[0]

    @pl.when(f == 0)
    def _():
        x = x_ref[...]
        y = x * lax.rsqrt(jnp.mean(x * x, axis=-1, keepdims=True) + EPS)
        h_scr[...] = (y * g_ref[...]).astype(BF16)
        acc_scr[...] = x

    h = h_scr[...]
    halves = ((wg_ref, cwg_ref, cbg_ref, pg_ref, sg_ref, ug_buf, cg_scr),
              (wv_ref, cwv_ref, cbv_ref, pv_ref, sv_ref, uv_buf, cv_scr))
    ys = []
    for w_ref, cw_ref, cb_ref, p_ref, s_ref, buf, c_scr in halves:
        u = jnp.dot(h, w_ref[...], preferred_element_type=F32)
        cw = cw_ref[...]
        parts = []
        for sgi in range(tm // seg):
            us = u[sgi * seg:(sgi + 1) * seg]
            if carried:
                head = jnp.where(i == 0, p_ref[0], c_scr[f])
            else:
                head = p_ref[sgi]
            buf[sgi, 0:SUBLANES] = head
            buf[sgi, SUBLANES:SUBLANES + seg] = us
            y = (cw[2:3] * us + cw[1:2] * buf[sgi, SUBLANES - 1:SUBLANES - 1 + seg]
                 + cw[0:1] * buf[sgi, SUBLANES - 2:SUBLANES - 2 + seg] + cb_ref[...])
            parts.append(y)
            last = us[seg - SUBLANES:seg]
            s_ref[sgi] = last
            if carried:
                c_scr[f] = last
        ys.append(parts[0] if len(parts) == 1 else jnp.concatenate(parts, axis=0))
    yg, yv = ys
    act = (yg / (1.0 + jnp.exp(-yg))) * yv
    acc_scr[...] += jnp.dot(act.astype(BF16), wd_ref[...], preferred_element_type=F32)

    @pl.when(f == pl.num_programs(1) - 1)
    def _():
        a = acc_scr[...]
        o_ref[...] = a
        y = a * lax.rsqrt(jnp.mean(a * a, axis=-1, keepdims=True) + EPS)
        hn_ref[...] = (y * gn_ref[...]).astype(hn_ref.dtype)


def _ffn(x, g, w_up, conv_w, conv_b, w_down, prev8, g_next, next_dtype, *, seg, carried, tm, tf):
    m, d = x.shape
    nf = D_FF // tf
    nseg = tm // seg
    assert m % tm == 0 and D_FF % tf == 0 and tm % seg == 0
    sidx = (lambda i: 0) if carried else (lambda i: i)
    body = functools.partial(_ffn_body, seg=seg, carried=carried)
    cb = conv_b.reshape(1, 2 * D_FF)
    return pl.pallas_call(
        body,
        grid=(m // tm, nf),
        in_specs=[pl.BlockSpec((tm, d), lambda i, f: (i, 0)),
                  pl.BlockSpec((1, d), lambda i, f: (0, 0)),
                  pl.BlockSpec((d, tf), lambda i, f: (0, f)),
                  pl.BlockSpec((d, tf), lambda i, f: (0, nf + f)),
                  pl.BlockSpec((CONV_W, tf), lambda i, f: (0, f)),
                  pl.BlockSpec((CONV_W, tf), lambda i, f: (0, nf + f)),
                  pl.BlockSpec((1, tf), lambda i, f: (0, f)),
                  pl.BlockSpec((1, tf), lambda i, f: (0, nf + f)),
                  pl.BlockSpec((tf, d), lambda i, f: (f, 0)),
                  pl.BlockSpec((nseg, SUBLANES, tf), lambda i, f: (sidx(i), 0, f)),
                  pl.BlockSpec((nseg, SUBLANES, tf), lambda i, f: (sidx(i), 0, nf + f)),
                  pl.BlockSpec((1, d), lambda i, f: (0, 0))],
        out_specs=[pl.BlockSpec((tm, d), lambda i, f: (i, 0)),
                   pl.BlockSpec((tm, d), lambda i, f: (i, 0)),
                   pl.BlockSpec((nseg, SUBLANES, tf), lambda i, f: (i, 0, f)),
                   pl.BlockSpec((nseg, SUBLANES, tf), lambda i, f: (i, 0, f))],
        out_shape=[jax.ShapeDtypeStruct((m, d), F32),
                   jax.ShapeDtypeStruct((m, d), next_dtype),
                   jax.ShapeDtypeStruct((m // seg, SUBLANES, D_FF), F32),
                   jax.ShapeDtypeStruct((m // seg, SUBLANES, D_FF), F32)],
        scratch_shapes=[pltpu.VMEM((tm, d), BF16),
                        pltpu.VMEM((tm, d), F32),
                        pltpu.VMEM((nseg, SUBLANES + seg, tf), F32),
                        pltpu.VMEM((nseg, SUBLANES + seg, tf), F32),
                        pltpu.VMEM((nf, SUBLANES, tf), F32),
                        pltpu.VMEM((nf, SUBLANES, tf), F32)],
        compiler_params=_cparams(("arbitrary", "arbitrary")),
        name="conv_ffn",
    )(x, g.reshape(1, d), w_up, w_up, conv_w, conv_w, cb, cb, w_down, prev8, prev8, g_next.reshape(1, d))


def _rope_tables(pos, head_dim, rot, lanes_valid=LANES):
    half = rot // 2
    freqs = ROPE_THETA ** (-jnp.arange(half, dtype=F32) / half)
    ang = pos.astype(F32)[:, None] * freqs[None, :]
    cos, sin = jnp.cos(ang), jnp.sin(ang)
    lane = np.arange(LANES)
    d = lane % head_dim
    first = (d < half) & (lane < lanes_valid)
    second = (d >= half) & (d < rot) & (lane < lanes_valid)
    fidx = np.clip(np.where(d < half, d, d - half), 0, half - 1)
    cosl, sinl = cos[:, fidx], sin[:, fidx]
    c = jnp.where(first | second, cosl, 1.0)
    s1 = jnp.where(second, sinl, 0.0)
    s2 = jnp.where(first, -sinl, 0.0)
    return c, s1, s2, half


def _prep_weights(w_in_ab, w_out_ab, w_dqkv, w_uq, w_ukv, w_o_mla, w_up, w_down):
    n_even, n_odd = w_in_ab.shape[0], w_dqkv.shape[0]
    w = {}
    n_a = H_A * DH + 2 * KV_A * DH + H_I * D_IDX
    w["in_a"] = w_in_ab[:, :, :n_a].astype(BF16)
    pad = jnp.zeros((n_even, D_MODEL, LANES - D_IDX - H_I), BF16)
    w["in_b"] = jnp.concatenate([w_in_ab[:, :, n_a:n_a + D_IDX + H_I].astype(BF16), pad,
                                 w_in_ab[:, :, n_a + D_IDX + H_I:].astype(BF16)], axis=-1)
    w["out_ab"] = w_out_ab.astype(BF16)
    w["dqkr"] = jnp.concatenate([w_dqkv.astype(BF16),
                                 jnp.zeros((n_odd, D_MODEL, LANES - ROPE_C), BF16)], axis=-1)
    uq = w_uq.reshape(n_odd, Q_RANK, H_C, NOPE + ROPE_C)
    uqn = uq[..., :NOPE].reshape(n_odd, Q_RANK, H_C * NOPE).astype(BF16)
    uqr = uq[..., NOPE:].astype(BF16)
    uqr = jnp.concatenate([uqr, jnp.zeros_like(uqr)], axis=-1).reshape(n_odd, Q_RANK, H_C * LANES)
    w["uq"] = jnp.concatenate([uqn, uqr], axis=-1)
    w["ukv"] = w_ukv.astype(BF16)
    w["o_mla"] = w_o_mla.astype(BF16)
    w["up"] = w_up.astype(BF16)
    w["down"] = w_down.astype(BF16)
    return w


def _with_past(past, new, s_pad):
    b, _, fdim = new.shape
    parts = [new.astype(BF16)] if past is None else [past.astype(BF16), new.astype(BF16)]
    n = sum(p.shape[1] for p in parts)
    if s_pad > n:
        parts.append(jnp.zeros((b, s_pad - n, fdim), BF16))
    return parts[0] if len(parts) == 1 else jnp.concatenate(parts, axis=1)


def _trunk(x, past, conv_state, w, p, cfg):
    b, t, d = x.shape
    m = b * t
    q0 = 0 if past is None else past[0].shape[2]
    s_valid = q0 + t
    tk = cfg["tk"]
    s_pad = -(-s_valid // tk) * tk
    pos = q0 + jnp.tile(jnp.arange(t), b)
    tab_a = _rope_tables(pos, DH, ROT_A)
    tab_i = _rope_tables(pos, D_IDX, ROT_I)
    tab_ki = _rope_tables(pos, D_IDX, ROT_I, lanes_valid=D_IDX)
    tab_kr = _rope_tables(pos, ROPE_C, ROPE_C, lanes_valid=ROPE_C)

    xf = x.reshape(m, d)
    ab_rows, mla_rows, conv_rows = [], [], []
    sc_ab = DH ** -0.5 * LOG2E
    sc_c = (NOPE + ROPE_C) ** -0.5 * LOG2E
    h = _rmsnorm(xf, p["norm_mix"][0], BF16)
    for l in range(DEPTH):
        if l % 2 == 0:
            e = l // 2
            lp = None if past is None else tuple(c[e] for c in past[:5])
            qa, ka32, ka16, va32, va16, qi = _proj(
                [h], w["in_a"][e],
                [dict(width=H_A * DH, dtypes=(BF16,), rope=0, scale=sc_ab),
                 dict(width=KV_A * DH, dtypes=(F32, BF16), rope=0, split_heads=True),
                 dict(width=KV_A * DH, dtypes=(F32, BF16), split_heads=True),
                 dict(width=H_I * D_IDX, dtypes=(BF16,), rope=1)],
                tabs=(tab_a, tab_i), name="proj_in_a")
            kiwi, qb, kb32, kb16, vb32, vb16 = _proj(
                [h], w["in_b"][e],
                [dict(width=LANES, dtypes=(F32,), rope=0),
                 dict(width=H_B * DH, dtypes=(BF16,), scale=sc_ab),
                 dict(width=H_B * DH, dtypes=(F32, BF16), split_heads=True),
                 dict(width=H_B * DH, dtypes=(F32, BF16), split_heads=True)],
                tabs=(tab_ki,), name="proj_in_b")
            ki32 = kiwi[:, :D_IDX]
            wi = kiwi[:, D_IDX:D_IDX + H_I]
            ab_rows.append((ka32.reshape(b, t, KV_A, DH), va32.reshape(b, t, KV_A, DH),
                            ki32.reshape(b, t, D_IDX), kb32.reshape(b, t, H_B, DH),
                            vb32.reshape(b, t, H_B, DH)))

            def full(idx, new16):
                pst = None if lp is None else lp[idx].reshape(b, q0, -1)
                return _with_past(pst, new16.reshape(b, t, -1), s_pad)

            ki_all = full(2, ki32)
            zk = jnp.zeros_like(ki_all)
            ki2 = jnp.concatenate([ki_all, zk, zk, ki_all], axis=-1)
            oa = _dsa(qa.reshape(b, t, -1), qi.reshape(b, t, -1), wi.reshape(b, t, H_I), full(0, ka16),
                      full(1, va16), ki2, tq=cfg["tq_a"], tk=tk, s_valid=s_valid, q0=q0)
            if lp is None:
                ob = _stick_breaking(qb.reshape(b, t, -1), full(3, kb16), full(4, vb16), tq=cfg["tq_b"],
                                     tk=cfg["tk_b"], q0=q0, hpg=cfg["hpg_b"], nchunk=cfg["nc_b"])
            else:
                ob = _stick_breaking(qb.reshape(b, t, -1), kb16.reshape(b, t, -1), vb16.reshape(b, t, -1),
                                     tq=cfg["tq_b"], tk=cfg["tk_b"], q0=q0, hpg=cfg["hpg_b"],
                                     nchunk=cfg["nc_b"], past=(lp[3], lp[4]))
            (xf,) = _proj([oa.reshape(m, -1), ob.reshape(m, -1)], w["out_ab"][e],
                          [dict(width=D_MODEL, dtypes=(F32,))], resid=xf, name="proj_out_ab")
        else:
            od = l // 2
            gains = jnp.concatenate([p["g_q"][od], p["g_kv"][od]])
            cq16, ckv32, ckv16, kr32w, kr16w = _proj(
                [h], w["dqkr"][od],
                [dict(width=Q_RANK, dtypes=(BF16,), gain_off=0),
                 dict(width=KV_RANK, dtypes=(F32, BF16), gain_off=Q_RANK),
                 dict(width=LANES, dtypes=(F32, BF16), rope=0)],
                tabs=(tab_kr,), gain=gains, name="proj_dqkr")
            mla_rows.append((ckv32.reshape(b, t, KV_RANK), kr32w[:, :ROPE_C].reshape(b, t, ROPE_C)))
            qn, qr = _proj([cq16], w["uq"][od],
                           [dict(width=H_C * NOPE, dtypes=(BF16,), scale=sc_c),
                            dict(width=H_C * LANES, dtypes=(BF16,), rope=0, scale=sc_c)],
                           tabs=(tab_kr,), name="proj_uq")
            pc = None if past is None else past[5][od]
            pk = None if past is None else jnp.pad(past[6][od], ((0, 0), (0, 0), (0, LANES - ROPE_C)))
            c_all = _with_past(pc, ckv16.reshape(b, t, KV_RANK), s_pad)
            kr_all = _with_past(pk, kr16w.reshape(b, t, LANES), s_pad)
            if cfg["latent_c"]:
                qa_hm, qr_hm = _absorb_q(qn, qr, w["ukv"][od])
                lat = _mla_latent(qa_hm, qr_hm, c_all, kr_all, t=t, tk=cfg["tk_c"], s_valid=s_valid, q0=q0)
                o = _latent_out(lat, w["ukv"][od])
            else:
                (kv,) = _proj([c_all.reshape(b * s_pad, KV_RANK)], w["ukv"][od],
                              [dict(width=H_C * (NOPE + VD), dtypes=(BF16,))], name="proj_ukv")
                o = _mla(qn.reshape(b, t, -1), qr.reshape(b, t, -1), kv.reshape(b, s_pad, -1), kr_all,
                         tq=cfg["tq_c"], tk=cfg["tk_c"], s_valid=s_valid, q0=q0, hpg=cfg["hpg_c"],
                         nchunk=cfg["nc_c"])
            (xf,) = _proj([o.reshape(m, -1)], w["o_mla"][od], [dict(width=D_MODEL, dtypes=(F32,))],
                          resid=xf, name="proj_o_mla")

        if conv_state is None:
            prev8 = jnp.zeros((b, SUBLANES, 2 * D_FF), F32)
        else:
            prev8 = jnp.concatenate([jnp.zeros((b, SUBLANES - (CONV_W - 1), 2 * D_FF), F32),
                                     conv_state[l]], axis=1)
        final = l == DEPTH - 1
        g_next = p["norm_final"] if final else p["norm_mix"][l + 1]
        xf, h, sg, sv = _ffn(xf, p["norm_ffn"][l], w["up"][l], p["conv_w"][l], p["conv_b"][l], w["down"][l],
                             prev8, g_next, F32 if final else BF16, seg=cfg["seg"], carried=cfg["carried"],
                             tm=cfg["tm_ffn"], tf=cfg["tf"])
        last = [s8.reshape(b, -1, SUBLANES, D_FF)[:, -1, SUBLANES - (CONV_W - 1):] for s8 in (sg, sv)]
        conv_rows.append(jnp.concatenate(last, axis=-1))

    y = h.reshape(b, t, d)
    new_ab = [jnp.stack([r[i] for r in ab_rows]) for i in range(5)]
    new_mla = [jnp.stack([r[i] for r in mla_rows]) for i in range(2)]
    return y, new_ab + new_mla + [jnp.stack(conv_rows)]


def kernel(x_prompt, x_sample, cache_k_a, cache_v_a, cache_idx_k, cache_k_b, cache_v_b, cache_ckv, cache_krope, state_conv, norm_mix, norm_ffn, norm_final, w_in_ab, w_out_ab, w_dqkv, g_q, g_kv, w_uq, w_ukv, w_o_mla, w_up, conv_w, conv_b, w_down):
    w = _prep_weights(w_in_ab, w_out_ab, w_dqkv, w_uq, w_ukv, w_o_mla, w_up, w_down)
    p = dict(norm_mix=norm_mix, norm_ffn=norm_ffn, norm_final=norm_final, g_q=g_q, g_kv=g_kv,
             conv_w=conv_w, conv_b=conv_b)
    cfg_p = dict(tk=512, tq_a=256, tq_b=256, tk_b=256, hpg_b=4, nc_b=1, tq_c=512, tk_c=1024, hpg_c=2, nc_c=2,
                 latent_c=False, seg=512, carried=True, tm_ffn=512, tf=512)
    t_s = x_sample.shape[1]
    cfg_s = dict(tk=256, tq_a=t_s, tq_b=t_s, tk_b=256, hpg_b=H_B, nc_b=1, tq_c=t_s, tk_c=256, hpg_c=8, nc_c=1,
                 latent_c=True, seg=t_s, carried=False, tm_ffn=x_sample.shape[0] * t_s, tf=512)
    y_p, st_p = _trunk(x_prompt, None, None, w, p, cfg_p)
    past = (cache_k_a, cache_v_a, cache_idx_k, cache_k_b, cache_v_b, cache_ckv, cache_krope)
    y_s, st_s = _trunk(x_sample, past, state_conv, w, p, cfg_s)
    return (y_p, y_s, *st_p, *st_s)
```

```python
import functools

import numpy as np
import jax
import jax.numpy as jnp
from jax import lax
from jax.experimental import pallas as pl
from jax.experimental.pallas import tpu as pltpu

F32 = jnp.float32
BF16 = jnp.bfloat16

D_MODEL = 2048
DEPTH = 4
CHUNK = 64
CHUNK_SHIFT = CHUNK.bit_length() - 1
ROPE_THETA = 500000.0
EPS = 1e-6
H_A, KV_A, DH = 8, 2, 128
ROT_A = DH // 4
H_I, D_IDX = 16, 64
ROT_I = D_IDX // 4
TOPK_MAX = 256
H_B = 8
H_C, Q_RANK, KV_RANK, NOPE, ROPE_C, VD = 16, 512, 512, 128, 64, 128
D_FF = 5632
CONV_W = 3

LANES = 128
SUBLANES = 8
VMEM_LIMIT = 56 * 1024 * 1024
NEG = -1e30
INT_MIN = np.int32(-2 ** 31)
INT_MAX = np.int32(2 ** 31 - 1)
LOG2E = float(np.log2(np.e))


def _cparams(sem):
    return pltpu.CompilerParams(dimension_semantics=sem, vmem_limit_bytes=VMEM_LIMIT)


def _dot_nt(a, b):
    return lax.dot_general(a, b, (((1,), (1,)), ((), ())), preferred_element_type=F32)


def _rms_body(x_ref, g_ref, o_ref):
    x = x_ref[...]
    y = x * lax.rsqrt(jnp.mean(x * x, axis=-1, keepdims=True) + EPS)
    o_ref[...] = (y * g_ref[...]).astype(o_ref.dtype)


def _rmsnorm(x, g, out_dtype):
    m, d = x.shape
    tm = min(m, 512)
    return pl.pallas_call(
        _rms_body,
        grid=(m // tm,),
        in_specs=[pl.BlockSpec((tm, d), lambda i: (i, 0)),
                  pl.BlockSpec((1, d), lambda i: (0, 0))],
        out_specs=pl.BlockSpec((tm, d), lambda i: (i, 0)),
        out_shape=jax.ShapeDtypeStruct((m, d), out_dtype),
        compiler_params=_cparams(("parallel",)),
        name="rmsnorm",
    )(x, g.reshape(1, d))


PROJ_CHUNK = 512
COUNT_ROWS = 64


def _proj_body(*refs, nx, groups, tab_halves, has_gain, has_resid):
    it = iter(refs)
    x_refs = [next(it) for _ in range(nx)]
    w_ref = next(it)
    tabs = [(next(it), next(it), next(it)) for _ in tab_halves]
    g_ref = next(it) if has_gain else None
    r_ref = next(it) if has_resid else None
    outs = list(it)

    xs = [r[...] for r in x_refs]
    x = xs[0] if nx == 1 else jnp.concatenate(xs, axis=1)
    accs = []
    c0 = 0
    for grp in groups:
        width = grp["width"]
        chunk = min(width, PROJ_CHUNK)
        for cc in range(0, width, chunk):
            accs.append(jnp.dot(x, w_ref[:, c0 + cc:c0 + cc + chunk], preferred_element_type=F32))
        c0 += width
    accs = iter(accs)
    oi = 0
    for grp in groups:
        width = grp["width"]
        chunk = min(width, PROJ_CHUNK)
        for cc in range(0, width, chunk):
            acc = next(accs)
            if grp.get("gain_off") is not None:
                assert chunk == width
                go = grp["gain_off"]
                acc = (acc * lax.rsqrt(jnp.mean(acc * acc, axis=-1, keepdims=True) + EPS)
                       * g_ref[:, go:go + width])
            if grp.get("rope") is not None:
                c_ref, s1_ref, s2_ref = tabs[grp["rope"]]
                half = tab_halves[grp["rope"]]
                c, s1, s2 = c_ref[...], s1_ref[...], s2_ref[...]
                parts = []
                for gi in range(chunk // LANES):
                    xg = acc[:, gi * LANES:(gi + 1) * LANES]
                    parts.append(xg * c + pltpu.roll(xg, half, 1) * s1
                                 + pltpu.roll(xg, LANES - half, 1) * s2)
                acc = parts[0] if len(parts) == 1 else jnp.concatenate(parts, axis=1)
            if grp.get("scale") is not None:
                acc = acc * grp["scale"]
            if has_resid:
                acc = acc + r_ref[:, cc:cc + chunk]
            for k in range(len(grp["dtypes"])):
                o = outs[oi + k]
                if len(o.shape) == 3:
                    for gi in range(chunk // LANES):
                        o[:, cc // LANES + gi, :] = acc[:, gi * LANES:(gi + 1) * LANES].astype(o.dtype)
                else:
                    o[:, cc:cc + chunk] = acc.astype(o.dtype)
        oi += len(grp["dtypes"])


def _proj(xs, w, groups, *, tabs=(), gain=None, resid=None, name="proj"):
    m = xs[0].shape[0]
    k, n = w.shape
    tm = min(m, 512)
    assert m % tm == 0 and n == sum(g["width"] for g in groups) and k == sum(x.shape[1] for x in xs)
    assert resid is None or len(groups) == 1
    in_specs = [pl.BlockSpec((tm, x.shape[1]), lambda i: (i, 0)) for x in xs]
    in_specs.append(pl.BlockSpec((k, n), lambda i: (0, 0)))
    args = list(xs) + [w]
    for c, s1, s2, _ in tabs:
        for t in (c, s1, s2):
            in_specs.append(pl.BlockSpec((tm, LANES), lambda i: (i, 0)))
            args.append(t)
    if gain is not None:
        in_specs.append(pl.BlockSpec((1, gain.shape[0]), lambda i: (0, 0)))
        args.append(gain.reshape(1, -1))
    if resid is not None:
        in_specs.append(pl.BlockSpec((tm, n), lambda i: (i, 0)))
        args.append(resid)
    out_specs, out_shape = [], []
    for g in groups:
        for dt in g["dtypes"]:
            if g.get("split_heads") and dt == F32:
                nh = g["width"] // LANES
                out_specs.append(pl.BlockSpec((tm, nh, LANES), lambda i: (i, 0, 0)))
                out_shape.append(jax.ShapeDtypeStruct((m, nh, LANES), dt))
            else:
                out_specs.append(pl.BlockSpec((tm, g["width"]), lambda i: (i, 0)))
                out_shape.append(jax.ShapeDtypeStruct((m, g["width"]), dt))
    body = functools.partial(_proj_body, nx=len(xs), groups=groups, tab_halves=[t[3] for t in tabs],
                             has_gain=gain is not None, has_resid=resid is not None)
    return pl.pallas_call(
        body,
        grid=(m // tm,),
        in_specs=in_specs,
        out_specs=out_specs,
        out_shape=out_shape,
        compiler_params=_cparams(("parallel",)),
        name=name,
    )(*args)


def _dsa_body(qa_ref, qi_ref, wi_ref, ka_ref, va_ref, ki_ref, tri_ref, o_ref,
              key_scr, qs_scr, m_scr, l_scr, acc_scr, *, tq, tk, s_valid, q0, topk):
    i = pl.program_id(1)
    q_first = q0 + i * tq
    q_last = q_first + tq - 1
    adm_end = jnp.minimum((q_last // CHUNK + 1) * CHUNK, s_valid)
    nkb = (adm_end + tk - 1) // tk
    q_pos = q_first + lax.broadcasted_iota(jnp.int32, (tq, tk), 0)
    q_lim = jnp.minimum(((q_pos >> CHUNK_SHIFT) + 1) * CHUNK, s_valid)
    col = lax.broadcasted_iota(jnp.int32, (tq, tk), 1)
    rep = H_A // KV_A

    w = wi_ref[0] * ((H_I * D_IDX) ** -0.5)
    pg = 4
    pairs = LANES // D_IDX
    qi = qi_ref[0]
    q_groups = [jnp.concatenate([qi[:, (g * pg + pp) * LANES:(g * pg + pp + 1) * LANES]
                                 for pp in range(pg)], axis=0) for g in range(H_I // (pairs * pg))]

    def to_key(x):
        bits = pltpu.bitcast(x, jnp.int32)
        return bits ^ ((bits >> 31) & INT_MAX)

    transposed = key_scr.shape[1] == tq

    if transposed:
        wt = w
        krow = lax.broadcasted_iota(jnp.int32, (tk, tq), 0)
        q_lim_t = jnp.minimum((((q_first + lax.broadcasted_iota(jnp.int32, (tk, tq), 1)) >> CHUNK_SHIFT) + 1) * CHUNK,
                              s_valid)

    def p1(j, c):
        k0 = pl.multiple_of(j * tk, tk)
        if transposed:
            sc = jnp.zeros((tk, tq), F32)
            for g, qg in enumerate(q_groups):
                for half in range(pairs):
                    rel = _dot_nt(ki_ref[0, pl.ds(k0, tk), half * LANES:(half + 1) * LANES], qg)
                    for pp in range(pg):
                        h = (g * pg + pp) * pairs + half
                        sc = sc + jnp.maximum(rel[:, pp * tq:(pp + 1) * tq], 0.0) * wt[h:h + 1, :]
            key_scr[pl.ds(k0, tk), :] = jnp.where(k0 + krow < q_lim_t, to_key(sc), INT_MIN)
            return c
        sc = jnp.zeros((tq, tk), F32)
        for g, qg in enumerate(q_groups):
            for half in range(pairs):
                rel = _dot_nt(qg, ki_ref[0, pl.ds(k0, tk), half * LANES:(half + 1) * LANES])
                for pp in range(pg):
                    h = (g * pg + pp) * pairs + half
                    sc = sc + jnp.maximum(rel[pp * tq:(pp + 1) * tq], 0.0) * w[:, h:h + 1]
        key_scr[:, pl.ds(k0, tk)] = jnp.where(k0 + col < q_lim, to_key(sc), INT_MIN)
        return c

    lax.fori_loop(0, nkb, p1, 0)

    def count_ge(v):
        def cb(j, c):
            k0 = pl.multiple_of(j * tk, tk)
            if transposed:
                ge = jnp.where(key_scr[pl.ds(k0, tk), :] >= v, 1.0, 0.0)
                return c + jnp.sum(ge.reshape(tk // COUNT_ROWS, COUNT_ROWS, tq), axis=0)
            ge = jnp.where(key_scr[:, pl.ds(k0, tk)] >= v, 1.0, 0.0)
            part = ge[:, 0:LANES]
            for cc in range(1, tk // LANES):
                part = part + ge[:, cc * LANES:(cc + 1) * LANES]
            return c + part
        if transposed:
            c = lax.fori_loop(0, nkb, cb, jnp.zeros((COUNT_ROWS, tq), F32))
            return jnp.sum(c, axis=0, keepdims=True)
        c = lax.fori_loop(0, nkb, cb, jnp.zeros((tq, LANES), F32))
        return jnp.sum(c, axis=1, keepdims=True)

    def bis(_, lohi):
        lo, hi = lohi
        mid = (lo >> 1) + (hi >> 1) + (lo & hi & 1)
        ok = count_ge(mid) >= float(topk)
        return jnp.where(ok, mid, lo), jnp.where(ok, hi, mid)

    vshape = (1, tq) if transposed else (tq, 1)
    thr, _ = lax.fori_loop(0, 32, bis, (jnp.full(vshape, INT_MIN, jnp.int32),
                                        jnp.full(vshape, INT_MAX, jnp.int32)))
    need = float(topk) - count_ge(thr + 1)
    if transposed:
        thr = pltpu.bitcast(jnp.transpose(pltpu.bitcast(jnp.broadcast_to(thr, (LANES, tq)), F32)), jnp.int32)
        need = jnp.transpose(jnp.broadcast_to(need, (LANES, tq)))
        thr = jnp.concatenate([thr] * (tk // LANES), axis=1)
        need = jnp.concatenate([need] * (tk // LANES), axis=1)

    qa = qa_ref[0]
    for g in range(KV_A):
        qs_scr[g] = jnp.concatenate(
            [qa[:, (g * rep + r) * DH:(g * rep + r + 1) * DH] for r in range(rep)], axis=0)
    m_scr[...] = jnp.full(m_scr.shape, NEG, F32)
    l_scr[...] = jnp.zeros(l_scr.shape, F32)
    acc_scr[...] = jnp.zeros(acc_scr.shape, F32)

    def p3(j, eqc):
        k0 = pl.multiple_of(j * tk, tk)
        if transposed:
            key = pltpu.bitcast(jnp.transpose(pltpu.bitcast(key_scr[pl.ds(k0, tk), :], F32)), jnp.int32)
        else:
            key = key_scr[:, pl.ds(k0, tk)]
        eq = key == thr
        eqf = jnp.where(eq, 1.0, 0.0)
        prefix = jnp.dot(eqf.astype(BF16), tri_ref[...], preferred_element_type=F32) + eqc
        bias = jnp.where(key > thr, 0.0, jnp.where(eq, jnp.where(prefix < need, 0.0, NEG), NEG))
        bias = jnp.where(key == INT_MIN, NEG, bias)
        eqc = eqc + jnp.sum(eqf, axis=1, keepdims=True)
        ss = [_dot_nt(qs_scr[g], ka_ref[0, pl.ds(k0, tk), g * DH:(g + 1) * DH])
              for g in range(KV_A)]
        for g in range(KV_A):
            vg = va_ref[0, pl.ds(k0, tk), g * DH:(g + 1) * DH]
            s = ss[g]
            ps, alphas = [], []
            for r in range(rep):
                h = g * rep + r
                sr = s[r * tq:(r + 1) * tq] + bias
                m_prev = m_scr[h]
                m_new = jnp.maximum(m_prev, jnp.max(sr, axis=1, keepdims=True))
                alpha = jnp.exp2(m_prev - m_new)
                p = jnp.exp2(sr - jnp.concatenate([m_new] * (tk // LANES), axis=1))
                l_scr[h] = alpha * l_scr[h] + jnp.sum(p, axis=1, keepdims=True)
                m_scr[h] = m_new
                ps.append(p.astype(BF16))
                alphas.append(alpha)
            for r in range(rep):
                h = g * rep + r
                acc_scr[h] = acc_scr[h] * alphas[r] + jnp.dot(ps[r], vg, preferred_element_type=F32)
        return eqc

    lax.fori_loop(0, nkb, p3, jnp.zeros((tq, 1), F32))
    for h in range(H_A):
        o_ref[0, :, h * DH:(h + 1) * DH] = (acc_scr[h] / l_scr[h]).astype(o_ref.dtype)


def _dsa(qa, qi, wi, ka, va, ki, *, tq, tk, s_valid, q0):
    b, t, _ = qa.shape
    s_pad = ka.shape[1]
    assert s_pad % tk == 0 and t % tq == 0 and tk >= TOPK_MAX
    topk = min(TOPK_MAX, s_valid // 4)
    tri = jnp.asarray(np.triu(np.ones((tk, tk), np.float32), 1), BF16)
    body = functools.partial(_dsa_body, tq=tq, tk=tk, s_valid=s_valid, q0=q0, topk=topk)
    rep = H_A // KV_A
    transposed = tq % LANES == 0
    if transposed:
        wi = jnp.transpose(wi, (0, 2, 1))
        wi_spec = pl.BlockSpec((1, H_I, tq), lambda bi, i: (bi, 0, i))
    else:
        wi_spec = pl.BlockSpec((1, tq, H_I), lambda bi, i: (bi, i, 0))
    return pl.pallas_call(
        body,
        grid=(b, t // tq),
        in_specs=[pl.BlockSpec((1, tq, H_A * DH), lambda bi, i: (bi, i, 0)),
                  pl.BlockSpec((1, tq, H_I * D_IDX), lambda bi, i: (bi, i, 0)),
                  wi_spec,
                  pl.BlockSpec((1, s_pad, KV_A * DH), lambda bi, i: (bi, 0, 0)),
                  pl.BlockSpec((1, s_pad, KV_A * DH), lambda bi, i: (bi, 0, 0)),
                  pl.BlockSpec((1, s_pad, 2 * LANES), lambda bi, i: (bi, 0, 0)),
                  pl.BlockSpec((tk, tk), lambda bi, i: (0, 0))],
        out_specs=pl.BlockSpec((1, tq, H_A * DH), lambda bi, i: (bi, i, 0)),
        out_shape=jax.ShapeDtypeStruct((b, t, H_A * DH), BF16),
        scratch_shapes=[pltpu.VMEM((s_pad, tq) if transposed else (tq, s_pad), jnp.int32),
                        pltpu.VMEM((KV_A, rep * tq, DH), BF16),
                        pltpu.VMEM((H_A, tq, LANES), F32),
                        pltpu.VMEM((H_A, tq, LANES), F32),
                        pltpu.VMEM((H_A, tq, DH), F32)],
        compiler_params=_cparams(("parallel", "arbitrary")),
        name="dsa",
    )(qa, qi, wi, ka, va, ki, tri)


def _sb_body(q_ref, k_ref, v_ref, low_ref, *rest, tq, tk, s_pad, q0, hpg, nchunk, past_heads):
    if past_heads:
        kp_ref, vp_ref, o_ref = rest
    else:
        (o_ref,) = rest
    i = pl.program_id(2)
    tqc = tq // nchunk
    q_first = q0 + i * tq
    q_last = q_first + tq - 1
    nkb = jnp.minimum((q_last + tk - 1) // tk, s_pad // tk)
    n_full = jnp.minimum(q_first // tk, nkb)
    row = lax.broadcasted_iota(jnp.int32, (tqc, tk), 0)
    col = lax.broadcasted_iota(jnp.int32, (tqc, tk), 1)
    chains = [(g, c) for g in range(hpg) for c in range(nchunk)]

    def step(j, carry, masked):
        k0 = pl.multiple_of(j * tk, tk)

        def load(new_ref, past_ref, g):
            if not past_heads:
                return new_ref[0, pl.ds(k0, tk), g * DH:(g + 1) * DH]
            if masked:
                return new_ref[0, :, g * DH:(g + 1) * DH]
            return past_ref[0, pl.ds(k0 * past_heads + g, tk, stride=past_heads), :].astype(BF16)

        zs = [_dot_nt(q_ref[0, c * tqc:(c + 1) * tqc, g * DH:(g + 1) * DH],
                      load(k_ref, kp_ref if past_heads else None, g)) for g, c in chains]
        lbs, lks, sufs, causals = [], [], [], []
        for (g, c), z in zip(chains, zs):
            lb = jnp.minimum(z, 0.0) - jnp.log(1.0 + jnp.exp2(-jnp.abs(z))) * LOG2E
            lk = lb - z
            causal = None
            if masked:
                causal = k0 + col < q_first + c * tqc + row
                lk = jnp.where(causal, lk, 0.0)
            hi = lk.astype(BF16)
            lo = (lk - hi.astype(F32)).astype(BF16)
            lbs.append(lb)
            lks.append(lk)
            causals.append(causal)
            sufs.append(jnp.dot(jnp.concatenate([hi, lo], axis=1), low_ref[...], preferred_element_type=F32))
        out = []
        for n, (g, c) in enumerate(chains):
            run, acc = carry[2 * n], carry[2 * n + 1]
            wgt = jnp.exp2(lbs[n] + sufs[n] + run)
            if masked:
                wgt = jnp.where(causals[n], wgt, 0.0)
            acc = acc + jnp.dot(wgt.astype(BF16), load(v_ref, vp_ref if past_heads else None, g),
                                preferred_element_type=F32)
            run = run + jnp.sum(lks[n], axis=1, keepdims=True)
            out += [run, acc]
        return tuple(out)

    carry = []
    for _ in chains:
        carry += [jnp.zeros((tqc, 1), F32), jnp.zeros((tqc, DH), F32)]
    carry = tuple(carry)
    carry = lax.fori_loop(0, nkb - n_full, lambda s, c: step(nkb - 1 - s, c, True), carry)
    carry = lax.fori_loop(0, n_full, lambda s, c: step(n_full - 1 - s, c, False), carry)
    for n, (g, c) in enumerate(chains):
        o_ref[0, c * tqc:(c + 1) * tqc, g * DH:(g + 1) * DH] = carry[2 * n + 1].astype(o_ref.dtype)


def _stick_breaking(qb, kb, vb, *, tq, tk, q0, hpg, nchunk, past=None):
    b, t, _ = qb.shape
    low = np.tril(np.ones((tk, tk), np.float32), -1)
    low = jnp.asarray(np.concatenate([low, low], axis=0), BF16)
    wd = hpg * DH
    if past is None:
        s_pad, extra, extra_specs, past_heads = kb.shape[1], [], [], 0
    else:
        assert hpg == H_B and t == tq and q0 % tk == 0 and t <= tk and past[0].shape[1:] == (q0, H_B, DH)
        s_pad, past_heads = q0 + tk, H_B
        kb, vb = (jnp.pad(a, ((0, 0), (0, tk - t), (0, 0))) for a in (kb, vb))
        extra = [c.reshape(b, q0 * H_B, DH) for c in past]
        extra_specs = [pl.BlockSpec((1, q0 * H_B, DH), lambda bi, h, i: (bi, 0, 0))] * 2
    s_new = kb.shape[1]
    assert s_pad % tk == 0 and t % tq == 0 and H_B % hpg == 0 and tq % nchunk == 0
    body = functools.partial(_sb_body, tq=tq, tk=tk, s_pad=s_pad, q0=q0, hpg=hpg, nchunk=nchunk,
                             past_heads=past_heads)
    return pl.pallas_call(
        body,
        grid=(b, H_B // hpg, t // tq),
        in_specs=[pl.BlockSpec((1, tq, wd), lambda bi, h, i: (bi, i, h)),
                  pl.BlockSpec((1, s_new, wd), lambda bi, h, i: (bi, 0, h)),
                  pl.BlockSpec((1, s_new, wd), lambda bi, h, i: (bi, 0, h)),
                  pl.BlockSpec((2 * tk, tk), lambda bi, h, i: (0, 0))] + extra_specs,
        out_specs=pl.BlockSpec((1, tq, wd), lambda bi, h, i: (bi, i, h)),
        out_shape=jax.ShapeDtypeStruct((b, t, H_B * DH), BF16),
        compiler_params=_cparams(("parallel", "parallel", "arbitrary")),
        name="stick_breaking",
    )(qb, kb, vb, low, *extra)


def _mla_body(qn_ref, qr_ref, kv_ref, kr_ref, o_ref, *, tq, tk, s_valid, q0, hpg, nchunk):
    i = pl.program_id(2)
    tqc = tq // nchunk
    q_first = q0 + i * tq
    q_last = q_first + tq - 1
    lim_first = jnp.minimum((q_first // CHUNK + 1) * CHUNK, s_valid)
    lim_last = jnp.minimum((q_last // CHUNK + 1) * CHUNK, s_valid)
    nkb = (lim_last + tk - 1) // tk
    n_full = lim_first // tk
    row = lax.broadcasted_iota(jnp.int32, (tqc, tk), 0)
    col = lax.broadcasted_iota(jnp.int32, (tqc, tk), 1)
    chains = [(g, c) for g in range(hpg) for c in range(nchunk)]
    hw = NOPE + VD

    def step(j, carry, masked):
        k0 = pl.multiple_of(j * tk, tk)
        kr = kr_ref[0, pl.ds(k0, tk), :]
        ss = []
        for g, c in chains:
            rows = slice(c * tqc, (c + 1) * tqc)
            qc = jnp.concatenate([qn_ref[0, rows, g * NOPE:(g + 1) * NOPE],
                                  qr_ref[0, rows, g * LANES:(g + 1) * LANES]], axis=1)
            kc = jnp.concatenate([kv_ref[0, pl.ds(k0, tk), g * hw:g * hw + NOPE], kr], axis=1)
            ss.append(_dot_nt(qc, kc))
        ps, stats = [], []
        for n, (g, c) in enumerate(chains):
            m_prev = carry[2 * n]
            s = ss[n]
            if masked:
                lim = jnp.minimum((((q_first + c * tqc + row) >> CHUNK_SHIFT) + 1) * CHUNK, s_valid)
                s = jnp.where(k0 + col < lim, s, NEG)
            m_new = jnp.maximum(m_prev, jnp.max(s, axis=1, keepdims=True))
            alpha = jnp.exp2(m_prev - m_new)
            p = jnp.exp2(s - jnp.concatenate([m_new] * (tk // LANES), axis=1))
            stats.append((m_new, alpha))
            ps.append(p.astype(BF16))
        out = []
        for n, (g, c) in enumerate(chains):
            m_new, alpha = stats[n]
            vb = jnp.concatenate([kv_ref[0, pl.ds(k0, tk), g * hw + NOPE:(g + 1) * hw], ones_k], axis=1)
            acc = (carry[2 * n + 1] * jnp.concatenate([alpha, alpha], axis=1)
                   + jnp.dot(ps[n], vb, preferred_element_type=F32))
            out += [m_new, acc]
        return tuple(out)

    ones_k = jnp.ones((tk, LANES), BF16)
    carry = []
    for _ in chains:
        carry += [jnp.full((tqc, LANES), NEG, F32), jnp.zeros((tqc, VD + LANES), F32)]
    carry = tuple(carry)
    carry = lax.fori_loop(0, n_full, lambda j, c: step(j, c, False), carry)
    carry = lax.fori_loop(n_full, nkb, lambda j, c: step(j, c, True), carry)
    for n, (g, c) in enumerate(chains):
        acc = carry[2 * n + 1]
        o_ref[0, c * tqc:(c + 1) * tqc, g * VD:(g + 1) * VD] = (acc[:, :VD] / acc[:, VD:]).astype(o_ref.dtype)


def _mla(qn, qr, kv, kr, *, tq, tk, s_valid, q0, hpg, nchunk):
    b, t, _ = qn.shape
    s_pad = kv.shape[1]
    assert s_pad % tk == 0 and t % tq == 0 and H_C % hpg == 0 and tq % nchunk == 0
    body = functools.partial(_mla_body, tq=tq, tk=tk, s_valid=s_valid, q0=q0, hpg=hpg, nchunk=nchunk)
    return pl.pallas_call(
        body,
        grid=(b, H_C // hpg, t // tq),
        in_specs=[pl.BlockSpec((1, tq, hpg * NOPE), lambda bi, h, i: (bi, i, h)),
                  pl.BlockSpec((1, tq, hpg * LANES), lambda bi, h, i: (bi, i, h)),
                  pl.BlockSpec((1, s_pad, hpg * (NOPE + VD)), lambda bi, h, i: (bi, 0, h)),
                  pl.BlockSpec((1, s_pad, LANES), lambda bi, h, i: (bi, 0, 0))],
        out_specs=pl.BlockSpec((1, tq, hpg * VD), lambda bi, h, i: (bi, i, h)),
        out_shape=jax.ShapeDtypeStruct((b, t, H_C * VD), BF16),
        compiler_params=_cparams(("parallel", "parallel", "arbitrary")),
        name="mla",
    )(qn, qr, kv, kr)


def _absorb_q_body(qn_ref, qr_ref, wk_ref, qa_ref, qrh_ref):
    qa_ref[0] = _dot_nt(qn_ref[...], wk_ref[...]).astype(qa_ref.dtype)
    qrh_ref[0] = qr_ref[...]


def _absorb_q(qn, qr, w_ukv):
    m = qn.shape[0]
    return pl.pallas_call(
        _absorb_q_body,
        grid=(H_C,),
        in_specs=[pl.BlockSpec((m, NOPE), lambda h: (0, h)),
                  pl.BlockSpec((m, LANES), lambda h: (0, h)),
                  pl.BlockSpec((KV_RANK, NOPE), lambda h: (0, 2 * h))],
        out_specs=[pl.BlockSpec((1, m, KV_RANK), lambda h: (h, 0, 0)),
                   pl.BlockSpec((1, m, LANES), lambda h: (h, 0, 0))],
        out_shape=[jax.ShapeDtypeStruct((H_C, m, KV_RANK), BF16),
                   jax.ShapeDtypeStruct((H_C, m, LANES), BF16)],
        compiler_params=_cparams(("parallel",)),
        name="mla_absorb_q",
    )(qn, qr, w_ukv)


def _mla_latent_body(qa_ref, qr_ref, c_ref, kr_ref, cp_ref, krp_ref, o_ref, *, t, tk, s_valid, q0, nsplit):
    hs = H_C // nsplit
    rows = hs * t
    lim_first = min((q0 // CHUNK + 1) * CHUNK, s_valid)
    lim_last = min(((q0 + t - 1) // CHUNK + 1) * CHUNK, s_valid)
    nkb = -(-lim_last // tk)
    n_full = lim_first // tk
    assert n_full == q0 // tk and nkb == n_full + 1
    q_pos = q0 + lax.broadcasted_iota(jnp.int32, (hs, t, tk), 1).reshape(rows, tk)
    lim = jnp.minimum(((q_pos >> CHUNK_SHIFT) + 1) * CHUNK, s_valid)
    col = lax.broadcasted_iota(jnp.int32, (rows, tk), 1)
    ones_k = jnp.ones((tk, LANES), BF16)
    qs = [jnp.concatenate([qa_ref[n * hs:(n + 1) * hs].reshape(rows, KV_RANK),
                           qr_ref[n * hs:(n + 1) * hs].reshape(rows, LANES)], axis=1) for n in range(nsplit)]

    def step(j, carry, masked):
        k0 = pl.multiple_of(j * tk, tk)
        if masked:
            cb, krb = c_ref[0], kr_ref[0]
        else:
            cb = cp_ref[0, pl.ds(k0, tk), :].astype(BF16)
            krb = jnp.concatenate([krp_ref[0, pl.ds(k0, tk), :].astype(BF16),
                                   jnp.zeros((tk, LANES - ROPE_C), BF16)], axis=1)
        kc = jnp.concatenate([cb, krb], axis=1)
        vc = jnp.concatenate([cb, ones_k], axis=1)
        ss = [_dot_nt(q, kc) for q in qs]
        ps, stats = [], []
        for n in range(nsplit):
            s = ss[n]
            if masked:
                s = jnp.where(k0 + col < lim, s, NEG)
            m_prev = carry[2 * n]
            m_new = jnp.maximum(m_prev, jnp.max(s, axis=1, keepdims=True))
            alpha = jnp.exp2(m_prev - m_new)
            ps.append(jnp.exp2(s - jnp.concatenate([m_new] * (tk // LANES), axis=1)).astype(BF16))
            stats.append((m_new, alpha))
        out = []
        for n in range(nsplit):
            m_new, alpha = stats[n]
            acc = (carry[2 * n + 1] * jnp.concatenate([alpha] * (KV_RANK // LANES + 1), axis=1)
                   + jnp.dot(ps[n], vc, preferred_element_type=F32))
            out += [m_new, acc]
        return tuple(out)

    carry = []
    for _ in range(nsplit):
        carry += [jnp.full((rows, LANES), NEG, F32), jnp.zeros((rows, KV_RANK + LANES), F32)]
    carry = tuple(carry)
    carry = lax.fori_loop(0, n_full, lambda j, c: step(j, c, False), carry)
    carry = lax.fori_loop(n_full, nkb, lambda j, c: step(j, c, True), carry)
    for n in range(nsplit):
        acc = carry[2 * n + 1]
        lat = acc[:, :KV_RANK] / jnp.concatenate([acc[:, KV_RANK:]] * (KV_RANK // LANES), axis=1)
        o_ref[0, n * hs:(n + 1) * hs] = lat.reshape(hs, t, KV_RANK).astype(o_ref.dtype)


def _mla_latent(qa_hm, qr_hm, c_new, kr_new, c_past, kr_past, *, tk, s_valid, q0):
    b, t, _ = c_new.shape
    assert q0 % tk == 0 and t <= tk and c_past.shape[1] == q0
    c_new, kr_new = (jnp.pad(a, ((0, 0), (0, tk - t), (0, 0))) for a in (c_new, kr_new))
    body = functools.partial(_mla_latent_body, t=t, tk=tk, s_valid=s_valid, q0=q0, nsplit=2)
    return pl.pallas_call(
        body,
        grid=(b,),
        in_specs=[pl.BlockSpec((H_C, t, KV_RANK), lambda bi: (0, bi, 0)),
                  pl.BlockSpec((H_C, t, LANES), lambda bi: (0, bi, 0)),
                  pl.BlockSpec((1, tk, KV_RANK), lambda bi: (bi, 0, 0)),
                  pl.BlockSpec((1, tk, LANES), lambda bi: (bi, 0, 0)),
                  pl.BlockSpec((1, q0, KV_RANK), lambda bi: (bi, 0, 0)),
                  pl.BlockSpec((1, q0, ROPE_C), lambda bi: (bi, 0, 0))],
        out_specs=pl.BlockSpec((1, H_C, t, KV_RANK), lambda bi: (bi, 0, 0, 0)),
        out_shape=jax.ShapeDtypeStruct((b, H_C, t, KV_RANK), BF16),
        compiler_params=_cparams(("parallel",)),
        name="mla_latent",
    )(qa_hm, qr_hm, c_new, kr_new, c_past, kr_past)


def _latent_out_body(x_ref, wv_ref, o_ref):
    nb, _, t, r = x_ref.shape
    o_ref[...] = jnp.dot(x_ref[...].reshape(nb * t, r), wv_ref[...],
                         preferred_element_type=F32).astype(o_ref.dtype)


def _latent_out(lat, w_ukv):
    b, _, t, _ = lat.shape
    return pl.pallas_call(
        _latent_out_body,
        grid=(H_C,),
        in_specs=[pl.BlockSpec((b, 1, t, KV_RANK), lambda h: (0, h, 0, 0)),
                  pl.BlockSpec((KV_RANK, VD), lambda h: (0, 2 * h + 1))],
        out_specs=pl.BlockSpec((b * t, VD), lambda h: (0, h)),
        out_shape=jax.ShapeDtypeStruct((b * t, H_C * VD), BF16),
        compiler_params=_cparams(("parallel",)),
        name="mla_latent_out",
    )(lat, w_ukv)


def _ffn_body(x_ref, g_ref, wg_ref, wv_ref, cwg_ref, cwv_ref, cbg_ref, cbv_ref, wd_ref, pg_ref, pv_ref, gn_ref,
              o_ref, hn_ref, sg_ref, sv_ref, h_scr, acc_scr, ug_buf, uv_buf, cg_scr, cv_scr, *, seg, carried):
    i = pl.program_id(0)
    f = pl.program_id(1)
    tm = x_ref.shape[0]

    @pl.when(f == 0)
    def _():
        x = x_ref[...]
        y = x * lax.rsqrt(jnp.mean(x * x, axis=-1, keepdims=True) + EPS)
        h_scr[...] = (y * g_ref[...]).astype(BF16)
        acc_scr[...] = x

    h = h_scr[...]
    halves = ((wg_ref, cwg_ref, cbg_ref, pg_ref, sg_ref, ug_buf, cg_scr),
              (wv_ref, cwv_ref, cbv_ref, pv_ref, sv_ref, uv_buf, cv_scr))
    ys = []
    for w_ref, cw_ref, cb_ref, p_ref, s_ref, buf, c_scr in halves:
        u = jnp.dot(h, w_ref[...], preferred_element_type=F32)
        cw = cw_ref[...]
        parts = []
        for sgi in range(tm // seg):
            us = u[sgi * seg:(sgi + 1) * seg]
            if carried:
                head = jnp.where(i == 0, p_ref[0], c_scr[f])
            else:
                head = p_ref[sgi]
            buf[sgi, 0:SUBLANES] = head
            buf[sgi, SUBLANES:SUBLANES + seg] = us
            y = (cw[2:3] * us + cw[1:2] * buf[sgi, SUBLANES - 1:SUBLANES - 1 + seg]
                 + cw[0:1] * buf[sgi, SUBLANES - 2:SUBLANES - 2 + seg] + cb_ref[...])
            parts.append(y)
            last = us[seg - SUBLANES:seg]
            s_ref[sgi] = last
            if carried:
                c_scr[f] = last
        ys.append(parts[0] if len(parts) == 1 else jnp.concatenate(parts, axis=0))
    yg, yv = ys
    act = (yg / (1.0 + jnp.exp(-yg))) * yv
    acc_scr[...] += jnp.dot(act.astype(BF16), wd_ref[...], preferred_element_type=F32)

    @pl.when(f == pl.num_programs(1) - 1)
    def _():
        a = acc_scr[...]
        o_ref[...] = a
        y = a * lax.rsqrt(jnp.mean(a * a, axis=-1, keepdims=True) + EPS)
        hn_ref[...] = (y * gn_ref[...]).astype(hn_ref.dtype)


def _ffn(x, g, w_up, conv_w, conv_b, w_down, prev8, g_next, next_dtype, *, seg, carried, tm, tf):
    m, d = x.shape
    nf = D_FF // tf
    nseg = tm // seg
    assert m % tm == 0 and D_FF % tf == 0 and tm % seg == 0
    sidx = (lambda i: 0) if carried else (lambda i: i)
    body = functools.partial(_ffn_body, seg=seg, carried=carried)
    cb = conv_b.reshape(1, 2 * D_FF)
    return pl.pallas_call(
        body,
        grid=(m // tm, nf),
        in_specs=[pl.BlockSpec((tm, d), lambda i, f: (i, 0)),
                  pl.BlockSpec((1, d), lambda i, f: (0, 0)),
                  pl.BlockSpec((d, tf), lambda i, f: (0, f)),
                  pl.BlockSpec((d, tf), lambda i, f: (0, nf + f)),
                  pl.BlockSpec((CONV_W, tf), lambda i, f: (0, f)),
                  pl.BlockSpec((CONV_W, tf), lambda i, f: (0, nf + f)),
                  pl.BlockSpec((1, tf), lambda i, f: (0, f)),
                  pl.BlockSpec((1, tf), lambda i, f: (0, nf + f)),
                  pl.BlockSpec((tf, d), lambda i, f: (f, 0)),
                  pl.BlockSpec((nseg, SUBLANES, tf), lambda i, f: (sidx(i), 0, f)),
                  pl.BlockSpec((nseg, SUBLANES, tf), lambda i, f: (sidx(i), 0, nf + f)),
                  pl.BlockSpec((1, d), lambda i, f: (0, 0))],
        out_specs=[pl.BlockSpec((tm, d), lambda i, f: (i, 0)),
                   pl.BlockSpec((tm, d), lambda i, f: (i, 0)),
                   pl.BlockSpec((nseg, SUBLANES, tf), lambda i, f: (i, 0, f)),
                   pl.BlockSpec((nseg, SUBLANES, tf), lambda i, f: (i, 0, f))],
        out_shape=[jax.ShapeDtypeStruct((m, d), F32),
                   jax.ShapeDtypeStruct((m, d), next_dtype),
                   jax.ShapeDtypeStruct((m // seg, SUBLANES, D_FF), F32),
                   jax.ShapeDtypeStruct((m // seg, SUBLANES, D_FF), F32)],
        scratch_shapes=[pltpu.VMEM((tm, d), BF16),
                        pltpu.VMEM((tm, d), F32),
                        pltpu.VMEM((nseg, SUBLANES + seg, tf), F32),
                        pltpu.VMEM((nseg, SUBLANES + seg, tf), F32),
                        pltpu.VMEM((nf, SUBLANES, tf), F32),
                        pltpu.VMEM((nf, SUBLANES, tf), F32)],
        compiler_params=_cparams(("arbitrary", "arbitrary")),
        name="conv_ffn",
    )(x, g.reshape(1, d), w_up, w_up, conv_w, conv_w, cb, cb, w_down, prev8, prev8, g_next.reshape(1, d))


def _rope_tables(pos, head_dim, rot, lanes_valid=LANES):
    half = rot // 2
    freqs = ROPE_THETA ** (-jnp.arange(half, dtype=F32) / half)
    ang = pos.astype(F32)[:, None] * freqs[None, :]
    cos, sin = jnp.cos(ang), jnp.sin(ang)
    lane = np.arange(LANES)
    d = lane % head_dim
    first = (d < half) & (lane < lanes_valid)
    second = (d >= half) & (d < rot) & (lane < lanes_valid)
    fidx = np.clip(np.where(d < half, d, d - half), 0, half - 1)
    cosl, sinl = cos[:, fidx], sin[:, fidx]
    c = jnp.where(first | second, cosl, 1.0)
    s1 = jnp.where(second, sinl, 0.0)
    s2 = jnp.where(first, -sinl, 0.0)
    return c, s1, s2, half


def _prep_weights(w_in_ab, w_out_ab, w_dqkv, w_uq, w_ukv, w_o_mla, w_up, w_down):
    n_even, n_odd = w_in_ab.shape[0], w_dqkv.shape[0]
    w = {}
    n_a = H_A * DH + 2 * KV_A * DH + H_I * D_IDX
    w["in_a"] = w_in_ab[:, :, :n_a].astype(BF16)
    pad = jnp.zeros((n_even, D_MODEL, LANES - D_IDX - H_I), BF16)
    w["in_b"] = jnp.concatenate([w_in_ab[:, :, n_a:n_a + D_IDX + H_I].astype(BF16), pad,
                                 w_in_ab[:, :, n_a + D_IDX + H_I:].astype(BF16)], axis=-1)
    w["out_ab"] = w_out_ab.astype(BF16)
    w["dqkr"] = jnp.concatenate([w_dqkv.astype(BF16),
                                 jnp.zeros((n_odd, D_MODEL, LANES - ROPE_C), BF16)], axis=-1)
    uq = w_uq.reshape(n_odd, Q_RANK, H_C, NOPE + ROPE_C)
    uqn = uq[..., :NOPE].reshape(n_odd, Q_RANK, H_C * NOPE).astype(BF16)
    uqr = uq[..., NOPE:].astype(BF16)
    uqr = jnp.concatenate([uqr, jnp.zeros_like(uqr)], axis=-1).reshape(n_odd, Q_RANK, H_C * LANES)
    w["uq"] = jnp.concatenate([uqn, uqr], axis=-1)
    w["ukv"] = w_ukv.astype(BF16)
    w["o_mla"] = w_o_mla.astype(BF16)
    w["up"] = w_up.astype(BF16)
    w["down"] = w_down.astype(BF16)
    return w


def _with_past(past, new, s_pad):
    b, _, fdim = new.shape
    parts = [new.astype(BF16)] if past is None else [past.astype(BF16), new.astype(BF16)]
    n = sum(p.shape[1] for p in parts)
    if s_pad > n:
        parts.append(jnp.zeros((b, s_pad - n, fdim), BF16))
    return parts[0] if len(parts) == 1 else jnp.concatenate(parts, axis=1)


def _trunk(x, past, conv_state, w, p, cfg):
    b, t, d = x.shape
    m = b * t
    q0 = 0 if past is None else past[0].shape[2]
    s_valid = q0 + t
    tk = cfg["tk"]
    s_pad = -(-s_valid // tk) * tk
    pos = q0 + jnp.tile(jnp.arange(t), b)
    tab_a = _rope_tables(pos, DH, ROT_A)
    tab_i = _rope_tables(pos, D_IDX, ROT_I)
    tab_ki = _rope_tables(pos, D_IDX, ROT_I, lanes_valid=D_IDX)
    tab_kr = _rope_tables(pos, ROPE_C, ROPE_C, lanes_valid=ROPE_C)

    xf = x.reshape(m, d)
    ab_rows, mla_rows, conv_rows = [], [], []
    sc_ab = DH ** -0.5 * LOG2E
    sc_c = (NOPE + ROPE_C) ** -0.5 * LOG2E
    h = _rmsnorm(xf, p["norm_mix"][0], BF16)
    for l in range(DEPTH):
        if l % 2 == 0:
            e = l // 2
            lp = None if past is None else tuple(c[e] for c in past[:5])
            qa, ka32, ka16, va32, va16, qi = _proj(
                [h], w["in_a"][e],
                [dict(width=H_A * DH, dtypes=(BF16,), rope=0, scale=sc_ab),
                 dict(width=KV_A * DH, dtypes=(F32, BF16), rope=0, split_heads=True),
                 dict(width=KV_A * DH, dtypes=(F32, BF16), split_heads=True),
                 dict(width=H_I * D_IDX, dtypes=(BF16,), rope=1)],
                tabs=(tab_a, tab_i), name="proj_in_a")
            kiwi, qb, kb32, kb16, vb32, vb16 = _proj(
                [h], w["in_b"][e],
                [dict(width=LANES, dtypes=(F32,), rope=0),
                 dict(width=H_B * DH, dtypes=(BF16,), scale=sc_ab),
                 dict(width=H_B * DH, dtypes=(F32, BF16), split_heads=True),
                 dict(width=H_B * DH, dtypes=(F32, BF16), split_heads=True)],
                tabs=(tab_ki,), name="proj_in_b")
            ki32 = kiwi[:, :D_IDX]
            wi = kiwi[:, D_IDX:D_IDX + H_I]
            ab_rows.append((ka32.reshape(b, t, KV_A, DH), va32.reshape(b, t, KV_A, DH),
                            ki32.reshape(b, t, D_IDX), kb32.reshape(b, t, H_B, DH),
                            vb32.reshape(b, t, H_B, DH)))

            def full(idx, new16):
                pst = None if lp is None else lp[idx].reshape(b, q0, -1)
                return _with_past(pst, new16.reshape(b, t, -1), s_pad)

            ki_all = full(2, ki32)
            zk = jnp.zeros_like(ki_all)
            ki2 = jnp.concatenate([ki_all, zk, zk, ki_all], axis=-1)
            oa = _dsa(qa.reshape(b, t, -1), qi.reshape(b, t, -1), wi.reshape(b, t, H_I), full(0, ka16),
                      full(1, va16), ki2, tq=cfg["tq_a"], tk=tk, s_valid=s_valid, q0=q0)
            if lp is None:
                ob = _stick_breaking(qb.reshape(b, t, -1), full(3, kb16), full(4, vb16), tq=cfg["tq_b"],
                                     tk=cfg["tk_b"], q0=q0, hpg=cfg["hpg_b"], nchunk=cfg["nc_b"])
            else:
                ob = _stick_breaking(qb.reshape(b, t, -1), kb16.reshape(b, t, -1), vb16.reshape(b, t, -1),
                                     tq=cfg["tq_b"], tk=cfg["tk_b"], q0=q0, hpg=cfg["hpg_b"],
                                     nchunk=cfg["nc_b"], past=(lp[3], lp[4]))
            (xf,) = _proj([oa.reshape(m, -1), ob.reshape(m, -1)], w["out_ab"][e],
                          [dict(width=D_MODEL, dtypes=(F32,))], resid=xf, name="proj_out_ab")
        else:
            od = l // 2
            gains = jnp.concatenate([p["g_q"][od], p["g_kv"][od]])
            cq16, ckv32, ckv16, kr32w, kr16w = _proj(
                [h], w["dqkr"][od],
                [dict(width=Q_RANK, dtypes=(BF16,), gain_off=0),
                 dict(width=KV_RANK, dtypes=(F32, BF16), gain_off=Q_RANK),
                 dict(width=LANES, dtypes=(F32, BF16), rope=0)],
                tabs=(tab_kr,), gain=gains, name="proj_dqkr")
            mla_rows.append((ckv32.reshape(b, t, KV_RANK), kr32w[:, :ROPE_C].reshape(b, t, ROPE_C)))
            qn, qr = _proj([cq16], w["uq"][od],
                           [dict(width=H_C * NOPE, dtypes=(BF16,), scale=sc_c),
                            dict(width=H_C * LANES, dtypes=(BF16,), rope=0, scale=sc_c)],
                           tabs=(tab_kr,), name="proj_uq")
            if cfg["latent_c"]:
                qa_hm, qr_hm = _absorb_q(qn, qr, w["ukv"][od])
                lat = _mla_latent(qa_hm, qr_hm, ckv16.reshape(b, t, KV_RANK), kr16w.reshape(b, t, LANES),
                                  past[5][od], past[6][od], tk=cfg["tk_c"], s_valid=s_valid, q0=q0)
                o = _latent_out(lat, w["ukv"][od])
            else:
                pc = None if past is None else past[5][od]
                pk = None if past is None else jnp.pad(past[6][od], ((0, 0), (0, 0), (0, LANES - ROPE_C)))
                c_all = _with_past(pc, ckv16.reshape(b, t, KV_RANK), s_pad)
                kr_all = _with_past(pk, kr16w.reshape(b, t, LANES), s_pad)
                (kv,) = _proj([c_all.reshape(b * s_pad, KV_RANK)], w["ukv"][od],
                              [dict(width=H_C * (NOPE + VD), dtypes=(BF16,))], name="proj_ukv")
                o = _mla(qn.reshape(b, t, -1), qr.reshape(b, t, -1), kv.reshape(b, s_pad, -1), kr_all,
                         tq=cfg["tq_c"], tk=cfg["tk_c"], s_valid=s_valid, q0=q0, hpg=cfg["hpg_c"],
                         nchunk=cfg["nc_c"])
            (xf,) = _proj([o.reshape(m, -1)], w["o_mla"][od], [dict(width=D_MODEL, dtypes=(F32,))],
                          resid=xf, name="proj_o_mla")

        if conv_state is None:
            prev8 = jnp.zeros((b, SUBLANES, 2 * D_FF), F32)
        else:
            prev8 = jnp.concatenate([jnp.zeros((b, SUBLANES - (CONV_W - 1), 2 * D_FF), F32),
                                     conv_state[l]], axis=1)
        final = l == DEPTH - 1
        g_next = p["norm_final"] if final else p["norm_mix"][l + 1]
        xf, h, sg, sv = _ffn(xf, p["norm_ffn"][l], w["up"][l], p["conv_w"][l], p["conv_b"][l], w["down"][l],
                             prev8, g_next, F32 if final else BF16, seg=cfg["seg"], carried=cfg["carried"],
                             tm=cfg["tm_ffn"], tf=cfg["tf"])
        last = [s8.reshape(b, -1, SUBLANES, D_FF)[:, -1, SUBLANES - (CONV_W - 1):] for s8 in (sg, sv)]
        conv_rows.append(jnp.concatenate(last, axis=-1))

    y = h.reshape(b, t, d)
    new_ab = [jnp.stack([r[i] for r in ab_rows]) for i in range(5)]
    new_mla = [jnp.stack([r[i] for r in mla_rows]) for i in range(2)]
    return y, new_ab + new_mla + [jnp.stack(conv_rows)]


def kernel(x_prompt, x_sample, cache_k_a, cache_v_a, cache_idx_k, cache_k_b, cache_v_b, cache_ckv, cache_krope, state_conv, norm_mix, norm_ffn, norm_final, w_in_ab, w_out_ab, w_dqkv, g_q, g_kv, w_uq, w_ukv, w_o_mla, w_up, conv_w, conv_b, w_down):
    w = _prep_weights(w_in_ab, w_out_ab, w_dqkv, w_uq, w_ukv, w_o_mla, w_up, w_down)
    p = dict(norm_mix=norm_mix, norm_ffn=norm_ffn, norm_final=norm_final, g_q=g_q, g_kv=g_kv,
             conv_w=conv_w, conv_b=conv_b)
    cfg_p = dict(tk=512, tq_a=256, tq_b=256, tk_b=256, hpg_b=4, nc_b=1, tq_c=512, tk_c=1024, hpg_c=2, nc_c=2,
                 latent_c=False, seg=512, carried=True, tm_ffn=512, tf=512)
    t_s = x_sample.shape[1]
    cfg_s = dict(tk=256, tq_a=t_s, tq_b=t_s, tk_b=256, hpg_b=H_B, nc_b=1, tq_c=t_s, tk_c=256, hpg_c=8, nc_c=1,
                 latent_c=True, seg=t_s, carried=False, tm_ffn=x_sample.shape[0] * t_s, tf=512)
    y_p, st_p = _trunk(x_prompt, None, None, w, p, cfg_p)
    past = (cache_k_a, cache_v_a, cache_idx_k, cache_k_b, cache_v_b, cache_ckv, cache_krope)
    y_s, st_s = _trunk(x_sample, past, state_conv, w, p, cfg_s)
    return (y_p, y_s, *st_p, *st_s)
```

```python
import functools

import numpy as np
import jax
import jax.numpy as jnp
from jax import lax
from jax.experimental import pallas as pl
from jax.experimental.pallas import tpu as pltpu

F32 = jnp.float32
BF16 = jnp.bfloat16

D_MODEL = 2048
DEPTH = 4
CHUNK = 64
CHUNK_SHIFT = CHUNK.bit_length() - 1
ROPE_THETA = 500000.0
EPS = 1e-6
H_A, KV_A, DH = 8, 2, 128
ROT_A = DH // 4
H_I, D_IDX = 16, 64
ROT_I = D_IDX // 4
TOPK_MAX = 256
H_B = 8
H_C, Q_RANK, KV_RANK, NOPE, ROPE_C, VD = 16, 512, 512, 128, 64, 128
D_FF = 5632
CONV_W = 3

LANES = 128
SUBLANES = 8
VMEM_LIMIT = 56 * 1024 * 1024
NEG = -1e30
INT_MIN = np.int32(-2 ** 31)
INT_MAX = np.int32(2 ** 31 - 1)
LOG2E = float(np.log2(np.e))


def _cparams(sem):
    return pltpu.CompilerParams(dimension_semantics=sem, vmem_limit_bytes=VMEM_LIMIT)


def _dot_nt(a, b):
    return lax.dot_general(a, b, (((1,), (1,)), ((), ())), preferred_element_type=F32)


def _rms_body(x_ref, g_ref, o_ref):
    x = x_ref[...]
    y = x * lax.rsqrt(jnp.mean(x * x, axis=-1, keepdims=True) + EPS)
    o_ref[...] = (y * g_ref[...]).astype(o_ref.dtype)


def _rmsnorm(x, g, out_dtype):
    m, d = x.shape
    tm = min(m, 512)
    return pl.pallas_call(
        _rms_body,
        grid=(m // tm,),
        in_specs=[pl.BlockSpec((tm, d), lambda i: (i, 0)),
                  pl.BlockSpec((1, d), lambda i: (0, 0))],
        out_specs=pl.BlockSpec((tm, d), lambda i: (i, 0)),
        out_shape=jax.ShapeDtypeStruct((m, d), out_dtype),
        compiler_params=_cparams(("parallel",)),
        name="rmsnorm",
    )(x, g.reshape(1, d))


PROJ_CHUNK = 512
COUNT_ROWS = 64


def _proj_body(*refs, nx, groups, tab_halves, has_gain, has_resid):
    it = iter(refs)
    x_refs = [next(it) for _ in range(nx)]
    w_ref = next(it)
    tabs = [(next(it), next(it), next(it)) for _ in tab_halves]
    g_ref = next(it) if has_gain else None
    r_ref = next(it) if has_resid else None
    outs = list(it)

    xs = [r[...] for r in x_refs]
    x = xs[0] if nx == 1 else jnp.concatenate(xs, axis=1)
    accs = []
    c0 = 0
    for grp in groups:
        width = grp["width"]
        chunk = min(width, PROJ_CHUNK)
        for cc in range(0, width, chunk):
            accs.append(jnp.dot(x, w_ref[:, c0 + cc:c0 + cc + chunk], preferred_element_type=F32))
        c0 += width
    accs = iter(accs)
    oi = 0
    for grp in groups:
        width = grp["width"]
        chunk = min(width, PROJ_CHUNK)
        for cc in range(0, width, chunk):
            acc = next(accs)
            if grp.get("gain_off") is not None:
                assert chunk == width
                go = grp["gain_off"]
                acc = (acc * lax.rsqrt(jnp.mean(acc * acc, axis=-1, keepdims=True) + EPS)
                       * g_ref[:, go:go + width])
            if grp.get("rope") is not None:
                c_ref, s1_ref, s2_ref = tabs[grp["rope"]]
                half = tab_halves[grp["rope"]]
                c, s1, s2 = c_ref[...], s1_ref[...], s2_ref[...]
                parts = []
                for gi in range(chunk // LANES):
                    xg = acc[:, gi * LANES:(gi + 1) * LANES]
                    parts.append(xg * c + pltpu.roll(xg, half, 1) * s1
                                 + pltpu.roll(xg, LANES - half, 1) * s2)
                acc = parts[0] if len(parts) == 1 else jnp.concatenate(parts, axis=1)
            if grp.get("scale") is not None:
                acc = acc * grp["scale"]
            if has_resid:
                acc = acc + r_ref[:, cc:cc + chunk]
            for k in range(len(grp["dtypes"])):
                o = outs[oi + k]
                if len(o.shape) == 3:
                    for gi in range(chunk // LANES):
                        o[:, cc // LANES + gi, :] = acc[:, gi * LANES:(gi + 1) * LANES].astype(o.dtype)
                else:
                    o[:, cc:cc + chunk] = acc.astype(o.dtype)
        oi += len(grp["dtypes"])


def _proj(xs, w, groups, *, tabs=(), gain=None, resid=None, name="proj"):
    m = xs[0].shape[0]
    k, n = w.shape
    tm = min(m, 512)
    assert m % tm == 0 and n == sum(g["width"] for g in groups) and k == sum(x.shape[1] for x in xs)
    assert resid is None or len(groups) == 1
    in_specs = [pl.BlockSpec((tm, x.shape[1]), lambda i: (i, 0)) for x in xs]
    in_specs.append(pl.BlockSpec((k, n), lambda i: (0, 0)))
    args = list(xs) + [w]
    for c, s1, s2, _ in tabs:
        for t in (c, s1, s2):
            in_specs.append(pl.BlockSpec((tm, LANES), lambda i: (i, 0)))
            args.append(t)
    if gain is not None:
        in_specs.append(pl.BlockSpec((1, gain.shape[0]), lambda i: (0, 0)))
        args.append(gain.reshape(1, -1))
    if resid is not None:
        in_specs.append(pl.BlockSpec((tm, n), lambda i: (i, 0)))
        args.append(resid)
    out_specs, out_shape = [], []
    for g in groups:
        for dt in g["dtypes"]:
            if g.get("split_heads") and dt == F32:
                nh = g["width"] // LANES
                out_specs.append(pl.BlockSpec((tm, nh, LANES), lambda i: (i, 0, 0)))
                out_shape.append(jax.ShapeDtypeStruct((m, nh, LANES), dt))
            else:
                out_specs.append(pl.BlockSpec((tm, g["width"]), lambda i: (i, 0)))
                out_shape.append(jax.ShapeDtypeStruct((m, g["width"]), dt))
    body = functools.partial(_proj_body, nx=len(xs), groups=groups, tab_halves=[t[3] for t in tabs],
                             has_gain=gain is not None, has_resid=resid is not None)
    return pl.pallas_call(
        body,
        grid=(m // tm,),
        in_specs=in_specs,
        out_specs=out_specs,
        out_shape=out_shape,
        compiler_params=_cparams(("parallel",)),
        name=name,
    )(*args)


def _dsa_body(qa_ref, qi_ref, wi_ref, ka_ref, va_ref, ki_ref, tri_ref, o_ref,
              key_scr, qs_scr, m_scr, acc_scr, *, tq, tk, s_valid, q0, topk):
    i = pl.program_id(1)
    q_first = q0 + i * tq
    q_last = q_first + tq - 1
    adm_end = jnp.minimum((q_last // CHUNK + 1) * CHUNK, s_valid)
    nkb = (adm_end + tk - 1) // tk
    q_pos = q_first + lax.broadcasted_iota(jnp.int32, (tq, tk), 0)
    q_lim = jnp.minimum(((q_pos >> CHUNK_SHIFT) + 1) * CHUNK, s_valid)
    col = lax.broadcasted_iota(jnp.int32, (tq, tk), 1)
    rep = H_A // KV_A

    w = wi_ref[0] * ((H_I * D_IDX) ** -0.5)
    pg = 4
    pairs = LANES // D_IDX
    qi = qi_ref[0]
    q_groups = [jnp.concatenate([qi[:, (g * pg + pp) * LANES:(g * pg + pp + 1) * LANES]
                                 for pp in range(pg)], axis=0) for g in range(H_I // (pairs * pg))]

    def to_key(x):
        bits = pltpu.bitcast(x, jnp.int32)
        return bits ^ ((bits >> 31) & INT_MAX)

    transposed = key_scr.shape[1] == tq

    if transposed:
        wt = w
        krow = lax.broadcasted_iota(jnp.int32, (tk, tq), 0)
        q_lim_t = jnp.minimum((((q_first + lax.broadcasted_iota(jnp.int32, (tk, tq), 1)) >> CHUNK_SHIFT) + 1) * CHUNK,
                              s_valid)

    def p1(j, c):
        k0 = pl.multiple_of(j * tk, tk)
        if transposed:
            sc = jnp.zeros((tk, tq), F32)
            for g, qg in enumerate(q_groups):
                for half in range(pairs):
                    rel = _dot_nt(ki_ref[0, pl.ds(k0, tk), half * LANES:(half + 1) * LANES], qg)
                    for pp in range(pg):
                        h = (g * pg + pp) * pairs + half
                        sc = sc + jnp.maximum(rel[:, pp * tq:(pp + 1) * tq], 0.0) * wt[h:h + 1, :]
            key_scr[pl.ds(k0, tk), :] = jnp.where(k0 + krow < q_lim_t, to_key(sc), INT_MIN)
            return c
        sc = jnp.zeros((tq, tk), F32)
        for g, qg in enumerate(q_groups):
            for half in range(pairs):
                rel = _dot_nt(qg, ki_ref[0, pl.ds(k0, tk), half * LANES:(half + 1) * LANES])
                for pp in range(pg):
                    h = (g * pg + pp) * pairs + half
                    sc = sc + jnp.maximum(rel[pp * tq:(pp + 1) * tq], 0.0) * w[:, h:h + 1]
        key_scr[:, pl.ds(k0, tk)] = jnp.where(k0 + col < q_lim, to_key(sc), INT_MIN)
        return c

    lax.fori_loop(0, nkb, p1, 0)

    def count_ge(v):
        def cb(j, c):
            k0 = pl.multiple_of(j * tk, tk)
            if transposed:
                ge = jnp.where(key_scr[pl.ds(k0, tk), :] >= v, 1.0, 0.0)
                return c + jnp.sum(ge.reshape(tk // COUNT_ROWS, COUNT_ROWS, tq), axis=0)
            ge = jnp.where(key_scr[:, pl.ds(k0, tk)] >= v, 1.0, 0.0)
            part = ge[:, 0:LANES]
            for cc in range(1, tk // LANES):
                part = part + ge[:, cc * LANES:(cc + 1) * LANES]
            return c + part
        if transposed:
            c = lax.fori_loop(0, nkb, cb, jnp.zeros((COUNT_ROWS, tq), F32))
            return jnp.sum(c, axis=0, keepdims=True)
        c = lax.fori_loop(0, nkb, cb, jnp.zeros((tq, LANES), F32))
        return jnp.sum(c, axis=1, keepdims=True)

    def bis(_, lohi):
        lo, hi = lohi
        mid = (lo >> 1) + (hi >> 1) + (lo & hi & 1)
        ok = count_ge(mid) >= float(topk)
        return jnp.where(ok, mid, lo), jnp.where(ok, hi, mid)

    vshape = (1, tq) if transposed else (tq, 1)
    thr, _ = lax.fori_loop(0, 32, bis, (jnp.full(vshape, INT_MIN, jnp.int32),
                                        jnp.full(vshape, INT_MAX, jnp.int32)))
    need = float(topk) - count_ge(thr + 1)
    if transposed:
        thr = pltpu.bitcast(jnp.transpose(pltpu.bitcast(jnp.broadcast_to(thr, (LANES, tq)), F32)), jnp.int32)
        need = jnp.transpose(jnp.broadcast_to(need, (LANES, tq)))
        thr = jnp.concatenate([thr] * (tk // LANES), axis=1)
        need = jnp.concatenate([need] * (tk // LANES), axis=1)

    qa = qa_ref[0]
    for g in range(KV_A):
        qs_scr[g] = jnp.concatenate(
            [qa[:, (g * rep + r) * DH:(g * rep + r + 1) * DH] for r in range(rep)], axis=0)
    m_scr[...] = jnp.full(m_scr.shape, NEG, F32)
    ones_k = jnp.ones((tk, LANES), BF16)
    acc_scr[...] = jnp.zeros(acc_scr.shape, F32)

    def p3(j, eqc):
        k0 = pl.multiple_of(j * tk, tk)
        if transposed:
            key = pltpu.bitcast(jnp.transpose(pltpu.bitcast(key_scr[pl.ds(k0, tk), :], F32)), jnp.int32)
        else:
            key = key_scr[:, pl.ds(k0, tk)]
        eq = key == thr
        eqf = jnp.where(eq, 1.0, 0.0)
        prefix = jnp.dot(eqf.astype(BF16), tri_ref[...], preferred_element_type=F32) + eqc
        bias = jnp.where(key > thr, 0.0, jnp.where(eq, jnp.where(prefix < need, 0.0, NEG), NEG))
        bias = jnp.where(key == INT_MIN, NEG, bias)
        eqc = eqc + jnp.sum(eqf, axis=1, keepdims=True)
        ss = [_dot_nt(qs_scr[g], ka_ref[0, pl.ds(k0, tk), g * DH:(g + 1) * DH])
              for g in range(KV_A)]
        for g in range(KV_A):
            vg = va_ref[0, pl.ds(k0, tk), g * DH:(g + 1) * DH]
            s = ss[g]
            ps, alphas = [], []
            for r in range(rep):
                h = g * rep + r
                sr = s[r * tq:(r + 1) * tq] + bias
                m_prev = m_scr[h]
                m_new = jnp.maximum(m_prev, jnp.max(sr, axis=1, keepdims=True))
                alpha = jnp.exp2(m_prev - m_new)
                p = jnp.exp2(sr - jnp.concatenate([m_new] * (tk // LANES), axis=1))
                m_scr[h] = m_new
                ps.append(p.astype(BF16))
                alphas.append(jnp.concatenate([alpha, alpha], axis=1))
            vg1 = jnp.concatenate([vg, ones_k], axis=1)
            for r in range(rep):
                h = g * rep + r
                acc_scr[h] = acc_scr[h] * alphas[r] + jnp.dot(ps[r], vg1, preferred_element_type=F32)
        return eqc

    lax.fori_loop(0, nkb, p3, jnp.zeros((tq, 1), F32))
    for h in range(H_A):
        acc = acc_scr[h]
        o_ref[0, :, h * DH:(h + 1) * DH] = (acc[:, :DH] / acc[:, DH:]).astype(o_ref.dtype)


def _dsa(qa, qi, wi, ka, va, ki, *, tq, tk, s_valid, q0):
    b, t, _ = qa.shape
    s_pad = ka.shape[1]
    assert s_pad % tk == 0 and t % tq == 0 and tk >= TOPK_MAX
    topk = min(TOPK_MAX, s_valid // 4)
    tri = jnp.asarray(np.triu(np.ones((tk, tk), np.float32), 1), BF16)
    body = functools.partial(_dsa_body, tq=tq, tk=tk, s_valid=s_valid, q0=q0, topk=topk)
    rep = H_A // KV_A
    transposed = tq % LANES == 0
    if transposed:
        wi = jnp.transpose(wi, (0, 2, 1))
        wi_spec = pl.BlockSpec((1, H_I, tq), lambda bi, i: (bi, 0, i))
    else:
        wi_spec = pl.BlockSpec((1, tq, H_I), lambda bi, i: (bi, i, 0))
    return pl.pallas_call(
        body,
        grid=(b, t // tq),
        in_specs=[pl.BlockSpec((1, tq, H_A * DH), lambda bi, i: (bi, i, 0)),
                  pl.BlockSpec((1, tq, H_I * D_IDX), lambda bi, i: (bi, i, 0)),
                  wi_spec,
                  pl.BlockSpec((1, s_pad, KV_A * DH), lambda bi, i: (bi, 0, 0)),
                  pl.BlockSpec((1, s_pad, KV_A * DH), lambda bi, i: (bi, 0, 0)),
                  pl.BlockSpec((1, s_pad, 2 * LANES), lambda bi, i: (bi, 0, 0)),
                  pl.BlockSpec((tk, tk), lambda bi, i: (0, 0))],
        out_specs=pl.BlockSpec((1, tq, H_A * DH), lambda bi, i: (bi, i, 0)),
        out_shape=jax.ShapeDtypeStruct((b, t, H_A * DH), BF16),
        scratch_shapes=[pltpu.VMEM((s_pad, tq) if transposed else (tq, s_pad), jnp.int32),
                        pltpu.VMEM((KV_A, rep * tq, DH), BF16),
                        pltpu.VMEM((H_A, tq, LANES), F32),
                        pltpu.VMEM((H_A, tq, DH + LANES), F32)],
        compiler_params=_cparams(("parallel", "arbitrary")),
        name="dsa",
    )(qa, qi, wi, ka, va, ki, tri)


def _sb_body(q_ref, k_ref, v_ref, low_ref, *rest, tq, tk, s_pad, q0, hpg, nchunk, past_heads):
    if past_heads:
        kp_ref, vp_ref, o_ref = rest
    else:
        (o_ref,) = rest
    i = pl.program_id(2)
    tqc = tq // nchunk
    q_first = q0 + i * tq
    q_last = q_first + tq - 1
    nkb = jnp.minimum((q_last + tk - 1) // tk, s_pad // tk)
    n_full = jnp.minimum(q_first // tk, nkb)
    row = lax.broadcasted_iota(jnp.int32, (tqc, tk), 0)
    col = lax.broadcasted_iota(jnp.int32, (tqc, tk), 1)
    chains = [(g, c) for g in range(hpg) for c in range(nchunk)]

    def step(j, carry, masked):
        k0 = pl.multiple_of(j * tk, tk)

        def load(new_ref, past_ref, g):
            if not past_heads:
                return new_ref[0, pl.ds(k0, tk), g * DH:(g + 1) * DH]
            if masked:
                return new_ref[0, :, g * DH:(g + 1) * DH]
            return past_ref[0, pl.ds(k0 * past_heads + g, tk, stride=past_heads), :].astype(BF16)

        zs = [_dot_nt(q_ref[0, c * tqc:(c + 1) * tqc, g * DH:(g + 1) * DH],
                      load(k_ref, kp_ref if past_heads else None, g)) for g, c in chains]
        lbs, lks, sufs, causals = [], [], [], []
        for (g, c), z in zip(chains, zs):
            lb = jnp.minimum(z, 0.0) - jnp.log(1.0 + jnp.exp2(-jnp.abs(z))) * LOG2E
            lk = lb - z
            causal = None
            if masked:
                causal = k0 + col < q_first + c * tqc + row
                lk = jnp.where(causal, lk, 0.0)
            hi = lk.astype(BF16)
            lo = (lk - hi.astype(F32)).astype(BF16)
            lbs.append(lb)
            lks.append(lk)
            causals.append(causal)
            sufs.append(jnp.dot(jnp.concatenate([hi, lo], axis=1), low_ref[...], preferred_element_type=F32))
        out = []
        for n, (g, c) in enumerate(chains):
            run, acc = carry[2 * n], carry[2 * n + 1]
            wgt = jnp.exp2(lbs[n] + sufs[n] + run)
            if masked:
                wgt = jnp.where(causals[n], wgt, 0.0)
            acc = acc + jnp.dot(wgt.astype(BF16), load(v_ref, vp_ref if past_heads else None, g),
                                preferred_element_type=F32)
            run = run + jnp.sum(lks[n], axis=1, keepdims=True)
            out += [run, acc]
        return tuple(out)

    carry = []
    for _ in chains:
        carry += [jnp.zeros((tqc, 1), F32), jnp.zeros((tqc, DH), F32)]
    carry = tuple(carry)
    carry = lax.fori_loop(0, nkb - n_full, lambda s, c: step(nkb - 1 - s, c, True), carry)
    carry = lax.fori_loop(0, n_full, lambda s, c: step(n_full - 1 - s, c, False), carry)
    for n, (g, c) in enumerate(chains):
        o_ref[0, c * tqc:(c + 1) * tqc, g * DH:(g + 1) * DH] = carry[2 * n + 1].astype(o_ref.dtype)


def _stick_breaking(qb, kb, vb, *, tq, tk, q0, hpg, nchunk, past=None):
    b, t, _ = qb.shape
    low = np.tril(np.ones((tk, tk), np.float32), -1)
    low = jnp.asarray(np.concatenate([low, low], axis=0), BF16)
    wd = hpg * DH
    if past is None:
        s_pad, extra, extra_specs, past_heads = kb.shape[1], [], [], 0
    else:
        assert hpg == H_B and t == tq and q0 % tk == 0 and t <= tk and past[0].shape[1:] == (q0, H_B, DH)
        s_pad, past_heads = q0 + tk, H_B
        kb, vb = (jnp.pad(a, ((0, 0), (0, tk - t), (0, 0))) for a in (kb, vb))
        extra = [c.reshape(b, q0 * H_B, DH) for c in past]
        extra_specs = [pl.BlockSpec((1, q0 * H_B, DH), lambda bi, h, i: (bi, 0, 0))] * 2
    s_new = kb.shape[1]
    assert s_pad % tk == 0 and t % tq == 0 and H_B % hpg == 0 and tq % nchunk == 0
    body = functools.partial(_sb_body, tq=tq, tk=tk, s_pad=s_pad, q0=q0, hpg=hpg, nchunk=nchunk,
                             past_heads=past_heads)
    return pl.pallas_call(
        body,
        grid=(b, H_B // hpg, t // tq),
        in_specs=[pl.BlockSpec((1, tq, wd), lambda bi, h, i: (bi, i, h)),
                  pl.BlockSpec((1, s_new, wd), lambda bi, h, i: (bi, 0, h)),
                  pl.BlockSpec((1, s_new, wd), lambda bi, h, i: (bi, 0, h)),
                  pl.BlockSpec((2 * tk, tk), lambda bi, h, i: (0, 0))] + extra_specs,
        out_specs=pl.BlockSpec((1, tq, wd), lambda bi, h, i: (bi, i, h)),
        out_shape=jax.ShapeDtypeStruct((b, t, H_B * DH), BF16),
        compiler_params=_cparams(("parallel", "parallel", "arbitrary")),
        name="stick_breaking",
    )(qb, kb, vb, low, *extra)


def _mla_body(qn_ref, qr_ref, kv_ref, kr_ref, o_ref, *, tq, tk, s_valid, q0, hpg, nchunk):
    i = pl.program_id(2)
    tqc = tq // nchunk
    q_first = q0 + i * tq
    q_last = q_first + tq - 1
    lim_first = jnp.minimum((q_first // CHUNK + 1) * CHUNK, s_valid)
    lim_last = jnp.minimum((q_last // CHUNK + 1) * CHUNK, s_valid)
    nkb = (lim_last + tk - 1) // tk
    n_full = lim_first // tk
    row = lax.broadcasted_iota(jnp.int32, (tqc, tk), 0)
    col = lax.broadcasted_iota(jnp.int32, (tqc, tk), 1)
    chains = [(g, c) for g in range(hpg) for c in range(nchunk)]
    hw = NOPE + VD

    def step(j, carry, masked):
        k0 = pl.multiple_of(j * tk, tk)
        kr = kr_ref[0, pl.ds(k0, tk), :]
        ss = []
        for g, c in chains:
            rows = slice(c * tqc, (c + 1) * tqc)
            qc = jnp.concatenate([qn_ref[0, rows, g * NOPE:(g + 1) * NOPE],
                                  qr_ref[0, rows, g * LANES:(g + 1) * LANES]], axis=1)
            kc = jnp.concatenate([kv_ref[0, pl.ds(k0, tk), g * hw:g * hw + NOPE], kr], axis=1)
            ss.append(_dot_nt(qc, kc))
        ps, stats = [], []
        for n, (g, c) in enumerate(chains):
            m_prev = carry[2 * n]
            s = ss[n]
            if masked:
                lim = jnp.minimum((((q_first + c * tqc + row) >> CHUNK_SHIFT) + 1) * CHUNK, s_valid)
                s = jnp.where(k0 + col < lim, s, NEG)
            m_new = jnp.maximum(m_prev, jnp.max(s, axis=1, keepdims=True))
            alpha = jnp.exp2(m_prev - m_new)
            p = jnp.exp2(s - jnp.concatenate([m_new] * (tk // LANES), axis=1))
            stats.append((m_new, alpha))
            ps.append(p.astype(BF16))
        out = []
        for n, (g, c) in enumerate(chains):
            m_new, alpha = stats[n]
            vb = jnp.concatenate([kv_ref[0, pl.ds(k0, tk), g * hw + NOPE:(g + 1) * hw], ones_k], axis=1)
            acc = (carry[2 * n + 1] * jnp.concatenate([alpha, alpha], axis=1)
                   + jnp.dot(ps[n], vb, preferred_element_type=F32))
            out += [m_new, acc]
        return tuple(out)

    ones_k = jnp.ones((tk, LANES), BF16)
    carry = []
    for _ in chains:
        carry += [jnp.full((tqc, LANES), NEG, F32), jnp.zeros((tqc, VD + LANES), F32)]
    carry = tuple(carry)
    carry = lax.fori_loop(0, n_full, lambda j, c: step(j, c, False), carry)
    carry = lax.fori_loop(n_full, nkb, lambda j, c: step(j, c, True), carry)
    for n, (g, c) in enumerate(chains):
        acc = carry[2 * n + 1]
        o_ref[0, c * tqc:(c + 1) * tqc, g * VD:(g + 1) * VD] = (acc[:, :VD] / acc[:, VD:]).astype(o_ref.dtype)


def _mla(qn, qr, kv, kr, *, tq, tk, s_valid, q0, hpg, nchunk):
    b, t, _ = qn.shape
    s_pad = kv.shape[1]
    assert s_pad % tk == 0 and t % tq == 0 and H_C % hpg == 0 and tq % nchunk == 0
    body = functools.partial(_mla_body, tq=tq, tk=tk, s_valid=s_valid, q0=q0, hpg=hpg, nchunk=nchunk)
    return pl.pallas_call(
        body,
        grid=(b, H_C // hpg, t // tq),
        in_specs=[pl.BlockSpec((1, tq, hpg * NOPE), lambda bi, h, i: (bi, i, h)),
                  pl.BlockSpec((1, tq, hpg * LANES), lambda bi, h, i: (bi, i, h)),
                  pl.BlockSpec((1, s_pad, hpg * (NOPE + VD)), lambda bi, h, i: (bi, 0, h)),
                  pl.BlockSpec((1, s_pad, LANES), lambda bi, h, i: (bi, 0, 0))],
        out_specs=pl.BlockSpec((1, tq, hpg * VD), lambda bi, h, i: (bi, i, h)),
        out_shape=jax.ShapeDtypeStruct((b, t, H_C * VD), BF16),
        compiler_params=_cparams(("parallel", "parallel", "arbitrary")),
        name="mla",
    )(qn, qr, kv, kr)


def _absorb_q_body(qn_ref, qr_ref, wk_ref, qa_ref, qrh_ref):
    qa_ref[0] = _dot_nt(qn_ref[...], wk_ref[...]).astype(qa_ref.dtype)
    qrh_ref[0] = qr_ref[...]


def _absorb_q(qn, qr, w_ukv):
    m = qn.shape[0]
    return pl.pallas_call(
        _absorb_q_body,
        grid=(H_C,),
        in_specs=[pl.BlockSpec((m, NOPE), lambda h: (0, h)),
                  pl.BlockSpec((m, LANES), lambda h: (0, h)),
                  pl.BlockSpec((KV_RANK, NOPE), lambda h: (0, 2 * h))],
        out_specs=[pl.BlockSpec((1, m, KV_RANK), lambda h: (h, 0, 0)),
                   pl.BlockSpec((1, m, LANES), lambda h: (h, 0, 0))],
        out_shape=[jax.ShapeDtypeStruct((H_C, m, KV_RANK), BF16),
                   jax.ShapeDtypeStruct((H_C, m, LANES), BF16)],
        compiler_params=_cparams(("parallel",)),
        name="mla_absorb_q",
    )(qn, qr, w_ukv)


def _mla_latent_body(qa_ref, qr_ref, c_ref, kr_ref, cp_ref, krp_ref, o_ref, *, t, tk, s_valid, q0, nsplit):
    hs = H_C // nsplit
    rows = hs * t
    lim_first = min((q0 // CHUNK + 1) * CHUNK, s_valid)
    lim_last = min(((q0 + t - 1) // CHUNK + 1) * CHUNK, s_valid)
    nkb = -(-lim_last // tk)
    n_full = lim_first // tk
    assert n_full == q0 // tk and nkb == n_full + 1
    q_pos = q0 + lax.broadcasted_iota(jnp.int32, (hs, t, tk), 1).reshape(rows, tk)
    lim = jnp.minimum(((q_pos >> CHUNK_SHIFT) + 1) * CHUNK, s_valid)
    col = lax.broadcasted_iota(jnp.int32, (rows, tk), 1)
    ones_k = jnp.ones((tk, LANES), BF16)
    qs = [jnp.concatenate([qa_ref[n * hs:(n + 1) * hs].reshape(rows, KV_RANK),
                           qr_ref[n * hs:(n + 1) * hs].reshape(rows, LANES)], axis=1) for n in range(nsplit)]

    def step(j, carry, masked):
        k0 = pl.multiple_of(j * tk, tk)
        if masked:
            cb, krb = c_ref[0], kr_ref[0]
        else:
            cb = cp_ref[0, pl.ds(k0, tk), :].astype(BF16)
            krb = jnp.concatenate([krp_ref[0, pl.ds(k0, tk), :].astype(BF16),
                                   jnp.zeros((tk, LANES - ROPE_C), BF16)], axis=1)
        kc = jnp.concatenate([cb, krb], axis=1)
        vc = jnp.concatenate([cb, ones_k], axis=1)
        ss = [_dot_nt(q, kc) for q in qs]
        ps, stats = [], []
        for n in range(nsplit):
            s = ss[n]
            if masked:
                s = jnp.where(k0 + col < lim, s, NEG)
            m_prev = carry[2 * n]
            m_new = jnp.maximum(m_prev, jnp.max(s, axis=1, keepdims=True))
            alpha = jnp.exp2(m_prev - m_new)
            ps.append(jnp.exp2(s - jnp.concatenate([m_new] * (tk // LANES), axis=1)).astype(BF16))
            stats.append((m_new, alpha))
        out = []
        for n in range(nsplit):
            m_new, alpha = stats[n]
            acc = (carry[2 * n + 1] * jnp.concatenate([alpha] * (KV_RANK // LANES + 1), axis=1)
                   + jnp.dot(ps[n], vc, preferred_element_type=F32))
            out += [m_new, acc]
        return tuple(out)

    carry = []
    for _ in range(nsplit):
        carry += [jnp.full((rows, LANES), NEG, F32), jnp.zeros((rows, KV_RANK + LANES), F32)]
    carry = tuple(carry)
    carry = lax.fori_loop(0, n_full, lambda j, c: step(j, c, False), carry)
    carry = lax.fori_loop(n_full, nkb, lambda j, c: step(j, c, True), carry)
    for n in range(nsplit):
        acc = carry[2 * n + 1]
        lat = acc[:, :KV_RANK] / jnp.concatenate([acc[:, KV_RANK:]] * (KV_RANK // LANES), axis=1)
        o_ref[0, n * hs:(n + 1) * hs] = lat.reshape(hs, t, KV_RANK).astype(o_ref.dtype)


def _mla_latent(qa_hm, qr_hm, c_new, kr_new, c_past, kr_past, *, tk, s_valid, q0):
    b, t, _ = c_new.shape
    assert q0 % tk == 0 and t <= tk and c_past.shape[1] == q0
    c_new, kr_new = (jnp.pad(a, ((0, 0), (0, tk - t), (0, 0))) for a in (c_new, kr_new))
    body = functools.partial(_mla_latent_body, t=t, tk=tk, s_valid=s_valid, q0=q0, nsplit=2)
    return pl.pallas_call(
        body,
        grid=(b,),
        in_specs=[pl.BlockSpec((H_C, t, KV_RANK), lambda bi: (0, bi, 0)),
                  pl.BlockSpec((H_C, t, LANES), lambda bi: (0, bi, 0)),
                  pl.BlockSpec((1, tk, KV_RANK), lambda bi: (bi, 0, 0)),
                  pl.BlockSpec((1, tk, LANES), lambda bi: (bi, 0, 0)),
                  pl.BlockSpec((1, q0, KV_RANK), lambda bi: (bi, 0, 0)),
                  pl.BlockSpec((1, q0, ROPE_C), lambda bi: (bi, 0, 0))],
        out_specs=pl.BlockSpec((1, H_C, t, KV_RANK), lambda bi: (bi, 0, 0, 0)),
        out_shape=jax.ShapeDtypeStruct((b, H_C, t, KV_RANK), BF16),
        compiler_params=_cparams(("parallel",)),
        name="mla_latent",
    )(qa_hm, qr_hm, c_new, kr_new, c_past, kr_past)


def _latent_out_body(x_ref, wv_ref, o_ref):
    nb, _, t, r = x_ref.shape
    o_ref[...] = jnp.dot(x_ref[...].reshape(nb * t, r), wv_ref[...],
                         preferred_element_type=F32).astype(o_ref.dtype)


def _latent_out(lat, w_ukv):
    b, _, t, _ = lat.shape
    return pl.pallas_call(
        _latent_out_body,
        grid=(H_C,),
        in_specs=[pl.BlockSpec((b, 1, t, KV_RANK), lambda h: (0, h, 0, 0)),
                  pl.BlockSpec((KV_RANK, VD), lambda h: (0, 2 * h + 1))],
        out_specs=pl.BlockSpec((b * t, VD), lambda h: (0, h)),
        out_shape=jax.ShapeDtypeStruct((b * t, H_C * VD), BF16),
        compiler_params=_cparams(("parallel",)),
        name="mla_latent_out",
    )(lat, w_ukv)


def _ffn_body(x_ref, g_ref, wg_ref, wv_ref, cwg_ref, cwv_ref, cbg_ref, cbv_ref, wd_ref, pg_ref, pv_ref, gn_ref,
              o_ref, hn_ref, sg_ref, sv_ref, h_scr, acc_scr, ug_buf, uv_buf, cg_scr, cv_scr, *, seg, carried):
    i = pl.program_id(0)
    f = pl.program_id(1)
    tm = x_ref.shape[0]

    @pl.when(f == 0)
    def _():
        x = x_ref[...]
        y = x * lax.rsqrt(jnp.mean(x * x, axis=-1, keepdims=True) + EPS)
        h_scr[...] = (y * g_ref[...]).astype(BF16)
        acc_scr[...] = x

    h = h_scr[...]
    halves = ((wg_ref, cwg_ref, cbg_ref, pg_ref, sg_ref, ug_buf, cg_scr),
              (wv_ref, cwv_ref, cbv_ref, pv_ref, sv_ref, uv_buf, cv_scr))
    ys = []
    for w_ref, cw_ref, cb_ref, p_ref, s_ref, buf, c_scr in halves:
        u = jnp.dot(h, w_ref[...], preferred_element_type=F32)
        cw = cw_ref[...]
        parts = []
        for sgi in range(tm // seg):
            us = u[sgi * seg:(sgi + 1) * seg]
            if carried:
                head = jnp.where(i == 0, p_ref[0], c_scr[f])
            else:
                head = p_ref[sgi]
            buf[sgi, 0:SUBLANES] = head
            buf[sgi, SUBLANES:SUBLANES + seg] = us
            y = (cw[2:3] * us + cw[1:2] * buf[sgi, SUBLANES - 1:SUBLANES - 1 + seg]
                 + cw[0:1] * buf[sgi, SUBLANES - 2:SUBLANES - 2 + seg] + cb_ref[...])
            parts.append(y)
            last = us[seg - SUBLANES:seg]
            s_ref[sgi] = last
            if carried:
                c_scr[f] = last
        ys.append(parts[0] if len(parts) == 1 else jnp.concatenate(parts, axis=0))
    yg, yv = ys
    act = (yg / (1.0 + jnp.exp(-yg))) * yv
    acc_scr[...] += jnp.dot(act.astype(BF16), wd_ref[...], preferred_element_type=F32)

    @pl.when(f == pl.num_programs(1) - 1)
    def _():
        a = acc_scr[...]
        o_ref[...] = a
        y = a * lax.rsqrt(jnp.mean(a * a, axis=-1, keepdims=True) + EPS)
        hn_ref[...] = (y * gn_ref[...]).astype(hn_ref.dtype)


def _ffn(x, g, w_up, conv_w, conv_b, w_down, prev8, g_next, next_dtype, *, seg, carried, tm, tf):
    m, d = x.shape
    nf = D_FF // tf
    nseg = tm // seg
    assert m % tm == 0 and D_FF % tf == 0 and tm % seg == 0
    sidx = (lambda i: 0) if carried else (lambda i: i)
    body = functools.partial(_ffn_body, seg=seg, carried=carried)
    cb = conv_b.reshape(1, 2 * D_FF)
    return pl.pallas_call(
        body,
        grid=(m // tm, nf),
        in_specs=[pl.BlockSpec((tm, d), lambda i, f: (i, 0)),
                  pl.BlockSpec((1, d), lambda i, f: (0, 0)),
                  pl.BlockSpec((d, tf), lambda i, f: (0, f)),
                  pl.BlockSpec((d, tf), lambda i, f: (0, nf + f)),
                  pl.BlockSpec((CONV_W, tf), lambda i, f: (0, f)),
                  pl.BlockSpec((CONV_W, tf), lambda i, f: (0, nf + f)),
                  pl.BlockSpec((1, tf), lambda i, f: (0, f)),
                  pl.BlockSpec((1, tf), lambda i, f: (0, nf + f)),
                  pl.BlockSpec((tf, d), lambda i, f: (f, 0)),
                  pl.BlockSpec((nseg, SUBLANES, tf), lambda i, f: (sidx(i), 0, f)),
                  pl.BlockSpec((nseg, SUBLANES, tf), lambda i, f: (sidx(i), 0, nf + f)),
                  pl.BlockSpec((1, d), lambda i, f: (0, 0))],
        out_specs=[pl.BlockSpec((tm, d), lambda i, f: (i, 0)),
                   pl.BlockSpec((tm, d), lambda i, f: (i, 0)),
                   pl.BlockSpec((nseg, SUBLANES, tf), lambda i, f: (i, 0, f)),
                   pl.BlockSpec((nseg, SUBLANES, tf), lambda i, f: (i, 0, f))],
        out_shape=[jax.ShapeDtypeStruct((m, d), F32),
                   jax.ShapeDtypeStruct((m, d), next_dtype),
                   jax.ShapeDtypeStruct((m // seg, SUBLANES, D_FF), F32),
                   jax.ShapeDtypeStruct((m // seg, SUBLANES, D_FF), F32)],
        scratch_shapes=[pltpu.VMEM((tm, d), BF16),
                        pltpu.VMEM((tm, d), F32),
                        pltpu.VMEM((nseg, SUBLANES + seg, tf), F32),
                        pltpu.VMEM((nseg, SUBLANES + seg, tf), F32),
                        pltpu.VMEM((nf, SUBLANES, tf), F32),
                        pltpu.VMEM((nf, SUBLANES, tf), F32)],
        compiler_params=_cparams(("arbitrary", "arbitrary")),
        name="conv_ffn",
    )(x, g.reshape(1, d), w_up, w_up, conv_w, conv_w, cb, cb, w_down, prev8, prev8, g_next.reshape(1, d))


def _rope_tables(pos, head_dim, rot, lanes_valid=LANES):
    half = rot // 2
    freqs = ROPE_THETA ** (-jnp.arange(half, dtype=F32) / half)
    ang = pos.astype(F32)[:, None] * freqs[None, :]
    cos, sin = jnp.cos(ang), jnp.sin(ang)
    lane = np.arange(LANES)
    d = lane % head_dim
    first = (d < half) & (lane < lanes_valid)
    second = (d >= half) & (d < rot) & (lane < lanes_valid)
    fidx = np.clip(np.where(d < half, d, d - half), 0, half - 1)
    cosl, sinl = cos[:, fidx], sin[:, fidx]
    c = jnp.where(first | second, cosl, 1.0)
    s1 = jnp.where(second, sinl, 0.0)
    s2 = jnp.where(first, -sinl, 0.0)
    return c, s1, s2, half


def _prep_weights(w_in_ab, w_out_ab, w_dqkv, w_uq, w_ukv, w_o_mla, w_up, w_down):
    n_even, n_odd = w_in_ab.shape[0], w_dqkv.shape[0]
    w = {}
    n_a = H_A * DH + 2 * KV_A * DH + H_I * D_IDX
    w["in_a"] = w_in_ab[:, :, :n_a].astype(BF16)
    pad = jnp.zeros((n_even, D_MODEL, LANES - D_IDX - H_I), BF16)
    w["in_b"] = jnp.concatenate([w_in_ab[:, :, n_a:n_a + D_IDX + H_I].astype(BF16), pad,
                                 w_in_ab[:, :, n_a + D_IDX + H_I:].astype(BF16)], axis=-1)
    w["out_ab"] = w_out_ab.astype(BF16)
    w["dqkr"] = jnp.concatenate([w_dqkv.astype(BF16),
                                 jnp.zeros((n_odd, D_MODEL, LANES - ROPE_C), BF16)], axis=-1)
    uq = w_uq.reshape(n_odd, Q_RANK, H_C, NOPE + ROPE_C)
    uqn = uq[..., :NOPE].reshape(n_odd, Q_RANK, H_C * NOPE).astype(BF16)
    uqr = uq[..., NOPE:].astype(BF16)
    uqr = jnp.concatenate([uqr, jnp.zeros_like(uqr)], axis=-1).reshape(n_odd, Q_RANK, H_C * LANES)
    w["uq"] = jnp.concatenate([uqn, uqr], axis=-1)
    w["ukv"] = w_ukv.astype(BF16)
    w["o_mla"] = w_o_mla.astype(BF16)
    w["up"] = w_up.astype(BF16)
    w["down"] = w_down.astype(BF16)
    return w


def _with_past(past, new, s_pad):
    b, _, fdim = new.shape
    parts = [new.astype(BF16)] if past is None else [past.astype(BF16), new.astype(BF16)]
    n = sum(p.shape[1] for p in parts)
    if s_pad > n:
        parts.append(jnp.zeros((b, s_pad - n, fdim), BF16))
    return parts[0] if len(parts) == 1 else jnp.concatenate(parts, axis=1)


def _trunk(x, past, conv_state, w, p, cfg):
    b, t, d = x.shape
    m = b * t
    q0 = 0 if past is None else past[0].shape[2]
    s_valid = q0 + t
    tk = cfg["tk"]
    s_pad = -(-s_valid // tk) * tk
    pos = q0 + jnp.tile(jnp.arange(t), b)
    tab_a = _rope_tables(pos, DH, ROT_A)
    tab_i = _rope_tables(pos, D_IDX, ROT_I)
    tab_ki = _rope_tables(pos, D_IDX, ROT_I, lanes_valid=D_IDX)
    tab_kr = _rope_tables(pos, ROPE_C, ROPE_C, lanes_valid=ROPE_C)

    xf = x.reshape(m, d)
    ab_rows, mla_rows, conv_rows = [], [], []
    sc_ab = DH ** -0.5 * LOG2E
    sc_c = (NOPE + ROPE_C) ** -0.5 * LOG2E
    h = _rmsnorm(xf, p["norm_mix"][0], BF16)
    for l in range(DEPTH):
        if l % 2 == 0:
            e = l // 2
            lp = None if past is None else tuple(c[e] for c in past[:5])
            qa, ka32, ka16, va32, va16, qi = _proj(
                [h], w["in_a"][e],
                [dict(width=H_A * DH, dtypes=(BF16,), rope=0, scale=sc_ab),
                 dict(width=KV_A * DH, dtypes=(F32, BF16), rope=0, split_heads=True),
                 dict(width=KV_A * DH, dtypes=(F32, BF16), split_heads=True),
                 dict(width=H_I * D_IDX, dtypes=(BF16,), rope=1)],
                tabs=(tab_a, tab_i), name="proj_in_a")
            kiwi, qb, kb32, kb16, vb32, vb16 = _proj(
                [h], w["in_b"][e],
                [dict(width=LANES, dtypes=(F32,), rope=0),
                 dict(width=H_B * DH, dtypes=(BF16,), scale=sc_ab),
                 dict(width=H_B * DH, dtypes=(F32, BF16), split_heads=True),
                 dict(width=H_B * DH, dtypes=(F32, BF16), split_heads=True)],
                tabs=(tab_ki,), name="proj_in_b")
            ki32 = kiwi[:, :D_IDX]
            wi = kiwi[:, D_IDX:D_IDX + H_I]
            ab_rows.append((ka32.reshape(b, t, KV_A, DH), va32.reshape(b, t, KV_A, DH),
                            ki32.reshape(b, t, D_IDX), kb32.reshape(b, t, H_B, DH),
                            vb32.reshape(b, t, H_B, DH)))

            def full(idx, new16):
                pst = None if lp is None else lp[idx].reshape(b, q0, -1)
                return _with_past(pst, new16.reshape(b, t, -1), s_pad)

            ki_all = full(2, ki32)
            zk = jnp.zeros_like(ki_all)
            ki2 = jnp.concatenate([ki_all, zk, zk, ki_all], axis=-1)
            oa = _dsa(qa.reshape(b, t, -1), qi.reshape(b, t, -1), wi.reshape(b, t, H_I), full(0, ka16),
                      full(1, va16), ki2, tq=cfg["tq_a"], tk=tk, s_valid=s_valid, q0=q0)
            if lp is None:
                ob = _stick_breaking(qb.reshape(b, t, -1), full(3, kb16), full(4, vb16), tq=cfg["tq_b"],
                                     tk=cfg["tk_b"], q0=q0, hpg=cfg["hpg_b"], nchunk=cfg["nc_b"])
            else:
                ob = _stick_breaking(qb.reshape(b, t, -1), kb16.reshape(b, t, -1), vb16.reshape(b, t, -1),
                                     tq=cfg["tq_b"], tk=cfg["tk_b"], q0=q0, hpg=cfg["hpg_b"],
                                     nchunk=cfg["nc_b"], past=(lp[3], lp[4]))
            (xf,) = _proj([oa.reshape(m, -1), ob.reshape(m, -1)], w["out_ab"][e],
                          [dict(width=D_MODEL, dtypes=(F32,))], resid=xf, name="proj_out_ab")
        else:
            od = l // 2
            gains = jnp.concatenate([p["g_q"][od], p["g_kv"][od]])
            cq16, ckv32, ckv16, kr32w, kr16w = _proj(
                [h], w["dqkr"][od],
                [dict(width=Q_RANK, dtypes=(BF16,), gain_off=0),
                 dict(width=KV_RANK, dtypes=(F32, BF16), gain_off=Q_RANK),
                 dict(width=LANES, dtypes=(F32, BF16), rope=0)],
                tabs=(tab_kr,), gain=gains, name="proj_dqkr")
            mla_rows.append((ckv32.reshape(b, t, KV_RANK), kr32w[:, :ROPE_C].reshape(b, t, ROPE_C)))
            qn, qr = _proj([cq16], w["uq"][od],
                           [dict(width=H_C * NOPE, dtypes=(BF16,), scale=sc_c),
                            dict(width=H_C * LANES, dtypes=(BF16,), rope=0, scale=sc_c)],
                           tabs=(tab_kr,), name="proj_uq")
            if cfg["latent_c"]:
                qa_hm, qr_hm = _absorb_q(qn, qr, w["ukv"][od])
                lat = _mla_latent(qa_hm, qr_hm, ckv16.reshape(b, t, KV_RANK), kr16w.reshape(b, t, LANES),
                                  past[5][od], past[6][od], tk=cfg["tk_c"], s_valid=s_valid, q0=q0)
                o = _latent_out(lat, w["ukv"][od])
            else:
                pc = None if past is None else past[5][od]
                pk = None if past is None else jnp.pad(past[6][od], ((0, 0), (0, 0), (0, LANES - ROPE_C)))
                c_all = _with_past(pc, ckv16.reshape(b, t, KV_RANK), s_pad)
                kr_all = _with_past(pk, kr16w.reshape(b, t, LANES), s_pad)
                (kv,) = _proj([c_all.reshape(b * s_pad, KV_RANK)], w["ukv"][od],
                              [dict(width=H_C * (NOPE + VD), dtypes=(BF16,))], name="proj_ukv")
                o = _mla(qn.reshape(b, t, -1), qr.reshape(b, t, -1), kv.reshape(b, s_pad, -1), kr_all,
                         tq=cfg["tq_c"], tk=cfg["tk_c"], s_valid=s_valid, q0=q0, hpg=cfg["hpg_c"],
                         nchunk=cfg["nc_c"])
            (xf,) = _proj([o.reshape(m, -1)], w["o_mla"][od], [dict(width=D_MODEL, dtypes=(F32,))],
                          resid=xf, name="proj_o_mla")

        if conv_state is None:
            prev8 = jnp.zeros((b, SUBLANES, 2 * D_FF), F32)
        else:
            prev8 = jnp.concatenate([jnp.zeros((b, SUBLANES - (CONV_W - 1), 2 * D_FF), F32),
                                     conv_state[l]], axis=1)
        final = l == DEPTH - 1
        g_next = p["norm_final"] if final else p["norm_mix"][l + 1]
        xf, h, sg, sv = _ffn(xf, p["norm_ffn"][l], w["up"][l], p["conv_w"][l], p["conv_b"][l], w["down"][l],
                             prev8, g_next, F32 if final else BF16, seg=cfg["seg"], carried=cfg["carried"],
                             tm=cfg["tm_ffn"], tf=cfg["tf"])
        last = [s8.reshape(b, -1, SUBLANES, D_FF)[:, -1, SUBLANES - (CONV_W - 1):] for s8 in (sg, sv)]
        conv_rows.append(jnp.concatenate(last, axis=-1))

    y = h.reshape(b, t, d)
    new_ab = [jnp.stack([r[i] for r in ab_rows]) for i in range(5)]
    new_mla = [jnp.stack([r[i] for r in mla_rows]) for i in range(2)]
    return y, new_ab + new_mla + [jnp.stack(conv_rows)]


def kernel(x_prompt, x_sample, cache_k_a, cache_v_a, cache_idx_k, cache_k_b, cache_v_b, cache_ckv, cache_krope, state_conv, norm_mix, norm_ffn, norm_final, w_in_ab, w_out_ab, w_dqkv, g_q, g_kv, w_uq, w_ukv, w_o_mla, w_up, conv_w, conv_b, w_down):
    w = _prep_weights(w_in_ab, w_out_ab, w_dqkv, w_uq, w_ukv, w_o_mla, w_up, w_down)
    p = dict(norm_mix=norm_mix, norm_ffn=norm_ffn, norm_final=norm_final, g_q=g_q, g_kv=g_kv,
             conv_w=conv_w, conv_b=conv_b)
    cfg_p = dict(tk=512, tq_a=256, tq_b=256, tk_b=256, hpg_b=4, nc_b=1, tq_c=512, tk_c=1024, hpg_c=2, nc_c=2,
                 latent_c=False, seg=512, carried=True, tm_ffn=512, tf=512)
    t_s = x_sample.shape[1]
    cfg_s = dict(tk=256, tq_a=t_s, tq_b=t_s, tk_b=256, hpg_b=H_B, nc_b=1, tq_c=t_s, tk_c=256, hpg_c=8, nc_c=1,
                 latent_c=True, seg=t_s, carried=False, tm_ffn=x_sample.shape[0] * t_s, tf=512)
    y_p, st_p = _trunk(x_prompt, None, None, w, p, cfg_p)
    past = (cache_k_a, cache_v_a, cache_idx_k, cache_k_b, cache_v_b, cache_ckv, cache_krope)
    y_s, st_s = _trunk(x_sample, past, state_conv, w, p, cfg_s)
    return (y_p, y_s, *st_p, *st_s)
```

```python
import functools

import numpy as np
import jax
import jax.numpy as jnp
from jax import lax
from jax.experimental import pallas as pl
from jax.experimental.pallas import tpu as pltpu

F32 = jnp.float32
BF16 = jnp.bfloat16

D_MODEL = 2048
DEPTH = 4
CHUNK = 64
CHUNK_SHIFT = CHUNK.bit_length() - 1
ROPE_THETA = 500000.0
EPS = 1e-6
H_A, KV_A, DH = 8, 2, 128
ROT_A = DH // 4
H_I, D_IDX = 16, 64
ROT_I = D_IDX // 4
TOPK_MAX = 256
H_B = 8
H_C, Q_RANK, KV_RANK, NOPE, ROPE_C, VD = 16, 512, 512, 128, 64, 128
D_FF = 5632
CONV_W = 3

LANES = 128
SUBLANES = 8
VMEM_LIMIT = 56 * 1024 * 1024
NEG = -1e30
INT_MIN = np.int32(-2 ** 31)
INT_MAX = np.int32(2 ** 31 - 1)
LOG2E = float(np.log2(np.e))


def _cparams(sem):
    return pltpu.CompilerParams(dimension_semantics=sem, vmem_limit_bytes=VMEM_LIMIT)


def _dot_nt(a, b):
    return lax.dot_general(a, b, (((1,), (1,)), ((), ())), preferred_element_type=F32)


def _rms_body(x_ref, g_ref, o_ref):
    x = x_ref[...]
    y = x * lax.rsqrt(jnp.mean(x * x, axis=-1, keepdims=True) + EPS)
    o_ref[...] = (y * g_ref[...]).astype(o_ref.dtype)


def _rmsnorm(x, g, out_dtype):
    m, d = x.shape
    tm = min(m, 512)
    return pl.pallas_call(
        _rms_body,
        grid=(m // tm,),
        in_specs=[pl.BlockSpec((tm, d), lambda i: (i, 0)),
                  pl.BlockSpec((1, d), lambda i: (0, 0))],
        out_specs=pl.BlockSpec((tm, d), lambda i: (i, 0)),
        out_shape=jax.ShapeDtypeStruct((m, d), out_dtype),
        compiler_params=_cparams(("parallel",)),
        name="rmsnorm",
    )(x, g.reshape(1, d))


PROJ_CHUNK = 512
COUNT_ROWS = 64


def _proj_body(*refs, nx, groups, tab_halves, has_gain, has_resid):
    it = iter(refs)
    x_refs = [next(it) for _ in range(nx)]
    w_ref = next(it)
    tabs = [(next(it), next(it), next(it)) for _ in tab_halves]
    g_ref = next(it) if has_gain else None
    r_ref = next(it) if has_resid else None
    outs = list(it)

    xs = [r[...] for r in x_refs]
    x = xs[0] if nx == 1 else jnp.concatenate(xs, axis=1)
    accs = []
    c0 = 0
    for grp in groups:
        width = grp["width"]
        chunk = min(width, PROJ_CHUNK)
        for cc in range(0, width, chunk):
            accs.append(jnp.dot(x, w_ref[:, c0 + cc:c0 + cc + chunk], preferred_element_type=F32))
        c0 += width
    accs = iter(accs)
    oi = 0
    for grp in groups:
        width = grp["width"]
        chunk = min(width, PROJ_CHUNK)
        for cc in range(0, width, chunk):
            acc = next(accs)
            if grp.get("gain_off") is not None:
                assert chunk == width
                go = grp["gain_off"]
                acc = (acc * lax.rsqrt(jnp.mean(acc * acc, axis=-1, keepdims=True) + EPS)
                       * g_ref[:, go:go + width])
            if grp.get("rope") is not None:
                c_ref, s1_ref, s2_ref = tabs[grp["rope"]]
                half = tab_halves[grp["rope"]]
                c, s1, s2 = c_ref[...], s1_ref[...], s2_ref[...]
                parts = []
                for gi in range(chunk // LANES):
                    xg = acc[:, gi * LANES:(gi + 1) * LANES]
                    parts.append(xg * c + pltpu.roll(xg, half, 1) * s1
                                 + pltpu.roll(xg, LANES - half, 1) * s2)
                acc = parts[0] if len(parts) == 1 else jnp.concatenate(parts, axis=1)
            if grp.get("scale") is not None:
                acc = acc * grp["scale"]
            if has_resid:
                acc = acc + r_ref[:, cc:cc + chunk]
            for k in range(len(grp["dtypes"])):
                o = outs[oi + k]
                if len(o.shape) == 3:
                    for gi in range(chunk // LANES):
                        o[:, cc // LANES + gi, :] = acc[:, gi * LANES:(gi + 1) * LANES].astype(o.dtype)
                else:
                    o[:, cc:cc + chunk] = acc.astype(o.dtype)
        oi += len(grp["dtypes"])


def _proj(xs, w, groups, *, tabs=(), gain=None, resid=None, name="proj"):
    m = xs[0].shape[0]
    k, n = w.shape
    tm = min(m, 512)
    assert m % tm == 0 and n == sum(g["width"] for g in groups) and k == sum(x.shape[1] for x in xs)
    assert resid is None or len(groups) == 1
    in_specs = [pl.BlockSpec((tm, x.shape[1]), lambda i: (i, 0)) for x in xs]
    in_specs.append(pl.BlockSpec((k, n), lambda i: (0, 0)))
    args = list(xs) + [w]
    for c, s1, s2, _ in tabs:
        for t in (c, s1, s2):
            in_specs.append(pl.BlockSpec((tm, LANES), lambda i: (i, 0)))
            args.append(t)
    if gain is not None:
        in_specs.append(pl.BlockSpec((1, gain.shape[0]), lambda i: (0, 0)))
        args.append(gain.reshape(1, -1))
    if resid is not None:
        in_specs.append(pl.BlockSpec((tm, n), lambda i: (i, 0)))
        args.append(resid)
    out_specs, out_shape = [], []
    for g in groups:
        for dt in g["dtypes"]:
            if g.get("split_heads") and dt == F32:
                nh = g["width"] // LANES
                out_specs.append(pl.BlockSpec((tm, nh, LANES), lambda i: (i, 0, 0)))
                out_shape.append(jax.ShapeDtypeStruct((m, nh, LANES), dt))
            else:
                out_specs.append(pl.BlockSpec((tm, g["width"]), lambda i: (i, 0)))
                out_shape.append(jax.ShapeDtypeStruct((m, g["width"]), dt))
    body = functools.partial(_proj_body, nx=len(xs), groups=groups, tab_halves=[t[3] for t in tabs],
                             has_gain=gain is not None, has_resid=resid is not None)
    return pl.pallas_call(
        body,
        grid=(m // tm,),
        in_specs=in_specs,
        out_specs=out_specs,
        out_shape=out_shape,
        compiler_params=_cparams(("parallel",)),
        name=name,
    )(*args)


def _dsa_body(qa_ref, qi_ref, wi_ref, ka_ref, va_ref, ki_ref, tri_ref, o_ref,
              key_scr, qs_scr, m_scr, acc_scr, *, tq, tk, s_valid, q0, topk):
    i = pl.program_id(1)
    q_first = q0 + i * tq
    q_last = q_first + tq - 1
    adm_end = jnp.minimum((q_last // CHUNK + 1) * CHUNK, s_valid)
    nkb = (adm_end + tk - 1) // tk
    q_pos = q_first + lax.broadcasted_iota(jnp.int32, (tq, tk), 0)
    q_lim = jnp.minimum(((q_pos >> CHUNK_SHIFT) + 1) * CHUNK, s_valid)
    col = lax.broadcasted_iota(jnp.int32, (tq, tk), 1)
    rep = H_A // KV_A

    w = wi_ref[0] * ((H_I * D_IDX) ** -0.5)
    pg = 4
    pairs = LANES // D_IDX
    qi = qi_ref[0]
    q_groups = [jnp.concatenate([qi[:, (g * pg + pp) * LANES:(g * pg + pp + 1) * LANES]
                                 for pp in range(pg)], axis=0) for g in range(H_I // (pairs * pg))]

    def to_key(x):
        bits = pltpu.bitcast(x, jnp.int32)
        return bits ^ ((bits >> 31) & INT_MAX)

    transposed = key_scr.shape[1] == tq

    if transposed:
        wt = w
        krow = lax.broadcasted_iota(jnp.int32, (tk, tq), 0)
        q_lim_t = jnp.minimum((((q_first + lax.broadcasted_iota(jnp.int32, (tk, tq), 1)) >> CHUNK_SHIFT) + 1) * CHUNK,
                              s_valid)

    def p1(j, c):
        k0 = pl.multiple_of(j * tk, tk)
        if transposed:
            sc = jnp.zeros((tk, tq), F32)
            for g, qg in enumerate(q_groups):
                for half in range(pairs):
                    rel = _dot_nt(ki_ref[0, pl.ds(k0, tk), half * LANES:(half + 1) * LANES], qg)
                    for pp in range(pg):
                        h = (g * pg + pp) * pairs + half
                        sc = sc + jnp.maximum(rel[:, pp * tq:(pp + 1) * tq], 0.0) * wt[h:h + 1, :]
            key_scr[pl.ds(k0, tk), :] = jnp.where(k0 + krow < q_lim_t, to_key(sc), INT_MIN)
            return c
        sc = jnp.zeros((tq, tk), F32)
        for g, qg in enumerate(q_groups):
            for half in range(pairs):
                rel = _dot_nt(qg, ki_ref[0, pl.ds(k0, tk), half * LANES:(half + 1) * LANES])
                for pp in range(pg):
                    h = (g * pg + pp) * pairs + half
                    sc = sc + jnp.maximum(rel[pp * tq:(pp + 1) * tq], 0.0) * w[:, h:h + 1]
        key_scr[:, pl.ds(k0, tk)] = jnp.where(k0 + col < q_lim, to_key(sc), INT_MIN)
        return c

    lax.fori_loop(0, nkb, p1, 0)

    def count_ge(v):
        def cb(j, c):
            k0 = pl.multiple_of(j * tk, tk)
            if transposed:
                ge = jnp.where(key_scr[pl.ds(k0, tk), :] >= v, 1.0, 0.0)
                return c + jnp.sum(ge.reshape(tk // COUNT_ROWS, COUNT_ROWS, tq), axis=0)
            ge = jnp.where(key_scr[:, pl.ds(k0, tk)] >= v, 1.0, 0.0)
            part = ge[:, 0:LANES]
            for cc in range(1, tk // LANES):
                part = part + ge[:, cc * LANES:(cc + 1) * LANES]
            return c + part
        if transposed:
            c = lax.fori_loop(0, nkb, cb, jnp.zeros((COUNT_ROWS, tq), F32))
            return jnp.sum(c, axis=0, keepdims=True)
        c = lax.fori_loop(0, nkb, cb, jnp.zeros((tq, LANES), F32))
        return jnp.sum(c, axis=1, keepdims=True)

    def bis(_, lohi):
        lo, hi = lohi
        mid = (lo >> 1) + (hi >> 1) + (lo & hi & 1)
        ok = count_ge(mid) >= float(topk)
        return jnp.where(ok, mid, lo), jnp.where(ok, hi, mid)

    vshape = (1, tq) if transposed else (tq, 1)
    thr, _ = lax.fori_loop(0, 32, bis, (jnp.full(vshape, INT_MIN, jnp.int32),
                                        jnp.full(vshape, INT_MAX, jnp.int32)))
    need = float(topk) - count_ge(thr + 1)
    if transposed:
        thr = pltpu.bitcast(jnp.transpose(pltpu.bitcast(jnp.broadcast_to(thr, (LANES, tq)), F32)), jnp.int32)
        need = jnp.transpose(jnp.broadcast_to(need, (LANES, tq)))
        thr = jnp.concatenate([thr] * (tk // LANES), axis=1)
        need = jnp.concatenate([need] * (tk // LANES), axis=1)

    qa = qa_ref[0]
    for g in range(KV_A):
        qs_scr[g] = jnp.concatenate(
            [qa[:, (g * rep + r) * DH:(g * rep + r + 1) * DH] for r in range(rep)], axis=0)
    m_scr[...] = jnp.full(m_scr.shape, NEG, F32)
    ones_k = jnp.ones((tk, LANES), BF16)
    acc_scr[...] = jnp.zeros(acc_scr.shape, F32)

    def p3(j, eqc):
        k0 = pl.multiple_of(j * tk, tk)
        if transposed:
            key = pltpu.bitcast(jnp.transpose(pltpu.bitcast(key_scr[pl.ds(k0, tk), :], F32)), jnp.int32)
        else:
            key = key_scr[:, pl.ds(k0, tk)]
        eq = key == thr
        eqf = jnp.where(eq, 1.0, 0.0)
        prefix = jnp.dot(eqf.astype(BF16), tri_ref[...], preferred_element_type=F32) + eqc
        bias = jnp.where(key > thr, 0.0, jnp.where(eq, jnp.where(prefix < need, 0.0, NEG), NEG))
        bias = jnp.where(key == INT_MIN, NEG, bias)
        eqc = eqc + jnp.sum(eqf, axis=1, keepdims=True)
        ss = [_dot_nt(qs_scr[g], ka_ref[0, pl.ds(k0, tk), g * DH:(g + 1) * DH])
              for g in range(KV_A)]
        for g in range(KV_A):
            vg = va_ref[0, pl.ds(k0, tk), g * DH:(g + 1) * DH]
            s = ss[g]
            ps, alphas = [], []
            for r in range(rep):
                h = g * rep + r
                sr = s[r * tq:(r + 1) * tq] + bias
                m_prev = m_scr[h]
                m_new = jnp.maximum(m_prev, jnp.max(sr, axis=1, keepdims=True))
                alpha = jnp.exp2(m_prev - m_new)
                p = jnp.exp2(sr - jnp.concatenate([m_new] * (tk // LANES), axis=1))
                m_scr[h] = m_new
                ps.append(p.astype(BF16))
                alphas.append(jnp.concatenate([alpha, alpha], axis=1))
            vg1 = jnp.concatenate([vg, ones_k], axis=1)
            for r in range(rep):
                h = g * rep + r
                acc_scr[h] = acc_scr[h] * alphas[r] + jnp.dot(ps[r], vg1, preferred_element_type=F32)
        return eqc

    lax.fori_loop(0, nkb, p3, jnp.zeros((tq, 1), F32))
    for h in range(H_A):
        acc = acc_scr[h]
        o_ref[0, :, h * DH:(h + 1) * DH] = (acc[:, :DH] / acc[:, DH:]).astype(o_ref.dtype)


def _dsa(qa, qi, wi, ka, va, ki, *, tq, tk, s_valid, q0):
    b, t, _ = qa.shape
    s_pad = ka.shape[1]
    assert s_pad % tk == 0 and t % tq == 0 and tk >= TOPK_MAX
    topk = min(TOPK_MAX, s_valid // 4)
    tri = jnp.asarray(np.triu(np.ones((tk, tk), np.float32), 1), BF16)
    body = functools.partial(_dsa_body, tq=tq, tk=tk, s_valid=s_valid, q0=q0, topk=topk)
    rep = H_A // KV_A
    transposed = tq % LANES == 0
    if transposed:
        wi = jnp.transpose(wi, (0, 2, 1))
        wi_spec = pl.BlockSpec((1, H_I, tq), lambda bi, i: (bi, 0, i))
    else:
        wi_spec = pl.BlockSpec((1, tq, H_I), lambda bi, i: (bi, i, 0))
    return pl.pallas_call(
        body,
        grid=(b, t // tq),
        in_specs=[pl.BlockSpec((1, tq, H_A * DH), lambda bi, i: (bi, i, 0)),
                  pl.BlockSpec((1, tq, H_I * D_IDX), lambda bi, i: (bi, i, 0)),
                  wi_spec,
                  pl.BlockSpec((1, s_pad, KV_A * DH), lambda bi, i: (bi, 0, 0)),
                  pl.BlockSpec((1, s_pad, KV_A * DH), lambda bi, i: (bi, 0, 0)),
                  pl.BlockSpec((1, s_pad, 2 * LANES), lambda bi, i: (bi, 0, 0)),
                  pl.BlockSpec((tk, tk), lambda bi, i: (0, 0))],
        out_specs=pl.BlockSpec((1, tq, H_A * DH), lambda bi, i: (bi, i, 0)),
        out_shape=jax.ShapeDtypeStruct((b, t, H_A * DH), BF16),
        scratch_shapes=[pltpu.VMEM((s_pad, tq) if transposed else (tq, s_pad), jnp.int32),
                        pltpu.VMEM((KV_A, rep * tq, DH), BF16),
                        pltpu.VMEM((H_A, tq, LANES), F32),
                        pltpu.VMEM((H_A, tq, DH + LANES), F32)],
        compiler_params=_cparams(("parallel", "arbitrary")),
        name="dsa",
    )(qa, qi, wi, ka, va, ki, tri)


def _sb_body(q_ref, k_ref, v_ref, low_ref, *rest, tq, tk, s_pad, q0, hpg, nchunk, past_heads):
    if past_heads:
        kp_ref, vp_ref, o_ref = rest
    else:
        (o_ref,) = rest
    i = pl.program_id(2)
    tqc = tq // nchunk
    q_first = q0 + i * tq
    q_last = q_first + tq - 1
    nkb = jnp.minimum((q_last + tk - 1) // tk, s_pad // tk)
    n_full = jnp.minimum(q_first // tk, nkb)
    row = lax.broadcasted_iota(jnp.int32, (tqc, tk), 0)
    col = lax.broadcasted_iota(jnp.int32, (tqc, tk), 1)
    chains = [(g, c) for g in range(hpg) for c in range(nchunk)]

    def step(j, carry, masked):
        k0 = pl.multiple_of(j * tk, tk)

        def load(new_ref, past_ref, g):
            if not past_heads:
                return new_ref[0, pl.ds(k0, tk), g * DH:(g + 1) * DH]
            if masked:
                return new_ref[0, :, g * DH:(g + 1) * DH]
            return past_ref[0, pl.ds(k0 * past_heads + g, tk, stride=past_heads), :].astype(BF16)

        zs = [_dot_nt(q_ref[0, c * tqc:(c + 1) * tqc, g * DH:(g + 1) * DH],
                      load(k_ref, kp_ref if past_heads else None, g)) for g, c in chains]
        lbs, lks, sufs, causals = [], [], [], []
        for (g, c), z in zip(chains, zs):
            lb = jnp.minimum(z, 0.0) - jnp.log(1.0 + jnp.exp2(-jnp.abs(z))) * LOG2E
            lk = lb - z
            causal = None
            if masked:
                causal = k0 + col < q_first + c * tqc + row
                lk = jnp.where(causal, lk, 0.0)
            hi = lk.astype(BF16)
            lo = (lk - hi.astype(F32)).astype(BF16)
            lbs.append(lb)
            lks.append(lk)
            causals.append(causal)
            sufs.append(jnp.dot(jnp.concatenate([hi, lo], axis=1), low_ref[...], preferred_element_type=F32))
        out = []
        for n, (g, c) in enumerate(chains):
            run, acc = carry[2 * n], carry[2 * n + 1]
            wgt = jnp.exp2(lbs[n] + sufs[n] + run)
            if masked:
                wgt = jnp.where(causals[n], wgt, 0.0)
            acc = acc + jnp.dot(wgt.astype(BF16), load(v_ref, vp_ref if past_heads else None, g),
                                preferred_element_type=F32)
            run = run + jnp.sum(lks[n], axis=1, keepdims=True)
            out += [run, acc]
        return tuple(out)

    carry = []
    for _ in chains:
        carry += [jnp.zeros((tqc, 1), F32), jnp.zeros((tqc, DH), F32)]
    carry = tuple(carry)
    carry = lax.fori_loop(0, nkb - n_full, lambda s, c: step(nkb - 1 - s, c, True), carry)
    carry = lax.fori_loop(0, n_full, lambda s, c: step(n_full - 1 - s, c, False), carry)
    for n, (g, c) in enumerate(chains):
        o_ref[0, c * tqc:(c + 1) * tqc, g * DH:(g + 1) * DH] = carry[2 * n + 1].astype(o_ref.dtype)


def _stick_breaking(qb, kb, vb, *, tq, tk, q0, hpg, nchunk, past=None):
    b, t, _ = qb.shape
    low = np.tril(np.ones((tk, tk), np.float32), -1)
    low = jnp.asarray(np.concatenate([low, low], axis=0), BF16)
    wd = hpg * DH
    if past is None:
        s_pad, extra, extra_specs, past_heads = kb.shape[1], [], [], 0
    else:
        assert hpg == H_B and t == tq and q0 % tk == 0 and t <= tk and past[0].shape[1:] == (q0, H_B, DH)
        s_pad, past_heads = q0 + tk, H_B
        kb, vb = (jnp.pad(a, ((0, 0), (0, tk - t), (0, 0))) for a in (kb, vb))
        extra = [c.reshape(b, q0 * H_B, DH) for c in past]
        extra_specs = [pl.BlockSpec((1, q0 * H_B, DH), lambda bi, h, i: (bi, 0, 0))] * 2
    s_new = kb.shape[1]
    assert s_pad % tk == 0 and t % tq == 0 and H_B % hpg == 0 and tq % nchunk == 0
    body = functools.partial(_sb_body, tq=tq, tk=tk, s_pad=s_pad, q0=q0, hpg=hpg, nchunk=nchunk,
                             past_heads=past_heads)
    return pl.pallas_call(
        body,
        grid=(b, H_B // hpg, t // tq),
        in_specs=[pl.BlockSpec((1, tq, wd), lambda bi, h, i: (bi, i, h)),
                  pl.BlockSpec((1, s_new, wd), lambda bi, h, i: (bi, 0, h)),
                  pl.BlockSpec((1, s_new, wd), lambda bi, h, i: (bi, 0, h)),
                  pl.BlockSpec((2 * tk, tk), lambda bi, h, i: (0, 0))] + extra_specs,
        out_specs=pl.BlockSpec((1, tq, wd), lambda bi, h, i: (bi, i, h)),
        out_shape=jax.ShapeDtypeStruct((b, t, H_B * DH), BF16),
        compiler_params=_cparams(("parallel", "parallel", "arbitrary")),
        name="stick_breaking",
    )(qb, kb, vb, low, *extra)


def _mla_body(qn_ref, qr_ref, kv_ref, kr_ref, o_ref, *, tq, tk, s_valid, q0, hpg, nchunk):
    i = pl.program_id(2)
    tqc = tq // nchunk
    q_first = q0 + i * tq
    q_last = q_first + tq - 1
    lim_first = jnp.minimum((q_first // CHUNK + 1) * CHUNK, s_valid)
    lim_last = jnp.minimum((q_last // CHUNK + 1) * CHUNK, s_valid)
    nkb = (lim_last + tk - 1) // tk
    n_full = lim_first // tk
    row = lax.broadcasted_iota(jnp.int32, (tqc, tk), 0)
    col = lax.broadcasted_iota(jnp.int32, (tqc, tk), 1)
    chains = [(g, c) for g in range(hpg) for c in range(nchunk)]
    hw = NOPE + VD

    def step(j, carry, masked):
        k0 = pl.multiple_of(j * tk, tk)
        kr = kr_ref[0, pl.ds(k0, tk), :]
        ss = []
        for g, c in chains:
            rows = slice(c * tqc, (c + 1) * tqc)
            qc = jnp.concatenate([qn_ref[0, rows, g * NOPE:(g + 1) * NOPE],
                                  qr_ref[0, rows, g * LANES:(g + 1) * LANES]], axis=1)
            kc = jnp.concatenate([kv_ref[0, pl.ds(k0, tk), g * hw:g * hw + NOPE], kr], axis=1)
            ss.append(_dot_nt(qc, kc))
        ps, stats = [], []
        for n, (g, c) in enumerate(chains):
            m_prev = carry[2 * n]
            s = ss[n]
            if masked:
                lim = jnp.minimum((((q_first + c * tqc + row) >> CHUNK_SHIFT) + 1) * CHUNK, s_valid)
                s = jnp.where(k0 + col < lim, s, NEG)
            m_new = jnp.maximum(m_prev, jnp.max(s, axis=1, keepdims=True))
            alpha = jnp.exp2(m_prev - m_new)
            p = jnp.exp2(s - jnp.concatenate([m_new] * (tk // LANES), axis=1))
            stats.append((m_new, alpha))
            ps.append(p.astype(BF16))
        out = []
        for n, (g, c) in enumerate(chains):
            m_new, alpha = stats[n]
            vb = jnp.concatenate([kv_ref[0, pl.ds(k0, tk), g * hw + NOPE:(g + 1) * hw], ones_k], axis=1)
            acc = (carry[2 * n + 1] * jnp.concatenate([alpha, alpha], axis=1)
                   + jnp.dot(ps[n], vb, preferred_element_type=F32))
            out += [m_new, acc]
        return tuple(out)

    ones_k = jnp.ones((tk, LANES), BF16)
    carry = []
    for _ in chains:
        carry += [jnp.full((tqc, LANES), NEG, F32), jnp.zeros((tqc, VD + LANES), F32)]
    carry = tuple(carry)
    carry = lax.fori_loop(0, n_full, lambda j, c: step(j, c, False), carry)
    carry = lax.fori_loop(n_full, nkb, lambda j, c: step(j, c, True), carry)
    for n, (g, c) in enumerate(chains):
        acc = carry[2 * n + 1]
        o_ref[0, c * tqc:(c + 1) * tqc, g * VD:(g + 1) * VD] = (acc[:, :VD] / acc[:, VD:]).astype(o_ref.dtype)


def _mla(qn, qr, kv, kr, *, tq, tk, s_valid, q0, hpg, nchunk):
    b, t, _ = qn.shape
    s_pad = kv.shape[1]
    assert s_pad % tk == 0 and t % tq == 0 and H_C % hpg == 0 and tq % nchunk == 0
    body = functools.partial(_mla_body, tq=tq, tk=tk, s_valid=s_valid, q0=q0, hpg=hpg, nchunk=nchunk)
    return pl.pallas_call(
        body,
        grid=(b, H_C // hpg, t // tq),
        in_specs=[pl.BlockSpec((1, tq, hpg * NOPE), lambda bi, h, i: (bi, i, h)),
                  pl.BlockSpec((1, tq, hpg * LANES), lambda bi, h, i: (bi, i, h)),
                  pl.BlockSpec((1, s_pad, hpg * (NOPE + VD)), lambda bi, h, i: (bi, 0, h)),
                  pl.BlockSpec((1, s_pad, LANES), lambda bi, h, i: (bi, 0, 0))],
        out_specs=pl.BlockSpec((1, tq, hpg * VD), lambda bi, h, i: (bi, i, h)),
        out_shape=jax.ShapeDtypeStruct((b, t, H_C * VD), BF16),
        compiler_params=_cparams(("parallel", "parallel", "arbitrary")),
        name="mla",
    )(qn, qr, kv, kr)


def _absorb_q_body(qn_ref, qr_ref, wk_ref, qa_ref, qrh_ref):
    qa_ref[0] = _dot_nt(qn_ref[...], wk_ref[...]).astype(qa_ref.dtype)
    qrh_ref[0] = qr_ref[...]


def _absorb_q(qn, qr, w_ukv):
    m = qn.shape[0]
    return pl.pallas_call(
        _absorb_q_body,
        grid=(H_C,),
        in_specs=[pl.BlockSpec((m, NOPE), lambda h: (0, h)),
                  pl.BlockSpec((m, LANES), lambda h: (0, h)),
                  pl.BlockSpec((KV_RANK, NOPE), lambda h: (0, 2 * h))],
        out_specs=[pl.BlockSpec((1, m, KV_RANK), lambda h: (h, 0, 0)),
                   pl.BlockSpec((1, m, LANES), lambda h: (h, 0, 0))],
        out_shape=[jax.ShapeDtypeStruct((H_C, m, KV_RANK), BF16),
                   jax.ShapeDtypeStruct((H_C, m, LANES), BF16)],
        compiler_params=_cparams(("parallel",)),
        name="mla_absorb_q",
    )(qn, qr, w_ukv)


def _mla_latent_body(qa_ref, qr_ref, c_ref, kr_ref, cp_ref, krp_ref, o_ref, *, t, tk, s_valid, q0, nsplit):
    hs = H_C // nsplit
    rows = hs * t
    lim_first = min((q0 // CHUNK + 1) * CHUNK, s_valid)
    lim_last = min(((q0 + t - 1) // CHUNK + 1) * CHUNK, s_valid)
    nkb = -(-lim_last // tk)
    n_full = lim_first // tk
    assert n_full == q0 // tk and nkb == n_full + 1
    q_pos = q0 + lax.broadcasted_iota(jnp.int32, (hs, t, tk), 1).reshape(rows, tk)
    lim = jnp.minimum(((q_pos >> CHUNK_SHIFT) + 1) * CHUNK, s_valid)
    col = lax.broadcasted_iota(jnp.int32, (rows, tk), 1)
    ones_k = jnp.ones((tk, LANES), BF16)
    qs = [jnp.concatenate([qa_ref[n * hs:(n + 1) * hs].reshape(rows, KV_RANK),
                           qr_ref[n * hs:(n + 1) * hs].reshape(rows, LANES)], axis=1) for n in range(nsplit)]

    def step(j, carry, masked):
        k0 = pl.multiple_of(j * tk, tk)
        if masked:
            cb, krb = c_ref[0], kr_ref[0]
        else:
            cb = cp_ref[0, pl.ds(k0, tk), :].astype(BF16)
            krb = jnp.concatenate([krp_ref[0, pl.ds(k0, tk), :].astype(BF16),
                                   jnp.zeros((tk, LANES - ROPE_C), BF16)], axis=1)
        kc = jnp.concatenate([cb, krb], axis=1)
        vc = jnp.concatenate([cb, ones_k], axis=1)
        ss = [_dot_nt(q, kc) for q in qs]
        ps, stats = [], []
        for n in range(nsplit):
            s = ss[n]
            if masked:
                s = jnp.where(k0 + col < lim, s, NEG)
            m_prev = carry[2 * n]
            m_new = jnp.maximum(m_prev, jnp.max(s, axis=1, keepdims=True))
            alpha = jnp.exp2(m_prev - m_new)
            ps.append(jnp.exp2(s - jnp.concatenate([m_new] * (tk // LANES), axis=1)).astype(BF16))
            stats.append((m_new, alpha))
        out = []
        for n in range(nsplit):
            m_new, alpha = stats[n]
            acc = (carry[2 * n + 1] * jnp.concatenate([alpha] * (KV_RANK // LANES + 1), axis=1)
                   + jnp.dot(ps[n], vc, preferred_element_type=F32))
            out += [m_new, acc]
        return tuple(out)

    carry = []
    for _ in range(nsplit):
        carry += [jnp.full((rows, LANES), NEG, F32), jnp.zeros((rows, KV_RANK + LANES), F32)]
    carry = tuple(carry)
    carry = lax.fori_loop(0, n_full, lambda j, c: step(j, c, False), carry)
    carry = lax.fori_loop(n_full, nkb, lambda j, c: step(j, c, True), carry)
    for n in range(nsplit):
        acc = carry[2 * n + 1]
        lat = acc[:, :KV_RANK] / jnp.concatenate([acc[:, KV_RANK:]] * (KV_RANK // LANES), axis=1)
        o_ref[0, n * hs:(n + 1) * hs] = lat.reshape(hs, t, KV_RANK).astype(o_ref.dtype)


def _mla_latent(qa_hm, qr_hm, c_new, kr_new, c_past, kr_past, *, tk, s_valid, q0):
    b, t, _ = c_new.shape
    assert q0 % tk == 0 and t <= tk and c_past.shape[1] == q0
    c_new, kr_new = (jnp.pad(a, ((0, 0), (0, tk - t), (0, 0))) for a in (c_new, kr_new))
    body = functools.partial(_mla_latent_body, t=t, tk=tk, s_valid=s_valid, q0=q0, nsplit=2)
    return pl.pallas_call(
        body,
        grid=(b,),
        in_specs=[pl.BlockSpec((H_C, t, KV_RANK), lambda bi: (0, bi, 0)),
                  pl.BlockSpec((H_C, t, LANES), lambda bi: (0, bi, 0)),
                  pl.BlockSpec((1, tk, KV_RANK), lambda bi: (bi, 0, 0)),
                  pl.BlockSpec((1, tk, LANES), lambda bi: (bi, 0, 0)),
                  pl.BlockSpec((1, q0, KV_RANK), lambda bi: (bi, 0, 0)),
                  pl.BlockSpec((1, q0, ROPE_C), lambda bi: (bi, 0, 0))],
        out_specs=pl.BlockSpec((1, H_C, t, KV_RANK), lambda bi: (bi, 0, 0, 0)),
        out_shape=jax.ShapeDtypeStruct((b, H_C, t, KV_RANK), BF16),
        compiler_params=_cparams(("parallel",)),
        name="mla_latent",
    )(qa_hm, qr_hm, c_new, kr_new, c_past, kr_past)


def _latent_out_body(x_ref, wv_ref, o_ref):
    nb, _, t, r = x_ref.shape
    o_ref[...] = jnp.dot(x_ref[...].reshape(nb * t, r), wv_ref[...],
                         preferred_element_type=F32).astype(o_ref.dtype)


def _latent_out(lat, w_ukv):
    b, _, t, _ = lat.shape
    return pl.pallas_call(
        _latent_out_body,
        grid=(H_C,),
        in_specs=[pl.BlockSpec((b, 1, t, KV_RANK), lambda h: (0, h, 0, 0)),
                  pl.BlockSpec((KV_RANK, VD), lambda h: (0, 2 * h + 1))],
        out_specs=pl.BlockSpec((b * t, VD), lambda h: (0, h)),
        out_shape=jax.ShapeDtypeStruct((b * t, H_C * VD), BF16),
        compiler_params=_cparams(("parallel",)),
        name="mla_latent_out",
    )(lat, w_ukv)


def _ffn_body(x_ref, g_ref, wg_ref, wv_ref, cwg_ref, cwv_ref, cbg_ref, cbv_ref, wd_ref, pg_ref, pv_ref, gn_ref,
              o_ref, hn_ref, sg_ref, sv_ref, h_scr, acc_scr, ug_buf, uv_buf, cg_scr, cv_scr, *, seg, carried):
    i = pl.program_id(0)
    f = pl.program_id(1)
    tm = x_ref.shape[0]

    @pl.when(f == 0)
    def _():
        x = x_ref[...]
        y = x * lax.rsqrt(jnp.mean(x * x, axis=-1, keepdims=True) + EPS)
        h_scr[...] = (y * g_ref[...]).astype(BF16)
        acc_scr[...] = x

    h = h_scr[...]
    halves = ((wg_ref, cwg_ref, cbg_ref, pg_ref, sg_ref, ug_buf, cg_scr),
              (wv_ref, cwv_ref, cbv_ref, pv_ref, sv_ref, uv_buf, cv_scr))
    ys = []
    for w_ref, cw_ref, cb_ref, p_ref, s_ref, buf, c_scr in halves:
        u = jnp.dot(h, w_ref[...], preferred_element_type=F32)
        cw = cw_ref[...]
        parts = []
        for sgi in range(tm // seg):
            us = u[sgi * seg:(sgi + 1) * seg]
            if carried:
                head = jnp.where(i == 0, p_ref[0], c_scr[f])
            else:
                head = p_ref[sgi]
            buf[sgi, 0:SUBLANES] = head
            buf[sgi, SUBLANES:SUBLANES + seg] = us
            y = (cw[2:3] * us + cw[1:2] * buf[sgi, SUBLANES - 1:SUBLANES - 1 + seg]
                 + cw[0:1] * buf[sgi, SUBLANES - 2:SUBLANES - 2 + seg] + cb_ref[...])
            parts.append(y)
            last = us[seg - SUBLANES:seg]
            s_ref[sgi] = last
            if carried:
                c_scr[f] = last
        ys.append(parts[0] if len(parts) == 1 else jnp.concatenate(parts, axis=0))
    yg, yv = ys
    act = (yg / (1.0 + jnp.exp(-yg))) * yv
    acc_scr[...] += jnp.dot(act.astype(BF16), wd_ref[...], preferred_element_type=F32)

    @pl.when(f == pl.num_programs(1) - 1)
    def _():
        a = acc_scr[...]
        o_ref[...] = a
        y = a * lax.rsqrt(jnp.mean(a * a, axis=-1, keepdims=True) + EPS)
        hn_ref[...] = (y * gn_ref[...]).astype(hn_ref.dtype)


def _ffn(x, g, w_up, conv_w, conv_b, w_down, prev8, g_next, next_dtype, *, seg, carried, tm, tf):
    m, d = x.shape
    nf = D_FF // tf
    nseg = tm // seg
    assert m % tm == 0 and D_FF % tf == 0 and tm % seg == 0
    sidx = (lambda i: 0) if carried else (lambda i: i)
    body = functools.partial(_ffn_body, seg=seg, carried=carried)
    cb = conv_b.reshape(1, 2 * D_FF)
    return pl.pallas_call(
        body,
        grid=(m // tm, nf),
        in_specs=[pl.BlockSpec((tm, d), lambda i, f: (i, 0)),
                  pl.BlockSpec((1, d), lambda i, f: (0, 0)),
                  pl.BlockSpec((d, tf), lambda i, f: (0, f)),
                  pl.BlockSpec((d, tf), lambda i, f: (0, nf + f)),
                  pl.BlockSpec((CONV_W, tf), lambda i, f: (0, f)),
                  pl.BlockSpec((CONV_W, tf), lambda i, f: (0, nf + f)),
                  pl.BlockSpec((1, tf), lambda i, f: (0, f)),
                  pl.BlockSpec((1, tf), lambda i, f: (0, nf + f)),
                  pl.BlockSpec((tf, d), lambda i, f: (f, 0)),
                  pl.BlockSpec((nseg, SUBLANES, tf), lambda i, f: (sidx(i), 0, f)),
                  pl.BlockSpec((nseg, SUBLANES, tf), lambda i, f: (sidx(i), 0, nf + f)),
                  pl.BlockSpec((1, d), lambda i, f: (0, 0))],
        out_specs=[pl.BlockSpec((tm, d), lambda i, f: (i, 0)),
                   pl.BlockSpec((tm, d), lambda i, f: (i, 0)),
                   pl.BlockSpec((nseg, SUBLANES, tf), lambda i, f: (i, 0, f)),
                   pl.BlockSpec((nseg, SUBLANES, tf), lambda i, f: (i, 0, f))],
        out_shape=[jax.ShapeDtypeStruct((m, d), F32),
                   jax.ShapeDtypeStruct((m, d), next_dtype),
                   jax.ShapeDtypeStruct((m // seg, SUBLANES, D_FF), F32),
                   jax.ShapeDtypeStruct((m // seg, SUBLANES, D_FF), F32)],
        scratch_shapes=[pltpu.VMEM((tm, d), BF16),
                        pltpu.VMEM((tm, d), F32),
                        pltpu.VMEM((nseg, SUBLANES + seg, tf), F32),
                        pltpu.VMEM((nseg, SUBLANES + seg, tf), F32),
                        pltpu.VMEM((nf, SUBLANES, tf), F32),
                        pltpu.VMEM((nf, SUBLANES, tf), F32)],
        compiler_params=_cparams(("arbitrary", "arbitrary")),
        name="conv_ffn",
    )(x, g.reshape(1, d), w_up, w_up, conv_w, conv_w, cb, cb, w_down, prev8, prev8, g_next.reshape(1, d))


def _rope_tables(pos, head_dim, rot, lanes_valid=LANES):
    half = rot // 2
    freqs = ROPE_THETA ** (-jnp.arange(half, dtype=F32) / half)
    ang = pos.astype(F32)[:, None] * freqs[None, :]
    cos, sin = jnp.cos(ang), jnp.sin(ang)
    lane = np.arange(LANES)
    d = lane % head_dim
    first = (d < half) & (lane < lanes_valid)
    second = (d >= half) & (d < rot) & (lane < lanes_valid)
    fidx = np.clip(np.where(d < half, d, d - half), 0, half - 1)
    cosl, sinl = cos[:, fidx], sin[:, fidx]
    c = jnp.where(first | second, cosl, 1.0)
    s1 = jnp.where(second, sinl, 0.0)
    s2 = jnp.where(first, -sinl, 0.0)
    return c, s1, s2, half


def _prep_weights(w_in_ab, w_out_ab, w_dqkv, w_uq, w_ukv, w_o_mla, w_up, w_down):
    n_even, n_odd = w_in_ab.shape[0], w_dqkv.shape[0]
    w = {}
    n_a = H_A * DH + 2 * KV_A * DH + H_I * D_IDX
    w["in_a"] = w_in_ab[:, :, :n_a].astype(BF16)
    pad = jnp.zeros((n_even, D_MODEL, LANES - D_IDX - H_I), BF16)
    w["in_b"] = jnp.concatenate([w_in_ab[:, :, n_a:n_a + D_IDX + H_I].astype(BF16), pad,
                                 w_in_ab[:, :, n_a + D_IDX + H_I:].astype(BF16)], axis=-1)
    w["out_ab"] = w_out_ab.astype(BF16)
    w["dqkr"] = jnp.concatenate([w_dqkv.astype(BF16),
                                 jnp.zeros((n_odd, D_MODEL, LANES - ROPE_C), BF16)], axis=-1)
    uq = w_uq.astype(BF16).reshape(n_odd, Q_RANK, H_C, NOPE + ROPE_C)
    uqn = uq[..., :NOPE].reshape(n_odd, Q_RANK, H_C * NOPE)
    uqr = uq[..., NOPE:]
    uqr = jnp.concatenate([uqr, jnp.zeros_like(uqr)], axis=-1).reshape(n_odd, Q_RANK, H_C * LANES)
    w["uq"] = jnp.concatenate([uqn, uqr], axis=-1)
    w["ukv"] = w_ukv.astype(BF16)
    w["o_mla"] = w_o_mla.astype(BF16)
    w["up"] = w_up.astype(BF16)
    w["down"] = w_down.astype(BF16)
    return w


def _with_past(past, new, s_pad):
    b, _, fdim = new.shape
    parts = [new.astype(BF16)] if past is None else [past.astype(BF16), new.astype(BF16)]
    n = sum(p.shape[1] for p in parts)
    if s_pad > n:
        parts.append(jnp.zeros((b, s_pad - n, fdim), BF16))
    return parts[0] if len(parts) == 1 else jnp.concatenate(parts, axis=1)


def _trunk(x, past, conv_state, w, p, cfg):
    b, t, d = x.shape
    m = b * t
    q0 = 0 if past is None else past[0].shape[2]
    s_valid = q0 + t
    tk = cfg["tk"]
    s_pad = -(-s_valid // tk) * tk
    pos = q0 + jnp.tile(jnp.arange(t), b)
    tab_a = _rope_tables(pos, DH, ROT_A)
    tab_i = _rope_tables(pos, D_IDX, ROT_I)
    tab_ki = _rope_tables(pos, D_IDX, ROT_I, lanes_valid=D_IDX)
    tab_kr = _rope_tables(pos, ROPE_C, ROPE_C, lanes_valid=ROPE_C)

    xf = x.reshape(m, d)
    ab_rows, mla_rows, conv_rows = [], [], []
    sc_ab = DH ** -0.5 * LOG2E
    sc_c = (NOPE + ROPE_C) ** -0.5 * LOG2E
    h = _rmsnorm(xf, p["norm_mix"][0], BF16)
    for l in range(DEPTH):
        if l % 2 == 0:
            e = l // 2
            lp = None if past is None else tuple(c[e] for c in past[:5])
            qa, ka32, ka16, va32, va16, qi = _proj(
                [h], w["in_a"][e],
                [dict(width=H_A * DH, dtypes=(BF16,), rope=0, scale=sc_ab),
                 dict(width=KV_A * DH, dtypes=(F32, BF16), rope=0, split_heads=True),
                 dict(width=KV_A * DH, dtypes=(F32, BF16), split_heads=True),
                 dict(width=H_I * D_IDX, dtypes=(BF16,), rope=1)],
                tabs=(tab_a, tab_i), name="proj_in_a")
            kiwi, qb, kb32, kb16, vb32, vb16 = _proj(
                [h], w["in_b"][e],
                [dict(width=LANES, dtypes=(F32,), rope=0),
                 dict(width=H_B * DH, dtypes=(BF16,), scale=sc_ab),
                 dict(width=H_B * DH, dtypes=(F32, BF16), split_heads=True),
                 dict(width=H_B * DH, dtypes=(F32, BF16), split_heads=True)],
                tabs=(tab_ki,), name="proj_in_b")
            ki32 = kiwi[:, :D_IDX]
            wi = kiwi[:, D_IDX:D_IDX + H_I]
            ab_rows.append((ka32.reshape(b, t, KV_A, DH), va32.reshape(b, t, KV_A, DH),
                            ki32.reshape(b, t, D_IDX), kb32.reshape(b, t, H_B, DH),
                            vb32.reshape(b, t, H_B, DH)))

            def full(idx, new16):
                pst = None if lp is None else lp[idx].reshape(b, q0, -1)
                return _with_past(pst, new16.reshape(b, t, -1), s_pad)

            ki_all = full(2, ki32)
            zk = jnp.zeros_like(ki_all)
            ki2 = jnp.concatenate([ki_all, zk, zk, ki_all], axis=-1)
            oa = _dsa(qa.reshape(b, t, -1), qi.reshape(b, t, -1), wi.reshape(b, t, H_I), full(0, ka16),
                      full(1, va16), ki2, tq=cfg["tq_a"], tk=tk, s_valid=s_valid, q0=q0)
            if lp is None:
                ob = _stick_breaking(qb.reshape(b, t, -1), full(3, kb16), full(4, vb16), tq=cfg["tq_b"],
                                     tk=cfg["tk_b"], q0=q0, hpg=cfg["hpg_b"], nchunk=cfg["nc_b"])
            else:
                ob = _stick_breaking(qb.reshape(b, t, -1), kb16.reshape(b, t, -1), vb16.reshape(b, t, -1),
                                     tq=cfg["tq_b"], tk=cfg["tk_b"], q0=q0, hpg=cfg["hpg_b"],
                                     nchunk=cfg["nc_b"], past=(lp[3], lp[4]))
            (xf,) = _proj([oa.reshape(m, -1), ob.reshape(m, -1)], w["out_ab"][e],
                          [dict(width=D_MODEL, dtypes=(F32,))], resid=xf, name="proj_out_ab")
        else:
            od = l // 2
            gains = jnp.concatenate([p["g_q"][od], p["g_kv"][od]])
            cq16, ckv32, ckv16, kr32w, kr16w = _proj(
                [h], w["dqkr"][od],
                [dict(width=Q_RANK, dtypes=(BF16,), gain_off=0),
                 dict(width=KV_RANK, dtypes=(F32, BF16), gain_off=Q_RANK),
                 dict(width=LANES, dtypes=(F32, BF16), rope=0)],
                tabs=(tab_kr,), gain=gains, name="proj_dqkr")
            mla_rows.append((ckv32.reshape(b, t, KV_RANK), kr32w[:, :ROPE_C].reshape(b, t, ROPE_C)))
            qn, qr = _proj([cq16], w["uq"][od],
                           [dict(width=H_C * NOPE, dtypes=(BF16,), scale=sc_c),
                            dict(width=H_C * LANES, dtypes=(BF16,), rope=0, scale=sc_c)],
                           tabs=(tab_kr,), name="proj_uq")
            if cfg["latent_c"]:
                qa_hm, qr_hm = _absorb_q(qn, qr, w["ukv"][od])
                lat = _mla_latent(qa_hm, qr_hm, ckv16.reshape(b, t, KV_RANK), kr16w.reshape(b, t, LANES),
                                  past[5][od], past[6][od], tk=cfg["tk_c"], s_valid=s_valid, q0=q0)
                o = _latent_out(lat, w["ukv"][od])
            else:
                pc = None if past is None else past[5][od]
                pk = None if past is None else jnp.pad(past[6][od], ((0, 0), (0, 0), (0, LANES - ROPE_C)))
                c_all = _with_past(pc, ckv16.reshape(b, t, KV_RANK), s_pad)
                kr_all = _with_past(pk, kr16w.reshape(b, t, LANES), s_pad)
                (kv,) = _proj([c_all.reshape(b * s_pad, KV_RANK)], w["ukv"][od],
                              [dict(width=H_C * (NOPE + VD), dtypes=(BF16,))], name="proj_ukv")
                o = _mla(qn.reshape(b, t, -1), qr.reshape(b, t, -1), kv.reshape(b, s_pad, -1), kr_all,
                         tq=cfg["tq_c"], tk=cfg["tk_c"], s_valid=s_valid, q0=q0, hpg=cfg["hpg_c"],
                         nchunk=cfg["nc_c"])
            (xf,) = _proj([o.reshape(m, -1)], w["o_mla"][od], [dict(width=D_MODEL, dtypes=(F32,))],
                          resid=xf, name="proj_o_mla")

        if conv_state is None:
            prev8 = jnp.zeros((b, SUBLANES, 2 * D_FF), F32)
        else:
            prev8 = jnp.concatenate([jnp.zeros((b, SUBLANES - (CONV_W - 1), 2 * D_FF), F32),
                                     conv_state[l]], axis=1)
        final = l == DEPTH - 1
        g_next = p["norm_final"] if final else p["norm_mix"][l + 1]
        xf, h, sg, sv = _ffn(xf, p["norm_ffn"][l], w["up"][l], p["conv_w"][l], p["conv_b"][l], w["down"][l],
                             prev8, g_next, F32 if final else BF16, seg=cfg["seg"], carried=cfg["carried"],
                             tm=cfg["tm_ffn"], tf=cfg["tf"])
        last = [s8.reshape(b, -1, SUBLANES, D_FF)[:, -1, SUBLANES - (CONV_W - 1):] for s8 in (sg, sv)]
        conv_rows.append(jnp.concatenate(last, axis=-1))

    y = h.reshape(b, t, d)
    new_ab = [jnp.stack([r[i] for r in ab_rows]) for i in range(5)]
    new_mla = [jnp.stack([r[i] for r in mla_rows]) for i in range(2)]
    return y, new_ab + new_mla + [jnp.stack(conv_rows)]


def kernel(x_prompt, x_sample, cache_k_a, cache_v_a, cache_idx_k, cache_k_b, cache_v_b, cache_ckv, cache_krope, state_conv, norm_mix, norm_ffn, norm_final, w_in_ab, w_out_ab, w_dqkv, g_q, g_kv, w_uq, w_ukv, w_o_mla, w_up, conv_w, conv_b, w_down):
    w = _prep_weights(w_in_ab, w_out_ab, w_dqkv, w_uq, w_ukv, w_o_mla, w_up, w_down)
    p = dict(norm_mix=norm_mix, norm_ffn=norm_ffn, norm_final=norm_final, g_q=g_q, g_kv=g_kv,
             conv_w=conv_w, conv_b=conv_b)
    cfg_p = dict(tk=512, tq_a=256, tq_b=256, tk_b=256, hpg_b=4, nc_b=1, tq_c=512, tk_c=1024, hpg_c=2, nc_c=2,
                 latent_c=False, seg=512, carried=True, tm_ffn=512, tf=512)
    t_s = x_sample.shape[1]
    cfg_s = dict(tk=256, tq_a=t_s, tq_b=t_s, tk_b=256, hpg_b=H_B, nc_b=1, tq_c=t_s, tk_c=256, hpg_c=8, nc_c=1,
                 latent_c=True, seg=t_s, carried=False, tm_ffn=x_sample.shape[0] * t_s, tf=512)
    y_p, st_p = _trunk(x_prompt, None, None, w, p, cfg_p)
    past = (cache_k_a, cache_v_a, cache_idx_k, cache_k_b, cache_v_b, cache_ckv, cache_krope)
    y_s, st_s = _trunk(x_sample, past, state_conv, w, p, cfg_s)
    return (y_p, y_s, *st_p, *st_s)
```

```python
import functools

import numpy as np
import jax
import jax.numpy as jnp
from jax import lax
from jax.experimental import pallas as pl
from jax.experimental.pallas import tpu as pltpu

F32 = jnp.float32
BF16 = jnp.bfloat16

D_MODEL = 2048
DEPTH = 4
CHUNK = 64
CHUNK_SHIFT = CHUNK.bit_length() - 1
ROPE_THETA = 500000.0
EPS = 1e-6
H_A, KV_A, DH = 8, 2, 128
ROT_A = DH // 4
H_I, D_IDX = 16, 64
ROT_I = D_IDX // 4
TOPK_MAX = 256
H_B = 8
H_C, Q_RANK, KV_RANK, NOPE, ROPE_C, VD = 16, 512, 512, 128, 64, 128
D_FF = 5632
CONV_W = 3

LANES = 128
SUBLANES = 8
VMEM_LIMIT = 56 * 1024 * 1024
NEG = -1e30
INT_MIN = np.int32(-2 ** 31)
INT_MAX = np.int32(2 ** 31 - 1)
LOG2E = float(np.log2(np.e))


def _cparams(sem):
    return pltpu.CompilerParams(dimension_semantics=sem, vmem_limit_bytes=VMEM_LIMIT)


def _dot_nt(a, b):
    return lax.dot_general(a, b, (((1,), (1,)), ((), ())), preferred_element_type=F32)


def _rms_body(x_ref, g_ref, o_ref):
    x = x_ref[...]
    y = x * lax.rsqrt(jnp.mean(x * x, axis=-1, keepdims=True) + EPS)
    o_ref[...] = (y * g_ref[...]).astype(o_ref.dtype)


def _rmsnorm(x, g, out_dtype):
    m, d = x.shape
    tm = min(m, 512)
    return pl.pallas_call(
        _rms_body,
        grid=(m // tm,),
        in_specs=[pl.BlockSpec((tm, d), lambda i: (i, 0)),
                  pl.BlockSpec((1, d), lambda i: (0, 0))],
        out_specs=pl.BlockSpec((tm, d), lambda i: (i, 0)),
        out_shape=jax.ShapeDtypeStruct((m, d), out_dtype),
        compiler_params=_cparams(("parallel",)),
        name="rmsnorm",
    )(x, g.reshape(1, d))


PROJ_CHUNK = 512
COUNT_ROWS = 64


def _proj_body(*refs, nx, groups, tab_halves, has_gain, has_resid):
    it = iter(refs)
    x_refs = [next(it) for _ in range(nx)]
    w_ref = next(it)
    tabs = [(next(it), next(it), next(it)) for _ in tab_halves]
    g_ref = next(it) if has_gain else None
    r_ref = next(it) if has_resid else None
    outs = list(it)

    xs = [r[...] for r in x_refs]
    x = xs[0] if nx == 1 else jnp.concatenate(xs, axis=1)
    accs = []
    c0 = 0
    for grp in groups:
        width = grp["width"]
        chunk = min(width, PROJ_CHUNK)
        for cc in range(0, width, chunk):
            accs.append(jnp.dot(x, w_ref[:, c0 + cc:c0 + cc + chunk], preferred_element_type=F32))
        c0 += width
    accs = iter(accs)
    oi = 0
    for grp in groups:
        width = grp["width"]
        chunk = min(width, PROJ_CHUNK)
        for cc in range(0, width, chunk):
            acc = next(accs)
            if grp.get("gain_off") is not None:
                assert chunk == width
                go = grp["gain_off"]
                acc = (acc * lax.rsqrt(jnp.mean(acc * acc, axis=-1, keepdims=True) + EPS)
                       * g_ref[:, go:go + width])
            if grp.get("rope") is not None:
                c_ref, s1_ref, s2_ref = tabs[grp["rope"]]
                half = tab_halves[grp["rope"]]
                c, s1, s2 = c_ref[...], s1_ref[...], s2_ref[...]
                parts = []
                for gi in range(chunk // LANES):
                    xg = acc[:, gi * LANES:(gi + 1) * LANES]
                    parts.append(xg * c + pltpu.roll(xg, half, 1) * s1
                                 + pltpu.roll(xg, LANES - half, 1) * s2)
                acc = parts[0] if len(parts) == 1 else jnp.concatenate(parts, axis=1)
            if grp.get("scale") is not None:
                acc = acc * grp["scale"]
            if has_resid:
                acc = acc + r_ref[:, cc:cc + chunk]
            for k in range(len(grp["dtypes"])):
                o = outs[oi + k]
                if len(o.shape) == 3:
                    for gi in range(chunk // LANES):
                        o[:, cc // LANES + gi, :] = acc[:, gi * LANES:(gi + 1) * LANES].astype(o.dtype)
                else:
                    o[:, cc:cc + chunk] = acc.astype(o.dtype)
        oi += len(grp["dtypes"])


def _proj(xs, w, groups, *, tabs=(), gain=None, resid=None, name="proj"):
    m = xs[0].shape[0]
    k, n = w.shape
    tm = min(m, 512)
    assert m % tm == 0 and n == sum(g["width"] for g in groups) and k == sum(x.shape[1] for x in xs)
    assert resid is None or len(groups) == 1
    in_specs = [pl.BlockSpec((tm, x.shape[1]), lambda i: (i, 0)) for x in xs]
    in_specs.append(pl.BlockSpec((k, n), lambda i: (0, 0)))
    args = list(xs) + [w]
    for c, s1, s2, _ in tabs:
        for t in (c, s1, s2):
            in_specs.append(pl.BlockSpec((tm, LANES), lambda i: (i, 0)))
            args.append(t)
    if gain is not None:
        in_specs.append(pl.BlockSpec((1, gain.shape[0]), lambda i: (0, 0)))
        args.append(gain.reshape(1, -1))
    if resid is not None:
        in_specs.append(pl.BlockSpec((tm, n), lambda i: (i, 0)))
        args.append(resid)
    out_specs, out_shape = [], []
    for g in groups:
        for dt in g["dtypes"]:
            if g.get("split_heads") and dt == F32:
                nh = g["width"] // LANES
                out_specs.append(pl.BlockSpec((tm, nh, LANES), lambda i: (i, 0, 0)))
                out_shape.append(jax.ShapeDtypeStruct((m, nh, LANES), dt))
            else:
                out_specs.append(pl.BlockSpec((tm, g["width"]), lambda i: (i, 0)))
                out_shape.append(jax.ShapeDtypeStruct((m, g["width"]), dt))
    body = functools.partial(_proj_body, nx=len(xs), groups=groups, tab_halves=[t[3] for t in tabs],
                             has_gain=gain is not None, has_resid=resid is not None)
    return pl.pallas_call(
        body,
        grid=(m // tm,),
        in_specs=in_specs,
        out_specs=out_specs,
        out_shape=out_shape,
        compiler_params=_cparams(("parallel",)),
        name=name,
    )(*args)


def _dsa_body(qa_ref, qi_ref, wi_ref, ka_ref, va_ref, ki_ref, tri_ref, o_ref,
              key_scr, qs_scr, m_scr, acc_scr, *, tq, tk, s_valid, q0, topk):
    i = pl.program_id(1)
    q_first = q0 + i * tq
    q_last = q_first + tq - 1
    adm_end = jnp.minimum((q_last // CHUNK + 1) * CHUNK, s_valid)
    nkb = (adm_end + tk - 1) // tk
    q_pos = q_first + lax.broadcasted_iota(jnp.int32, (tq, tk), 0)
    q_lim = jnp.minimum(((q_pos >> CHUNK_SHIFT) + 1) * CHUNK, s_valid)
    col = lax.broadcasted_iota(jnp.int32, (tq, tk), 1)
    rep = H_A // KV_A

    w = wi_ref[0] * ((H_I * D_IDX) ** -0.5)
    pg = 4
    pairs = LANES // D_IDX
    qi = qi_ref[0]
    q_groups = [jnp.concatenate([qi[:, (g * pg + pp) * LANES:(g * pg + pp + 1) * LANES]
                                 for pp in range(pg)], axis=0) for g in range(H_I // (pairs * pg))]

    def to_key(x):
        bits = pltpu.bitcast(x, jnp.int32)
        return bits ^ ((bits >> 31) & INT_MAX)

    transposed = key_scr.shape[1] == tq

    if transposed:
        wt = w
        krow = lax.broadcasted_iota(jnp.int32, (tk, tq), 0)
        q_lim_t = jnp.minimum((((q_first + lax.broadcasted_iota(jnp.int32, (tk, tq), 1)) >> CHUNK_SHIFT) + 1) * CHUNK,
                              s_valid)

    def p1(j, c):
        k0 = pl.multiple_of(j * tk, tk)
        if transposed:
            sc = jnp.zeros((tk, tq), F32)
            for g, qg in enumerate(q_groups):
                for half in range(pairs):
                    rel = _dot_nt(ki_ref[0, pl.ds(k0, tk), half * LANES:(half + 1) * LANES], qg)
                    for pp in range(pg):
                        h = (g * pg + pp) * pairs + half
                        sc = sc + jnp.maximum(rel[:, pp * tq:(pp + 1) * tq], 0.0) * wt[h:h + 1, :]
            key_scr[pl.ds(k0, tk), :] = jnp.where(k0 + krow < q_lim_t, to_key(sc), INT_MIN)
            return c
        sc = jnp.zeros((tq, tk), F32)
        for g, qg in enumerate(q_groups):
            for half in range(pairs):
                rel = _dot_nt(qg, ki_ref[0, pl.ds(k0, tk), half * LANES:(half + 1) * LANES])
                for pp in range(pg):
                    h = (g * pg + pp) * pairs + half
                    sc = sc + jnp.maximum(rel[pp * tq:(pp + 1) * tq], 0.0) * w[:, h:h + 1]
        key_scr[:, pl.ds(k0, tk)] = jnp.where(k0 + col < q_lim, to_key(sc), INT_MIN)
        return c

    lax.fori_loop(0, nkb, p1, 0)

    def count_ge(v):
        def cb(j, c):
            k0 = pl.multiple_of(j * tk, tk)
            if transposed:
                ge = jnp.where(key_scr[pl.ds(k0, tk), :] >= v, 1.0, 0.0)
                return c + jnp.sum(ge.reshape(tk // COUNT_ROWS, COUNT_ROWS, tq), axis=0)
            ge = jnp.where(key_scr[:, pl.ds(k0, tk)] >= v, 1.0, 0.0)
            part = ge[:, 0:LANES]
            for cc in range(1, tk // LANES):
                part = part + ge[:, cc * LANES:(cc + 1) * LANES]
            return c + part
        if transposed:
            c = lax.fori_loop(0, nkb, cb, jnp.zeros((COUNT_ROWS, tq), F32))
            return jnp.sum(c, axis=0, keepdims=True)
        c = lax.fori_loop(0, nkb, cb, jnp.zeros((tq, LANES), F32))
        return jnp.sum(c, axis=1, keepdims=True)

    def bis(_, lohi):
        lo, hi = lohi
        mid = (lo >> 1) + (hi >> 1) + (lo & hi & 1)
        ok = count_ge(mid) >= float(topk)
        return jnp.where(ok, mid, lo), jnp.where(ok, hi, mid)

    vshape = (1, tq) if transposed else (tq, 1)
    thr, _ = lax.fori_loop(0, 32, bis, (jnp.full(vshape, INT_MIN, jnp.int32),
                                        jnp.full(vshape, INT_MAX, jnp.int32)))
    need = float(topk) - count_ge(thr + 1)
    if transposed:
        thr = pltpu.bitcast(jnp.transpose(pltpu.bitcast(jnp.broadcast_to(thr, (LANES, tq)), F32)), jnp.int32)
        need = jnp.transpose(jnp.broadcast_to(need, (LANES, tq)))
        thr = jnp.concatenate([thr] * (tk // LANES), axis=1)
        need = jnp.concatenate([need] * (tk // LANES), axis=1)

    qa = qa_ref[0]
    for g in range(KV_A):
        qs_scr[g] = jnp.concatenate(
            [qa[:, (g * rep + r) * DH:(g * rep + r + 1) * DH] for r in range(rep)], axis=0)
    m_scr[...] = jnp.full(m_scr.shape, NEG, F32)
    ones_k = jnp.ones((tk, LANES), BF16)
    acc_scr[...] = jnp.zeros(acc_scr.shape, F32)

    def p3(j, eqc):
        k0 = pl.multiple_of(j * tk, tk)
        if transposed:
            key = pltpu.bitcast(jnp.transpose(pltpu.bitcast(key_scr[pl.ds(k0, tk), :], F32)), jnp.int32)
        else:
            key = key_scr[:, pl.ds(k0, tk)]
        eq = key == thr
        eqf = jnp.where(eq, 1.0, 0.0)
        prefix = jnp.dot(eqf.astype(BF16), tri_ref[...], preferred_element_type=F32) + eqc
        bias = jnp.where(key > thr, 0.0, jnp.where(eq, jnp.where(prefix < need, 0.0, NEG), NEG))
        bias = jnp.where(key == INT_MIN, NEG, bias)
        eqc = eqc + jnp.sum(eqf, axis=1, keepdims=True)
        ss = [_dot_nt(qs_scr[g], ka_ref[0, pl.ds(k0, tk), g * DH:(g + 1) * DH])
              for g in range(KV_A)]
        for g in range(KV_A):
            vg = va_ref[0, pl.ds(k0, tk), g * DH:(g + 1) * DH]
            s = ss[g]
            ps, alphas = [], []
            for r in range(rep):
                h = g * rep + r
                sr = s[r * tq:(r + 1) * tq] + bias
                m_prev = m_scr[h]
                m_new = jnp.maximum(m_prev, jnp.max(sr, axis=1, keepdims=True))
                alpha = jnp.exp2(m_prev - m_new)
                p = jnp.exp2(sr - jnp.concatenate([m_new] * (tk // LANES), axis=1))
                m_scr[h] = m_new
                ps.append(p.astype(BF16))
                alphas.append(jnp.concatenate([alpha, alpha], axis=1))
            vg1 = jnp.concatenate([vg, ones_k], axis=1)
            for r in range(rep):
                h = g * rep + r
                acc_scr[h] = acc_scr[h] * alphas[r] + jnp.dot(ps[r], vg1, preferred_element_type=F32)
        return eqc

    lax.fori_loop(0, nkb, p3, jnp.zeros((tq, 1), F32))
    for h in range(H_A):
        acc = acc_scr[h]
        o_ref[0, :, h * DH:(h + 1) * DH] = (acc[:, :DH] / acc[:, DH:]).astype(o_ref.dtype)


def _dsa(qa, qi, wi, ka, va, ki, *, tq, tk, s_valid, q0):
    b, t, _ = qa.shape
    s_pad = ka.shape[1]
    assert s_pad % tk == 0 and t % tq == 0 and tk >= TOPK_MAX
    topk = min(TOPK_MAX, s_valid // 4)
    tri = jnp.asarray(np.triu(np.ones((tk, tk), np.float32), 1), BF16)
    body = functools.partial(_dsa_body, tq=tq, tk=tk, s_valid=s_valid, q0=q0, topk=topk)
    rep = H_A // KV_A
    transposed = tq % LANES == 0
    if transposed:
        wi = jnp.transpose(wi, (0, 2, 1))
        wi_spec = pl.BlockSpec((1, H_I, tq), lambda bi, i: (bi, 0, i))
    else:
        wi_spec = pl.BlockSpec((1, tq, H_I), lambda bi, i: (bi, i, 0))
    return pl.pallas_call(
        body,
        grid=(b, t // tq),
        in_specs=[pl.BlockSpec((1, tq, H_A * DH), lambda bi, i: (bi, i, 0)),
                  pl.BlockSpec((1, tq, H_I * D_IDX), lambda bi, i: (bi, i, 0)),
                  wi_spec,
                  pl.BlockSpec((1, s_pad, KV_A * DH), lambda bi, i: (bi, 0, 0)),
                  pl.BlockSpec((1, s_pad, KV_A * DH), lambda bi, i: (bi, 0, 0)),
                  pl.BlockSpec((1, s_pad, 2 * LANES), lambda bi, i: (bi, 0, 0)),
                  pl.BlockSpec((tk, tk), lambda bi, i: (0, 0))],
        out_specs=pl.BlockSpec((1, tq, H_A * DH), lambda bi, i: (bi, i, 0)),
        out_shape=jax.ShapeDtypeStruct((b, t, H_A * DH), BF16),
        scratch_shapes=[pltpu.VMEM((s_pad, tq) if transposed else (tq, s_pad), jnp.int32),
                        pltpu.VMEM((KV_A, rep * tq, DH), BF16),
                        pltpu.VMEM((H_A, tq, LANES), F32),
                        pltpu.VMEM((H_A, tq, DH + LANES), F32)],
        compiler_params=_cparams(("parallel", "arbitrary")),
        name="dsa",
    )(qa, qi, wi, ka, va, ki, tri)


def _sb_body(q_ref, k_ref, v_ref, low_ref, *rest, tq, tk, s_pad, q0, hpg, nchunk, past_heads):
    if past_heads:
        kp_ref, vp_ref, o_ref = rest
    else:
        (o_ref,) = rest
    i = pl.program_id(2)
    tqc = tq // nchunk
    q_first = q0 + i * tq
    q_last = q_first + tq - 1
    nkb = jnp.minimum((q_last + tk - 1) // tk, s_pad // tk)
    n_full = jnp.minimum(q_first // tk, nkb)
    row = lax.broadcasted_iota(jnp.int32, (tqc, tk), 0)
    col = lax.broadcasted_iota(jnp.int32, (tqc, tk), 1)
    chains = [(g, c) for g in range(hpg) for c in range(nchunk)]

    def step(j, carry, masked):
        k0 = pl.multiple_of(j * tk, tk)

        def load(new_ref, past_ref, g):
            if not past_heads:
                return new_ref[0, pl.ds(k0, tk), g * DH:(g + 1) * DH]
            if masked:
                return new_ref[0, :, g * DH:(g + 1) * DH]
            return past_ref[0, pl.ds(k0 * past_heads + g, tk, stride=past_heads), :].astype(BF16)

        zs = [_dot_nt(q_ref[0, c * tqc:(c + 1) * tqc, g * DH:(g + 1) * DH],
                      load(k_ref, kp_ref if past_heads else None, g)) for g, c in chains]
        lbs, lks, sufs, causals = [], [], [], []
        for (g, c), z in zip(chains, zs):
            lb = jnp.minimum(z, 0.0) - jnp.log(1.0 + jnp.exp2(-jnp.abs(z))) * LOG2E
            lk = lb - z
            causal = None
            if masked:
                causal = k0 + col < q_first + c * tqc + row
                lk = jnp.where(causal, lk, 0.0)
            hi = lk.astype(BF16)
            lo = (lk - hi.astype(F32)).astype(BF16)
            lbs.append(lb)
            lks.append(lk)
            causals.append(causal)
            sufs.append(jnp.dot(jnp.concatenate([hi, lo], axis=1), low_ref[...], preferred_element_type=F32))
        out = []
        for n, (g, c) in enumerate(chains):
            run, acc = carry[2 * n], carry[2 * n + 1]
            wgt = jnp.exp2(lbs[n] + sufs[n] + run)
            if masked:
                wgt = jnp.where(causals[n], wgt, 0.0)
            acc = acc + jnp.dot(wgt.astype(BF16), load(v_ref, vp_ref if past_heads else None, g),
                                preferred_element_type=F32)
            run = run + jnp.sum(lks[n], axis=1, keepdims=True)
            out += [run, acc]
        return tuple(out)

    carry = []
    for _ in chains:
        carry += [jnp.zeros((tqc, 1), F32), jnp.zeros((tqc, DH), F32)]
    carry = tuple(carry)
    carry = lax.fori_loop(0, nkb - n_full, lambda s, c: step(nkb - 1 - s, c, True), carry)
    carry = lax.fori_loop(0, n_full, lambda s, c: step(n_full - 1 - s, c, False), carry)
    for n, (g, c) in enumerate(chains):
        o_ref[0, c * tqc:(c + 1) * tqc, g * DH:(g + 1) * DH] = carry[2 * n + 1].astype(o_ref.dtype)


def _stick_breaking(qb, kb, vb, *, tq, tk, q0, hpg, nchunk, past=None):
    b, t, _ = qb.shape
    low = np.tril(np.ones((tk, tk), np.float32), -1)
    low = jnp.asarray(np.concatenate([low, low], axis=0), BF16)
    wd = hpg * DH
    if past is None:
        s_pad, extra, extra_specs, past_heads = kb.shape[1], [], [], 0
    else:
        *caches, layer = past
        assert hpg == H_B and t == tq and q0 % tk == 0 and t <= tk and caches[0].shape[2:] == (q0, H_B, DH)
        s_pad, past_heads = q0 + tk, H_B
        kb, vb = (jnp.pad(a, ((0, 0), (0, tk - t), (0, 0))) for a in (kb, vb))
        extra = [c.reshape(c.shape[0], b, q0 * H_B, DH) for c in caches]
        extra_specs = [pl.BlockSpec((None, 1, q0 * H_B, DH), lambda bi, h, i: (layer, bi, 0, 0))] * 2
    s_new = kb.shape[1]
    assert s_pad % tk == 0 and t % tq == 0 and H_B % hpg == 0 and tq % nchunk == 0
    body = functools.partial(_sb_body, tq=tq, tk=tk, s_pad=s_pad, q0=q0, hpg=hpg, nchunk=nchunk,
                             past_heads=past_heads)
    return pl.pallas_call(
        body,
        grid=(b, H_B // hpg, t // tq),
        in_specs=[pl.BlockSpec((1, tq, wd), lambda bi, h, i: (bi, i, h)),
                  pl.BlockSpec((1, s_new, wd), lambda bi, h, i: (bi, 0, h)),
                  pl.BlockSpec((1, s_new, wd), lambda bi, h, i: (bi, 0, h)),
                  pl.BlockSpec((2 * tk, tk), lambda bi, h, i: (0, 0))] + extra_specs,
        out_specs=pl.BlockSpec((1, tq, wd), lambda bi, h, i: (bi, i, h)),
        out_shape=jax.ShapeDtypeStruct((b, t, H_B * DH), BF16),
        compiler_params=_cparams(("parallel", "parallel", "arbitrary")),
        name="stick_breaking",
    )(qb, kb, vb, low, *extra)


def _mla_body(qn_ref, qr_ref, kv_ref, kr_ref, o_ref, *, tq, tk, s_valid, q0, hpg, nchunk):
    i = pl.program_id(2)
    tqc = tq // nchunk
    q_first = q0 + i * tq
    q_last = q_first + tq - 1
    lim_first = jnp.minimum((q_first // CHUNK + 1) * CHUNK, s_valid)
    lim_last = jnp.minimum((q_last // CHUNK + 1) * CHUNK, s_valid)
    nkb = (lim_last + tk - 1) // tk
    n_full = lim_first // tk
    row = lax.broadcasted_iota(jnp.int32, (tqc, tk), 0)
    col = lax.broadcasted_iota(jnp.int32, (tqc, tk), 1)
    chains = [(g, c) for g in range(hpg) for c in range(nchunk)]
    hw = NOPE + VD

    def step(j, carry, masked):
        k0 = pl.multiple_of(j * tk, tk)
        kr = kr_ref[0, pl.ds(k0, tk), :]
        ss = []
        for g, c in chains:
            rows = slice(c * tqc, (c + 1) * tqc)
            qc = jnp.concatenate([qn_ref[0, rows, g * NOPE:(g + 1) * NOPE],
                                  qr_ref[0, rows, g * LANES:(g + 1) * LANES]], axis=1)
            kc = jnp.concatenate([kv_ref[0, pl.ds(k0, tk), g * hw:g * hw + NOPE], kr], axis=1)
            ss.append(_dot_nt(qc, kc))
        ps, stats = [], []
        for n, (g, c) in enumerate(chains):
            m_prev = carry[2 * n]
            s = ss[n]
            if masked:
                lim = jnp.minimum((((q_first + c * tqc + row) >> CHUNK_SHIFT) + 1) * CHUNK, s_valid)
                s = jnp.where(k0 + col < lim, s, NEG)
            m_new = jnp.maximum(m_prev, jnp.max(s, axis=1, keepdims=True))
            alpha = jnp.exp2(m_prev - m_new)
            p = jnp.exp2(s - jnp.concatenate([m_new] * (tk // LANES), axis=1))
            stats.append((m_new, alpha))
            ps.append(p.astype(BF16))
        out = []
        for n, (g, c) in enumerate(chains):
            m_new, alpha = stats[n]
            vb = jnp.concatenate([kv_ref[0, pl.ds(k0, tk), g * hw + NOPE:(g + 1) * hw], ones_k], axis=1)
            acc = (carry[2 * n + 1] * jnp.concatenate([alpha, alpha], axis=1)
                   + jnp.dot(ps[n], vb, preferred_element_type=F32))
            out += [m_new, acc]
        return tuple(out)

    ones_k = jnp.ones((tk, LANES), BF16)
    carry = []
    for _ in chains:
        carry += [jnp.full((tqc, LANES), NEG, F32), jnp.zeros((tqc, VD + LANES), F32)]
    carry = tuple(carry)
    carry = lax.fori_loop(0, n_full, lambda j, c: step(j, c, False), carry)
    carry = lax.fori_loop(n_full, nkb, lambda j, c: step(j, c, True), carry)
    for n, (g, c) in enumerate(chains):
        acc = carry[2 * n + 1]
        o_ref[0, c * tqc:(c + 1) * tqc, g * VD:(g + 1) * VD] = (acc[:, :VD] / acc[:, VD:]).astype(o_ref.dtype)


def _mla(qn, qr, kv, kr, *, tq, tk, s_valid, q0, hpg, nchunk):
    b, t, _ = qn.shape
    s_pad = kv.shape[1]
    assert s_pad % tk == 0 and t % tq == 0 and H_C % hpg == 0 and tq % nchunk == 0
    body = functools.partial(_mla_body, tq=tq, tk=tk, s_valid=s_valid, q0=q0, hpg=hpg, nchunk=nchunk)
    return pl.pallas_call(
        body,
        grid=(b, H_C // hpg, t // tq),
        in_specs=[pl.BlockSpec((1, tq, hpg * NOPE), lambda bi, h, i: (bi, i, h)),
                  pl.BlockSpec((1, tq, hpg * LANES), lambda bi, h, i: (bi, i, h)),
                  pl.BlockSpec((1, s_pad, hpg * (NOPE + VD)), lambda bi, h, i: (bi, 0, h)),
                  pl.BlockSpec((1, s_pad, LANES), lambda bi, h, i: (bi, 0, 0))],
        out_specs=pl.BlockSpec((1, tq, hpg * VD), lambda bi, h, i: (bi, i, h)),
        out_shape=jax.ShapeDtypeStruct((b, t, H_C * VD), BF16),
        compiler_params=_cparams(("parallel", "parallel", "arbitrary")),
        name="mla",
    )(qn, qr, kv, kr)


def _absorb_q_body(qn_ref, qr_ref, wk_ref, qa_ref, qrh_ref):
    qa_ref[0] = _dot_nt(qn_ref[...], wk_ref[...]).astype(qa_ref.dtype)
    qrh_ref[0] = qr_ref[...]


def _absorb_q(qn, qr, w_ukv):
    m = qn.shape[0]
    return pl.pallas_call(
        _absorb_q_body,
        grid=(H_C,),
        in_specs=[pl.BlockSpec((m, NOPE), lambda h: (0, h)),
                  pl.BlockSpec((m, LANES), lambda h: (0, h)),
                  pl.BlockSpec((KV_RANK, NOPE), lambda h: (0, 2 * h))],
        out_specs=[pl.BlockSpec((1, m, KV_RANK), lambda h: (h, 0, 0)),
                   pl.BlockSpec((1, m, LANES), lambda h: (h, 0, 0))],
        out_shape=[jax.ShapeDtypeStruct((H_C, m, KV_RANK), BF16),
                   jax.ShapeDtypeStruct((H_C, m, LANES), BF16)],
        compiler_params=_cparams(("parallel",)),
        name="mla_absorb_q",
    )(qn, qr, w_ukv)


def _mla_latent_body(qa_ref, qr_ref, c_ref, kr_ref, cp_ref, krp_ref, o_ref, *, t, tk, s_valid, q0, nsplit):
    hs = H_C // nsplit
    rows = hs * t
    lim_first = min((q0 // CHUNK + 1) * CHUNK, s_valid)
    lim_last = min(((q0 + t - 1) // CHUNK + 1) * CHUNK, s_valid)
    nkb = -(-lim_last // tk)
    n_full = lim_first // tk
    assert n_full == q0 // tk and nkb == n_full + 1
    q_pos = q0 + lax.broadcasted_iota(jnp.int32, (hs, t, tk), 1).reshape(rows, tk)
    lim = jnp.minimum(((q_pos >> CHUNK_SHIFT) + 1) * CHUNK, s_valid)
    col = lax.broadcasted_iota(jnp.int32, (rows, tk), 1)
    ones_k = jnp.ones((tk, LANES), BF16)
    qs = [jnp.concatenate([qa_ref[n * hs:(n + 1) * hs].reshape(rows, KV_RANK),
                           qr_ref[n * hs:(n + 1) * hs].reshape(rows, LANES)], axis=1) for n in range(nsplit)]

    def step(j, carry, masked):
        k0 = pl.multiple_of(j * tk, tk)
        if masked:
            cb, krb = c_ref[0], kr_ref[0]
        else:
            cb = cp_ref[0, pl.ds(k0, tk), :].astype(BF16)
            krb = jnp.concatenate([krp_ref[0, pl.ds(k0, tk), :].astype(BF16),
                                   jnp.zeros((tk, LANES - ROPE_C), BF16)], axis=1)
        kc = jnp.concatenate([cb, krb], axis=1)
        vc = jnp.concatenate([cb, ones_k], axis=1)
        ss = [_dot_nt(q, kc) for q in qs]
        ps, stats = [], []
        for n in range(nsplit):
            s = ss[n]
            if masked:
                s = jnp.where(k0 + col < lim, s, NEG)
            m_prev = carry[2 * n]
            m_new = jnp.maximum(m_prev, jnp.max(s, axis=1, keepdims=True))
            alpha = jnp.exp2(m_prev - m_new)
            ps.append(jnp.exp2(s - jnp.concatenate([m_new] * (tk // LANES), axis=1)).astype(BF16))
            stats.append((m_new, alpha))
        out = []
        for n in range(nsplit):
            m_new, alpha = stats[n]
            acc = (carry[2 * n + 1] * jnp.concatenate([alpha] * (KV_RANK // LANES + 1), axis=1)
                   + jnp.dot(ps[n], vc, preferred_element_type=F32))
            out += [m_new, acc]
        return tuple(out)

    carry = []
    for _ in range(nsplit):
        carry += [jnp.full((rows, LANES), NEG, F32), jnp.zeros((rows, KV_RANK + LANES), F32)]
    carry = tuple(carry)
    carry = lax.fori_loop(0, n_full, lambda j, c: step(j, c, False), carry)
    carry = lax.fori_loop(n_full, nkb, lambda j, c: step(j, c, True), carry)
    for n in range(nsplit):
        acc = carry[2 * n + 1]
        lat = acc[:, :KV_RANK] / jnp.concatenate([acc[:, KV_RANK:]] * (KV_RANK // LANES), axis=1)
        o_ref[0, n * hs:(n + 1) * hs] = lat.reshape(hs, t, KV_RANK).astype(o_ref.dtype)


def _mla_latent(qa_hm, qr_hm, c_new, kr_new, c_past, kr_past, layer, *, tk, s_valid, q0):
    b, t, _ = c_new.shape
    assert q0 % tk == 0 and t <= tk and c_past.shape[2] == q0
    c_new, kr_new = (jnp.pad(a, ((0, 0), (0, tk - t), (0, 0))) for a in (c_new, kr_new))
    body = functools.partial(_mla_latent_body, t=t, tk=tk, s_valid=s_valid, q0=q0, nsplit=2)
    return pl.pallas_call(
        body,
        grid=(b,),
        in_specs=[pl.BlockSpec((H_C, t, KV_RANK), lambda bi: (0, bi, 0)),
                  pl.BlockSpec((H_C, t, LANES), lambda bi: (0, bi, 0)),
                  pl.BlockSpec((1, tk, KV_RANK), lambda bi: (bi, 0, 0)),
                  pl.BlockSpec((1, tk, LANES), lambda bi: (bi, 0, 0)),
                  pl.BlockSpec((None, 1, q0, KV_RANK), lambda bi: (layer, bi, 0, 0)),
                  pl.BlockSpec((None, 1, q0, ROPE_C), lambda bi: (layer, bi, 0, 0))],
        out_specs=pl.BlockSpec((1, H_C, t, KV_RANK), lambda bi: (bi, 0, 0, 0)),
        out_shape=jax.ShapeDtypeStruct((b, H_C, t, KV_RANK), BF16),
        compiler_params=_cparams(("parallel",)),
        name="mla_latent",
    )(qa_hm, qr_hm, c_new, kr_new, c_past, kr_past)


def _latent_out_body(x_ref, wv_ref, o_ref):
    nb, _, t, r = x_ref.shape
    o_ref[...] = jnp.dot(x_ref[...].reshape(nb * t, r), wv_ref[...],
                         preferred_element_type=F32).astype(o_ref.dtype)


def _latent_out(lat, w_ukv):
    b, _, t, _ = lat.shape
    return pl.pallas_call(
        _latent_out_body,
        grid=(H_C,),
        in_specs=[pl.BlockSpec((b, 1, t, KV_RANK), lambda h: (0, h, 0, 0)),
                  pl.BlockSpec((KV_RANK, VD), lambda h: (0, 2 * h + 1))],
        out_specs=pl.BlockSpec((b * t, VD), lambda h: (0, h)),
        out_shape=jax.ShapeDtypeStruct((b * t, H_C * VD), BF16),
        compiler_params=_cparams(("parallel",)),
        name="mla_latent_out",
    )(lat, w_ukv)


def _ffn_body(x_ref, g_ref, wg_ref, wv_ref, cwg_ref, cwv_ref, cbg_ref, cbv_ref, wd_ref, pg_ref, pv_ref, gn_ref,
              o_ref, hn_ref, sg_ref, sv_ref, h_scr, acc_scr, ug_buf, uv_buf, cg_scr, cv_scr, *, seg, carried):
    i = pl.program_id(0)
    f = pl.program_id(1)
    tm = x_ref.shape[0]

    @pl.when(f == 0)
    def _():
        x = x_ref[...]
        y = x * lax.rsqrt(jnp.mean(x * x, axis=-1, keepdims=True) + EPS)
        h_scr[...] = (y * g_ref[...]).astype(BF16)
        acc_scr[...] = x

    h = h_scr[...]
    halves = ((wg_ref, cwg_ref, cbg_ref, pg_ref, sg_ref, ug_buf, cg_scr),
              (wv_ref, cwv_ref, cbv_ref, pv_ref, sv_ref, uv_buf, cv_scr))
    ys = []
    for w_ref, cw_ref, cb_ref, p_ref, s_ref, buf, c_scr in halves:
        u = jnp.dot(h, w_ref[...], preferred_element_type=F32)
        cw = cw_ref[...]
        parts = []
        for sgi in range(tm // seg):
            us = u[sgi * seg:(sgi + 1) * seg]
            if carried:
                head = jnp.where(i == 0, p_ref[0], c_scr[f])
            else:
                head = p_ref[sgi]
            buf[sgi, 0:SUBLANES] = head
            buf[sgi, SUBLANES:SUBLANES + seg] = us
            y = (cw[2:3] * us + cw[1:2] * buf[sgi, SUBLANES - 1:SUBLANES - 1 + seg]
                 + cw[0:1] * buf[sgi, SUBLANES - 2:SUBLANES - 2 + seg] + cb_ref[...])
            parts.append(y)
            last = us[seg - SUBLANES:seg]
            s_ref[sgi] = last
            if carried:
                c_scr[f] = last
        ys.append(parts[0] if len(parts) == 1 else jnp.concatenate(parts, axis=0))
    yg, yv = ys
    act = (yg / (1.0 + jnp.exp(-yg))) * yv
    acc_scr[...] += jnp.dot(act.astype(BF16), wd_ref[...], preferred_element_type=F32)

    @pl.when(f == pl.num_programs(1) - 1)
    def _():
        a = acc_scr[...]
        o_ref[...] = a
        y = a * lax.rsqrt(jnp.mean(a * a, axis=-1, keepdims=True) + EPS)
        hn_ref[...] = (y * gn_ref[...]).astype(hn_ref.dtype)


def _ffn(x, g, w_up, conv_w, conv_b, w_down, prev8, g_next, next_dtype, *, seg, carried, tm, tf):
    m, d = x.shape
    nf = D_FF // tf
    nseg = tm // seg
    assert m % tm == 0 and D_FF % tf == 0 and tm % seg == 0
    sidx = (lambda i: 0) if carried else (lambda i: i)
    body = functools.partial(_ffn_body, seg=seg, carried=carried)
    cb = conv_b.reshape(1, 2 * D_FF)
    return pl.pallas_call(
        body,
        grid=(m // tm, nf),
        in_specs=[pl.BlockSpec((tm, d), lambda i, f: (i, 0)),
                  pl.BlockSpec((1, d), lambda i, f: (0, 0)),
                  pl.BlockSpec((d, tf), lambda i, f: (0, f)),
                  pl.BlockSpec((d, tf), lambda i, f: (0, nf + f)),
                  pl.BlockSpec((CONV_W, tf), lambda i, f: (0, f)),
                  pl.BlockSpec((CONV_W, tf), lambda i, f: (0, nf + f)),
                  pl.BlockSpec((1, tf), lambda i, f: (0, f)),
                  pl.BlockSpec((1, tf), lambda i, f: (0, nf + f)),
                  pl.BlockSpec((tf, d), lambda i, f: (f, 0)),
                  pl.BlockSpec((nseg, SUBLANES, tf), lambda i, f: (sidx(i), 0, f)),
                  pl.BlockSpec((nseg, SUBLANES, tf), lambda i, f: (sidx(i), 0, nf + f)),
                  pl.BlockSpec((1, d), lambda i, f: (0, 0))],
        out_specs=[pl.BlockSpec((tm, d), lambda i, f: (i, 0)),
                   pl.BlockSpec((tm, d), lambda i, f: (i, 0)),
                   pl.BlockSpec((nseg, SUBLANES, tf), lambda i, f: (i, 0, f)),
                   pl.BlockSpec((nseg, SUBLANES, tf), lambda i, f: (i, 0, f))],
        out_shape=[jax.ShapeDtypeStruct((m, d), F32),
                   jax.ShapeDtypeStruct((m, d), next_dtype),
                   jax.ShapeDtypeStruct((m // seg, SUBLANES, D_FF), F32),
                   jax.ShapeDtypeStruct((m // seg, SUBLANES, D_FF), F32)],
        scratch_shapes=[pltpu.VMEM((tm, d), BF16),
                        pltpu.VMEM((tm, d), F32),
                        pltpu.VMEM((nseg, SUBLANES + seg, tf), F32),
                        pltpu.VMEM((nseg, SUBLANES + seg, tf), F32),
                        pltpu.VMEM((nf, SUBLANES, tf), F32),
                        pltpu.VMEM((nf, SUBLANES, tf), F32)],
        compiler_params=_cparams(("arbitrary", "arbitrary")),
        name="conv_ffn",
    )(x, g.reshape(1, d), w_up, w_up, conv_w, conv_w, cb, cb, w_down, prev8, prev8, g_next.reshape(1, d))


def _rope_tables(pos, head_dim, rot, lanes_valid=LANES):
    half = rot // 2
    freqs = ROPE_THETA ** (-jnp.arange(half, dtype=F32) / half)
    ang = pos.astype(F32)[:, None] * freqs[None, :]
    cos, sin = jnp.cos(ang), jnp.sin(ang)
    lane = np.arange(LANES)
    d = lane % head_dim
    first = (d < half) & (lane < lanes_valid)
    second = (d >= half) & (d < rot) & (lane < lanes_valid)
    fidx = np.clip(np.where(d < half, d, d - half), 0, half - 1)
    cosl, sinl = cos[:, fidx], sin[:, fidx]
    c = jnp.where(first | second, cosl, 1.0)
    s1 = jnp.where(second, sinl, 0.0)
    s2 = jnp.where(first, -sinl, 0.0)
    return c, s1, s2, half


def _prep_weights(w_in_ab, w_out_ab, w_dqkv, w_uq, w_ukv, w_o_mla, w_up, w_down):
    n_even, n_odd = w_in_ab.shape[0], w_dqkv.shape[0]
    w = {}
    n_a = H_A * DH + 2 * KV_A * DH + H_I * D_IDX
    w["in_a"] = w_in_ab[:, :, :n_a].astype(BF16)
    pad = jnp.zeros((n_even, D_MODEL, LANES - D_IDX - H_I), BF16)
    w["in_b"] = jnp.concatenate([w_in_ab[:, :, n_a:n_a + D_IDX + H_I].astype(BF16), pad,
                                 w_in_ab[:, :, n_a + D_IDX + H_I:].astype(BF16)], axis=-1)
    w["out_ab"] = w_out_ab.astype(BF16)
    w["dqkr"] = jnp.concatenate([w_dqkv.astype(BF16),
                                 jnp.zeros((n_odd, D_MODEL, LANES - ROPE_C), BF16)], axis=-1)
    uq = w_uq.astype(BF16).reshape(n_odd, Q_RANK, H_C, NOPE + ROPE_C)
    uqn = uq[..., :NOPE].reshape(n_odd, Q_RANK, H_C * NOPE)
    uqr = uq[..., NOPE:]
    uqr = jnp.concatenate([uqr, jnp.zeros_like(uqr)], axis=-1).reshape(n_odd, Q_RANK, H_C * LANES)
    w["uq"] = jnp.concatenate([uqn, uqr], axis=-1)
    w["ukv"] = w_ukv.astype(BF16)
    w["o_mla"] = w_o_mla.astype(BF16)
    w["up"] = w_up.astype(BF16)
    w["down"] = w_down.astype(BF16)
    return w


def _with_past(past, new, s_pad):
    b, _, fdim = new.shape
    parts = [new.astype(BF16)] if past is None else [past.astype(BF16), new.astype(BF16)]
    n = sum(p.shape[1] for p in parts)
    if s_pad > n:
        parts.append(jnp.zeros((b, s_pad - n, fdim), BF16))
    return parts[0] if len(parts) == 1 else jnp.concatenate(parts, axis=1)


def _trunk(x, past, conv_state, w, p, cfg):
    b, t, d = x.shape
    m = b * t
    q0 = 0 if past is None else past[0].shape[2]
    s_valid = q0 + t
    tk = cfg["tk"]
    s_pad = -(-s_valid // tk) * tk
    pos = q0 + jnp.tile(jnp.arange(t), b)
    tab_a = _rope_tables(pos, DH, ROT_A)
    tab_i = _rope_tables(pos, D_IDX, ROT_I)
    tab_ki = _rope_tables(pos, D_IDX, ROT_I, lanes_valid=D_IDX)
    tab_kr = _rope_tables(pos, ROPE_C, ROPE_C, lanes_valid=ROPE_C)

    xf = x.reshape(m, d)
    ab_rows, mla_rows, conv_rows = [], [], []
    sc_ab = DH ** -0.5 * LOG2E
    sc_c = (NOPE + ROPE_C) ** -0.5 * LOG2E
    h = _rmsnorm(xf, p["norm_mix"][0], BF16)
    for l in range(DEPTH):
        if l % 2 == 0:
            e = l // 2
            lp = None if past is None else tuple(c[e] for c in past[:5])
            qa, ka32, ka16, va32, va16, qi = _proj(
                [h], w["in_a"][e],
                [dict(width=H_A * DH, dtypes=(BF16,), rope=0, scale=sc_ab),
                 dict(width=KV_A * DH, dtypes=(F32, BF16), rope=0, split_heads=True),
                 dict(width=KV_A * DH, dtypes=(F32, BF16), split_heads=True),
                 dict(width=H_I * D_IDX, dtypes=(BF16,), rope=1)],
                tabs=(tab_a, tab_i), name="proj_in_a")
            kiwi, qb, kb32, kb16, vb32, vb16 = _proj(
                [h], w["in_b"][e],
                [dict(width=LANES, dtypes=(F32,), rope=0),
                 dict(width=H_B * DH, dtypes=(BF16,), scale=sc_ab),
                 dict(width=H_B * DH, dtypes=(F32, BF16), split_heads=True),
                 dict(width=H_B * DH, dtypes=(F32, BF16), split_heads=True)],
                tabs=(tab_ki,), name="proj_in_b")
            ki32 = kiwi[:, :D_IDX]
            wi = kiwi[:, D_IDX:D_IDX + H_I]
            ab_rows.append((ka32.reshape(b, t, KV_A, DH), va32.reshape(b, t, KV_A, DH),
                            ki32.reshape(b, t, D_IDX), kb32.reshape(b, t, H_B, DH),
                            vb32.reshape(b, t, H_B, DH)))

            def full(idx, new16):
                pst = None if lp is None else lp[idx].reshape(b, q0, -1)
                return _with_past(pst, new16.reshape(b, t, -1), s_pad)

            ki_all = full(2, ki32)
            zk = jnp.zeros_like(ki_all)
            ki2 = jnp.concatenate([ki_all, zk, zk, ki_all], axis=-1)
            oa = _dsa(qa.reshape(b, t, -1), qi.reshape(b, t, -1), wi.reshape(b, t, H_I), full(0, ka16),
                      full(1, va16), ki2, tq=cfg["tq_a"], tk=tk, s_valid=s_valid, q0=q0)
            if lp is None:
                ob = _stick_breaking(qb.reshape(b, t, -1), full(3, kb16), full(4, vb16), tq=cfg["tq_b"],
                                     tk=cfg["tk_b"], q0=q0, hpg=cfg["hpg_b"], nchunk=cfg["nc_b"])
            else:
                ob = _stick_breaking(qb.reshape(b, t, -1), kb16.reshape(b, t, -1), vb16.reshape(b, t, -1),
                                     tq=cfg["tq_b"], tk=cfg["tk_b"], q0=q0, hpg=cfg["hpg_b"],
                                     nchunk=cfg["nc_b"], past=(past[3], past[4], e))
            (xf,) = _proj([oa.reshape(m, -1), ob.reshape(m, -1)], w["out_ab"][e],
                          [dict(width=D_MODEL, dtypes=(F32,))], resid=xf, name="proj_out_ab")
        else:
            od = l // 2
            gains = jnp.concatenate([p["g_q"][od], p["g_kv"][od]])
            cq16, ckv32, ckv16, kr32w, kr16w = _proj(
                [h], w["dqkr"][od],
                [dict(width=Q_RANK, dtypes=(BF16,), gain_off=0),
                 dict(width=KV_RANK, dtypes=(F32, BF16), gain_off=Q_RANK),
                 dict(width=LANES, dtypes=(F32, BF16), rope=0)],
                tabs=(tab_kr,), gain=gains, name="proj_dqkr")
            mla_rows.append((ckv32.reshape(b, t, KV_RANK), kr32w[:, :ROPE_C].reshape(b, t, ROPE_C)))
            qn, qr = _proj([cq16], w["uq"][od],
                           [dict(width=H_C * NOPE, dtypes=(BF16,), scale=sc_c),
                            dict(width=H_C * LANES, dtypes=(BF16,), rope=0, scale=sc_c)],
                           tabs=(tab_kr,), name="proj_uq")
            if cfg["latent_c"]:
                qa_hm, qr_hm = _absorb_q(qn, qr, w["ukv"][od])
                lat = _mla_latent(qa_hm, qr_hm, ckv16.reshape(b, t, KV_RANK), kr16w.reshape(b, t, LANES),
                                  past[5], past[6], od, tk=cfg["tk_c"], s_valid=s_valid, q0=q0)
                o = _latent_out(lat, w["ukv"][od])
            else:
                pc = None if past is None else past[5][od]
                pk = None if past is None else jnp.pad(past[6][od], ((0, 0), (0, 0), (0, LANES - ROPE_C)))
                c_all = _with_past(pc, ckv16.reshape(b, t, KV_RANK), s_pad)
                kr_all = _with_past(pk, kr16w.reshape(b, t, LANES), s_pad)
                (kv,) = _proj([c_all.reshape(b * s_pad, KV_RANK)], w["ukv"][od],
                              [dict(width=H_C * (NOPE + VD), dtypes=(BF16,))], name="proj_ukv")
                o = _mla(qn.reshape(b, t, -1), qr.reshape(b, t, -1), kv.reshape(b, s_pad, -1), kr_all,
                         tq=cfg["tq_c"], tk=cfg["tk_c"], s_valid=s_valid, q0=q0, hpg=cfg["hpg_c"],
                         nchunk=cfg["nc_c"])
            (xf,) = _proj([o.reshape(m, -1)], w["o_mla"][od], [dict(width=D_MODEL, dtypes=(F32,))],
                          resid=xf, name="proj_o_mla")

        if conv_state is None:
            prev8 = jnp.zeros((b, SUBLANES, 2 * D_FF), F32)
        else:
            prev8 = jnp.concatenate([jnp.zeros((b, SUBLANES - (CONV_W - 1), 2 * D_FF), F32),
                                     conv_state[l]], axis=1)
        final = l == DEPTH - 1
        g_next = p["norm_final"] if final else p["norm_mix"][l + 1]
        xf, h, sg, sv = _ffn(xf, p["norm_ffn"][l], w["up"][l], p["conv_w"][l], p["conv_b"][l], w["down"][l],
                             prev8, g_next, F32 if final else BF16, seg=cfg["seg"], carried=cfg["carried"],
                             tm=cfg["tm_ffn"], tf=cfg["tf"])
        last = [s8.reshape(b, -1, SUBLANES, D_FF)[:, -1, SUBLANES - (CONV_W - 1):] for s8 in (sg, sv)]
        conv_rows.append(jnp.concatenate(last, axis=-1))

    y = h.reshape(b, t, d)
    new_ab = [jnp.stack([r[i] for r in ab_rows]) for i in range(5)]
    new_mla = [jnp.stack([r[i] for r in mla_rows]) for i in range(2)]
    return y, new_ab + new_mla + [jnp.stack(conv_rows)]


def kernel(x_prompt, x_sample, cache_k_a, cache_v_a, cache_idx_k, cache_k_b, cache_v_b, cache_ckv, cache_krope, state_conv, norm_mix, norm_ffn, norm_final, w_in_ab, w_out_ab, w_dqkv, g_q, g_kv, w_uq, w_ukv, w_o_mla, w_up, conv_w, conv_b, w_down):
    w = _prep_weights(w_in_ab, w_out_ab, w_dqkv, w_uq, w_ukv, w_o_mla, w_up, w_down)
    p = dict(norm_mix=norm_mix, norm_ffn=norm_ffn, norm_final=norm_final, g_q=g_q, g_kv=g_kv,
             conv_w=conv_w, conv_b=conv_b)
    cfg_p = dict(tk=512, tq_a=256, tq_b=256, tk_b=256, hpg_b=4, nc_b=1, tq_c=512, tk_c=1024, hpg_c=2, nc_c=2,
                 latent_c=False, seg=512, carried=True, tm_ffn=512, tf=512)
    t_s = x_sample.shape[1]
    cfg_s = dict(tk=256, tq_a=t_s, tq_b=t_s, tk_b=256, hpg_b=H_B, nc_b=1, tq_c=t_s, tk_c=256, hpg_c=8, nc_c=1,
                 latent_c=True, seg=t_s, carried=False, tm_ffn=x_sample.shape[0] * t_s, tf=512)
    y_p, st_p = _trunk(x_prompt, None, None, w, p, cfg_p)
    past = (cache_k_a, cache_v_a, cache_idx_k, cache_k_b, cache_v_b, cache_ckv, cache_krope)
    y_s, st_s = _trunk(x_sample, past, state_conv, w, p, cfg_s)
    return (y_p, y_s, *st_p, *st_s)
```

```python
import functools

import numpy as np
import jax
import jax.numpy as jnp
from jax import lax
from jax.experimental import pallas as pl
from jax.experimental.pallas import tpu as pltpu

F32 = jnp.float32
BF16 = jnp.bfloat16

D_MODEL = 2048
DEPTH = 4
CHUNK = 64
CHUNK_SHIFT = CHUNK.bit_length() - 1
ROPE_THETA = 500000.0
EPS = 1e-6
H_A, KV_A, DH = 8, 2, 128
ROT_A = DH // 4
H_I, D_IDX = 16, 64
ROT_I = D_IDX // 4
TOPK_MAX = 256
H_B = 8
H_C, Q_RANK, KV_RANK, NOPE, ROPE_C, VD = 16, 512, 512, 128, 64, 128
D_FF = 5632
CONV_W = 3

LANES = 128
SUBLANES = 8
VMEM_LIMIT = 56 * 1024 * 1024
NEG = -1e30
INT_MIN = np.int32(-2 ** 31)
INT_MAX = np.int32(2 ** 31 - 1)
LOG2E = float(np.log2(np.e))


def _cparams(sem):
    return pltpu.CompilerParams(dimension_semantics=sem, vmem_limit_bytes=VMEM_LIMIT)


def _dot_nt(a, b):
    return lax.dot_general(a, b, (((1,), (1,)), ((), ())), preferred_element_type=F32)


def _rms_body(x_ref, g_ref, o_ref):
    x = x_ref[...]
    y = x * lax.rsqrt(jnp.mean(x * x, axis=-1, keepdims=True) + EPS)
    o_ref[...] = (y * g_ref[...]).astype(o_ref.dtype)


def _rmsnorm(x, g, out_dtype):
    m, d = x.shape
    tm = min(m, 512)
    return pl.pallas_call(
        _rms_body,
        grid=(m // tm,),
        in_specs=[pl.BlockSpec((tm, d), lambda i: (i, 0)),
                  pl.BlockSpec((1, d), lambda i: (0, 0))],
        out_specs=pl.BlockSpec((tm, d), lambda i: (i, 0)),
        out_shape=jax.ShapeDtypeStruct((m, d), out_dtype),
        compiler_params=_cparams(("parallel",)),
        name="rmsnorm",
    )(x, g.reshape(1, d))


PROJ_CHUNK = 512
COUNT_ROWS = 64


def _proj_body(*refs, nx, groups, tab_halves, has_gain, has_resid):
    it = iter(refs)
    x_refs = [next(it) for _ in range(nx)]
    w_ref = next(it)
    tabs = [(next(it), next(it), next(it)) for _ in tab_halves]
    g_ref = next(it) if has_gain else None
    r_ref = next(it) if has_resid else None
    outs = list(it)

    xs = [r[...] for r in x_refs]
    x = xs[0] if nx == 1 else jnp.concatenate(xs, axis=1)
    accs = []
    c0 = 0
    for grp in groups:
        width = grp["width"]
        chunk = min(width, PROJ_CHUNK)
        for cc in range(0, width, chunk):
            accs.append(jnp.dot(x, w_ref[:, c0 + cc:c0 + cc + chunk], preferred_element_type=F32))
        c0 += width
    accs = iter(accs)
    oi = 0
    for grp in groups:
        width = grp["width"]
        chunk = min(width, PROJ_CHUNK)
        for cc in range(0, width, chunk):
            acc = next(accs)
            if grp.get("gain_off") is not None:
                assert chunk == width
                go = grp["gain_off"]
                acc = (acc * lax.rsqrt(jnp.mean(acc * acc, axis=-1, keepdims=True) + EPS)
                       * g_ref[:, go:go + width])
            if grp.get("rope") is not None:
                c_ref, s1_ref, s2_ref = tabs[grp["rope"]]
                half = tab_halves[grp["rope"]]
                c, s1, s2 = c_ref[...], s1_ref[...], s2_ref[...]
                parts = []
                for gi in range(chunk // LANES):
                    xg = acc[:, gi * LANES:(gi + 1) * LANES]
                    parts.append(xg * c + pltpu.roll(xg, half, 1) * s1
                                 + pltpu.roll(xg, LANES - half, 1) * s2)
                acc = parts[0] if len(parts) == 1 else jnp.concatenate(parts, axis=1)
            if grp.get("scale") is not None:
                acc = acc * grp["scale"]
            if has_resid:
                acc = acc + r_ref[:, cc:cc + chunk]
            for k in range(len(grp["dtypes"])):
                o = outs[oi + k]
                if len(o.shape) == 3:
                    for gi in range(chunk // LANES):
                        o[:, cc // LANES + gi, :] = acc[:, gi * LANES:(gi + 1) * LANES].astype(o.dtype)
                else:
                    o[:, cc:cc + chunk] = acc.astype(o.dtype)
        oi += len(grp["dtypes"])


def _proj(xs, w, groups, *, tabs=(), gain=None, resid=None, name="proj"):
    m = xs[0].shape[0]
    k, n = w.shape
    tm = min(m, 512)
    assert m % tm == 0 and n == sum(g["width"] for g in groups) and k == sum(x.shape[1] for x in xs)
    assert resid is None or len(groups) == 1
    in_specs = [pl.BlockSpec((tm, x.shape[1]), lambda i: (i, 0)) for x in xs]
    in_specs.append(pl.BlockSpec((k, n), lambda i: (0, 0)))
    args = list(xs) + [w]
    for c, s1, s2, _ in tabs:
        for t in (c, s1, s2):
            in_specs.append(pl.BlockSpec((tm, LANES), lambda i: (i, 0)))
            args.append(t)
    if gain is not None:
        in_specs.append(pl.BlockSpec((1, gain.shape[0]), lambda i: (0, 0)))
        args.append(gain.reshape(1, -1))
    if resid is not None:
        in_specs.append(pl.BlockSpec((tm, n), lambda i: (i, 0)))
        args.append(resid)
    out_specs, out_shape = [], []
    for g in groups:
        for dt in g["dtypes"]:
            if g.get("split_heads") and dt == F32:
                nh = g["width"] // LANES
                out_specs.append(pl.BlockSpec((tm, nh, LANES), lambda i: (i, 0, 0)))
                out_shape.append(jax.ShapeDtypeStruct((m, nh, LANES), dt))
            else:
                out_specs.append(pl.BlockSpec((tm, g["width"]), lambda i: (i, 0)))
                out_shape.append(jax.ShapeDtypeStruct((m, g["width"]), dt))
    body = functools.partial(_proj_body, nx=len(xs), groups=groups, tab_halves=[t[3] for t in tabs],
                             has_gain=gain is not None, has_resid=resid is not None)
    return pl.pallas_call(
        body,
        grid=(m // tm,),
        in_specs=in_specs,
        out_specs=out_specs,
        out_shape=out_shape,
        compiler_params=_cparams(("parallel",)),
        name=name,
    )(*args)


def _dsa_body(qa_ref, qi_ref, wi_ref, ka_ref, va_ref, ki_ref, tri_ref, o_ref,
              key_scr, qs_scr, m_scr, acc_scr, *, tq, tk, s_valid, q0, topk):
    i = pl.program_id(1)
    q_first = q0 + i * tq
    q_last = q_first + tq - 1
    adm_end = jnp.minimum((q_last // CHUNK + 1) * CHUNK, s_valid)
    nkb = (adm_end + tk - 1) // tk
    q_pos = q_first + lax.broadcasted_iota(jnp.int32, (tq, tk), 0)
    q_lim = jnp.minimum(((q_pos >> CHUNK_SHIFT) + 1) * CHUNK, s_valid)
    col = lax.broadcasted_iota(jnp.int32, (tq, tk), 1)
    rep = H_A // KV_A

    w = wi_ref[0] * ((H_I * D_IDX) ** -0.5)
    pg = 4
    pairs = LANES // D_IDX
    qi = qi_ref[0]
    q_groups = [jnp.concatenate([qi[:, (g * pg + pp) * LANES:(g * pg + pp + 1) * LANES]
                                 for pp in range(pg)], axis=0) for g in range(H_I // (pairs * pg))]

    def to_key(x):
        bits = pltpu.bitcast(x, jnp.int32)
        return bits ^ ((bits >> 31) & INT_MAX)

    transposed = key_scr.shape[1] == tq

    if transposed:
        wt = w
        krow = lax.broadcasted_iota(jnp.int32, (tk, tq), 0)
        q_lim_t = jnp.minimum((((q_first + lax.broadcasted_iota(jnp.int32, (tk, tq), 1)) >> CHUNK_SHIFT) + 1) * CHUNK,
                              s_valid)

    def p1(j, c):
        k0 = pl.multiple_of(j * tk, tk)
        if transposed:
            sc = jnp.zeros((tk, tq), F32)
            for g, qg in enumerate(q_groups):
                for half in range(pairs):
                    rel = _dot_nt(ki_ref[0, pl.ds(k0, tk), half * LANES:(half + 1) * LANES], qg)
                    for pp in range(pg):
                        h = (g * pg + pp) * pairs + half
                        sc = sc + jnp.maximum(rel[:, pp * tq:(pp + 1) * tq], 0.0) * wt[h:h + 1, :]
            key_scr[pl.ds(k0, tk), :] = jnp.where(k0 + krow < q_lim_t, to_key(sc), INT_MIN)
            return c
        sc = jnp.zeros((tq, tk), F32)
        for g, qg in enumerate(q_groups):
            for half in range(pairs):
                rel = _dot_nt(qg, ki_ref[0, pl.ds(k0, tk), half * LANES:(half + 1) * LANES])
                for pp in range(pg):
                    h = (g * pg + pp) * pairs + half
                    sc = sc + jnp.maximum(rel[pp * tq:(pp + 1) * tq], 0.0) * w[:, h:h + 1]
        key_scr[:, pl.ds(k0, tk)] = jnp.where(k0 + col < q_lim, to_key(sc), INT_MIN)
        return c

    lax.fori_loop(0, nkb, p1, 0)

    def count_ge(v):
        def cb(j, c):
            k0 = pl.multiple_of(j * tk, tk)
            if transposed:
                ge = jnp.where(key_scr[pl.ds(k0, tk), :] >= v, 1.0, 0.0)
                return c + jnp.sum(ge.reshape(tk // COUNT_ROWS, COUNT_ROWS, tq), axis=0)
            ge = jnp.where(key_scr[:, pl.ds(k0, tk)] >= v, 1.0, 0.0)
            part = ge[:, 0:LANES]
            for cc in range(1, tk // LANES):
                part = part + ge[:, cc * LANES:(cc + 1) * LANES]
            return c + part
        if transposed:
            c = lax.fori_loop(0, nkb, cb, jnp.zeros((COUNT_ROWS, tq), F32))
            return jnp.sum(c, axis=0, keepdims=True)
        c = lax.fori_loop(0, nkb, cb, jnp.zeros((tq, LANES), F32))
        return jnp.sum(c, axis=1, keepdims=True)

    def bis(_, lohi):
        lo, hi = lohi
        mid = (lo >> 1) + (hi >> 1) + (lo & hi & 1)
        ok = count_ge(mid) >= float(topk)
        return jnp.where(ok, mid, lo), jnp.where(ok, hi, mid)

    vshape = (1, tq) if transposed else (tq, 1)
    thr, _ = lax.fori_loop(0, 32, bis, (jnp.full(vshape, INT_MIN, jnp.int32),
                                        jnp.full(vshape, INT_MAX, jnp.int32)))
    need = float(topk) - count_ge(thr + 1)
    if transposed:
        thr = pltpu.bitcast(jnp.transpose(pltpu.bitcast(jnp.broadcast_to(thr, (LANES, tq)), F32)), jnp.int32)
        need = jnp.transpose(jnp.broadcast_to(need, (LANES, tq)))
        thr = jnp.concatenate([thr] * (tk // LANES), axis=1)
        need = jnp.concatenate([need] * (tk // LANES), axis=1)

    qa = qa_ref[0]
    for g in range(KV_A):
        qs_scr[g] = jnp.concatenate(
            [qa[:, (g * rep + r) * DH:(g * rep + r + 1) * DH] for r in range(rep)], axis=0)
    m_scr[...] = jnp.full(m_scr.shape, NEG, F32)
    ones_k = jnp.ones((tk, LANES), BF16)
    acc_scr[...] = jnp.zeros(acc_scr.shape, F32)

    def p3(j, eqc):
        k0 = pl.multiple_of(j * tk, tk)
        if transposed:
            key = pltpu.bitcast(jnp.transpose(pltpu.bitcast(key_scr[pl.ds(k0, tk), :], F32)), jnp.int32)
        else:
            key = key_scr[:, pl.ds(k0, tk)]
        eq = key == thr
        eqf = jnp.where(eq, 1.0, 0.0)
        prefix = jnp.dot(eqf.astype(BF16), tri_ref[...], preferred_element_type=F32) + eqc
        bias = jnp.where(key > thr, 0.0, jnp.where(eq, jnp.where(prefix < need, 0.0, NEG), NEG))
        bias = jnp.where(key == INT_MIN, NEG, bias)
        eqc = eqc + jnp.sum(eqf, axis=1, keepdims=True)
        ss = [_dot_nt(qs_scr[g], ka_ref[0, pl.ds(k0, tk), g * DH:(g + 1) * DH])
              for g in range(KV_A)]
        for g in range(KV_A):
            vg = va_ref[0, pl.ds(k0, tk), g * DH:(g + 1) * DH]
            s = ss[g]
            ps, alphas = [], []
            for r in range(rep):
                h = g * rep + r
                sr = s[r * tq:(r + 1) * tq] + bias
                m_prev = m_scr[h]
                m_new = jnp.maximum(m_prev, jnp.max(sr, axis=1, keepdims=True))
                alpha = jnp.exp2(m_prev - m_new)
                p = jnp.exp2(sr - jnp.concatenate([m_new] * (tk // LANES), axis=1))
                m_scr[h] = m_new
                ps.append(p.astype(BF16))
                alphas.append(jnp.concatenate([alpha, alpha], axis=1))
            vg1 = jnp.concatenate([vg, ones_k], axis=1)
            for r in range(rep):
                h = g * rep + r
                acc_scr[h] = acc_scr[h] * alphas[r] + jnp.dot(ps[r], vg1, preferred_element_type=F32)
        return eqc

    lax.fori_loop(0, nkb, p3, jnp.zeros((tq, 1), F32))
    for h in range(H_A):
        acc = acc_scr[h]
        o_ref[0, :, h * DH:(h + 1) * DH] = (acc[:, :DH] / acc[:, DH:]).astype(o_ref.dtype)


def _dsa(qa, qi, wi, ka, va, ki, *, tq, tk, s_valid, q0):
    b, t, _ = qa.shape
    s_pad = ka.shape[1]
    assert s_pad % tk == 0 and t % tq == 0 and tk >= TOPK_MAX
    topk = min(TOPK_MAX, s_valid // 4)
    tri = jnp.asarray(np.triu(np.ones((tk, tk), np.float32), 1), BF16)
    body = functools.partial(_dsa_body, tq=tq, tk=tk, s_valid=s_valid, q0=q0, topk=topk)
    rep = H_A // KV_A
    transposed = tq % LANES == 0
    if transposed:
        wi = jnp.transpose(wi, (0, 2, 1))
        wi_spec = pl.BlockSpec((1, H_I, tq), lambda bi, i: (bi, 0, i))
    else:
        wi_spec = pl.BlockSpec((1, tq, H_I), lambda bi, i: (bi, i, 0))
    return pl.pallas_call(
        body,
        grid=(b, t // tq),
        in_specs=[pl.BlockSpec((1, tq, H_A * DH), lambda bi, i: (bi, i, 0)),
                  pl.BlockSpec((1, tq, H_I * D_IDX), lambda bi, i: (bi, i, 0)),
                  wi_spec,
                  pl.BlockSpec((1, s_pad, KV_A * DH), lambda bi, i: (bi, 0, 0)),
                  pl.BlockSpec((1, s_pad, KV_A * DH), lambda bi, i: (bi, 0, 0)),
                  pl.BlockSpec((1, s_pad, 2 * LANES), lambda bi, i: (bi, 0, 0)),
                  pl.BlockSpec((tk, tk), lambda bi, i: (0, 0))],
        out_specs=pl.BlockSpec((1, tq, H_A * DH), lambda bi, i: (bi, i, 0)),
        out_shape=jax.ShapeDtypeStruct((b, t, H_A * DH), BF16),
        scratch_shapes=[pltpu.VMEM((s_pad, tq) if transposed else (tq, s_pad), jnp.int32),
                        pltpu.VMEM((KV_A, rep * tq, DH), BF16),
                        pltpu.VMEM((H_A, tq, LANES), F32),
                        pltpu.VMEM((H_A, tq, DH + LANES), F32)],
        compiler_params=_cparams(("parallel", "arbitrary")),
        name="dsa",
    )(qa, qi, wi, ka, va, ki, tri)


def _sb_body(q_ref, k_ref, v_ref, low_ref, *rest, tq, tk, s_pad, q0, hpg, nchunk, past_heads):
    if past_heads:
        kp_ref, vp_ref, o_ref = rest
    else:
        (o_ref,) = rest
    i = pl.program_id(2)
    tqc = tq // nchunk
    q_first = q0 + i * tq
    q_last = q_first + tq - 1
    nkb = jnp.minimum((q_last + tk - 1) // tk, s_pad // tk)
    n_full = jnp.minimum(q_first // tk, nkb)
    row = lax.broadcasted_iota(jnp.int32, (tqc, tk), 0)
    col = lax.broadcasted_iota(jnp.int32, (tqc, tk), 1)
    chains = [(g, c) for g in range(hpg) for c in range(nchunk)]

    def step(j, carry, masked):
        k0 = pl.multiple_of(j * tk, tk)

        def load(new_ref, past_ref, g):
            if not past_heads:
                return new_ref[0, pl.ds(k0, tk), g * DH:(g + 1) * DH]
            if masked:
                return new_ref[0, :, g * DH:(g + 1) * DH]
            return past_ref[0, pl.ds(k0 * past_heads + g, tk, stride=past_heads), :].astype(BF16)

        zs = [_dot_nt(q_ref[0, c * tqc:(c + 1) * tqc, g * DH:(g + 1) * DH],
                      load(k_ref, kp_ref if past_heads else None, g)) for g, c in chains]
        lbs, lks, sufs, causals = [], [], [], []
        for (g, c), z in zip(chains, zs):
            lb = jnp.minimum(z, 0.0) - jnp.log(1.0 + jnp.exp2(-jnp.abs(z))) * LOG2E
            lk = lb - z
            causal = None
            if masked:
                causal = k0 + col < q_first + c * tqc + row
                lk = jnp.where(causal, lk, 0.0)
            hi = lk.astype(BF16)
            lo = (lk - hi.astype(F32)).astype(BF16)
            lbs.append(lb)
            lks.append(lk)
            causals.append(causal)
            sufs.append(jnp.dot(jnp.concatenate([hi, lo], axis=1), low_ref[...], preferred_element_type=F32))
        out = []
        for n, (g, c) in enumerate(chains):
            run, acc = carry[2 * n], carry[2 * n + 1]
            wgt = jnp.exp2(lbs[n] + sufs[n] + run)
            if masked:
                wgt = jnp.where(causals[n], wgt, 0.0)
            acc = acc + jnp.dot(wgt.astype(BF16), load(v_ref, vp_ref if past_heads else None, g),
                                preferred_element_type=F32)
            run = run + jnp.sum(lks[n], axis=1, keepdims=True)
            out += [run, acc]
        return tuple(out)

    carry = []
    for _ in chains:
        carry += [jnp.zeros((tqc, 1), F32), jnp.zeros((tqc, DH), F32)]
    carry = tuple(carry)
    carry = lax.fori_loop(0, nkb - n_full, lambda s, c: step(nkb - 1 - s, c, True), carry)
    carry = lax.fori_loop(0, n_full, lambda s, c: step(n_full - 1 - s, c, False), carry)
    for n, (g, c) in enumerate(chains):
        o_ref[0, c * tqc:(c + 1) * tqc, g * DH:(g + 1) * DH] = carry[2 * n + 1].astype(o_ref.dtype)


def _stick_breaking(qb, kb, vb, *, tq, tk, q0, hpg, nchunk, past=None):
    b, t, _ = qb.shape
    low = np.tril(np.ones((tk, tk), np.float32), -1)
    low = jnp.asarray(np.concatenate([low, low], axis=0), BF16)
    wd = hpg * DH
    if past is None:
        s_pad, extra, extra_specs, past_heads = kb.shape[1], [], [], 0
    else:
        *caches, layer = past
        assert hpg == H_B and t == tq and q0 % tk == 0 and t <= tk and caches[0].shape[2:] == (q0, H_B, DH)
        s_pad, past_heads = q0 + tk, H_B
        kb, vb = (jnp.pad(a, ((0, 0), (0, tk - t), (0, 0))) for a in (kb, vb))
        extra = [c.reshape(c.shape[0], b, q0 * H_B, DH) for c in caches]
        extra_specs = [pl.BlockSpec((None, 1, q0 * H_B, DH), lambda bi, h, i: (layer, bi, 0, 0))] * 2
    s_new = kb.shape[1]
    assert s_pad % tk == 0 and t % tq == 0 and H_B % hpg == 0 and tq % nchunk == 0
    body = functools.partial(_sb_body, tq=tq, tk=tk, s_pad=s_pad, q0=q0, hpg=hpg, nchunk=nchunk,
                             past_heads=past_heads)
    return pl.pallas_call(
        body,
        grid=(b, H_B // hpg, t // tq),
        in_specs=[pl.BlockSpec((1, tq, wd), lambda bi, h, i: (bi, i, h)),
                  pl.BlockSpec((1, s_new, wd), lambda bi, h, i: (bi, 0, h)),
                  pl.BlockSpec((1, s_new, wd), lambda bi, h, i: (bi, 0, h)),
                  pl.BlockSpec((2 * tk, tk), lambda bi, h, i: (0, 0))] + extra_specs,
        out_specs=pl.BlockSpec((1, tq, wd), lambda bi, h, i: (bi, i, h)),
        out_shape=jax.ShapeDtypeStruct((b, t, H_B * DH), BF16),
        compiler_params=_cparams(("parallel", "parallel", "arbitrary")),
        name="stick_breaking",
    )(qb, kb, vb, low, *extra)


def _mla_body(qn_ref, qr_ref, kv_ref, kr_ref, o_ref, *, tq, tk, s_valid, q0, hpg, nchunk):
    i = pl.program_id(2)
    tqc = tq // nchunk
    q_first = q0 + i * tq
    q_last = q_first + tq - 1
    lim_first = jnp.minimum((q_first // CHUNK + 1) * CHUNK, s_valid)
    lim_last = jnp.minimum((q_last // CHUNK + 1) * CHUNK, s_valid)
    nkb = (lim_last + tk - 1) // tk
    n_full = lim_first // tk
    row = lax.broadcasted_iota(jnp.int32, (tqc, tk), 0)
    col = lax.broadcasted_iota(jnp.int32, (tqc, tk), 1)
    chains = [(g, c) for g in range(hpg) for c in range(nchunk)]
    hw = NOPE + VD

    def step(j, carry, masked):
        k0 = pl.multiple_of(j * tk, tk)
        kr = kr_ref[0, pl.ds(k0, tk), :]
        ss = []
        for g, c in chains:
            rows = slice(c * tqc, (c + 1) * tqc)
            qc = jnp.concatenate([qn_ref[0, rows, g * NOPE:(g + 1) * NOPE],
                                  qr_ref[0, rows, g * LANES:(g + 1) * LANES]], axis=1)
            kc = jnp.concatenate([kv_ref[0, pl.ds(k0, tk), g * hw:g * hw + NOPE], kr], axis=1)
            ss.append(_dot_nt(qc, kc))
        ps, stats = [], []
        for n, (g, c) in enumerate(chains):
            m_prev = carry[2 * n]
            s = ss[n]
            if masked:
                lim = jnp.minimum((((q_first + c * tqc + row) >> CHUNK_SHIFT) + 1) * CHUNK, s_valid)
                s = jnp.where(k0 + col < lim, s, NEG)
            m_new = jnp.maximum(m_prev, jnp.max(s, axis=1, keepdims=True))
            alpha = jnp.exp2(m_prev - m_new)
            p = jnp.exp2(s - jnp.concatenate([m_new] * (tk // LANES), axis=1))
            stats.append((m_new, alpha))
            ps.append(p.astype(BF16))
        out = []
        for n, (g, c) in enumerate(chains):
            m_new, alpha = stats[n]
            vb = jnp.concatenate([kv_ref[0, pl.ds(k0, tk), g * hw + NOPE:(g + 1) * hw], ones_k], axis=1)
            acc = (carry[2 * n + 1] * jnp.concatenate([alpha, alpha], axis=1)
                   + jnp.dot(ps[n], vb, preferred_element_type=F32))
            out += [m_new, acc]
        return tuple(out)

    ones_k = jnp.ones((tk, LANES), BF16)
    carry = []
    for _ in chains:
        carry += [jnp.full((tqc, LANES), NEG, F32), jnp.zeros((tqc, VD + LANES), F32)]
    carry = tuple(carry)
    carry = lax.fori_loop(0, n_full, lambda j, c: step(j, c, False), carry)
    carry = lax.fori_loop(n_full, nkb, lambda j, c: step(j, c, True), carry)
    for n, (g, c) in enumerate(chains):
        acc = carry[2 * n + 1]
        o_ref[0, c * tqc:(c + 1) * tqc, g * VD:(g + 1) * VD] = (acc[:, :VD] / acc[:, VD:]).astype(o_ref.dtype)


def _mla(qn, qr, kv, kr, *, tq, tk, s_valid, q0, hpg, nchunk):
    b, t, _ = qn.shape
    s_pad = kv.shape[1]
    assert s_pad % tk == 0 and t % tq == 0 and H_C % hpg == 0 and tq % nchunk == 0
    body = functools.partial(_mla_body, tq=tq, tk=tk, s_valid=s_valid, q0=q0, hpg=hpg, nchunk=nchunk)
    return pl.pallas_call(
        body,
        grid=(b, H_C // hpg, t // tq),
        in_specs=[pl.BlockSpec((1, tq, hpg * NOPE), lambda bi, h, i: (bi, i, h)),
                  pl.BlockSpec((1, tq, hpg * LANES), lambda bi, h, i: (bi, i, h)),
                  pl.BlockSpec((1, s_pad, hpg * (NOPE + VD)), lambda bi, h, i: (bi, 0, h)),
                  pl.BlockSpec((1, s_pad, LANES), lambda bi, h, i: (bi, 0, 0))],
        out_specs=pl.BlockSpec((1, tq, hpg * VD), lambda bi, h, i: (bi, i, h)),
        out_shape=jax.ShapeDtypeStruct((b, t, H_C * VD), BF16),
        compiler_params=_cparams(("parallel", "parallel", "arbitrary")),
        name="mla",
    )(qn, qr, kv, kr)


def _absorb_q_body(qn_ref, qr_ref, wk_ref, qa_ref, qrh_ref):
    qa_ref[0] = _dot_nt(qn_ref[...], wk_ref[...]).astype(qa_ref.dtype)
    qrh_ref[0] = qr_ref[...]


def _absorb_q(qn, qr, w_ukv):
    m = qn.shape[0]
    return pl.pallas_call(
        _absorb_q_body,
        grid=(H_C,),
        in_specs=[pl.BlockSpec((m, NOPE), lambda h: (0, h)),
                  pl.BlockSpec((m, LANES), lambda h: (0, h)),
                  pl.BlockSpec((KV_RANK, NOPE), lambda h: (0, 2 * h))],
        out_specs=[pl.BlockSpec((1, m, KV_RANK), lambda h: (h, 0, 0)),
                   pl.BlockSpec((1, m, LANES), lambda h: (h, 0, 0))],
        out_shape=[jax.ShapeDtypeStruct((H_C, m, KV_RANK), BF16),
                   jax.ShapeDtypeStruct((H_C, m, LANES), BF16)],
        compiler_params=_cparams(("parallel",)),
        name="mla_absorb_q",
    )(qn, qr, w_ukv)


def _mla_latent_body(qa_ref, qr_ref, c_ref, kr_ref, cp_ref, krp_ref, o_ref, *, t, tk, s_valid, q0, nsplit):
    hs = H_C // nsplit
    rows = hs * t
    lim_first = min((q0 // CHUNK + 1) * CHUNK, s_valid)
    lim_last = min(((q0 + t - 1) // CHUNK + 1) * CHUNK, s_valid)
    nkb = -(-lim_last // tk)
    n_full = lim_first // tk
    assert n_full == q0 // tk and nkb == n_full + 1
    q_pos = q0 + lax.broadcasted_iota(jnp.int32, (hs, t, tk), 1).reshape(rows, tk)
    lim = jnp.minimum(((q_pos >> CHUNK_SHIFT) + 1) * CHUNK, s_valid)
    col = lax.broadcasted_iota(jnp.int32, (rows, tk), 1)
    ones_k = jnp.ones((tk, LANES), BF16)
    qs = [jnp.concatenate([qa_ref[n * hs:(n + 1) * hs].reshape(rows, KV_RANK),
                           qr_ref[n * hs:(n + 1) * hs].reshape(rows, LANES)], axis=1) for n in range(nsplit)]

    def step(j, carry, masked):
        k0 = pl.multiple_of(j * tk, tk)
        if masked:
            cb, krb = c_ref[0], kr_ref[0]
        else:
            cb = cp_ref[0, pl.ds(k0, tk), :].astype(BF16)
            krb = jnp.concatenate([krp_ref[0, pl.ds(k0, tk), :].astype(BF16),
                                   jnp.zeros((tk, LANES - ROPE_C), BF16)], axis=1)
        kc = jnp.concatenate([cb, krb], axis=1)
        vc = jnp.concatenate([cb, ones_k], axis=1)
        ss = [_dot_nt(q, kc) for q in qs]
        ps, stats = [], []
        for n in range(nsplit):
            s = ss[n]
            if masked:
                s = jnp.where(k0 + col < lim, s, NEG)
            m_prev = carry[2 * n]
            m_new = jnp.maximum(m_prev, jnp.max(s, axis=1, keepdims=True))
            alpha = jnp.exp2(m_prev - m_new)
            ps.append(jnp.exp2(s - jnp.concatenate([m_new] * (tk // LANES), axis=1)).astype(BF16))
            stats.append((m_new, alpha))
        out = []
        for n in range(nsplit):
            m_new, alpha = stats[n]
            acc = (carry[2 * n + 1] * jnp.concatenate([alpha] * (KV_RANK // LANES + 1), axis=1)
                   + jnp.dot(ps[n], vc, preferred_element_type=F32))
            out += [m_new, acc]
        return tuple(out)

    carry = []
    for _ in range(nsplit):
        carry += [jnp.full((rows, LANES), NEG, F32), jnp.zeros((rows, KV_RANK + LANES), F32)]
    carry = tuple(carry)
    carry = lax.fori_loop(0, n_full, lambda j, c: step(j, c, False), carry)
    carry = lax.fori_loop(n_full, nkb, lambda j, c: step(j, c, True), carry)
    for n in range(nsplit):
        acc = carry[2 * n + 1]
        lat = acc[:, :KV_RANK] / jnp.concatenate([acc[:, KV_RANK:]] * (KV_RANK // LANES), axis=1)
        o_ref[0, n * hs:(n + 1) * hs] = lat.reshape(hs, t, KV_RANK).astype(o_ref.dtype)


def _mla_latent(qa_hm, qr_hm, c_new, kr_new, c_past, kr_past, layer, *, tk, s_valid, q0):
    b, t, _ = c_new.shape
    assert q0 % tk == 0 and t <= tk and c_past.shape[2] == q0
    c_new, kr_new = (jnp.pad(a, ((0, 0), (0, tk - t), (0, 0))) for a in (c_new, kr_new))
    body = functools.partial(_mla_latent_body, t=t, tk=tk, s_valid=s_valid, q0=q0, nsplit=2)
    return pl.pallas_call(
        body,
        grid=(b,),
        in_specs=[pl.BlockSpec((H_C, t, KV_RANK), lambda bi: (0, bi, 0)),
                  pl.BlockSpec((H_C, t, LANES), lambda bi: (0, bi, 0)),
                  pl.BlockSpec((1, tk, KV_RANK), lambda bi: (bi, 0, 0)),
                  pl.BlockSpec((1, tk, LANES), lambda bi: (bi, 0, 0)),
                  pl.BlockSpec((None, 1, q0, KV_RANK), lambda bi: (layer, bi, 0, 0)),
                  pl.BlockSpec((None, 1, q0, ROPE_C), lambda bi: (layer, bi, 0, 0))],
        out_specs=pl.BlockSpec((1, H_C, t, KV_RANK), lambda bi: (bi, 0, 0, 0)),
        out_shape=jax.ShapeDtypeStruct((b, H_C, t, KV_RANK), BF16),
        compiler_params=_cparams(("parallel",)),
        name="mla_latent",
    )(qa_hm, qr_hm, c_new, kr_new, c_past, kr_past)


def _latent_out_body(x_ref, wv_ref, o_ref):
    nb, _, t, r = x_ref.shape
    o_ref[...] = jnp.dot(x_ref[...].reshape(nb * t, r), wv_ref[...],
                         preferred_element_type=F32).astype(o_ref.dtype)


def _latent_out(lat, w_ukv):
    b, _, t, _ = lat.shape
    return pl.pallas_call(
        _latent_out_body,
        grid=(H_C,),
        in_specs=[pl.BlockSpec((b, 1, t, KV_RANK), lambda h: (0, h, 0, 0)),
                  pl.BlockSpec((KV_RANK, VD), lambda h: (0, 2 * h + 1))],
        out_specs=pl.BlockSpec((b * t, VD), lambda h: (0, h)),
        out_shape=jax.ShapeDtypeStruct((b * t, H_C * VD), BF16),
        compiler_params=_cparams(("parallel",)),
        name="mla_latent_out",
    )(lat, w_ukv)


def _ffn_body(x_ref, g_ref, wg_ref, wv_ref, cwg_ref, cwv_ref, cbg_ref, cbv_ref, wd_ref, pg_ref, pv_ref, gn_ref,
              o_ref, hn_ref, sg_ref, sv_ref, h_scr, acc_scr, ug_buf, uv_buf, cg_scr, cv_scr, *, seg, carried):
    i = pl.program_id(0)
    f = pl.program_id(1)
    tm = x_ref.shape[0]

    @pl.when(f == 0)
    def _():
        x = x_ref[...]
        y = x * lax.rsqrt(jnp.mean(x * x, axis=-1, keepdims=True) + EPS)
        h_scr[...] = (y * g_ref[...]).astype(BF16)
        acc_scr[...] = x

    h = h_scr[...]
    halves = ((wg_ref, cwg_ref, cbg_ref, pg_ref, sg_ref, ug_buf, cg_scr),
              (wv_ref, cwv_ref, cbv_ref, pv_ref, sv_ref, uv_buf, cv_scr))
    ys = []
    for w_ref, cw_ref, cb_ref, p_ref, s_ref, buf, c_scr in halves:
        u = jnp.dot(h, w_ref[...], preferred_element_type=F32)
        cw = cw_ref[...]
        parts = []
        for sgi in range(tm // seg):
            us = u[sgi * seg:(sgi + 1) * seg]
            if carried:
                head = jnp.where(i == 0, p_ref[0], c_scr[f])
            else:
                head = p_ref[sgi]
            buf[sgi, 0:SUBLANES] = head
            buf[sgi, SUBLANES:SUBLANES + seg] = us
            y = (cw[2:3] * us + cw[1:2] * buf[sgi, SUBLANES - 1:SUBLANES - 1 + seg]
                 + cw[0:1] * buf[sgi, SUBLANES - 2:SUBLANES - 2 + seg] + cb_ref[...])
            parts.append(y)
            last = us[seg - SUBLANES:seg]
            s_ref[sgi] = last
            if carried:
                c_scr[f] = last
        ys.append(parts[0] if len(parts) == 1 else jnp.concatenate(parts, axis=0))
    yg, yv = ys
    act = (yg / (1.0 + jnp.exp(-yg))) * yv
    acc_scr[...] += jnp.dot(act.astype(BF16), wd_ref[...], preferred_element_type=F32)

    @pl.when(f == pl.num_programs(1) - 1)
    def _():
        a = acc_scr[...]
        o_ref[...] = a
        y = a * lax.rsqrt(jnp.mean(a * a, axis=-1, keepdims=True) + EPS)
        hn_ref[...] = (y * gn_ref[...]).astype(hn_ref.dtype)


def _ffn(x, g, w_up, conv_w, conv_b, w_down, layer, prev8, g_next, next_dtype, *, seg, carried, tm, tf):
    m, d = x.shape
    nf = D_FF // tf
    nseg = tm // seg
    assert m % tm == 0 and D_FF % tf == 0 and tm % seg == 0
    sidx = (lambda i: 0) if carried else (lambda i: i)
    body = functools.partial(_ffn_body, seg=seg, carried=carried)
    cb = conv_b.reshape(conv_b.shape[0], 1, 2 * D_FF)
    return pl.pallas_call(
        body,
        grid=(m // tm, nf),
        in_specs=[pl.BlockSpec((tm, d), lambda i, f: (i, 0)),
                  pl.BlockSpec((1, d), lambda i, f: (0, 0)),
                  pl.BlockSpec((None, d, tf), lambda i, f: (layer, 0, f)),
                  pl.BlockSpec((None, d, tf), lambda i, f: (layer, 0, nf + f)),
                  pl.BlockSpec((None, CONV_W, tf), lambda i, f: (layer, 0, f)),
                  pl.BlockSpec((None, CONV_W, tf), lambda i, f: (layer, 0, nf + f)),
                  pl.BlockSpec((None, 1, tf), lambda i, f: (layer, 0, f)),
                  pl.BlockSpec((None, 1, tf), lambda i, f: (layer, 0, nf + f)),
                  pl.BlockSpec((None, tf, d), lambda i, f: (layer, f, 0)),
                  pl.BlockSpec((nseg, SUBLANES, tf), lambda i, f: (sidx(i), 0, f)),
                  pl.BlockSpec((nseg, SUBLANES, tf), lambda i, f: (sidx(i), 0, nf + f)),
                  pl.BlockSpec((1, d), lambda i, f: (0, 0))],
        out_specs=[pl.BlockSpec((tm, d), lambda i, f: (i, 0)),
                   pl.BlockSpec((tm, d), lambda i, f: (i, 0)),
                   pl.BlockSpec((nseg, SUBLANES, tf), lambda i, f: (i, 0, f)),
                   pl.BlockSpec((nseg, SUBLANES, tf), lambda i, f: (i, 0, f))],
        out_shape=[jax.ShapeDtypeStruct((m, d), F32),
                   jax.ShapeDtypeStruct((m, d), next_dtype),
                   jax.ShapeDtypeStruct((m // seg, SUBLANES, D_FF), F32),
                   jax.ShapeDtypeStruct((m // seg, SUBLANES, D_FF), F32)],
        scratch_shapes=[pltpu.VMEM((tm, d), BF16),
                        pltpu.VMEM((tm, d), F32),
                        pltpu.VMEM((nseg, SUBLANES + seg, tf), F32),
                        pltpu.VMEM((nseg, SUBLANES + seg, tf), F32),
                        pltpu.VMEM((nf, SUBLANES, tf), F32),
                        pltpu.VMEM((nf, SUBLANES, tf), F32)],
        compiler_params=_cparams(("arbitrary", "arbitrary")),
        name="conv_ffn",
    )(x, g.reshape(1, d), w_up, w_up, conv_w, conv_w, cb, cb, w_down, prev8, prev8, g_next.reshape(1, d))


def _rope_tables(pos, head_dim, rot, lanes_valid=LANES):
    half = rot // 2
    freqs = ROPE_THETA ** (-jnp.arange(half, dtype=F32) / half)
    ang = pos.astype(F32)[:, None] * freqs[None, :]
    cos, sin = jnp.cos(ang), jnp.sin(ang)
    lane = np.arange(LANES)
    d = lane % head_dim
    first = (d < half) & (lane < lanes_valid)
    second = (d >= half) & (d < rot) & (lane < lanes_valid)
    fidx = np.clip(np.where(d < half, d, d - half), 0, half - 1)
    cosl, sinl = cos[:, fidx], sin[:, fidx]
    c = jnp.where(first | second, cosl, 1.0)
    s1 = jnp.where(second, sinl, 0.0)
    s2 = jnp.where(first, -sinl, 0.0)
    return c, s1, s2, half


def _prep_weights(w_in_ab, w_out_ab, w_dqkv, w_uq, w_ukv, w_o_mla, w_up, w_down):
    n_even, n_odd = w_in_ab.shape[0], w_dqkv.shape[0]
    w = {}
    n_a = H_A * DH + 2 * KV_A * DH + H_I * D_IDX
    w["in_a"] = w_in_ab[:, :, :n_a].astype(BF16)
    pad = jnp.zeros((n_even, D_MODEL, LANES - D_IDX - H_I), BF16)
    w["in_b"] = jnp.concatenate([w_in_ab[:, :, n_a:n_a + D_IDX + H_I].astype(BF16), pad,
                                 w_in_ab[:, :, n_a + D_IDX + H_I:].astype(BF16)], axis=-1)
    w["out_ab"] = w_out_ab.astype(BF16)
    w["dqkr"] = jnp.concatenate([w_dqkv.astype(BF16),
                                 jnp.zeros((n_odd, D_MODEL, LANES - ROPE_C), BF16)], axis=-1)
    uq = w_uq.astype(BF16).reshape(n_odd, Q_RANK, H_C, NOPE + ROPE_C)
    uqn = uq[..., :NOPE].reshape(n_odd, Q_RANK, H_C * NOPE)
    uqr = uq[..., NOPE:]
    uqr = jnp.concatenate([uqr, jnp.zeros_like(uqr)], axis=-1).reshape(n_odd, Q_RANK, H_C * LANES)
    w["uq"] = jnp.concatenate([uqn, uqr], axis=-1)
    w["ukv"] = w_ukv.astype(BF16)
    w["o_mla"] = w_o_mla.astype(BF16)
    w["up"] = w_up.astype(BF16)
    w["down"] = w_down.astype(BF16)
    return w


def _with_past(past, new, s_pad):
    b, _, fdim = new.shape
    parts = [new.astype(BF16)] if past is None else [past.astype(BF16), new.astype(BF16)]
    n = sum(p.shape[1] for p in parts)
    if s_pad > n:
        parts.append(jnp.zeros((b, s_pad - n, fdim), BF16))
    return parts[0] if len(parts) == 1 else jnp.concatenate(parts, axis=1)


def _trunk(x, past, conv_state, w, p, cfg):
    b, t, d = x.shape
    m = b * t
    q0 = 0 if past is None else past[0].shape[2]
    s_valid = q0 + t
    tk = cfg["tk"]
    s_pad = -(-s_valid // tk) * tk
    pos = q0 + jnp.tile(jnp.arange(t), b)
    tab_a = _rope_tables(pos, DH, ROT_A)
    tab_i = _rope_tables(pos, D_IDX, ROT_I)
    tab_ki = _rope_tables(pos, D_IDX, ROT_I, lanes_valid=D_IDX)
    tab_kr = _rope_tables(pos, ROPE_C, ROPE_C, lanes_valid=ROPE_C)

    xf = x.reshape(m, d)
    ab_rows, mla_rows, conv_rows = [], [], []
    sc_ab = DH ** -0.5 * LOG2E
    sc_c = (NOPE + ROPE_C) ** -0.5 * LOG2E
    h = _rmsnorm(xf, p["norm_mix"][0], BF16)
    for l in range(DEPTH):
        if l % 2 == 0:
            e = l // 2
            lp = None if past is None else tuple(c[e] for c in past[:5])
            qa, ka32, ka16, va32, va16, qi = _proj(
                [h], w["in_a"][e],
                [dict(width=H_A * DH, dtypes=(BF16,), rope=0, scale=sc_ab),
                 dict(width=KV_A * DH, dtypes=(F32, BF16), rope=0, split_heads=True),
                 dict(width=KV_A * DH, dtypes=(F32, BF16), split_heads=True),
                 dict(width=H_I * D_IDX, dtypes=(BF16,), rope=1)],
                tabs=(tab_a, tab_i), name="proj_in_a")
            kiwi, qb, kb32, kb16, vb32, vb16 = _proj(
                [h], w["in_b"][e],
                [dict(width=LANES, dtypes=(F32,), rope=0),
                 dict(width=H_B * DH, dtypes=(BF16,), scale=sc_ab),
                 dict(width=H_B * DH, dtypes=(F32, BF16), split_heads=True),
                 dict(width=H_B * DH, dtypes=(F32, BF16), split_heads=True)],
                tabs=(tab_ki,), name="proj_in_b")
            ki32 = kiwi[:, :D_IDX]
            wi = kiwi[:, D_IDX:D_IDX + H_I]
            ab_rows.append((ka32.reshape(b, t, KV_A, DH), va32.reshape(b, t, KV_A, DH),
                            ki32.reshape(b, t, D_IDX), kb32.reshape(b, t, H_B, DH),
                            vb32.reshape(b, t, H_B, DH)))

            def full(idx, new16):
                pst = None if lp is None else lp[idx].reshape(b, q0, -1)
                return _with_past(pst, new16.reshape(b, t, -1), s_pad)

            ki_all = full(2, ki32)
            zk = jnp.zeros_like(ki_all)
            ki2 = jnp.concatenate([ki_all, zk, zk, ki_all], axis=-1)
            oa = _dsa(qa.reshape(b, t, -1), qi.reshape(b, t, -1), wi.reshape(b, t, H_I), full(0, ka16),
                      full(1, va16), ki2, tq=cfg["tq_a"], tk=tk, s_valid=s_valid, q0=q0)
            if lp is None:
                ob = _stick_breaking(qb.reshape(b, t, -1), full(3, kb16), full(4, vb16), tq=cfg["tq_b"],
                                     tk=cfg["tk_b"], q0=q0, hpg=cfg["hpg_b"], nchunk=cfg["nc_b"])
            else:
                ob = _stick_breaking(qb.reshape(b, t, -1), kb16.reshape(b, t, -1), vb16.reshape(b, t, -1),
                                     tq=cfg["tq_b"], tk=cfg["tk_b"], q0=q0, hpg=cfg["hpg_b"],
                                     nchunk=cfg["nc_b"], past=(past[3], past[4], e))
            (xf,) = _proj([oa.reshape(m, -1), ob.reshape(m, -1)], w["out_ab"][e],
                          [dict(width=D_MODEL, dtypes=(F32,))], resid=xf, name="proj_out_ab")
        else:
            od = l // 2
            gains = jnp.concatenate([p["g_q"][od], p["g_kv"][od]])
            cq16, ckv32, ckv16, kr32w, kr16w = _proj(
                [h], w["dqkr"][od],
                [dict(width=Q_RANK, dtypes=(BF16,), gain_off=0),
                 dict(width=KV_RANK, dtypes=(F32, BF16), gain_off=Q_RANK),
                 dict(width=LANES, dtypes=(F32, BF16), rope=0)],
                tabs=(tab_kr,), gain=gains, name="proj_dqkr")
            mla_rows.append((ckv32.reshape(b, t, KV_RANK), kr32w[:, :ROPE_C].reshape(b, t, ROPE_C)))
            qn, qr = _proj([cq16], w["uq"][od],
                           [dict(width=H_C * NOPE, dtypes=(BF16,), scale=sc_c),
                            dict(width=H_C * LANES, dtypes=(BF16,), rope=0, scale=sc_c)],
                           tabs=(tab_kr,), name="proj_uq")
            if cfg["latent_c"]:
                qa_hm, qr_hm = _absorb_q(qn, qr, w["ukv"][od])
                lat = _mla_latent(qa_hm, qr_hm, ckv16.reshape(b, t, KV_RANK), kr16w.reshape(b, t, LANES),
                                  past[5], past[6], od, tk=cfg["tk_c"], s_valid=s_valid, q0=q0)
                o = _latent_out(lat, w["ukv"][od])
            else:
                pc = None if past is None else past[5][od]
                pk = None if past is None else jnp.pad(past[6][od], ((0, 0), (0, 0), (0, LANES - ROPE_C)))
                c_all = _with_past(pc, ckv16.reshape(b, t, KV_RANK), s_pad)
                kr_all = _with_past(pk, kr16w.reshape(b, t, LANES), s_pad)
                (kv,) = _proj([c_all.reshape(b * s_pad, KV_RANK)], w["ukv"][od],
                              [dict(width=H_C * (NOPE + VD), dtypes=(BF16,))], name="proj_ukv")
                o = _mla(qn.reshape(b, t, -1), qr.reshape(b, t, -1), kv.reshape(b, s_pad, -1), kr_all,
                         tq=cfg["tq_c"], tk=cfg["tk_c"], s_valid=s_valid, q0=q0, hpg=cfg["hpg_c"],
                         nchunk=cfg["nc_c"])
            (xf,) = _proj([o.reshape(m, -1)], w["o_mla"][od], [dict(width=D_MODEL, dtypes=(F32,))],
                          resid=xf, name="proj_o_mla")

        if conv_state is None:
            prev8 = jnp.zeros((b, SUBLANES, 2 * D_FF), F32)
        else:
            prev8 = jnp.concatenate([jnp.zeros((b, SUBLANES - (CONV_W - 1), 2 * D_FF), F32),
                                     conv_state[l]], axis=1)
        final = l == DEPTH - 1
        g_next = p["norm_final"] if final else p["norm_mix"][l + 1]
        xf, h, sg, sv = _ffn(xf, p["norm_ffn"][l], w["up"], p["conv_w"], p["conv_b"], w["down"], l,
                             prev8, g_next, F32 if final else BF16, seg=cfg["seg"], carried=cfg["carried"],
                             tm=cfg["tm_ffn"], tf=cfg["tf"])
        last = [s8.reshape(b, -1, SUBLANES, D_FF)[:, -1, SUBLANES - (CONV_W - 1):] for s8 in (sg, sv)]
        conv_rows.append(jnp.concatenate(last, axis=-1))

    y = h.reshape(b, t, d)
    new_ab = [jnp.stack([r[i] for r in ab_rows]) for i in range(5)]
    new_mla = [jnp.stack([r[i] for r in mla_rows]) for i in range(2)]
    return y, new_ab + new_mla + [jnp.stack(conv_rows)]


def kernel(x_prompt, x_sample, cache_k_a, cache_v_a, cache_idx_k, cache_k_b, cache_v_b, cache_ckv, cache_krope, state_conv, norm_mix, norm_ffn, norm_final, w_in_ab, w_out_ab, w_dqkv, g_q, g_kv, w_uq, w_ukv, w_o_mla, w_up, conv_w, conv_b, w_down):
    w = _prep_weights(w_in_ab, w_out_ab, w_dqkv, w_uq, w_ukv, w_o_mla, w_up, w_down)
    p = dict(norm_mix=norm_mix, norm_ffn=norm_ffn, norm_final=norm_final, g_q=g_q, g_kv=g_kv,
             conv_w=conv_w, conv_b=conv_b)
    cfg_p = dict(tk=512, tq_a=256, tq_b=256, tk_b=256, hpg_b=4, nc_b=1, tq_c=512, tk_c=1024, hpg_c=2, nc_c=2,
                 latent_c=False, seg=512, carried=True, tm_ffn=512, tf=512)
    t_s = x_sample.shape[1]
    cfg_s = dict(tk=256, tq_a=t_s, tq_b=t_s, tk_b=256, hpg_b=H_B, nc_b=1, tq_c=t_s, tk_c=256, hpg_c=8, nc_c=1,
                 latent_c=True, seg=t_s, carried=False, tm_ffn=x_sample.shape[0] * t_s, tf=512)
    y_p, st_p = _trunk(x_prompt, None, None, w, p, cfg_p)
    past = (cache_k_a, cache_v_a, cache_idx_k, cache_k_b, cache_v_b, cache_ckv, cache_krope)
    y_s, st_s = _trunk(x_sample, past, state_conv, w, p, cfg_s)
    return (y_p, y_s, *st_p, *st_s)
```
